```python
import math
import functools
import jax
import jax.numpy as jnp
from jax import lax
import numpy as np

D_MODEL = 1024
BATCH = 2
SEQ = 16384
DEPTH = 2

HEAD_DIM = 64
ROPE_DIM = HEAD_DIM // 4
NOPE_DIM = HEAD_DIM - ROPE_DIM
ROPE_THETA = 500000.0
N_HEADS_A = (3 * D_MODEL // 8) // HEAD_DIM
A_WIDTH = N_HEADS_A * HEAD_DIM
KV_LATENT = D_MODEL // 8
N_IDX_HEADS = 4
IDX_DIM = 64
IDX_ROPE_DIM = IDX_DIM // 4
TOPK_MAX = 256
N_POOL_GROUPS = 4
POOL_WINDOWS = (2, 4, 8, 16)
POOL_WIDTH = D_MODEL // 4
POOL_GROUP_DIM = POOL_WIDTH // N_POOL_GROUPS
N_HEADS_C = (D_MODEL // 4) // HEAD_DIM
C_WIDTH = N_HEADS_C * HEAD_DIM
N_BRANCHES = 3
IN_SIZES = (A_WIDTH, KV_LATENT, ROPE_DIM, N_IDX_HEADS * IDX_DIM, IDX_DIM, N_IDX_HEADS,
            POOL_WIDTH, C_WIDTH, C_WIDTH, C_WIDTH, N_BRANCHES * D_MODEL)
IN_WIDTH = sum(IN_SIZES)
Q_BLOCK = 128
N_SUPER = 4
D_FF = 2816
N_EXPERTS = 8
TOP_K = 2
D_FF_EXPERT = D_FF // TOP_K
RMS_EPS = 1e-6
N_DENSE = (DEPTH + 1) // 2
N_MOE = DEPTH // 2

kernel_name = "hybrid_dsa_pool_stickbreak_moe_block"


def rmsnorm(x, g):
    xf = x.astype(jnp.float32)
    y = xf * lax.rsqrt(jnp.mean(xf * xf, axis=-1, keepdims=True) + RMS_EPS)
    return (y * g.astype(jnp.float32)).astype(x.dtype)


def split_sizes(x, sizes):
    outs, off = [], 0
    for s in sizes:
        outs.append(x[..., off:off + s])
        off += s
    return outs


def rope_tables(positions, rot_dim):
    inv = ROPE_THETA ** (-jnp.arange(0, rot_dim, 2, dtype=jnp.float32) / rot_dim)
    ang = positions.astype(jnp.float32)[..., None] * inv
    return jnp.cos(ang)[:, :, None, :], jnp.sin(ang)[:, :, None, :]


def partial_rope(x, cos, sin):
    half = cos.shape[-1]
    r = 2 * half
    x1 = x[..., :half].astype(jnp.float32)
    x2 = x[..., half:r].astype(jnp.float32)
    rot = jnp.concatenate([x1 * cos - x2 * sin, x2 * cos + x1 * sin], axis=-1).astype(x.dtype)
    return jnp.concatenate([rot, x[..., r:]], axis=-1)


def to_blocks(a):
    b, l = a.shape[:2]
    a = a.reshape((b, l // Q_BLOCK, Q_BLOCK) + a.shape[2:])
    return jnp.moveaxis(a, 1, 0)


def from_blocks(a):
    a = jnp.moveaxis(a, 0, 1)
    return a.reshape((a.shape[0], a.shape[1] * a.shape[2]) + a.shape[3:])


def causal_chunks(L):
    n_super = math.gcd(N_SUPER, L // Q_BLOCK)
    return n_super, L // n_super


def _dsa_block(args, kc, vc, kic, k_sel):
    qb, qib, wib, start = args
    lk = kc.shape[1]
    key_pos = jnp.arange(lk, dtype=jnp.int32)
    qpos = start + jnp.arange(Q_BLOCK, dtype=jnp.int32)
    s = jnp.einsum('bqhd,bkd->bqhk', qib, kic).astype(jnp.float32) * IDX_DIM ** -0.5
    w = wib.astype(jnp.float32) * N_IDX_HEADS ** -0.5
    score = jnp.einsum('bqh,bqhk->bqk', w, jax.nn.relu(s))
    causal = key_pos[None, :] <= qpos[:, None]
    score = jnp.where(causal[None], score, -jnp.inf)
    _, idx = lax.top_k(score, k_sel)
    valid = idx <= qpos[None, :, None]
    kg = jax.vmap(lambda kk, ii: kk[ii])(kc, idx)
    vg = jax.vmap(lambda vv, ii: vv[ii])(vc, idx)
    logits = jnp.einsum('bqhd,bqkd->bqhk', qb, kg).astype(jnp.float32) * HEAD_DIM ** -0.5
    logits = jnp.where(valid[:, :, None, :], logits, -jnp.inf)
    p = jax.nn.softmax(logits, axis=-1).astype(vg.dtype)
    return jnp.einsum('bqhk,bqkd->bqhd', p, vg)


def dsa_attention(q, k, v, q_idx, k_idx, w_idx):
    L = q.shape[1]
    k_sel = min(TOPK_MAX, L // 4)
    n_super, chunk = causal_chunks(L)
    outs = []
    for ci in range(n_super):
        lk = (ci + 1) * chunk
        fn = functools.partial(_dsa_block, kc=k[:, :lk], vc=v[:, :lk], kic=k_idx[:, :lk], k_sel=k_sel)
        sl = slice(ci * chunk, (ci + 1) * chunk)
        starts = ci * chunk + jnp.arange(chunk // Q_BLOCK, dtype=jnp.int32) * Q_BLOCK
        out = lax.map(fn, (to_blocks(q[:, sl]), to_blocks(q_idx[:, sl]), to_blocks(w_idx[:, sl]), starts))
        outs.append(from_blocks(out))
    return jnp.concatenate(outs, axis=1)


def _stick_block(args, kc, vc):
    qb, start = args
    b, _, h, _ = qb.shape
    lk = kc.shape[1]
    nk = lk // Q_BLOCK
    key_pos = jnp.arange(lk, dtype=jnp.int32)
    qpos = start + jnp.arange(Q_BLOCK, dtype=jnp.int32)
    z = jnp.einsum('bqhd,bkhd->bhqk', qb, kc).astype(jnp.float32) * HEAD_DIM ** -0.5
    mask = (key_pos[None, :] < qpos[:, None])[None, None]
    log_keep = jnp.where(mask, jax.nn.log_sigmoid(-z), 0.0)
    blk = log_keep.reshape(b, h, Q_BLOCK, nk, Q_BLOCK)
    later = (jnp.arange(Q_BLOCK)[:, None] > jnp.arange(Q_BLOCK)[None, :]).astype(jnp.float32)
    within = jnp.einsum('bhqnj,js->bhqns', blk, later, precision=lax.Precision.HIGHEST)
    tot = jnp.sum(blk, axis=-1)
    after_blocks = lax.cumsum(tot, axis=3, reverse=True) - tot
    after = (within + after_blocks[..., None]).reshape(b, h, Q_BLOCK, lk)
    a = jnp.where(mask, jnp.exp(jax.nn.log_sigmoid(z) + after), 0.0).astype(vc.dtype)
    return jnp.einsum('bhqk,bkhd->bqhd', a, vc)


def stick_breaking_attention(q, k, v):
    L = q.shape[1]
    n_super, chunk = causal_chunks(L)
    outs = []
    for ci in range(n_super):
        lk = (ci + 1) * chunk
        fn = functools.partial(_stick_block, kc=k[:, :lk], vc=v[:, :lk])
        starts = ci * chunk + jnp.arange(chunk // Q_BLOCK, dtype=jnp.int32) * Q_BLOCK
        out = lax.map(fn, (to_blocks(q[:, ci * chunk:(ci + 1) * chunk]), starts))
        outs.append(from_blocks(out))
    return jnp.concatenate(outs, axis=1)


def multiscale_pool(u, w_pool, pool_scale):
    b, l, _ = u.shape
    ug = u.astype(jnp.float32).reshape(b, l, N_POOL_GROUPS, POOL_GROUP_DIM)
    csum = jnp.cumsum(ug, axis=1)
    count_base = jnp.arange(1, l + 1, dtype=jnp.float32)
    pooled = []
    for g, win in enumerate(POOL_WINDOWS):
        cg = csum[:, :, g]
        lagged = jnp.pad(cg, ((0, 0), (win, 0), (0, 0)))[:, :l]
        cnt = jnp.minimum(count_base, float(win))[None, :, None]
        pooled.append((cg - lagged) / cnt)
    p = (jnp.stack(pooled, axis=2) - ug).astype(u.dtype)
    y = jnp.einsum('blgc,gcd->blgd', p, w_pool) * pool_scale.reshape(N_POOL_GROUPS, POOL_GROUP_DIM)
    return y.reshape(b, l, POOL_WIDTH)


def hybrid_mixer(h, rope_a, rope_i, w_in, g_lat, w_uk, w_uv, w_pool, pool_scale,
                 w_br_a, w_br_b, w_br_c, w_out):
    b, l, _ = h.shape
    cos_a, sin_a = rope_a
    cos_i, sin_i = rope_i
    proj = h @ w_in
    (q_a, lat, k_rope, q_idx, k_idx, w_idx, u_pool, q_c, k_c, v_c, gates) = split_sizes(proj, IN_SIZES)
    q_a = partial_rope(q_a.reshape(b, l, N_HEADS_A, HEAD_DIM), cos_a, sin_a)
    lat = rmsnorm(lat, g_lat)
    k_a = jnp.concatenate([partial_rope(k_rope[:, :, None, :], cos_a, sin_a)[:, :, 0], lat @ w_uk], axis=-1)
    v_a = lat @ w_uv
    q_idx = partial_rope(q_idx.reshape(b, l, N_IDX_HEADS, IDX_DIM), cos_i, sin_i)
    k_idx = partial_rope(k_idx[:, :, None, :], cos_i, sin_i)[:, :, 0]
    y_a = dsa_attention(q_a, k_a, v_a, q_idx, k_idx, w_idx).reshape(b, l, A_WIDTH)
    y_b = multiscale_pool(u_pool, w_pool, pool_scale)
    y_c = stick_breaking_attention(q_c.reshape(b, l, N_HEADS_C, HEAD_DIM),
                                   k_c.reshape(b, l, N_HEADS_C, HEAD_DIM),
                                   v_c.reshape(b, l, N_HEADS_C, HEAD_DIM)).reshape(b, l, C_WIDTH)
    g_a, g_b, g_c = jnp.split(jax.nn.sigmoid(gates), N_BRANCHES, axis=-1)
    merged = g_a * (y_a @ w_br_a) + g_b * (y_b @ w_br_b) + g_c * (y_c @ w_br_c)
    return merged @ w_out


def swiglu(x, w_gate, w_up, w_down):
    return (jax.nn.silu(x @ w_gate) * (x @ w_up)) @ w_down


def moe_swiglu(h, w_router, w_gate, w_up, w_down):
    b, l, d = h.shape
    xf = h.reshape(b * l, d)
    logits = (xf @ w_router).astype(jnp.float32)
    top_val, top_idx = lax.top_k(logits, TOP_K)
    probs = jax.nn.softmax(top_val, axis=-1)
    combine = jnp.sum(jax.nn.one_hot(top_idx, N_EXPERTS, dtype=jnp.float32) * probs[..., None], axis=1)
    out = jnp.zeros_like(xf)
    for e in range(N_EXPERTS):
        out = out + combine[:, e:e + 1].astype(xf.dtype) * swiglu(xf, w_gate[e], w_up[e], w_down[e])
    return out.reshape(b, l, d)


def setup_inputs(seed: int = 0) -> dict:
    key = jax.random.key(seed)
    ks = jax.random.split(key, 22)
    f32 = jnp.float32

    def nrm(k, shape, fan_in):
        return jax.random.normal(k, shape, f32) * fan_in ** -0.5

    return {
        "x": jax.random.normal(ks[0], (BATCH, SEQ, D_MODEL), f32),
        "c": jax.random.normal(ks[1], (BATCH, D_MODEL), f32),
        "positions": jnp.broadcast_to(jnp.arange(SEQ, dtype=jnp.int32), (BATCH, SEQ)),
        "w_ada": nrm(ks[2], (DEPTH, D_MODEL, 6 * D_MODEL), D_MODEL),
        "b_ada": 0.02 * jax.random.normal(ks[3], (DEPTH, 6 * D_MODEL), f32),
        "norm_gains": 1.0 + 0.05 * jax.random.normal(ks[4], (DEPTH, 4, D_MODEL), f32),
        "w_in": nrm(ks[5], (DEPTH, D_MODEL, IN_WIDTH), D_MODEL),
        "g_kv_latent": 1.0 + 0.05 * jax.random.normal(ks[6], (DEPTH, KV_LATENT), f32),
        "w_uk": nrm(ks[7], (DEPTH, KV_LATENT, NOPE_DIM), KV_LATENT),
        "w_uv": nrm(ks[8], (DEPTH, KV_LATENT, HEAD_DIM), KV_LATENT),
        "w_pool": nrm(ks[9], (DEPTH, N_POOL_GROUPS, POOL_GROUP_DIM, POOL_GROUP_DIM), POOL_GROUP_DIM),
        "pool_scale": 1.0 + 0.1 * jax.random.normal(ks[10], (DEPTH, POOL_WIDTH), f32),
        "w_br_a": nrm(ks[11], (DEPTH, A_WIDTH, D_MODEL), A_WIDTH),
        "w_br_b": nrm(ks[12], (DEPTH, POOL_WIDTH, D_MODEL), POOL_WIDTH),
        "w_br_c": nrm(ks[13], (DEPTH, C_WIDTH, D_MODEL), C_WIDTH),
        "w_out": nrm(ks[14], (DEPTH, D_MODEL, D_MODEL), D_MODEL),
        "w_gate_dense": nrm(ks[15], (N_DENSE, D_MODEL, D_FF), D_MODEL),
        "w_up_dense": nrm(ks[16], (N_DENSE, D_MODEL, D_FF), D_MODEL),
        "w_down_dense": nrm(ks[17], (N_DENSE, D_FF, D_MODEL), D_FF),
        "w_router": nrm(ks[18], (N_MOE, D_MODEL, N_EXPERTS), D_MODEL),
        "w_gate_moe": nrm(ks[19], (N_MOE, N_EXPERTS, D_MODEL, D_FF_EXPERT), D_MODEL),
        "w_up_moe": nrm(ks[20], (N_MOE, N_EXPERTS, D_MODEL, D_FF_EXPERT), D_MODEL),
        "w_down_moe": nrm(ks[21], (N_MOE, N_EXPERTS, D_FF_EXPERT, D_MODEL), D_FF_EXPERT),
    }


def reference(x, c, positions, w_ada, b_ada, norm_gains, w_in, g_kv_latent, w_uk, w_uv,
              w_pool, pool_scale, w_br_a, w_br_b, w_br_c, w_out,
              w_gate_dense, w_up_dense, w_down_dense,
              w_router, w_gate_moe, w_up_moe, w_down_moe):
    rope_a = rope_tables(positions, ROPE_DIM)
    rope_i = rope_tables(positions, IDX_ROPE_DIM)
    cond = jax.nn.silu(c)
    for layer in range(DEPTH):
        mod = cond @ w_ada[layer] + b_ada[layer]
        sh1, sc1, gt1, sh2, sc2, gt2 = [m[:, None, :] for m in jnp.split(mod, 6, axis=-1)]
        h = rmsnorm(x, norm_gains[layer, 0]) * (1.0 + sc1) + sh1
        y = hybrid_mixer(h, rope_a, rope_i, w_in[layer], g_kv_latent[layer], w_uk[layer], w_uv[layer],
                         w_pool[layer], pool_scale[layer], w_br_a[layer], w_br_b[layer], w_br_c[layer],
                         w_out[layer])
        x = x + gt1 * rmsnorm(y, norm_gains[layer, 1])
        h = rmsnorm(x, norm_gains[layer, 2]) * (1.0 + sc2) + sh2
        i = layer // 2
        if layer % 2 == 0:
            y = swiglu(h, w_gate_dense[i], w_up_dense[i], w_down_dense[i])
        else:
            y = moe_swiglu(h, w_router[i], w_gate_moe[i], w_up_moe[i], w_down_moe[i])
        x = x + gt2 * rmsnorm(y, norm_gains[layer, 3])
    return x
```

```python
import functools

import jax
import jax.numpy as jnp
from jax import lax
from jax.experimental import pallas as pl
from jax.experimental.pallas import tpu as pltpu

F32 = jnp.float32
BF16 = jnp.bfloat16
I32 = jnp.int32

D_MODEL = 1024
HEAD_DIM = 64
ROPE_DIM = HEAD_DIM // 4
NOPE_DIM = HEAD_DIM - ROPE_DIM
ROPE_THETA = 500000.0
N_HEADS_A = (3 * D_MODEL // 8) // HEAD_DIM
A_WIDTH = N_HEADS_A * HEAD_DIM
KV_LATENT = D_MODEL // 8
N_IDX_HEADS = 4
IDX_DIM = 64
TOPK_MAX = 256
N_POOL_GROUPS = 4
POOL_WINDOWS = (2, 4, 8, 16)
POOL_WIDTH = D_MODEL // 4
POOL_GROUP_DIM = POOL_WIDTH // N_POOL_GROUPS
N_HEADS_C = (D_MODEL // 4) // HEAD_DIM
C_WIDTH = N_HEADS_C * HEAD_DIM
N_BRANCHES = 3
IN_SIZES = (A_WIDTH, KV_LATENT, ROPE_DIM, N_IDX_HEADS * IDX_DIM, IDX_DIM, N_IDX_HEADS,
            POOL_WIDTH, C_WIDTH, C_WIDTH, C_WIDTH, N_BRANCHES * D_MODEL)
D_FF = 2816
N_EXPERTS = 8
TOP_K = 2
D_FF_EXPERT = D_FF // TOP_K
RMS_EPS = 1e-6

LANES = 128
QB = 128
KC = 512
TM_IN = 256
TM_MERGE = 256
TM_FFN = 512
POOL_HALO = 16
VMEM_LIMIT = 56 * 1024 * 1024
INT_MIN = -2147483648
NEG_BIG = -1e30
NEG_MASK = -2e30

R_LAT, R_KR, R_KRR, R_KI, R_KIR, R_UP, R_KC, R_GATE = 0, 128, 256, 384, 512, 640, 896, 1152
R_WIDTH = R_GATE + N_BRANCHES * D_MODEL
T_QA, T_QAR, T_QI, T_QIR, T_QC, T_VC, T_WI = 0, 384, 768, 1024, 1280, 1792, 2048
T_WI_ROWS = 16
T_HEIGHT = T_WI + T_WI_ROWS


def _params(sem):
    return pltpu.CompilerParams(dimension_semantics=sem, vmem_limit_bytes=VMEM_LIMIT)


def _sigmoid(v):
    return 1.0 / (1.0 + jnp.exp(-v))


def _rms(v, gain):
    return v * lax.rsqrt(jnp.mean(v * v, axis=-1, keepdims=True) + RMS_EPS) * gain


def _mod_kernel(c_ref, w_ref, b_ref, o_ref):
    c = c_ref[...]
    cond = c * _sigmoid(c)
    o_ref[0] = jnp.dot(cond, w_ref[0], preferred_element_type=F32) + b_ref[0]


def _mod_call(c_pad, w_ada, b_ada):
    depth, d, n = w_ada.shape
    tn = 1024
    return pl.pallas_call(
        _mod_kernel,
        grid=(depth, n // tn),
        in_specs=[
            pl.BlockSpec((8, d), lambda l, j: (0, 0)),
            pl.BlockSpec((1, d, tn), lambda l, j: (l, 0, j)),
            pl.BlockSpec((1, 1, tn), lambda l, j: (l, 0, j)),
        ],
        out_specs=pl.BlockSpec((1, 8, tn), lambda l, j: (l, 0, j)),
        out_shape=jax.ShapeDtypeStruct((depth, 8, n), F32),
        compiler_params=_params(("arbitrary", "arbitrary")),
        name="adaln_mod",
    )(c_pad, w_ada, b_ada.reshape(depth, 1, n))


def _inproj_kernel(x_ref, sh_ref, sc_ref, g_ref, wrow_ref, wt_ref, c_ref, s_ref, ct_ref, st_ref,
                   glat_ref, wuk_ref, wuvt_ref, wpool_ref, pscale_ref, wbrb_ref,
                   qat_ref, qit_ref, wit_ref, ka_ref, vat_ref, ki_ref, qct_ref, kc_ref, vct_ref,
                   ga_ref, gc_ref, mb_ref,
                   h_scr, ht_scr, ext_scr, prev_scr):
    i = pl.program_id(1)
    tm = x_ref.shape[1]
    x = x_ref[0]
    h = _rms(x, g_ref[...]) * (1.0 + sc_ref[0]) + sh_ref[0]
    h_scr[...] = h.astype(BF16)
    ht_scr[...] = h.T.astype(BF16)

    def rowdot(a, width):
        return jnp.dot(h_scr[...], wrow_ref[:, a:a + width], preferred_element_type=F32)

    def tdot(a, height):
        return jnp.dot(wt_ref[a:a + height, :], ht_scr[...], preferred_element_type=F32)

    cos_r, sin_r = c_ref[0], s_ref[0]
    cos_t, sin_t = ct_ref[0], st_ref[0]

    latn = _rms(rowdot(R_LAT, KV_LATENT), glat_ref[...])
    ka = (rowdot(R_KR, LANES) * cos_r + rowdot(R_KRR, LANES) * sin_r
          + jnp.dot(latn.astype(BF16), wuk_ref[...], preferred_element_type=F32))
    ka_ref[0] = ka[:, :HEAD_DIM].astype(BF16)
    vat = jnp.dot(wuvt_ref[...], latn.T.astype(BF16), preferred_element_type=F32)
    for j in range(tm // QB):
        vat_ref[0, j] = vat[:, j * QB:(j + 1) * QB].astype(BF16)

    ki = rowdot(R_KI, LANES) * cos_r + rowdot(R_KIR, LANES) * sin_r
    ki_ref[0] = ki[:, :IDX_DIM].astype(BF16)

    qa, qar = tdot(T_QA, A_WIDTH), tdot(T_QAR, A_WIDTH)
    for hh in range(N_HEADS_A):
        r = slice(hh * HEAD_DIM, (hh + 1) * HEAD_DIM)
        qat_ref[0, r, :] = (qa[r] * cos_t + qar[r] * sin_t).astype(BF16)
    qi, qir = tdot(T_QI, N_IDX_HEADS * IDX_DIM), tdot(T_QIR, N_IDX_HEADS * IDX_DIM)
    for hh in range(N_IDX_HEADS):
        r = slice(hh * IDX_DIM, (hh + 1) * IDX_DIM)
        qit_ref[0, r, :] = (qi[r] * cos_t + qir[r] * sin_t).astype(BF16)
    wit_ref[0] = tdot(T_WI, T_WI_ROWS)

    qct_ref[0] = tdot(T_QC, 2 * C_WIDTH).astype(BF16)
    kc_ref[0] = rowdot(R_KC, C_WIDTH).astype(BF16)
    vct = tdot(T_VC, C_WIDTH)
    for j in range(tm // QB):
        vct_ref[0, j] = vct[:, j * QB:(j + 1) * QB].astype(BF16)

    up = rowdot(R_UP, POOL_WIDTH)

    @pl.when(i == 0)
    def _():
        prev_scr[...] = jnp.zeros_like(prev_scr)

    ext_scr[0:POOL_HALO, :] = prev_scr[...]
    ext_scr[POOL_HALO:POOL_HALO + tm, :] = up
    prev_scr[...] = up[tm - POOL_HALO:, :]
    lag = [ext_scr[POOL_HALO - j:POOL_HALO - j + tm, :] for j in range(POOL_HALO)]
    sums = {}
    run = lag[0]
    for j in range(1, POOL_HALO):
        run = run + lag[j]
        if j + 1 in POOL_WINDOWS:
            sums[j + 1] = run
    lane = lax.broadcasted_iota(I32, (tm, POOL_WIDTH), 1)
    pos = i * tm + lax.broadcasted_iota(I32, (tm, POOL_WIDTH), 0)
    pooled_sum = sums[POOL_WINDOWS[-1]]
    win = jnp.full((tm, POOL_WIDTH), POOL_WINDOWS[-1], I32)
    for g in range(N_POOL_GROUPS - 2, -1, -1):
        in_group = lane < (g + 1) * POOL_GROUP_DIM
        pooled_sum = jnp.where(in_group, sums[POOL_WINDOWS[g]], pooled_sum)
        win = jnp.where(in_group, POOL_WINDOWS[g], win)
    cnt = jnp.minimum(pos + 1, win).astype(F32)
    pooled = pooled_sum / cnt - up
    yb = jnp.dot(pooled.astype(BF16), wpool_ref[...], preferred_element_type=F32) * pscale_ref[...]

    ga_ref[0] = _sigmoid(rowdot(R_GATE, D_MODEL)).astype(BF16)
    gb = _sigmoid(rowdot(R_GATE + D_MODEL, D_MODEL))
    mb_ref[0] = (gb * jnp.dot(yb.astype(BF16), wbrb_ref[...], preferred_element_type=F32)).astype(BF16)
    gc_ref[0] = _sigmoid(rowdot(R_GATE + 2 * D_MODEL, D_MODEL)).astype(BF16)


def _inproj_call(x, sh, sc, gain, wrow, wt, cos_r, sin_r, cos_t, sin_t, glat, wuk, wuvt, wpool, pscale, wbrb):
    b, l, d = x.shape
    tm = min(TM_IN, l)
    nq = l // QB
    tok = lambda w: pl.BlockSpec((1, tm, w), lambda bi, i: (bi, i, 0))
    feat = lambda hgt: pl.BlockSpec((1, hgt, tm), lambda bi, i: (bi, 0, i))
    blk = lambda hgt: pl.BlockSpec((1, tm // QB, hgt, QB), lambda bi, i: (bi, i, 0, 0))
    full = lambda a: pl.BlockSpec(a.shape, lambda bi, i: (0,) * a.ndim)
    vec = pl.BlockSpec((1, 1, d), lambda bi, i: (bi, 0, 0))
    out_shape = (
        jax.ShapeDtypeStruct((b, A_WIDTH, l), BF16),
        jax.ShapeDtypeStruct((b, N_IDX_HEADS * IDX_DIM, l), BF16),
        jax.ShapeDtypeStruct((b, T_WI_ROWS, l), F32),
        jax.ShapeDtypeStruct((b, l, HEAD_DIM), BF16),
        jax.ShapeDtypeStruct((b, nq, HEAD_DIM, QB), BF16),
        jax.ShapeDtypeStruct((b, l, IDX_DIM), BF16),
        jax.ShapeDtypeStruct((b, 2 * C_WIDTH, l), BF16),
        jax.ShapeDtypeStruct((b, l, C_WIDTH), BF16),
        jax.ShapeDtypeStruct((b, nq, C_WIDTH, QB), BF16),
        jax.ShapeDtypeStruct((b, l, d), BF16),
        jax.ShapeDtypeStruct((b, l, d), BF16),
        jax.ShapeDtypeStruct((b, l, d), BF16),
    )
    out_specs = (feat(A_WIDTH), feat(N_IDX_HEADS * IDX_DIM), feat(T_WI_ROWS), tok(HEAD_DIM), blk(HEAD_DIM),
                 tok(IDX_DIM), feat(2 * C_WIDTH), tok(C_WIDTH), blk(C_WIDTH), tok(d), tok(d), tok(d))
    in_specs = [tok(d), vec, vec, full(gain), full(wrow), full(wt), tok(LANES), tok(LANES),
                feat(HEAD_DIM), feat(HEAD_DIM), full(glat), full(wuk), full(wuvt), full(wpool),
                full(pscale), full(wbrb)]
    return pl.pallas_call(
        _inproj_kernel,
        grid=(b, l // tm),
        in_specs=in_specs,
        out_specs=out_specs,
        out_shape=out_shape,
        scratch_shapes=[pltpu.VMEM((tm, d), BF16), pltpu.VMEM((d, tm), BF16),
                        pltpu.VMEM((tm + POOL_HALO, POOL_WIDTH), F32), pltpu.VMEM((POOL_HALO, POOL_WIDTH), F32)],
        compiler_params=_params(("arbitrary", "arbitrary")),
        name="inproj",
    )(x, sh, sc, gain, wrow, wt, cos_r, sin_r, cos_t, sin_t, glat, wuk, wuvt, wpool, pscale, wbrb)


def _dsa_kernel(qit_ref, wit_ref, qat_ref, ki_ref, ka_ref, vat_ref, tri_ref, ya_ref, keys_scr, acc_scr, *, k_sel):
    i = pl.program_id(1)
    n_chunks = i // (KC // QB) + 1
    qpos = i * QB + lax.broadcasted_iota(I32, (1, QB), 1)
    qit = qit_ref[0]
    w_idx = wit_ref[0]
    qat = qat_ref[0]

    def score_chunk(c, carry):
        r0 = pl.multiple_of(c * KC, KC)
        kblk = ki_ref[0, pl.ds(r0, KC), :]
        score = jnp.zeros((KC, QB), F32)
        for hh in range(N_IDX_HEADS):
            s = jnp.dot(kblk, qit[hh * IDX_DIM:(hh + 1) * IDX_DIM, :], preferred_element_type=F32)
            score = score + jnp.maximum(s, 0.0) * w_idx[hh:hh + 1, :]
        bits = lax.bitcast_convert_type(score, I32)
        key = jnp.where(bits < 0, INT_MIN - bits, bits)
        kpos = r0 + lax.broadcasted_iota(I32, (KC, QB), 0)
        keys_scr[pl.ds(r0, KC), :] = jnp.where(kpos <= qpos, key, INT_MIN)
        return carry

    lax.fori_loop(0, n_chunks, score_chunk, 0)

    def count_ge(trial):
        def body(c, acc):
            r0 = pl.multiple_of(c * KC, KC)
            hit = jnp.where(keys_scr[pl.ds(r0, KC), :] >= trial, 1, 0)
            return acc + jnp.sum(hit.reshape(KC // 8, 8, QB), axis=0)

        acc = lax.fori_loop(0, n_chunks, body, jnp.zeros((8, QB), I32))
        return jnp.sum(acc, axis=0, keepdims=True)

    def refine(it, ans):
        trial = ans + lax.shift_left(jnp.int32(1), 31 - it)
        return jnp.where(count_ge(trial) >= k_sel, trial, ans)

    thr = lax.fori_loop(0, 32, refine, jnp.full((1, QB), INT_MIN, I32))
    n_above = count_ge(thr + 1)
    n_ties = jnp.where(thr == INT_MIN, 0, k_sel - n_above).astype(F32)

    acc_scr[...] = jnp.zeros_like(acc_scr)

    def attend(kb, carry):
        seen, ms, ls = carry
        r0 = pl.multiple_of(kb * QB, QB)
        keyb = keys_scr[pl.ds(r0, QB), :]
        above = keyb > thr
        tied = keyb == thr
        tied_f = jnp.where(tied, 1.0, 0.0)
        rank = seen + jnp.dot(tri_ref[...], tied_f.astype(BF16), preferred_element_type=F32)
        keep = jnp.logical_or(above, jnp.logical_and(tied, rank <= n_ties))
        bias = jnp.where(keep, 0.0, NEG_MASK)
        seen = seen + jnp.sum(tied_f, axis=0, keepdims=True)
        kblk = ka_ref[0, pl.ds(r0, QB), :]
        vblk = vat_ref[0, kb]
        new_ms, new_ls = [], []
        for hh in range(N_HEADS_A):
            r = slice(hh * HEAD_DIM, (hh + 1) * HEAD_DIM)
            s = jnp.dot(kblk, qat[r, :], preferred_element_type=F32) + bias
            m_new = jnp.maximum(ms[hh], jnp.max(s, axis=0, keepdims=True))
            p = jnp.exp(s - m_new)
            alpha = jnp.exp(ms[hh] - m_new)
            new_ls.append(alpha * ls[hh] + jnp.sum(p, axis=0, keepdims=True))
            new_ms.append(m_new)
            acc_scr[r, :] = acc_scr[r, :] * alpha + jnp.dot(vblk, p.astype(BF16), preferred_element_type=F32)
        return seen, tuple(new_ms), tuple(new_ls)

    zero = jnp.zeros((1, QB), F32)
    init = (zero, tuple(jnp.full((1, QB), NEG_BIG, F32) for _ in range(N_HEADS_A)),
            tuple(zero for _ in range(N_HEADS_A)))
    _, _, ls = lax.fori_loop(0, i + 1, attend, init)
    for hh in range(N_HEADS_A):
        r = slice(hh * HEAD_DIM, (hh + 1) * HEAD_DIM)
        acc_scr[r, :] = acc_scr[r, :] / ls[hh]
    ya_ref[0] = acc_scr[...].T.astype(BF16)


def _dsa_call(qit, wit, qat, ki, ka, vat, tri):
    b, l, _ = ki.shape
    nq = l // QB
    k_sel = min(TOPK_MAX, l // 4)
    return pl.pallas_call(
        functools.partial(_dsa_kernel, k_sel=k_sel),
        grid=(b, nq),
        in_specs=[
            pl.BlockSpec((1, N_IDX_HEADS * IDX_DIM, QB), lambda bi, i: (bi, 0, i)),
            pl.BlockSpec((1, T_WI_ROWS, QB), lambda bi, i: (bi, 0, i)),
            pl.BlockSpec((1, A_WIDTH, QB), lambda bi, i: (bi, 0, i)),
            pl.BlockSpec((1, l, IDX_DIM), lambda bi, i: (bi, 0, 0)),
            pl.BlockSpec((1, l, HEAD_DIM), lambda bi, i: (bi, 0, 0)),
            pl.BlockSpec((1, nq, HEAD_DIM, QB), lambda bi, i: (bi, 0, 0, 0)),
            pl.BlockSpec((QB, QB), lambda bi, i: (0, 0)),
        ],
        out_specs=pl.BlockSpec((1, QB, A_WIDTH), lambda bi, i: (bi, i, 0)),
        out_shape=jax.ShapeDtypeStruct((b, l, A_WIDTH), BF16),
        scratch_shapes=[pltpu.VMEM((-(-l // KC) * KC, QB), I32), pltpu.VMEM((A_WIDTH, QB), F32)],
        compiler_params=_params(("arbitrary", "arbitrary")),
        name="dsa_attention",
    )(qit, wit, qat, ki, ka, vat, tri)


def _stick_kernel(qct_ref, kc_ref, vct_ref, later_ref, yc_ref, acc_scr):
    i = pl.program_id(1)
    qct = qct_ref[0]
    qpos = i * QB + lax.broadcasted_iota(I32, (1, QB), 1)
    acc_scr[...] = jnp.zeros_like(acc_scr)

    def block(kb, tail, masked):
        r0 = pl.multiple_of(kb * QB, QB)
        kfull = kc_ref[0, pl.ds(r0, QB), :]
        vt = vct_ref[0, kb]
        if masked:
            mask = (r0 + lax.broadcasted_iota(I32, (QB, QB), 0)) < qpos
        new_tail = []
        for hh in range(N_HEADS_C):
            kp = kfull[:, (hh // 2) * LANES:(hh // 2 + 1) * LANES]
            z = jnp.dot(kp, qct[hh * LANES:(hh + 1) * LANES, :], preferred_element_type=F32)
            log_beta = jnp.minimum(z, 0.0) - jnp.log(1.0 + jnp.exp(-jnp.abs(z)))
            log_keep = log_beta - z
            if masked:
                log_keep = jnp.where(mask, log_keep, 0.0)
            hi = log_keep.astype(BF16)
            lo = (log_keep - hi.astype(F32)).astype(BF16)
            within = (jnp.dot(later_ref[...], hi, preferred_element_type=F32)
                      + jnp.dot(later_ref[...], lo, preferred_element_type=F32))
            a = jnp.exp(log_beta + within + tail[hh])
            if masked:
                a = jnp.where(mask, a, 0.0)
            r = slice(hh * HEAD_DIM, (hh + 1) * HEAD_DIM)
            acc_scr[r, :] = acc_scr[r, :] + jnp.dot(vt[r, :], a.astype(BF16), preferred_element_type=F32)
            new_tail.append(tail[hh] + jnp.sum(log_keep, axis=0, keepdims=True))
        return tuple(new_tail)

    tail = block(i, tuple(jnp.zeros((1, QB), F32) for _ in range(N_HEADS_C)), True)
    lax.fori_loop(0, i, lambda t, tl: block(i - 1 - t, tl, False), tail)
    yc_ref[0] = acc_scr[...].T.astype(BF16)


def _stick_call(qct, kc, vct, later):
    b, l, _ = kc.shape
    nq = l // QB
    return pl.pallas_call(
        _stick_kernel,
        grid=(b, nq),
        in_specs=[
            pl.BlockSpec((1, 2 * C_WIDTH, QB), lambda bi, i: (bi, 0, i)),
            pl.BlockSpec((1, l, C_WIDTH), lambda bi, i: (bi, 0, 0)),
            pl.BlockSpec((1, nq, C_WIDTH, QB), lambda bi, i: (bi, 0, 0, 0)),
            pl.BlockSpec((QB, QB), lambda bi, i: (0, 0)),
        ],
        out_specs=pl.BlockSpec((1, QB, C_WIDTH), lambda bi, i: (bi, i, 0)),
        out_shape=jax.ShapeDtypeStruct((b, l, C_WIDTH), BF16),
        scratch_shapes=[pltpu.VMEM((C_WIDTH, QB), F32)],
        compiler_params=_params(("arbitrary", "arbitrary")),
        name="stick_attention",
    )(qct, kc, vct, later)


def _merge_kernel(x_ref, ya_ref, yc_ref, mb_ref, ga_ref, gc_ref, gt_ref, sh_ref, sc_ref, g1_ref, g2_ref,
                  wbra_ref, wbrc_ref, wout_ref, *rest, with_router):
    if with_router:
        wr_ref, x1_ref, h2_ref, comb_ref = rest
    else:
        x1_ref, h2_ref = rest
    merged = (ga_ref[0].astype(F32) * jnp.dot(ya_ref[0], wbra_ref[...], preferred_element_type=F32)
              + mb_ref[0].astype(F32)
              + gc_ref[0].astype(F32) * jnp.dot(yc_ref[0], wbrc_ref[...], preferred_element_type=F32))
    y = jnp.dot(merged.astype(BF16), wout_ref[...], preferred_element_type=F32)
    x1 = x_ref[0] + gt_ref[0] * _rms(y, g1_ref[...])
    x1_ref[0] = x1
    h2 = _rms(x1, g2_ref[...]) * (1.0 + sc_ref[0]) + sh_ref[0]
    h2_ref[0] = h2.astype(BF16)
    if with_router:
        logits = jnp.dot(h2, wr_ref[...], preferred_element_type=F32)
        lane = lax.broadcasted_iota(I32, logits.shape, 1)
        valid = lane < N_EXPERTS
        l1 = jnp.where(valid, logits, -jnp.inf)
        v1 = jnp.max(l1, axis=-1, keepdims=True)
        i1 = jnp.min(jnp.where(l1 == v1, lane, LANES), axis=-1, keepdims=True)
        l2 = jnp.where(lane == i1, -jnp.inf, l1)
        v2 = jnp.max(l2, axis=-1, keepdims=True)
        i2 = jnp.min(jnp.where(l2 == v2, lane, LANES), axis=-1, keepdims=True)
        e2 = jnp.exp(v2 - v1)
        p1 = 1.0 / (1.0 + e2)
        p2 = e2 / (1.0 + e2)
        comb_ref[0] = jnp.where(lane == i1, p1, 0.0) + jnp.where(lane == i2, p2, 0.0)


def _merge_call(x, ya, yc, mb, ga, gc, gt, sh, sc, g1, g2, wbra, wbrc, wout, wr=None):
    b, l, d = x.shape
    tm = min(TM_MERGE, l)
    tok = lambda w: pl.BlockSpec((1, tm, w), lambda bi, i: (bi, i, 0))
    full = lambda a: pl.BlockSpec(a.shape, lambda bi, i: (0,) * a.ndim)
    vec = pl.BlockSpec((1, 1, d), lambda bi, i: (bi, 0, 0))
    in_specs = [tok(d), tok(A_WIDTH), tok(C_WIDTH), tok(d), tok(d), tok(d), vec, vec, vec,
                full(g1), full(g2), full(wbra), full(wbrc), full(wout)]
    args = [x, ya, yc, mb, ga, gc, gt, sh, sc, g1, g2, wbra, wbrc, wout]
    out_shape = [jax.ShapeDtypeStruct((b, l, d), F32), jax.ShapeDtypeStruct((b, l, d), BF16)]
    out_specs = [tok(d), tok(d)]
    if wr is not None:
        in_specs.append(full(wr))
        args.append(wr)
        out_shape.append(jax.ShapeDtypeStruct((b, l, LANES), F32))
        out_specs.append(tok(LANES))
    return pl.pallas_call(
        functools.partial(_merge_kernel, with_router=wr is not None),
        grid=(b, l // tm),
        in_specs=in_specs,
        out_specs=out_specs,
        out_shape=out_shape,
        compiler_params=_params(("arbitrary", "arbitrary")),
        name="merge_router" if wr is not None else "merge",
    )(*args)


def _ffn_kernel(x1_ref, h2_ref, gt_ref, g3_ref, wg_ref, wu_ref, wd_ref, o_ref):
    h2 = h2_ref[0]
    gate = jnp.dot(h2, wg_ref[...], preferred_element_type=F32)
    up = jnp.dot(h2, wu_ref[...], preferred_element_type=F32)
    act = (gate * _sigmoid(gate) * up).astype(BF16)
    y = jnp.dot(act, wd_ref[...], preferred_element_type=F32)
    o_ref[0] = x1_ref[0] + gt_ref[0] * _rms(y, g3_ref[...])


def _ffn_call(x1, h2, gt, g3, wg, wu, wd):
    b, l, d = x1.shape
    tm = min(TM_FFN, l)
    tok = pl.BlockSpec((1, tm, d), lambda bi, i: (bi, i, 0))
    full = lambda a: pl.BlockSpec(a.shape, lambda bi, i: (0,) * a.ndim)
    vec = pl.BlockSpec((1, 1, d), lambda bi, i: (bi, 0, 0))
    return pl.pallas_call(
        _ffn_kernel,
        grid=(b, l // tm),
        in_specs=[tok, tok, vec, full(g3), full(wg), full(wu), full(wd)],
        out_specs=tok,
        out_shape=jax.ShapeDtypeStruct((b, l, d), F32),
        compiler_params=_params(("arbitrary", "arbitrary")),
        name="ffn_dense",
    )(x1, h2, gt, g3, wg, wu, wd)


def _moe_kernel(x1_ref, h2_ref, comb_ref, gt_ref, g3_ref, wg_ref, wu_ref, wd_ref, o_ref, acc_scr):
    e = pl.program_id(2)

    @pl.when(e == 0)
    def _():
        acc_scr[...] = jnp.zeros_like(acc_scr)

    h2 = h2_ref[0]
    gate = jnp.dot(h2, wg_ref[0], preferred_element_type=F32)
    up = jnp.dot(h2, wu_ref[0], preferred_element_type=F32)
    act = (gate * _sigmoid(gate) * up).astype(BF16)
    y = jnp.dot(act, wd_ref[0], preferred_element_type=F32)
    comb = comb_ref[0]
    lane = lax.broadcasted_iota(I32, comb.shape, 1)
    weight = jnp.sum(jnp.where(lane == e, comb, 0.0), axis=-1, keepdims=True)
    acc_scr[...] = acc_scr[...] + weight * y

    @pl.when(e == pl.num_programs(2) - 1)
    def _():
        o_ref[0] = x1_ref[0] + gt_ref[0] * _rms(acc_scr[...], g3_ref[...])


def _moe_call(x1, h2, comb, gt, g3, wg, wu, wd):
    b, l, d = x1.shape
    tm = min(TM_FFN, l)
    n_e, _, ff = wg.shape
    tok = lambda w: pl.BlockSpec((1, tm, w), lambda bi, i, e: (bi, i, 0))
    vec = pl.BlockSpec((1, 1, d), lambda bi, i, e: (bi, 0, 0))
    return pl.pallas_call(
        _moe_kernel,
        grid=(b, l // tm, n_e),
        in_specs=[tok(d), tok(d), tok(LANES), vec, pl.BlockSpec(g3.shape, lambda bi, i, e: (0, 0)),
                  pl.BlockSpec((1, d, ff), lambda bi, i, e: (e, 0, 0)),
                  pl.BlockSpec((1, d, ff), lambda bi, i, e: (e, 0, 0)),
                  pl.BlockSpec((1, ff, d), lambda bi, i, e: (e, 0, 0))],
        out_specs=tok(d),
        out_shape=jax.ShapeDtypeStruct((b, l, d), F32),
        scratch_shapes=[pltpu.VMEM((tm, d), F32)],
        compiler_params=_params(("arbitrary", "arbitrary", "arbitrary")),
        name="ffn_moe",
    )(x1, h2, comb, gt, g3, wg, wu, wd)


def _rot_cols(w, n_heads, head_dim):
    d = w.shape[0]
    w = w.reshape(d, n_heads, head_dim)
    half = ROPE_DIM // 2
    rot = jnp.concatenate([-w[..., half:ROPE_DIM], w[..., :half],
                           jnp.zeros((d, n_heads, head_dim - ROPE_DIM), w.dtype)], axis=-1)
    return rot.reshape(d, n_heads * head_dim)


def _pad_cols(w, width):
    return jnp.pad(w, ((0, 0), (0, width - w.shape[1])))


def _layer_weights(w_in, w_uk, w_uv, w_pool):
    d = w_in.shape[0]
    offs, o = [], 0
    for s in IN_SIZES:
        offs.append(o)
        o += s
    piece = lambda k: w_in[:, offs[k]:offs[k] + IN_SIZES[k]]
    w_qa, w_lat, w_kr, w_qi, w_ki, w_wi, w_up, w_qc, w_kc, w_vc, w_gate = [piece(k) for k in range(len(IN_SIZES))]
    attn_scale = HEAD_DIM ** -0.5
    idx_scale = IDX_DIM ** -0.5 * N_IDX_HEADS ** -0.5
    wrow = jnp.concatenate([
        w_lat,
        _pad_cols(w_kr, LANES), _pad_cols(_rot_cols(w_kr, 1, ROPE_DIM), LANES),
        _pad_cols(w_ki, LANES), _pad_cols(_rot_cols(w_ki, 1, IDX_DIM), LANES),
        w_up, w_kc, w_gate], axis=1).astype(BF16)
    w_qc_t = (w_qc * attn_scale).T.reshape(N_HEADS_C, HEAD_DIM, d)
    zeros = jnp.zeros_like(w_qc_t)
    even = (jnp.arange(N_HEADS_C) % 2 == 0)[:, None, None]
    w_qc_pad = jnp.concatenate([jnp.where(even, w_qc_t, zeros), jnp.where(even, zeros, w_qc_t)], axis=1)
    w_qc_pad = w_qc_pad.reshape(2 * C_WIDTH, d)
    wt = jnp.concatenate([
        (w_qa * attn_scale).T, (_rot_cols(w_qa, N_HEADS_A, HEAD_DIM) * attn_scale).T,
        w_qi.T, _rot_cols(w_qi, N_IDX_HEADS, IDX_DIM).T,
        w_qc_pad, w_vc.T,
        jnp.pad((w_wi * idx_scale).T, ((0, T_WI_ROWS - N_IDX_HEADS), (0, 0)))], axis=0).astype(BF16)
    wuk = jnp.zeros((KV_LATENT, LANES), F32).at[:, ROPE_DIM:HEAD_DIM].set(w_uk).astype(BF16)
    wuvt = w_uv.T.astype(BF16)
    wpool = jnp.zeros((POOL_WIDTH, POOL_WIDTH), F32)
    for g in range(N_POOL_GROUPS):
        sl = slice(g * POOL_GROUP_DIM, (g + 1) * POOL_GROUP_DIM)
        wpool = wpool.at[sl, sl].set(w_pool[g])
    return wrow, wt, wuk, wuvt, wpool.astype(BF16)


def _rope_tables(positions):
    inv = ROPE_THETA ** (-jnp.arange(0, ROPE_DIM, 2, dtype=F32) / ROPE_DIM)
    ang = positions.astype(F32)[..., None] * inv
    cos, sin = jnp.cos(ang), jnp.sin(ang)
    b, l = positions.shape
    ones = jnp.ones((b, l, HEAD_DIM - ROPE_DIM), F32)
    cos_h = jnp.concatenate([cos, cos, ones], axis=-1)
    sin_h = jnp.concatenate([sin, sin, jnp.zeros_like(ones)], axis=-1)
    cos_r = jnp.concatenate([cos_h, cos_h], axis=-1)
    sin_r = jnp.concatenate([sin_h, sin_h], axis=-1)
    return cos_r, sin_r, cos_h.transpose(0, 2, 1), sin_h.transpose(0, 2, 1)


def kernel(x, c, positions, w_ada, b_ada, norm_gains, w_in, g_kv_latent, w_uk, w_uv, w_pool, pool_scale,
           w_br_a, w_br_b, w_br_c, w_out, w_gate_dense, w_up_dense, w_down_dense,
           w_router, w_gate_moe, w_up_moe, w_down_moe):
    b, l, d = x.shape
    depth = w_in.shape[0]
    assert d == D_MODEL and l % KC == 0 and l % TM_FFN == 0
    cos_r, sin_r, cos_t, sin_t = _rope_tables(positions)
    c_pad = jnp.pad(c, ((0, 8 - b), (0, 0)))
    mod = _mod_call(c_pad, w_ada, b_ada)[:, :b]
    idx = lax.broadcasted_iota(I32, (QB, QB), 0)
    jdx = lax.broadcasted_iota(I32, (QB, QB), 1)
    tri = (jdx <= idx).astype(BF16)
    later = (jdx > idx).astype(BF16)
    for layer in range(depth):
        sh1, sc1, gt1, sh2, sc2, gt2 = [m.reshape(b, 1, d) for m in jnp.split(mod[layer], 6, axis=-1)]
        gains = norm_gains[layer].reshape(4, 1, d)
        wrow, wt, wuk, wuvt, wpool = _layer_weights(w_in[layer], w_uk[layer], w_uv[layer], w_pool[layer])
        (qat, qit, wit, ka, vat, ki, qct, kc, vct, ga, gc, mb) = _inproj_call(
            x, sh1, sc1, gains[0], wrow, wt, cos_r, sin_r, cos_t, sin_t,
            g_kv_latent[layer].reshape(1, KV_LATENT), wuk, wuvt, wpool,
            pool_scale[layer].reshape(1, POOL_WIDTH), w_br_b[layer].astype(BF16))
        ya = _dsa_call(qit, wit, qat, ki, ka, vat, tri)
        yc = _stick_call(qct, kc, vct, later)
        i = layer // 2
        merge_args = (x, ya, yc, mb, ga, gc, gt1, sh2, sc2, gains[1], gains[2],
                      w_br_a[layer].astype(BF16), w_br_c[layer].astype(BF16), w_out[layer].astype(BF16))
        if layer % 2 == 0:
            x1, h2 = _merge_call(*merge_args)
            x = _ffn_call(x1, h2, gt2, gains[3], w_gate_dense[i].astype(BF16), w_up_dense[i].astype(BF16),
                          w_down_dense[i].astype(BF16))
        else:
            x1, h2, comb = _merge_call(*merge_args, wr=_pad_cols(w_router[i], LANES))
            x = _moe_call(x1, h2, comb, gt2, gains[3], w_gate_moe[i].astype(BF16), w_up_moe[i].astype(BF16),
                          w_down_moe[i].astype(BF16))
    return x
```

```python
import functools

import jax
import jax.numpy as jnp
from jax import lax
from jax.experimental import pallas as pl
from jax.experimental.pallas import tpu as pltpu

F32 = jnp.float32
BF16 = jnp.bfloat16
I32 = jnp.int32

D_MODEL = 1024
HEAD_DIM = 64
ROPE_DIM = HEAD_DIM // 4
NOPE_DIM = HEAD_DIM - ROPE_DIM
ROPE_THETA = 500000.0
N_HEADS_A = (3 * D_MODEL // 8) // HEAD_DIM
A_WIDTH = N_HEADS_A * HEAD_DIM
KV_LATENT = D_MODEL // 8
N_IDX_HEADS = 4
IDX_DIM = 64
TOPK_MAX = 256
N_POOL_GROUPS = 4
POOL_WINDOWS = (2, 4, 8, 16)
POOL_WIDTH = D_MODEL // 4
POOL_GROUP_DIM = POOL_WIDTH // N_POOL_GROUPS
N_HEADS_C = (D_MODEL // 4) // HEAD_DIM
C_WIDTH = N_HEADS_C * HEAD_DIM
N_BRANCHES = 3
IN_SIZES = (A_WIDTH, KV_LATENT, ROPE_DIM, N_IDX_HEADS * IDX_DIM, IDX_DIM, N_IDX_HEADS,
            POOL_WIDTH, C_WIDTH, C_WIDTH, C_WIDTH, N_BRANCHES * D_MODEL)
D_FF = 2816
N_EXPERTS = 8
TOP_K = 2
D_FF_EXPERT = D_FF // TOP_K
RMS_EPS = 1e-6

LANES = 128
QB = 128
KC = 512
QS_STICK = 512
TM_IN = 256
TM_MERGE = 256
TM_FFN = 512
POOL_HALO = 16
VMEM_LIMIT = 56 * 1024 * 1024
INT_MIN = -2147483648
NEG_BIG = -1e30
NEG_MASK = -2e30

R_LAT, R_KR, R_KRR, R_KI, R_KIR, R_UP, R_KC, R_GATE = 0, 128, 256, 384, 512, 640, 896, 1152
R_WIDTH = R_GATE + N_BRANCHES * D_MODEL
T_QA, T_QAR, T_QI, T_QIR, T_QC, T_VC, T_WI = 0, 384, 768, 1024, 1280, 1792, 2048
T_WI_ROWS = 16
T_HEIGHT = T_WI + T_WI_ROWS


def _params(sem):
    return pltpu.CompilerParams(dimension_semantics=sem, vmem_limit_bytes=VMEM_LIMIT)


def _sigmoid(v):
    return 1.0 / (1.0 + jnp.exp(-v))


def _rms(v, gain):
    return v * lax.rsqrt(jnp.mean(v * v, axis=-1, keepdims=True) + RMS_EPS) * gain


def _mod_kernel(c_ref, w_ref, b_ref, o_ref):
    c = c_ref[...]
    cond = c * _sigmoid(c)
    o_ref[0] = jnp.dot(cond, w_ref[0], preferred_element_type=F32) + b_ref[0]


def _mod_call(c_pad, w_ada, b_ada):
    depth, d, n = w_ada.shape
    tn = 1024
    return pl.pallas_call(
        _mod_kernel,
        grid=(depth, n // tn),
        in_specs=[
            pl.BlockSpec((8, d), lambda l, j: (0, 0)),
            pl.BlockSpec((1, d, tn), lambda l, j: (l, 0, j)),
            pl.BlockSpec((1, 1, tn), lambda l, j: (l, 0, j)),
        ],
        out_specs=pl.BlockSpec((1, 8, tn), lambda l, j: (l, 0, j)),
        out_shape=jax.ShapeDtypeStruct((depth, 8, n), F32),
        compiler_params=_params(("arbitrary", "arbitrary")),
        name="adaln_mod",
    )(c_pad, w_ada, b_ada.reshape(depth, 1, n))


def _inproj_kernel(x_ref, sh_ref, sc_ref, g_ref, wrow_ref, wt_ref, c_ref, s_ref, ct_ref, st_ref,
                   glat_ref, wuk_ref, wuvt_ref, wpool_ref, pscale_ref, wbrb_ref,
                   qat_ref, qit_ref, wit_ref, ka_ref, vat_ref, ki_ref, qct_ref, kc_ref, vct_ref,
                   ga_ref, gc_ref, mb_ref,
                   h_scr, ht_scr, ext_scr, prev_scr):
    i = pl.program_id(1)
    tm = x_ref.shape[1]
    x = x_ref[0]
    h = _rms(x, g_ref[...]) * (1.0 + sc_ref[0]) + sh_ref[0]
    h_scr[...] = h.astype(BF16)
    ht_scr[...] = h.T.astype(BF16)

    def rowdot(a, width):
        return jnp.dot(h_scr[...], wrow_ref[:, a:a + width], preferred_element_type=F32)

    def tdot(a, height):
        return jnp.dot(wt_ref[a:a + height, :], ht_scr[...], preferred_element_type=F32)

    cos_r, sin_r = c_ref[0], s_ref[0]
    cos_t, sin_t = ct_ref[0], st_ref[0]

    latn = _rms(rowdot(R_LAT, KV_LATENT), glat_ref[...])
    ka = (rowdot(R_KR, LANES) * cos_r + rowdot(R_KRR, LANES) * sin_r
          + jnp.dot(latn.astype(BF16), wuk_ref[...], preferred_element_type=F32))
    ka_ref[0] = ka[:, :HEAD_DIM].astype(BF16)
    vat = jnp.dot(wuvt_ref[...], latn.T.astype(BF16), preferred_element_type=F32)
    for j in range(tm // QB):
        vat_ref[0, j] = vat[:, j * QB:(j + 1) * QB].astype(BF16)

    ki = rowdot(R_KI, LANES) * cos_r + rowdot(R_KIR, LANES) * sin_r
    ki_ref[0] = ki[:, :IDX_DIM].astype(BF16)

    qa, qar = tdot(T_QA, A_WIDTH), tdot(T_QAR, A_WIDTH)
    for hh in range(N_HEADS_A):
        r = slice(hh * HEAD_DIM, (hh + 1) * HEAD_DIM)
        qat_ref[0, r, :] = (qa[r] * cos_t + qar[r] * sin_t).astype(BF16)
    qi, qir = tdot(T_QI, N_IDX_HEADS * IDX_DIM), tdot(T_QIR, N_IDX_HEADS * IDX_DIM)
    for hh in range(N_IDX_HEADS):
        r = slice(hh * IDX_DIM, (hh + 1) * IDX_DIM)
        qit_ref[0, r, :] = (qi[r] * cos_t + qir[r] * sin_t).astype(BF16)
    wit_ref[0] = tdot(T_WI, T_WI_ROWS)

    qct_ref[0] = tdot(T_QC, 2 * C_WIDTH).astype(BF16)
    kc_ref[0] = rowdot(R_KC, C_WIDTH).astype(BF16)
    vct = tdot(T_VC, C_WIDTH)
    for j in range(tm // QB):
        vct_ref[0, j] = vct[:, j * QB:(j + 1) * QB].astype(BF16)

    up = rowdot(R_UP, POOL_WIDTH)

    @pl.when(i == 0)
    def _():
        prev_scr[...] = jnp.zeros_like(prev_scr)

    ext_scr[0:POOL_HALO, :] = prev_scr[...]
    ext_scr[POOL_HALO:POOL_HALO + tm, :] = up
    prev_scr[...] = up[tm - POOL_HALO:, :]
    lag = [ext_scr[POOL_HALO - j:POOL_HALO - j + tm, :] for j in range(POOL_HALO)]
    sums = {}
    run = lag[0]
    for j in range(1, POOL_HALO):
        run = run + lag[j]
        if j + 1 in POOL_WINDOWS:
            sums[j + 1] = run
    lane = lax.broadcasted_iota(I32, (tm, POOL_WIDTH), 1)
    pos = i * tm + lax.broadcasted_iota(I32, (tm, POOL_WIDTH), 0)
    pooled_sum = sums[POOL_WINDOWS[-1]]
    win = jnp.full((tm, POOL_WIDTH), POOL_WINDOWS[-1], I32)
    for g in range(N_POOL_GROUPS - 2, -1, -1):
        in_group = lane < (g + 1) * POOL_GROUP_DIM
        pooled_sum = jnp.where(in_group, sums[POOL_WINDOWS[g]], pooled_sum)
        win = jnp.where(in_group, POOL_WINDOWS[g], win)
    cnt = jnp.minimum(pos + 1, win).astype(F32)
    pooled = pooled_sum / cnt - up
    yb = jnp.dot(pooled.astype(BF16), wpool_ref[...], preferred_element_type=F32) * pscale_ref[...]

    ga_ref[0] = _sigmoid(rowdot(R_GATE, D_MODEL)).astype(BF16)
    gb = _sigmoid(rowdot(R_GATE + D_MODEL, D_MODEL))
    mb_ref[0] = (gb * jnp.dot(yb.astype(BF16), wbrb_ref[...], preferred_element_type=F32)).astype(BF16)
    gc_ref[0] = _sigmoid(rowdot(R_GATE + 2 * D_MODEL, D_MODEL)).astype(BF16)


def _inproj_call(x, sh, sc, gain, wrow, wt, cos_r, sin_r, cos_t, sin_t, glat, wuk, wuvt, wpool, pscale, wbrb):
    b, l, d = x.shape
    tm = min(TM_IN, l)
    nq = l // QB
    tok = lambda w: pl.BlockSpec((1, tm, w), lambda bi, i: (bi, i, 0))
    feat = lambda hgt: pl.BlockSpec((1, hgt, tm), lambda bi, i: (bi, 0, i))
    blk = lambda hgt: pl.BlockSpec((1, tm // QB, hgt, QB), lambda bi, i: (bi, i, 0, 0))
    full = lambda a: pl.BlockSpec(a.shape, lambda bi, i: (0,) * a.ndim)
    vec = pl.BlockSpec((1, 1, d), lambda bi, i: (bi, 0, 0))
    out_shape = (
        jax.ShapeDtypeStruct((b, A_WIDTH, l), BF16),
        jax.ShapeDtypeStruct((b, N_IDX_HEADS * IDX_DIM, l), BF16),
        jax.ShapeDtypeStruct((b, T_WI_ROWS, l), F32),
        jax.ShapeDtypeStruct((b, l, HEAD_DIM), BF16),
        jax.ShapeDtypeStruct((b, nq, HEAD_DIM, QB), BF16),
        jax.ShapeDtypeStruct((b, l, IDX_DIM), BF16),
        jax.ShapeDtypeStruct((b, 2 * C_WIDTH, l), BF16),
        jax.ShapeDtypeStruct((b, l, C_WIDTH), BF16),
        jax.ShapeDtypeStruct((b, nq, C_WIDTH, QB), BF16),
        jax.ShapeDtypeStruct((b, l, d), BF16),
        jax.ShapeDtypeStruct((b, l, d), BF16),
        jax.ShapeDtypeStruct((b, l, d), BF16),
    )
    out_specs = (feat(A_WIDTH), feat(N_IDX_HEADS * IDX_DIM), feat(T_WI_ROWS), tok(HEAD_DIM), blk(HEAD_DIM),
                 tok(IDX_DIM), feat(2 * C_WIDTH), tok(C_WIDTH), blk(C_WIDTH), tok(d), tok(d), tok(d))
    in_specs = [tok(d), vec, vec, full(gain), full(wrow), full(wt), tok(LANES), tok(LANES),
                feat(HEAD_DIM), feat(HEAD_DIM), full(glat), full(wuk), full(wuvt), full(wpool),
                full(pscale), full(wbrb)]
    return pl.pallas_call(
        _inproj_kernel,
        grid=(b, l // tm),
        in_specs=in_specs,
        out_specs=out_specs,
        out_shape=out_shape,
        scratch_shapes=[pltpu.VMEM((tm, d), BF16), pltpu.VMEM((d, tm), BF16),
                        pltpu.VMEM((tm + POOL_HALO, POOL_WIDTH), F32), pltpu.VMEM((POOL_HALO, POOL_WIDTH), F32)],
        compiler_params=_params(("arbitrary", "arbitrary")),
        name="inproj",
    )(x, sh, sc, gain, wrow, wt, cos_r, sin_r, cos_t, sin_t, glat, wuk, wuvt, wpool, pscale, wbrb)


def _dsa_kernel(qit_ref, wit_ref, qat_ref, ki_ref, ka_ref, vat_ref, tri_ref, ya_ref, keys_scr, acc_scr, *, k_sel):
    i = pl.program_id(1)
    n_chunks = i // (KC // QB) + 1
    qpos = i * QB + lax.broadcasted_iota(I32, (1, QB), 1)
    qit = qit_ref[0]
    w_idx = wit_ref[0]
    qat = qat_ref[0]

    def score_chunk(c, carry):
        r0 = pl.multiple_of(c * KC, KC)
        kblk = ki_ref[0, pl.ds(r0, KC), :]
        score = jnp.zeros((KC, QB), F32)
        for hh in range(N_IDX_HEADS):
            s = jnp.dot(kblk, qit[hh * IDX_DIM:(hh + 1) * IDX_DIM, :], preferred_element_type=F32)
            score = score + jnp.maximum(s, 0.0) * w_idx[hh:hh + 1, :]
        bits = lax.bitcast_convert_type(score, I32)
        key = jnp.where(bits < 0, INT_MIN - bits, bits)
        kpos = r0 + lax.broadcasted_iota(I32, (KC, QB), 0)
        keys_scr[pl.ds(r0, KC), :] = jnp.where(kpos <= qpos, key, INT_MIN)
        return carry

    lax.fori_loop(0, n_chunks, score_chunk, 0)

    def count_ge(trial):
        def body(c, acc):
            r0 = pl.multiple_of(c * KC, KC)
            hit = jnp.where(keys_scr[pl.ds(r0, KC), :] >= trial, 1, 0)
            return acc + jnp.sum(hit.reshape(KC // 8, 8, QB), axis=0)

        acc = lax.fori_loop(0, n_chunks, body, jnp.zeros((8, QB), I32))
        return jnp.sum(acc, axis=0, keepdims=True)

    def refine(it, ans):
        trial = ans + lax.shift_left(jnp.int32(1), 31 - it)
        return jnp.where(count_ge(trial) >= k_sel, trial, ans)

    thr = lax.fori_loop(0, 32, refine, jnp.full((1, QB), INT_MIN, I32))
    n_above = count_ge(thr + 1)
    n_ties = jnp.where(thr == INT_MIN, 0, k_sel - n_above).astype(F32)

    acc_scr[...] = jnp.zeros_like(acc_scr)

    def attend(kb, carry):
        seen, ms, ls = carry
        r0 = pl.multiple_of(kb * QB, QB)
        keyb = keys_scr[pl.ds(r0, QB), :]
        above = keyb > thr
        tied = keyb == thr
        tied_f = jnp.where(tied, 1.0, 0.0)
        rank = seen + jnp.dot(tri_ref[...], tied_f.astype(BF16), preferred_element_type=F32)
        keep = jnp.logical_or(above, jnp.logical_and(tied, rank <= n_ties))
        bias = jnp.where(keep, 0.0, NEG_MASK)
        seen = seen + jnp.sum(tied_f, axis=0, keepdims=True)
        kblk = ka_ref[0, pl.ds(r0, QB), :]
        vblk = vat_ref[0, kb]
        new_ms, new_ls = [], []
        for hh in range(N_HEADS_A):
            r = slice(hh * HEAD_DIM, (hh + 1) * HEAD_DIM)
            s = jnp.dot(kblk, qat[r, :], preferred_element_type=F32) + bias
            m_new = jnp.maximum(ms[hh], jnp.max(s, axis=0, keepdims=True))
            p = jnp.exp(s - m_new)
            alpha = jnp.exp(ms[hh] - m_new)
            new_ls.append(alpha * ls[hh] + jnp.sum(p, axis=0, keepdims=True))
            new_ms.append(m_new)
            acc_scr[r, :] = acc_scr[r, :] * alpha + jnp.dot(vblk, p.astype(BF16), preferred_element_type=F32)
        return seen, tuple(new_ms), tuple(new_ls)

    zero = jnp.zeros((1, QB), F32)
    init = (zero, tuple(jnp.full((1, QB), NEG_BIG, F32) for _ in range(N_HEADS_A)),
            tuple(zero for _ in range(N_HEADS_A)))
    _, _, ls = lax.fori_loop(0, i + 1, attend, init)
    for hh in range(N_HEADS_A):
        r = slice(hh * HEAD_DIM, (hh + 1) * HEAD_DIM)
        acc_scr[r, :] = acc_scr[r, :] / ls[hh]
    ya_ref[0] = acc_scr[...].T.astype(BF16)


def _dsa_call(qit, wit, qat, ki, ka, vat, tri):
    b, l, _ = ki.shape
    nq = l // QB
    k_sel = min(TOPK_MAX, l // 4)
    return pl.pallas_call(
        functools.partial(_dsa_kernel, k_sel=k_sel),
        grid=(b, nq),
        in_specs=[
            pl.BlockSpec((1, N_IDX_HEADS * IDX_DIM, QB), lambda bi, i: (bi, 0, i)),
            pl.BlockSpec((1, T_WI_ROWS, QB), lambda bi, i: (bi, 0, i)),
            pl.BlockSpec((1, A_WIDTH, QB), lambda bi, i: (bi, 0, i)),
            pl.BlockSpec((1, l, IDX_DIM), lambda bi, i: (bi, 0, 0)),
            pl.BlockSpec((1, l, HEAD_DIM), lambda bi, i: (bi, 0, 0)),
            pl.BlockSpec((1, nq, HEAD_DIM, QB), lambda bi, i: (bi, 0, 0, 0)),
            pl.BlockSpec((QB, QB), lambda bi, i: (0, 0)),
        ],
        out_specs=pl.BlockSpec((1, QB, A_WIDTH), lambda bi, i: (bi, i, 0)),
        out_shape=jax.ShapeDtypeStruct((b, l, A_WIDTH), BF16),
        scratch_shapes=[pltpu.VMEM((-(-l // KC) * KC, QB), I32), pltpu.VMEM((A_WIDTH, QB), F32)],
        compiler_params=_params(("arbitrary", "arbitrary")),
        name="dsa_attention",
    )(qit, wit, qat, ki, ka, vat, tri)


def _stick_kernel(qct_ref, kc_ref, vct_ref, later_ref, yc_ref, acc_scr):
    i = pl.program_id(1)
    qs = qct_ref.shape[2]
    diag_blocks = qs // QB
    qpos = i * qs + lax.broadcasted_iota(I32, (1, qs), 1)
    acc_scr[...] = jnp.zeros_like(acc_scr)

    def block(kb, tail, masked):
        r0 = pl.multiple_of(kb * QB, QB)
        kfull = kc_ref[0, pl.ds(r0, QB), :]
        vt = vct_ref[0, kb]
        heads = range(N_HEADS_C)
        if masked:
            mask = (r0 + lax.broadcasted_iota(I32, (QB, qs), 0)) < qpos
        zs = [jnp.dot(kfull[:, (hh // 2) * LANES:(hh // 2 + 1) * LANES], qct_ref[0, hh * LANES:(hh + 1) * LANES, :],
                      preferred_element_type=F32) for hh in heads]
        log_betas, splits, new_tail = [], [], []
        for hh in heads:
            z = zs[hh]
            log_beta = jnp.minimum(z, 0.0) - jnp.log(1.0 + jnp.exp(-jnp.abs(z)))
            log_keep = log_beta - z
            if masked:
                log_keep = jnp.where(mask, log_keep, 0.0)
            hi = log_keep.astype(BF16)
            lo = (log_keep - hi.astype(F32)).astype(BF16)
            log_betas.append(log_beta)
            splits.append(jnp.concatenate([hi, lo], axis=0))
            new_tail.append(tail[hh] + jnp.sum(log_keep, axis=0, keepdims=True))
        withins = [jnp.dot(later_ref[...], splits[hh], preferred_element_type=F32) for hh in heads]
        weights = []
        for hh in heads:
            a = jnp.exp(log_betas[hh] + withins[hh] + tail[hh])
            if masked:
                a = jnp.where(mask, a, 0.0)
            weights.append(a.astype(BF16))
        for hh in heads:
            r = slice(hh * HEAD_DIM, (hh + 1) * HEAD_DIM)
            acc_scr[r, :] = acc_scr[r, :] + jnp.dot(vt[r, :], weights[hh], preferred_element_type=F32)
        return tuple(new_tail)

    first = (i + 1) * diag_blocks - 1
    tail = tuple(jnp.zeros((1, qs), F32) for _ in range(N_HEADS_C))
    tail = lax.fori_loop(0, diag_blocks, lambda t, tl: block(first - t, tl, True), tail)
    lax.fori_loop(0, i * diag_blocks, lambda t, tl: block(i * diag_blocks - 1 - t, tl, False), tail)
    yc_ref[0] = acc_scr[...].T.astype(BF16)


def _stick_call(qct, kc, vct, later2):
    b, l, _ = kc.shape
    nq = l // QB
    qs = min(QS_STICK, l)
    return pl.pallas_call(
        _stick_kernel,
        grid=(b, l // qs),
        in_specs=[
            pl.BlockSpec((1, 2 * C_WIDTH, qs), lambda bi, i: (bi, 0, i)),
            pl.BlockSpec((1, l, C_WIDTH), lambda bi, i: (bi, 0, 0)),
            pl.BlockSpec((1, nq, C_WIDTH, QB), lambda bi, i: (bi, 0, 0, 0)),
            pl.BlockSpec((QB, 2 * QB), lambda bi, i: (0, 0)),
        ],
        out_specs=pl.BlockSpec((1, qs, C_WIDTH), lambda bi, i: (bi, i, 0)),
        out_shape=jax.ShapeDtypeStruct((b, l, C_WIDTH), BF16),
        scratch_shapes=[pltpu.VMEM((C_WIDTH, qs), F32)],
        compiler_params=_params(("arbitrary", "arbitrary")),
        name="stick_attention",
    )(qct, kc, vct, later2)


def _merge_kernel(x_ref, ya_ref, yc_ref, mb_ref, ga_ref, gc_ref, gt_ref, sh_ref, sc_ref, g1_ref, g2_ref,
                  wbra_ref, wbrc_ref, wout_ref, *rest, with_router):
    if with_router:
        wr_ref, x1_ref, h2_ref, comb_ref = rest
    else:
        x1_ref, h2_ref = rest
    merged = (ga_ref[0].astype(F32) * jnp.dot(ya_ref[0], wbra_ref[...], preferred_element_type=F32)
              + mb_ref[0].astype(F32)
              + gc_ref[0].astype(F32) * jnp.dot(yc_ref[0], wbrc_ref[...], preferred_element_type=F32))
    y = jnp.dot(merged.astype(BF16), wout_ref[...], preferred_element_type=F32)
    x1 = x_ref[0] + gt_ref[0] * _rms(y, g1_ref[...])
    x1_ref[0] = x1
    h2 = _rms(x1, g2_ref[...]) * (1.0 + sc_ref[0]) + sh_ref[0]
    h2_ref[0] = h2.astype(BF16)
    if with_router:
        logits = jnp.dot(h2, wr_ref[...], preferred_element_type=F32)
        lane = lax.broadcasted_iota(I32, logits.shape, 1)
        valid = lane < N_EXPERTS
        l1 = jnp.where(valid, logits, -jnp.inf)
        v1 = jnp.max(l1, axis=-1, keepdims=True)
        i1 = jnp.min(jnp.where(l1 == v1, lane, LANES), axis=-1, keepdims=True)
        l2 = jnp.where(lane == i1, -jnp.inf, l1)
        v2 = jnp.max(l2, axis=-1, keepdims=True)
        i2 = jnp.min(jnp.where(l2 == v2, lane, LANES), axis=-1, keepdims=True)
        e2 = jnp.exp(v2 - v1)
        p1 = 1.0 / (1.0 + e2)
        p2 = e2 / (1.0 + e2)
        comb_ref[0] = jnp.where(lane == i1, p1, 0.0) + jnp.where(lane == i2, p2, 0.0)


def _merge_call(x, ya, yc, mb, ga, gc, gt, sh, sc, g1, g2, wbra, wbrc, wout, wr=None):
    b, l, d = x.shape
    tm = min(TM_MERGE, l)
    tok = lambda w: pl.BlockSpec((1, tm, w), lambda bi, i: (bi, i, 0))
    full = lambda a: pl.BlockSpec(a.shape, lambda bi, i: (0,) * a.ndim)
    vec = pl.BlockSpec((1, 1, d), lambda bi, i: (bi, 0, 0))
    in_specs = [tok(d), tok(A_WIDTH), tok(C_WIDTH), tok(d), tok(d), tok(d), vec, vec, vec,
                full(g1), full(g2), full(wbra), full(wbrc), full(wout)]
    args = [x, ya, yc, mb, ga, gc, gt, sh, sc, g1, g2, wbra, wbrc, wout]
    out_shape = [jax.ShapeDtypeStruct((b, l, d), F32), jax.ShapeDtypeStruct((b, l, d), BF16)]
    out_specs = [tok(d), tok(d)]
    if wr is not None:
        in_specs.append(full(wr))
        args.append(wr)
        out_shape.append(jax.ShapeDtypeStruct((b, l, LANES), F32))
        out_specs.append(tok(LANES))
    return pl.pallas_call(
        functools.partial(_merge_kernel, with_router=wr is not None),
        grid=(b, l // tm),
        in_specs=in_specs,
        out_specs=out_specs,
        out_shape=out_shape,
        compiler_params=_params(("arbitrary", "arbitrary")),
        name="merge_router" if wr is not None else "merge",
    )(*args)


def _ffn_kernel(x1_ref, h2_ref, gt_ref, g3_ref, wg_ref, wu_ref, wd_ref, o_ref):
    h2 = h2_ref[0]
    gate = jnp.dot(h2, wg_ref[...], preferred_element_type=F32)
    up = jnp.dot(h2, wu_ref[...], preferred_element_type=F32)
    act = (gate * _sigmoid(gate) * up).astype(BF16)
    y = jnp.dot(act, wd_ref[...], preferred_element_type=F32)
    o_ref[0] = x1_ref[0] + gt_ref[0] * _rms(y, g3_ref[...])


def _ffn_call(x1, h2, gt, g3, wg, wu, wd):
    b, l, d = x1.shape
    tm = min(TM_FFN, l)
    tok = pl.BlockSpec((1, tm, d), lambda bi, i: (bi, i, 0))
    full = lambda a: pl.BlockSpec(a.shape, lambda bi, i: (0,) * a.ndim)
    vec = pl.BlockSpec((1, 1, d), lambda bi, i: (bi, 0, 0))
    return pl.pallas_call(
        _ffn_kernel,
        grid=(b, l // tm),
        in_specs=[tok, tok, vec, full(g3), full(wg), full(wu), full(wd)],
        out_specs=tok,
        out_shape=jax.ShapeDtypeStruct((b, l, d), F32),
        compiler_params=_params(("arbitrary", "arbitrary")),
        name="ffn_dense",
    )(x1, h2, gt, g3, wg, wu, wd)


def _moe_kernel(x1_ref, h2_ref, comb_ref, gt_ref, g3_ref, wg_ref, wu_ref, wd_ref, o_ref, acc_scr):
    e = pl.program_id(2)

    @pl.when(e == 0)
    def _():
        acc_scr[...] = jnp.zeros_like(acc_scr)

    h2 = h2_ref[0]
    gate = jnp.dot(h2, wg_ref[0], preferred_element_type=F32)
    up = jnp.dot(h2, wu_ref[0], preferred_element_type=F32)
    act = (gate * _sigmoid(gate) * up).astype(BF16)
    y = jnp.dot(act, wd_ref[0], preferred_element_type=F32)
    comb = comb_ref[0]
    lane = lax.broadcasted_iota(I32, comb.shape, 1)
    weight = jnp.sum(jnp.where(lane == e, comb, 0.0), axis=-1, keepdims=True)
    acc_scr[...] = acc_scr[...] + weight * y

    @pl.when(e == pl.num_programs(2) - 1)
    def _():
        o_ref[0] = x1_ref[0] + gt_ref[0] * _rms(acc_scr[...], g3_ref[...])


def _moe_call(x1, h2, comb, gt, g3, wg, wu, wd):
    b, l, d = x1.shape
    tm = min(TM_FFN, l)
    n_e, _, ff = wg.shape
    tok = lambda w: pl.BlockSpec((1, tm, w), lambda bi, i, e: (bi, i, 0))
    vec = pl.BlockSpec((1, 1, d), lambda bi, i, e: (bi, 0, 0))
    return pl.pallas_call(
        _moe_kernel,
        grid=(b, l // tm, n_e),
        in_specs=[tok(d), tok(d), tok(LANES), vec, pl.BlockSpec(g3.shape, lambda bi, i, e: (0, 0)),
                  pl.BlockSpec((1, d, ff), lambda bi, i, e: (e, 0, 0)),
                  pl.BlockSpec((1, d, ff), lambda bi, i, e: (e, 0, 0)),
                  pl.BlockSpec((1, ff, d), lambda bi, i, e: (e, 0, 0))],
        out_specs=tok(d),
        out_shape=jax.ShapeDtypeStruct((b, l, d), F32),
        scratch_shapes=[pltpu.VMEM((tm, d), F32)],
        compiler_params=_params(("arbitrary", "arbitrary", "arbitrary")),
        name="ffn_moe",
    )(x1, h2, comb, gt, g3, wg, wu, wd)


def _rot_cols(w, n_heads, head_dim):
    d = w.shape[0]
    w = w.reshape(d, n_heads, head_dim)
    half = ROPE_DIM // 2
    rot = jnp.concatenate([-w[..., half:ROPE_DIM], w[..., :half],
                           jnp.zeros((d, n_heads, head_dim - ROPE_DIM), w.dtype)], axis=-1)
    return rot.reshape(d, n_heads * head_dim)


def _pad_cols(w, width):
    return jnp.pad(w, ((0, 0), (0, width - w.shape[1])))


def _layer_weights(w_in, w_uk, w_uv, w_pool):
    d = w_in.shape[0]
    offs, o = [], 0
    for s in IN_SIZES:
        offs.append(o)
        o += s
    piece = lambda k: w_in[:, offs[k]:offs[k] + IN_SIZES[k]]
    w_qa, w_lat, w_kr, w_qi, w_ki, w_wi, w_up, w_qc, w_kc, w_vc, w_gate = [piece(k) for k in range(len(IN_SIZES))]
    attn_scale = HEAD_DIM ** -0.5
    idx_scale = IDX_DIM ** -0.5 * N_IDX_HEADS ** -0.5
    wrow = jnp.concatenate([
        w_lat,
        _pad_cols(w_kr, LANES), _pad_cols(_rot_cols(w_kr, 1, ROPE_DIM), LANES),
        _pad_cols(w_ki, LANES), _pad_cols(_rot_cols(w_ki, 1, IDX_DIM), LANES),
        w_up, w_kc, w_gate], axis=1).astype(BF16)
    w_qc_t = (w_qc * attn_scale).T.reshape(N_HEADS_C, HEAD_DIM, d)
    zeros = jnp.zeros_like(w_qc_t)
    even = (jnp.arange(N_HEADS_C) % 2 == 0)[:, None, None]
    w_qc_pad = jnp.concatenate([jnp.where(even, w_qc_t, zeros), jnp.where(even, zeros, w_qc_t)], axis=1)
    w_qc_pad = w_qc_pad.reshape(2 * C_WIDTH, d)
    wt = jnp.concatenate([
        (w_qa * attn_scale).T, (_rot_cols(w_qa, N_HEADS_A, HEAD_DIM) * attn_scale).T,
        w_qi.T, _rot_cols(w_qi, N_IDX_HEADS, IDX_DIM).T,
        w_qc_pad, w_vc.T,
        jnp.pad((w_wi * idx_scale).T, ((0, T_WI_ROWS - N_IDX_HEADS), (0, 0)))], axis=0).astype(BF16)
    wuk = jnp.zeros((KV_LATENT, LANES), F32).at[:, ROPE_DIM:HEAD_DIM].set(w_uk).astype(BF16)
    wuvt = w_uv.T.astype(BF16)
    wpool = jnp.zeros((POOL_WIDTH, POOL_WIDTH), F32)
    for g in range(N_POOL_GROUPS):
        sl = slice(g * POOL_GROUP_DIM, (g + 1) * POOL_GROUP_DIM)
        wpool = wpool.at[sl, sl].set(w_pool[g])
    return wrow, wt, wuk, wuvt, wpool.astype(BF16)


def _rope_tables(positions):
    inv = ROPE_THETA ** (-jnp.arange(0, ROPE_DIM, 2, dtype=F32) / ROPE_DIM)
    ang = positions.astype(F32)[..., None] * inv
    cos, sin = jnp.cos(ang), jnp.sin(ang)
    b, l = positions.shape
    ones = jnp.ones((b, l, HEAD_DIM - ROPE_DIM), F32)
    cos_h = jnp.concatenate([cos, cos, ones], axis=-1)
    sin_h = jnp.concatenate([sin, sin, jnp.zeros_like(ones)], axis=-1)
    cos_r = jnp.concatenate([cos_h, cos_h], axis=-1)
    sin_r = jnp.concatenate([sin_h, sin_h], axis=-1)
    return cos_r, sin_r, cos_h.transpose(0, 2, 1), sin_h.transpose(0, 2, 1)


def kernel(x, c, positions, w_ada, b_ada, norm_gains, w_in, g_kv_latent, w_uk, w_uv, w_pool, pool_scale,
           w_br_a, w_br_b, w_br_c, w_out, w_gate_dense, w_up_dense, w_down_dense,
           w_router, w_gate_moe, w_up_moe, w_down_moe):
    b, l, d = x.shape
    depth = w_in.shape[0]
    assert d == D_MODEL and l % KC == 0 and l % TM_FFN == 0
    cos_r, sin_r, cos_t, sin_t = _rope_tables(positions)
    c_pad = jnp.pad(c, ((0, 8 - b), (0, 0)))
    mod = _mod_call(c_pad, w_ada, b_ada)[:, :b]
    idx = lax.broadcasted_iota(I32, (QB, QB), 0)
    jdx = lax.broadcasted_iota(I32, (QB, QB), 1)
    tri = (jdx <= idx).astype(BF16)
    later = (jdx > idx).astype(BF16)
    later2 = jnp.concatenate([later, later], axis=1)
    for layer in range(depth):
        sh1, sc1, gt1, sh2, sc2, gt2 = [m.reshape(b, 1, d) for m in jnp.split(mod[layer], 6, axis=-1)]
        gains = norm_gains[layer].reshape(4, 1, d)
        wrow, wt, wuk, wuvt, wpool = _layer_weights(w_in[layer], w_uk[layer], w_uv[layer], w_pool[layer])
        (qat, qit, wit, ka, vat, ki, qct, kc, vct, ga, gc, mb) = _inproj_call(
            x, sh1, sc1, gains[0], wrow, wt, cos_r, sin_r, cos_t, sin_t,
            g_kv_latent[layer].reshape(1, KV_LATENT), wuk, wuvt, wpool,
            pool_scale[layer].reshape(1, POOL_WIDTH), w_br_b[layer].astype(BF16))
        ya = _dsa_call(qit, wit, qat, ki, ka, vat, tri)
        yc = _stick_call(qct, kc, vct, later2)
        i = layer // 2
        merge_args = (x, ya, yc, mb, ga, gc, gt1, sh2, sc2, gains[1], gains[2],
                      w_br_a[layer].astype(BF16), w_br_c[layer].astype(BF16), w_out[layer].astype(BF16))
        if layer % 2 == 0:
            x1, h2 = _merge_call(*merge_args)
            x = _ffn_call(x1, h2, gt2, gains[3], w_gate_dense[i].astype(BF16), w_up_dense[i].astype(BF16),
                          w_down_dense[i].astype(BF16))
        else:
            x1, h2, comb = _merge_call(*merge_args, wr=_pad_cols(w_router[i], LANES))
            x = _moe_call(x1, h2, comb, gt2, gains[3], w_gate_moe[i].astype(BF16), w_up_moe[i].astype(BF16),
                          w_down_moe[i].astype(BF16))
    return x
```

```python
import functools

import jax
import jax.numpy as jnp
from jax import lax
from jax.experimental import pallas as pl
from jax.experimental.pallas import tpu as pltpu

F32 = jnp.float32
BF16 = jnp.bfloat16
I32 = jnp.int32

D_MODEL = 1024
HEAD_DIM = 64
ROPE_DIM = HEAD_DIM // 4
NOPE_DIM = HEAD_DIM - ROPE_DIM
ROPE_THETA = 500000.0
N_HEADS_A = (3 * D_MODEL // 8) // HEAD_DIM
A_WIDTH = N_HEADS_A * HEAD_DIM
KV_LATENT = D_MODEL // 8
N_IDX_HEADS = 4
IDX_DIM = 64
TOPK_MAX = 256
N_POOL_GROUPS = 4
POOL_WINDOWS = (2, 4, 8, 16)
POOL_WIDTH = D_MODEL // 4
POOL_GROUP_DIM = POOL_WIDTH // N_POOL_GROUPS
N_HEADS_C = (D_MODEL // 4) // HEAD_DIM
C_WIDTH = N_HEADS_C * HEAD_DIM
N_BRANCHES = 3
IN_SIZES = (A_WIDTH, KV_LATENT, ROPE_DIM, N_IDX_HEADS * IDX_DIM, IDX_DIM, N_IDX_HEADS,
            POOL_WIDTH, C_WIDTH, C_WIDTH, C_WIDTH, N_BRANCHES * D_MODEL)
D_FF = 2816
N_EXPERTS = 8
TOP_K = 2
D_FF_EXPERT = D_FF // TOP_K
RMS_EPS = 1e-6

LANES = 128
QB = 128
KC = 512
QS_STICK = 512
QS_DSA = 256
KB_DSA = 256
STICK_DEAD = -106.0
TM_IN = 256
TM_MERGE = 256
TM_FFN = 512
POOL_HALO = 16
VMEM_LIMIT = 56 * 1024 * 1024
INT_MIN = -2147483648
LOG2_E = 1.4426950408889634
NEG_BIG = -1e30
NEG_MASK = -2e30

R_LAT, R_KR, R_KRR, R_KI, R_KIR, R_UP, R_KC, R_GATE = 0, 128, 256, 384, 512, 640, 896, 1152
R_WIDTH = R_GATE + N_BRANCHES * D_MODEL
T_QA, T_QAR, T_QI, T_QIR, T_QC, T_VC, T_WI = 0, 384, 768, 1024, 1280, 1792, 2048
T_WI_ROWS = 16
T_HEIGHT = T_WI + T_WI_ROWS


def _params(sem):
    return pltpu.CompilerParams(dimension_semantics=sem, vmem_limit_bytes=VMEM_LIMIT)


def _sigmoid(v):
    return 1.0 / (1.0 + jnp.exp(-v))


def _rms(v, gain):
    return v * lax.rsqrt(jnp.mean(v * v, axis=-1, keepdims=True) + RMS_EPS) * gain


def _mod_kernel(c_ref, w_ref, b_ref, o_ref):
    c = c_ref[...]
    cond = c * _sigmoid(c)
    o_ref[0] = jnp.dot(cond, w_ref[0], preferred_element_type=F32) + b_ref[0]


def _mod_call(c_pad, w_ada, b_ada):
    depth, d, n = w_ada.shape
    tn = 1024
    return pl.pallas_call(
        _mod_kernel,
        grid=(depth, n // tn),
        in_specs=[
            pl.BlockSpec((8, d), lambda l, j: (0, 0)),
            pl.BlockSpec((1, d, tn), lambda l, j: (l, 0, j)),
            pl.BlockSpec((1, 1, tn), lambda l, j: (l, 0, j)),
        ],
        out_specs=pl.BlockSpec((1, 8, tn), lambda l, j: (l, 0, j)),
        out_shape=jax.ShapeDtypeStruct((depth, 8, n), F32),
        compiler_params=_params(("arbitrary", "arbitrary")),
        name="adaln_mod",
    )(c_pad, w_ada, b_ada.reshape(depth, 1, n))


def _inproj_kernel(x_ref, sh_ref, sc_ref, g_ref, wrow_ref, wt_ref, c_ref, s_ref, ct_ref, st_ref,
                   glat_ref, wuk_ref, wuvt_ref, wpool_ref, pscale_ref, wbrb_ref,
                   qat_ref, qit_ref, wit_ref, ka_ref, vat_ref, ki_ref, qct_ref, kc_ref, vct_ref,
                   ga_ref, gc_ref, mb_ref,
                   h_scr, ht_scr, ext_scr, prev_scr):
    i = pl.program_id(1)
    tm = x_ref.shape[1]
    x = x_ref[0]
    h = _rms(x, g_ref[...]) * (1.0 + sc_ref[0]) + sh_ref[0]
    h_scr[...] = h.astype(BF16)
    ht_scr[...] = h.T.astype(BF16)

    def rowdot(a, width):
        return jnp.dot(h_scr[...], wrow_ref[:, a:a + width], preferred_element_type=F32)

    def tdot(a, height):
        return jnp.dot(wt_ref[a:a + height, :], ht_scr[...], preferred_element_type=F32)

    cos_r, sin_r = c_ref[0], s_ref[0]
    cos_t, sin_t = ct_ref[0], st_ref[0]

    latn = _rms(rowdot(R_LAT, KV_LATENT), glat_ref[...])
    ka = (rowdot(R_KR, LANES) * cos_r + rowdot(R_KRR, LANES) * sin_r
          + jnp.dot(latn.astype(BF16), wuk_ref[...], preferred_element_type=F32))
    ka_ref[0] = ka[:, :HEAD_DIM].astype(BF16)
    vat = jnp.dot(wuvt_ref[...], latn.T.astype(BF16), preferred_element_type=F32)
    for j in range(tm // KB_DSA):
        vat_ref[0, j] = vat[:, j * KB_DSA:(j + 1) * KB_DSA].astype(BF16)

    ki =rowdot(R_KI, LANES) * cos_r + rowdot(R_KIR, LANES) * sin_r
    ki_ref[0] = ki[:, :IDX_DIM].astype(BF16)

    qa, qar = tdot(T_QA, A_WIDTH), tdot(T_QAR, A_WIDTH)
    for hh in range(N_HEADS_A):
        r = slice(hh * HEAD_DIM, (hh + 1) * HEAD_DIM)
        qat_ref[0, r, :] = (qa[r] * cos_t + qar[r] * sin_t).astype(BF16)
    qi, qir = tdot(T_QI, N_IDX_HEADS * IDX_DIM), tdot(T_QIR, N_IDX_HEADS * IDX_DIM)
    for hh in range(N_IDX_HEADS):
        r = slice(hh * IDX_DIM, (hh + 1) * IDX_DIM)
        qit_ref[0, r, :] = (qi[r] * cos_t + qir[r] * sin_t).astype(BF16)
    wit_ref[0] = tdot(T_WI, T_WI_ROWS)

    qct_ref[0] = tdot(T_QC, 2 * C_WIDTH).astype(BF16)
    kc_ref[0] = rowdot(R_KC, C_WIDTH).astype(BF16)
    vct = tdot(T_VC, C_WIDTH)
    for j in range(tm // QB):
        vct_ref[0, j] = vct[:, j * QB:(j + 1) * QB].astype(BF16)

    up = rowdot(R_UP, POOL_WIDTH)

    @pl.when(i == 0)
    def _():
        prev_scr[...] = jnp.zeros_like(prev_scr)

    ext_scr[0:POOL_HALO, :] = prev_scr[...]
    ext_scr[POOL_HALO:POOL_HALO + tm, :] = up
    prev_scr[...] = up[tm - POOL_HALO:, :]
    lag = [ext_scr[POOL_HALO - j:POOL_HALO - j + tm, :] for j in range(POOL_HALO)]
    sums = {}
    run = lag[0]
    for j in range(1, POOL_HALO):
        run = run + lag[j]
        if j + 1 in POOL_WINDOWS:
            sums[j + 1] = run
    lane = lax.broadcasted_iota(I32, (tm, POOL_WIDTH), 1)
    pos = i * tm + lax.broadcasted_iota(I32, (tm, POOL_WIDTH), 0)
    pooled_sum = sums[POOL_WINDOWS[-1]]
    win = jnp.full((tm, POOL_WIDTH), POOL_WINDOWS[-1], I32)
    for g in range(N_POOL_GROUPS - 2, -1, -1):
        in_group = lane < (g + 1) * POOL_GROUP_DIM
        pooled_sum = jnp.where(in_group, sums[POOL_WINDOWS[g]], pooled_sum)
        win = jnp.where(in_group, POOL_WINDOWS[g], win)
    cnt = jnp.minimum(pos + 1, win).astype(F32)
    pooled = pooled_sum / cnt - up
    yb = jnp.dot(pooled.astype(BF16), wpool_ref[...], preferred_element_type=F32) * pscale_ref[...]

    ga_ref[0] = _sigmoid(rowdot(R_GATE, D_MODEL)).astype(BF16)
    gb = _sigmoid(rowdot(R_GATE + D_MODEL, D_MODEL))
    mb_ref[0] = (gb * jnp.dot(yb.astype(BF16), wbrb_ref[...], preferred_element_type=F32)).astype(BF16)
    gc_ref[0] = _sigmoid(rowdot(R_GATE + 2 * D_MODEL, D_MODEL)).astype(BF16)


def _inproj_call(x, sh, sc, gain, wrow, wt, cos_r, sin_r, cos_t, sin_t, glat, wuk, wuvt, wpool, pscale, wbrb):
    b, l, d = x.shape
    tm = min(TM_IN, l)
    nq = l // QB
    tok = lambda w: pl.BlockSpec((1, tm, w), lambda bi, i: (bi, i, 0))
    feat = lambda hgt: pl.BlockSpec((1, hgt, tm), lambda bi, i: (bi, 0, i))
    blk = lambda hgt, w: pl.BlockSpec((1, tm // w, hgt, w), lambda bi, i: (bi, i, 0, 0))
    full = lambda a: pl.BlockSpec(a.shape, lambda bi, i: (0,) * a.ndim)
    vec = pl.BlockSpec((1, 1, d), lambda bi, i: (bi, 0, 0))
    out_shape = (
        jax.ShapeDtypeStruct((b, A_WIDTH, l), BF16),
        jax.ShapeDtypeStruct((b, N_IDX_HEADS * IDX_DIM, l), BF16),
        jax.ShapeDtypeStruct((b, T_WI_ROWS, l), F32),
        jax.ShapeDtypeStruct((b, l, HEAD_DIM), BF16),
        jax.ShapeDtypeStruct((b, l // KB_DSA, HEAD_DIM, KB_DSA), BF16),
        jax.ShapeDtypeStruct((b, l, IDX_DIM), BF16),
        jax.ShapeDtypeStruct((b, 2 * C_WIDTH, l), BF16),
        jax.ShapeDtypeStruct((b, l, C_WIDTH), BF16),
        jax.ShapeDtypeStruct((b, nq, C_WIDTH, QB), BF16),
        jax.ShapeDtypeStruct((b, l, d), BF16),
        jax.ShapeDtypeStruct((b, l, d), BF16),
        jax.ShapeDtypeStruct((b, l, d), BF16),
    )
    out_specs = (feat(A_WIDTH), feat(N_IDX_HEADS * IDX_DIM), feat(T_WI_ROWS), tok(HEAD_DIM), blk(HEAD_DIM, KB_DSA),
                 tok(IDX_DIM), feat(2 * C_WIDTH), tok(C_WIDTH), blk(C_WIDTH, QB), tok(d), tok(d), tok(d))
    in_specs = [tok(d), vec, vec, full(gain), full(wrow), full(wt), tok(LANES), tok(LANES),
                feat(HEAD_DIM), feat(HEAD_DIM), full(glat), full(wuk), full(wuvt), full(wpool),
                full(pscale), full(wbrb)]
    return pl.pallas_call(
        _inproj_kernel,
        grid=(b, l // tm),
        in_specs=in_specs,
        out_specs=out_specs,
        out_shape=out_shape,
        scratch_shapes=[pltpu.VMEM((tm, d), BF16), pltpu.VMEM((d, tm), BF16),
                        pltpu.VMEM((tm + POOL_HALO, POOL_WIDTH), F32), pltpu.VMEM((POOL_HALO, POOL_WIDTH), F32)],
        compiler_params=_params(("arbitrary", "arbitrary")),
        name="inproj",
    )(x, sh, sc, gain, wrow, wt, cos_r, sin_r, cos_t, sin_t, glat, wuk, wuvt, wpool, pscale, wbrb)


def _dsa_kernel(qit_ref, wit_ref, qat_ref, ki_ref, ka_ref, vat_ref, tri_ref, ya_ref, keys_scr, acc_scr, *, k_sel):
    i = pl.program_id(1)
    qs = qat_ref.shape[2]
    n_blocks = (i + 1) * (qs // QB)
    n_chunks = (n_blocks * QB + KC - 1) // KC
    qpos = i * qs + lax.broadcasted_iota(I32, (1, qs), 1)
    w_idx = wit_ref[0]

    def score_chunk(c, carry):
        r0 = pl.multiple_of(c * KC, KC)
        kblk = ki_ref[0, pl.ds(r0, KC), :]
        parts = [jnp.dot(kblk, qit_ref[0, hh * IDX_DIM:(hh + 1) * IDX_DIM, :], preferred_element_type=F32)
                 for hh in range(N_IDX_HEADS)]
        score = jnp.zeros((KC, qs), F32)
        for hh in range(N_IDX_HEADS):
            score = score + jnp.maximum(parts[hh], 0.0) * w_idx[hh:hh + 1, :]
        bits = lax.bitcast_convert_type(score, I32)
        key = jnp.where(bits < 0, INT_MIN - bits, bits)
        kpos = r0 + lax.broadcasted_iota(I32, (KC, qs), 0)
        keys_scr[pl.ds(r0, KC), :] = jnp.where(kpos <= qpos, key, INT_MIN)
        return carry

    lax.fori_loop(0, n_chunks, score_chunk, 0)

    def count_ge(trial):
        n_acc = 4

        def body(c, accs):
            r0 = pl.multiple_of(c * KC, KC)
            rows = keys_scr[pl.ds(r0, KC), :].reshape(KC // 8, 8, qs)
            accs = list(accs)
            for j in range(KC // 8):
                a = accs[j % n_acc]
                accs[j % n_acc] = jnp.where(rows[j] >= trial, a + 1, a)
            return tuple(accs)

        accs = lax.fori_loop(0, n_chunks, body, tuple(jnp.zeros((8, qs), I32) for _ in range(n_acc)))
        return jnp.sum(accs[0] + accs[1] + accs[2] + accs[3], axis=0, keepdims=True)

    c_zero = count_ge(jnp.zeros((1, qs), I32))
    c_pos = count_ge(jnp.ones((1, qs), I32))
    nonneg = c_zero >= k_sel
    tie_at_zero = jnp.logical_and(nonneg, c_pos < k_sel)
    ans0 = jnp.where(nonneg, 0, INT_MIN)
    c_ans0 = jnp.where(nonneg, c_zero, jnp.int32(2 ** 30))

    def unsettled(c_ans):
        settled = jnp.logical_or(tie_at_zero, c_ans == k_sel)
        return jnp.max(jnp.where(settled, 0, 1))

    def refine(carry):
        bit, ans, c_ans, _ = carry
        trial = ans + lax.shift_left(jnp.int32(1), bit)
        c = count_ge(trial)
        ok = c >= k_sel
        c_ans = jnp.where(ok, c, c_ans)
        return bit - 1, jnp.where(ok, trial, ans), c_ans, unsettled(c_ans)

    init = (jnp.int32(30), ans0, c_ans0, unsettled(c_ans0))
    _, thr, _, _ = lax.while_loop(lambda cr: jnp.logical_and(cr[0] >= 0, cr[3] > 0), refine, init)
    n_above = count_ge(thr + 1)
    n_ties = jnp.where(thr == INT_MIN, 0, k_sel - n_above).astype(F32)

    acc_scr[...] = jnp.zeros_like(acc_scr)
    heads = range(N_HEADS_A)

    kb_rows = tri_ref.shape[0]

    def attend(kb, carry):
        seen, ms, ls = carry
        r0 = pl.multiple_of(kb * kb_rows, kb_rows)
        keyb = keys_scr[pl.ds(r0, kb_rows), :]
        tied = keyb == thr
        tied_f = jnp.where(tied, 1.0, 0.0)
        rank = seen + jnp.dot(tri_ref[...], tied_f.astype(BF16), preferred_element_type=F32)
        keep = jnp.logical_or(keyb > thr, jnp.logical_and(tied, rank <= n_ties))
        bias = jnp.where(keep, 0.0, NEG_MASK)
        seen = seen + jnp.sum(tied_f, axis=0, keepdims=True)
        kblk = ka_ref[0, pl.ds(r0, kb_rows), :]
        vblk = vat_ref[0, kb]
        logits = [jnp.dot(kblk, qat_ref[0, hh * HEAD_DIM:(hh + 1) * HEAD_DIM, :], preferred_element_type=F32) + bias
                  for hh in heads]
        new_ms = [jnp.maximum(ms[hh], jnp.max(logits[hh], axis=0, keepdims=True)) for hh in heads]
        probs = [jnp.exp2(logits[hh] - new_ms[hh]) for hh in heads]
        alphas = [jnp.exp2(ms[hh] - new_ms[hh]) for hh in heads]
        new_ls = [alphas[hh] * ls[hh] + jnp.sum(probs[hh], axis=0, keepdims=True) for hh in heads]
        outs = [jnp.dot(vblk, probs[hh].astype(BF16), preferred_element_type=F32) for hh in heads]
        for hh in heads:
            r = slice(hh * HEAD_DIM, (hh + 1) * HEAD_DIM)
            acc_scr[r, :] = acc_scr[r, :] * alphas[hh] + outs[hh]
        return seen, tuple(new_ms), tuple(new_ls)

    zero = jnp.zeros((1, qs), F32)
    init = (zero, tuple(jnp.full((1, qs), NEG_BIG, F32) for _ in heads), tuple(zero for _ in heads))
    _, _, ls = lax.fori_loop(0, (i + 1) * (qs // kb_rows), attend, init)
    for hh in heads:
        r = slice(hh * HEAD_DIM, (hh + 1) * HEAD_DIM)
        acc_scr[r, :] = acc_scr[r, :] / ls[hh]
    ya_ref[0] = acc_scr[...].T.astype(BF16)


def _dsa_call(qit, wit, qat, ki, ka, vat, tri):
    b, l, _ = ki.shape
    nq = l // QB
    qs = min(QS_DSA, l)
    k_sel = min(TOPK_MAX, l // 4)
    return pl.pallas_call(
        functools.partial(_dsa_kernel, k_sel=k_sel),
        grid=(b, l // qs),
        in_specs=[
            pl.BlockSpec((1, N_IDX_HEADS * IDX_DIM, qs), lambda bi, i: (bi, 0, i)),
            pl.BlockSpec((1, T_WI_ROWS, qs), lambda bi, i: (bi, 0, i)),
            pl.BlockSpec((1, A_WIDTH, qs), lambda bi, i: (bi, 0, i)),
            pl.BlockSpec((1, l, IDX_DIM), lambda bi, i: (bi, 0, 0)),
            pl.BlockSpec((1, l, HEAD_DIM), lambda bi, i: (bi, 0, 0)),
            pl.BlockSpec((1, l // KB_DSA, HEAD_DIM, KB_DSA), lambda bi, i: (bi, 0, 0, 0)),
            pl.BlockSpec((KB_DSA, KB_DSA), lambda bi, i: (0, 0)),
        ],
        out_specs=pl.BlockSpec((1, qs, A_WIDTH), lambda bi, i: (bi, i, 0)),
        out_shape=jax.ShapeDtypeStruct((b, l, A_WIDTH), BF16),
        scratch_shapes=[pltpu.VMEM((-(-l // KC) * KC, qs), I32), pltpu.VMEM((A_WIDTH, qs), F32)],
        compiler_params=_params(("arbitrary", "arbitrary")),
        name="dsa_attention",
    )(qit, wit, qat, ki, ka, vat, tri)


def _stick_kernel(qct_ref, kc_ref, vct_ref, later_ref, yc_ref, acc_scr):
    i = pl.program_id(1)
    qs = qct_ref.shape[2]
    diag_blocks = qs // QB
    qpos = i * qs + lax.broadcasted_iota(I32, (1, qs), 1)
    acc_scr[...] = jnp.zeros_like(acc_scr)

    def block(kb, tail, masked):
        r0 = pl.multiple_of(kb * QB, QB)
        kfull = kc_ref[0, pl.ds(r0, QB), :]
        vt = vct_ref[0, kb]
        heads = range(N_HEADS_C)
        if masked:
            mask = (r0 + lax.broadcasted_iota(I32, (QB, qs), 0)) < qpos
        zs = [jnp.dot(kfull[:, (hh // 2) * LANES:(hh // 2 + 1) * LANES], qct_ref[0, hh * LANES:(hh + 1) * LANES, :],
                      preferred_element_type=F32) for hh in heads]
        log_betas, splits, new_tail = [], [], []
        for hh in heads:
            z = zs[hh]
            log_beta = jnp.minimum(z, 0.0) - jnp.log(1.0 + jnp.exp(-jnp.abs(z)))
            log_keep = log_beta - z
            if masked:
                log_keep = jnp.where(mask, log_keep, 0.0)
            hi = log_keep.astype(BF16)
            lo = (log_keep - hi.astype(F32)).astype(BF16)
            log_betas.append(log_beta)
            splits.append(jnp.concatenate([hi, lo], axis=0))
            new_tail.append(tail[hh] + jnp.sum(log_keep, axis=0, keepdims=True))
        withins = [jnp.dot(later_ref[...], splits[hh], preferred_element_type=F32) for hh in heads]
        weights = []
        for hh in heads:
            a = jnp.exp(log_betas[hh] + withins[hh] + tail[hh])
            if masked:
                a = jnp.where(mask, a, 0.0)
            weights.append(a.astype(BF16))
        for hh in heads:
            r = slice(hh * HEAD_DIM, (hh + 1) * HEAD_DIM)
            acc_scr[r, :] = acc_scr[r, :] + jnp.dot(vt[r, :], weights[hh], preferred_element_type=F32)
        return tuple(new_tail)

    def largest(tail):
        worst = tail[0]
        for hh in range(1, N_HEADS_C):
            worst = jnp.maximum(worst, tail[hh])
        return jnp.max(worst)

    first = (i + 1) * diag_blocks - 1
    n_rest = i * diag_blocks
    tail = tuple(jnp.zeros((1, qs), F32) for _ in range(N_HEADS_C))
    tail = lax.fori_loop(0, diag_blocks, lambda t, tl: block(first - t, tl, True), tail)

    def live(cr):
        return jnp.logical_and(cr[0] < n_rest, cr[2] > STICK_DEAD)

    def step(cr):
        t, tl, _ = cr
        tl = block(n_rest - 1 - t, tl, False)
        return t + 1, tl, largest(tl)

    lax.while_loop(live, step, (jnp.int32(0), tail, largest(tail)))
    yc_ref[0] = acc_scr[...].T.astype(BF16)


def _stick_call(qct, kc, vct, later2):
    b, l, _ = kc.shape
    nq = l // QB
    qs = min(QS_STICK, l)
    return pl.pallas_call(
        _stick_kernel,
        grid=(b, l // qs),
        in_specs=[
            pl.BlockSpec((1, 2 * C_WIDTH, qs), lambda bi, i: (bi, 0, i)),
            pl.BlockSpec((1, l, C_WIDTH), lambda bi, i: (bi, 0, 0)),
            pl.BlockSpec((1, nq, C_WIDTH, QB), lambda bi, i: (bi, 0, 0, 0)),
            pl.BlockSpec((QB, 2 * QB), lambda bi, i: (0, 0)),
        ],
        out_specs=pl.BlockSpec((1, qs, C_WIDTH), lambda bi, i: (bi, i, 0)),
        out_shape=jax.ShapeDtypeStruct((b, l, C_WIDTH), BF16),
        scratch_shapes=[pltpu.VMEM((C_WIDTH, qs), F32)],
        compiler_params=_params(("arbitrary", "arbitrary")),
        name="stick_attention",
    )(qct, kc, vct, later2)


def _merge_kernel(x_ref, ya_ref, yc_ref, mb_ref, ga_ref, gc_ref, gt_ref, sh_ref, sc_ref, g1_ref, g2_ref,
                  wbra_ref, wbrc_ref, wout_ref, *rest, with_router):
    if with_router:
        wr_ref, x1_ref, h2_ref, comb_ref = rest
    else:
        x1_ref, h2_ref = rest
    merged = (ga_ref[0].astype(F32) * jnp.dot(ya_ref[0], wbra_ref[...], preferred_element_type=F32)
              + mb_ref[0].astype(F32)
              + gc_ref[0].astype(F32) * jnp.dot(yc_ref[0], wbrc_ref[...], preferred_element_type=F32))
    y = jnp.dot(merged.astype(BF16), wout_ref[...], preferred_element_type=F32)
    x1 = x_ref[0] + gt_ref[0] * _rms(y, g1_ref[...])
    x1_ref[0] = x1
    h2 = _rms(x1, g2_ref[...]) * (1.0 + sc_ref[0]) + sh_ref[0]
    h2_ref[0] = h2.astype(BF16)
    if with_router:
        logits = jnp.dot(h2, wr_ref[...], preferred_element_type=F32)
        lane = lax.broadcasted_iota(I32, logits.shape, 1)
        valid = lane < N_EXPERTS
        l1 = jnp.where(valid, logits, -jnp.inf)
        v1 = jnp.max(l1, axis=-1, keepdims=True)
        i1 = jnp.min(jnp.where(l1 == v1, lane, LANES), axis=-1, keepdims=True)
        l2 = jnp.where(lane == i1, -jnp.inf, l1)
        v2 = jnp.max(l2, axis=-1, keepdims=True)
        i2 = jnp.min(jnp.where(l2 == v2, lane, LANES), axis=-1, keepdims=True)
        e2 = jnp.exp(v2 - v1)
        p1 = 1.0 / (1.0 + e2)
        p2 = e2 / (1.0 + e2)
        comb_ref[0] = jnp.where(lane == i1, p1, 0.0) + jnp.where(lane == i2, p2, 0.0)


def _merge_call(x, ya, yc, mb, ga, gc, gt, sh, sc, g1, g2, wbra, wbrc, wout, wr=None):
    b, l, d = x.shape
    tm = min(TM_MERGE, l)
    tok = lambda w: pl.BlockSpec((1, tm, w), lambda bi, i: (bi, i, 0))
    full = lambda a: pl.BlockSpec(a.shape, lambda bi, i: (0,) * a.ndim)
    vec = pl.BlockSpec((1, 1, d), lambda bi, i: (bi, 0, 0))
    in_specs = [tok(d), tok(A_WIDTH), tok(C_WIDTH), tok(d), tok(d), tok(d), vec, vec, vec,
                full(g1), full(g2), full(wbra), full(wbrc), full(wout)]
    args = [x, ya, yc, mb, ga, gc, gt, sh, sc, g1, g2, wbra, wbrc, wout]
    out_shape = [jax.ShapeDtypeStruct((b, l, d), F32), jax.ShapeDtypeStruct((b, l, d), BF16)]
    out_specs = [tok(d), tok(d)]
    if wr is not None:
        in_specs.append(full(wr))
        args.append(wr)
        out_shape.append(jax.ShapeDtypeStruct((b, l, LANES), F32))
        out_specs.append(tok(LANES))
    return pl.pallas_call(
        functools.partial(_merge_kernel, with_router=wr is not None),
        grid=(b, l // tm),
        in_specs=in_specs,
        out_specs=out_specs,
        out_shape=out_shape,
        compiler_params=_params(("arbitrary", "arbitrary")),
        name="merge_router" if wr is not None else "merge",
    )(*args)


def _ffn_kernel(x1_ref, h2_ref, gt_ref, g3_ref, wg_ref, wu_ref, wd_ref, o_ref):
    h2 = h2_ref[0]
    gate = jnp.dot(h2, wg_ref[...], preferred_element_type=F32)
    up = jnp.dot(h2, wu_ref[...], preferred_element_type=F32)
    act = (gate * _sigmoid(gate) * up).astype(BF16)
    y = jnp.dot(act, wd_ref[...], preferred_element_type=F32)
    o_ref[0] = x1_ref[0] + gt_ref[0] * _rms(y, g3_ref[...])


def _ffn_call(x1, h2, gt, g3, wg, wu, wd):
    b, l, d = x1.shape
    tm = min(TM_FFN, l)
    tok = pl.BlockSpec((1, tm, d), lambda bi, i: (bi, i, 0))
    full = lambda a: pl.BlockSpec(a.shape, lambda bi, i: (0,) * a.ndim)
    vec = pl.BlockSpec((1, 1, d), lambda bi, i: (bi, 0, 0))
    return pl.pallas_call(
        _ffn_kernel,
        grid=(b, l // tm),
        in_specs=[tok, tok, vec, full(g3), full(wg), full(wu), full(wd)],
        out_specs=tok,
        out_shape=jax.ShapeDtypeStruct((b, l, d), F32),
        compiler_params=_params(("arbitrary", "arbitrary")),
        name="ffn_dense",
    )(x1, h2, gt, g3, wg, wu, wd)


def _moe_kernel(x1_ref, h2_ref, comb_ref, gt_ref, g3_ref, wg_ref, wu_ref, wd_ref, o_ref, acc_scr):
    e = pl.program_id(2)

    @pl.when(e == 0)
    def _():
        acc_scr[...] = jnp.zeros_like(acc_scr)

    h2 = h2_ref[0]
    gate = jnp.dot(h2, wg_ref[0], preferred_element_type=F32)
    up = jnp.dot(h2, wu_ref[0], preferred_element_type=F32)
    act = (gate * _sigmoid(gate) * up).astype(BF16)
    y = jnp.dot(act, wd_ref[0], preferred_element_type=F32)
    comb = comb_ref[0]
    lane = lax.broadcasted_iota(I32, comb.shape, 1)
    weight = jnp.sum(jnp.where(lane == e, comb, 0.0), axis=-1, keepdims=True)
    acc_scr[...] = acc_scr[...] + weight * y

    @pl.when(e == pl.num_programs(2) - 1)
    def _():
        o_ref[0] = x1_ref[0] + gt_ref[0] * _rms(acc_scr[...], g3_ref[...])


def _moe_call(x1, h2, comb, gt, g3, wg, wu, wd):
    b, l, d = x1.shape
    tm = min(TM_FFN, l)
    n_e, _, ff = wg.shape
    tok = lambda w: pl.BlockSpec((1, tm, w), lambda bi, i, e: (bi, i, 0))
    vec = pl.BlockSpec((1, 1, d), lambda bi, i, e: (bi, 0, 0))
    return pl.pallas_call(
        _moe_kernel,
        grid=(b, l // tm, n_e),
        in_specs=[tok(d), tok(d), tok(LANES), vec, pl.BlockSpec(g3.shape, lambda bi, i, e: (0, 0)),
                  pl.BlockSpec((1, d, ff), lambda bi, i, e: (e, 0, 0)),
                  pl.BlockSpec((1, d, ff), lambda bi, i, e: (e, 0, 0)),
                  pl.BlockSpec((1, ff, d), lambda bi, i, e: (e, 0, 0))],
        out_specs=tok(d),
        out_shape=jax.ShapeDtypeStruct((b, l, d), F32),
        scratch_shapes=[pltpu.VMEM((tm, d), F32)],
        compiler_params=_params(("arbitrary", "arbitrary", "arbitrary")),
        name="ffn_moe",
    )(x1, h2, comb, gt, g3, wg, wu, wd)


def _rot_cols(w, n_heads, head_dim):
    d = w.shape[0]
    w = w.reshape(d, n_heads, head_dim)
    half = ROPE_DIM // 2
    rot = jnp.concatenate([-w[..., half:ROPE_DIM], w[..., :half],
                           jnp.zeros((d, n_heads, head_dim - ROPE_DIM), w.dtype)], axis=-1)
    return rot.reshape(d, n_heads * head_dim)


def _pad_cols(w, width):
    return jnp.pad(w, ((0, 0), (0, width - w.shape[1])))


def _layer_weights(w_in, w_uk, w_uv, w_pool):
    d = w_in.shape[0]
    offs, o = [], 0
    for s in IN_SIZES:
        offs.append(o)
        o += s
    piece = lambda k: w_in[:, offs[k]:offs[k] + IN_SIZES[k]]
    w_qa, w_lat, w_kr, w_qi, w_ki, w_wi, w_up, w_qc, w_kc, w_vc, w_gate = [piece(k) for k in range(len(IN_SIZES))]
    attn_scale = HEAD_DIM ** -0.5
    idx_scale = IDX_DIM ** -0.5 * N_IDX_HEADS ** -0.5
    wrow = jnp.concatenate([
        w_lat,
        _pad_cols(w_kr, LANES), _pad_cols(_rot_cols(w_kr, 1, ROPE_DIM), LANES),
        _pad_cols(w_ki, LANES), _pad_cols(_rot_cols(w_ki, 1, IDX_DIM), LANES),
        w_up, w_kc, w_gate], axis=1).astype(BF16)
    w_qc_t = (w_qc * attn_scale).T.reshape(N_HEADS_C, HEAD_DIM, d)
    zeros = jnp.zeros_like(w_qc_t)
    even = (jnp.arange(N_HEADS_C) % 2 == 0)[:, None, None]
    w_qc_pad = jnp.concatenate([jnp.where(even, w_qc_t, zeros), jnp.where(even, zeros, w_qc_t)], axis=1)
    w_qc_pad = w_qc_pad.reshape(2 * C_WIDTH, d)
    softmax_scale = attn_scale * LOG2_E
    wt = jnp.concatenate([
        (w_qa * softmax_scale).T, (_rot_cols(w_qa, N_HEADS_A, HEAD_DIM) * softmax_scale).T,
        w_qi.T, _rot_cols(w_qi, N_IDX_HEADS, IDX_DIM).T,
        w_qc_pad, w_vc.T,
        jnp.pad((w_wi * idx_scale).T, ((0, T_WI_ROWS - N_IDX_HEADS), (0, 0)))], axis=0).astype(BF16)
    wuk = jnp.zeros((KV_LATENT, LANES), F32).at[:, ROPE_DIM:HEAD_DIM].set(w_uk).astype(BF16)
    wuvt = w_uv.T.astype(BF16)
    wpool = jnp.zeros((POOL_WIDTH, POOL_WIDTH), F32)
    for g in range(N_POOL_GROUPS):
        sl = slice(g * POOL_GROUP_DIM, (g + 1) * POOL_GROUP_DIM)
        wpool = wpool.at[sl, sl].set(w_pool[g])
    return wrow, wt, wuk, wuvt, wpool.astype(BF16)


def _rope_tables(positions):
    inv = ROPE_THETA ** (-jnp.arange(0, ROPE_DIM, 2, dtype=F32) / ROPE_DIM)
    ang = positions.astype(F32)[..., None] * inv
    cos, sin = jnp.cos(ang), jnp.sin(ang)
    b, l = positions.shape
    ones = jnp.ones((b, l, HEAD_DIM - ROPE_DIM), F32)
    cos_h = jnp.concatenate([cos, cos, ones], axis=-1)
    sin_h = jnp.concatenate([sin, sin, jnp.zeros_like(ones)], axis=-1)
    cos_r = jnp.concatenate([cos_h, cos_h], axis=-1)
    sin_r = jnp.concatenate([sin_h, sin_h], axis=-1)
    return cos_r, sin_r, cos_h.transpose(0, 2, 1), sin_h.transpose(0, 2, 1)


def kernel(x, c, positions, w_ada, b_ada, norm_gains, w_in, g_kv_latent, w_uk, w_uv, w_pool, pool_scale,
           w_br_a, w_br_b, w_br_c, w_out, w_gate_dense, w_up_dense, w_down_dense,
           w_router, w_gate_moe, w_up_moe, w_down_moe):
    b, l, d = x.shape
    depth = w_in.shape[0]
    assert d == D_MODEL and l % KC == 0 and l % TM_FFN == 0
    cos_r, sin_r, cos_t, sin_t = _rope_tables(positions)
    c_pad = jnp.pad(c, ((0, 8 - b), (0, 0)))
    mod = _mod_call(c_pad, w_ada, b_ada)[:, :b]
    idx = lax.broadcasted_iota(I32, (KB_DSA, KB_DSA), 0)
    jdx = lax.broadcasted_iota(I32, (KB_DSA, KB_DSA), 1)
    tri = (jdx <= idx).astype(BF16)
    later = (jdx > idx)[:QB, :QB].astype(BF16)
    later2 = jnp.concatenate([later, later], axis=1)
    for layer in range(depth):
        sh1, sc1, gt1, sh2, sc2, gt2 = [m.reshape(b, 1, d) for m in jnp.split(mod[layer], 6, axis=-1)]
        gains = norm_gains[layer].reshape(4, 1, d)
        wrow, wt, wuk, wuvt, wpool = _layer_weights(w_in[layer], w_uk[layer], w_uv[layer], w_pool[layer])
        (qat, qit, wit, ka, vat, ki, qct, kc, vct, ga, gc, mb) = _inproj_call(
            x, sh1, sc1, gains[0], wrow, wt, cos_r, sin_r, cos_t, sin_t,
            g_kv_latent[layer].reshape(1, KV_LATENT), wuk, wuvt, wpool,
            pool_scale[layer].reshape(1, POOL_WIDTH), w_br_b[layer].astype(BF16))
        ya = _dsa_call(qit, wit, qat, ki, ka, vat, tri)
        yc = _stick_call(qct, kc, vct, later2)
        i = layer // 2
        merge_args = (x, ya, yc, mb, ga, gc, gt1, sh2, sc2, gains[1], gains[2],
                      w_br_a[layer].astype(BF16), w_br_c[layer].astype(BF16), w_out[layer].astype(BF16))
        if layer % 2 == 0:
            x1, h2 = _merge_call(*merge_args)
            x = _ffn_call(x1, h2, gt2, gains[3], w_gate_dense[i].astype(BF16), w_up_dense[i].astype(BF16),
                          w_down_dense[i].astype(BF16))
        else:
            x1, h2, comb = _merge_call(*merge_args, wr=_pad_cols(w_router[i], LANES))
            x = _moe_call(x1, h2, comb, gt2, gains[3], w_gate_moe[i].astype(BF16), w_up_moe[i].astype(BF16),
                          w_down_moe[i].astype(BF16))
    return x
```

```python
import functools

import jax
import jax.numpy as jnp
from jax import lax
from jax.experimental import pallas as pl
from jax.experimental.pallas import tpu as pltpu

F32 = jnp.float32
BF16 = jnp.bfloat16
I32 = jnp.int32
I16 = jnp.int16

D_MODEL = 1024
HEAD_DIM = 64
ROPE_DIM = HEAD_DIM // 4
NOPE_DIM = HEAD_DIM - ROPE_DIM
ROPE_THETA = 500000.0
N_HEADS_A = (3 * D_MODEL // 8) // HEAD_DIM
A_WIDTH = N_HEADS_A * HEAD_DIM
KV_LATENT = D_MODEL // 8
N_IDX_HEADS = 4
IDX_DIM = 64
TOPK_MAX = 256
N_POOL_GROUPS = 4
POOL_WINDOWS = (2, 4, 8, 16)
POOL_WIDTH = D_MODEL // 4
POOL_GROUP_DIM = POOL_WIDTH // N_POOL_GROUPS
N_HEADS_C = (D_MODEL // 4) // HEAD_DIM
C_WIDTH = N_HEADS_C * HEAD_DIM
N_BRANCHES = 3
IN_SIZES = (A_WIDTH, KV_LATENT, ROPE_DIM, N_IDX_HEADS * IDX_DIM, IDX_DIM, N_IDX_HEADS,
            POOL_WIDTH, C_WIDTH, C_WIDTH, C_WIDTH, N_BRANCHES * D_MODEL)
D_FF = 2816
N_EXPERTS = 8
TOP_K = 2
D_FF_EXPERT = D_FF // TOP_K
RMS_EPS = 1e-6

LANES = 128
QB = 128
KC = 512
QS_STICK = 512
QS_DSA = 256
KB_DSA = 512
VB_DSA = 256
STICK_DEAD = -106.0
TM_IN = 256
TM_MERGE = 256
TM_FFN = 512
POOL_HALO = 16
VMEM_LIMIT = 56 * 1024 * 1024
INT_MIN = -2147483648
LOG2_E = 1.4426950408889634
NEG_BIG = -1e30
NEG_MASK = -2e30

R_LAT, R_KR, R_KRR, R_KI, R_KIR, R_UP, R_KC, R_GATE = 0, 128, 256, 384, 512, 640, 896, 1152
R_WIDTH = R_GATE + N_BRANCHES * D_MODEL
T_QA, T_QAR, T_QI, T_QIR, T_QC, T_VC, T_WI = 0, 384, 768, 1024, 1280, 1792, 2048
T_WI_ROWS = 16
T_HEIGHT = T_WI + T_WI_ROWS


def _params(sem):
    return pltpu.CompilerParams(dimension_semantics=sem, vmem_limit_bytes=VMEM_LIMIT)


def _sigmoid(v):
    return 1.0 / (1.0 + jnp.exp(-v))


def _rms(v, gain):
    return v * lax.rsqrt(jnp.mean(v * v, axis=-1, keepdims=True) + RMS_EPS) * gain


def _mod_kernel(c_ref, w_ref, b_ref, o_ref):
    c = c_ref[...]
    cond = c * _sigmoid(c)
    o_ref[0] = jnp.dot(cond, w_ref[0], preferred_element_type=F32) + b_ref[0]


def _mod_call(c_pad, w_ada, b_ada):
    depth, d, n = w_ada.shape
    tn = 1024
    return pl.pallas_call(
        _mod_kernel,
        grid=(depth, n // tn),
        in_specs=[
            pl.BlockSpec((8, d), lambda l, j: (0, 0)),
            pl.BlockSpec((1, d, tn), lambda l, j: (l, 0, j)),
            pl.BlockSpec((1, 1, tn), lambda l, j: (l, 0, j)),
        ],
        out_specs=pl.BlockSpec((1, 8, tn), lambda l, j: (l, 0, j)),
        out_shape=jax.ShapeDtypeStruct((depth, 8, n), F32),
        compiler_params=_params(("arbitrary", "arbitrary")),
        name="adaln_mod",
    )(c_pad, w_ada, b_ada.reshape(depth, 1, n))


def _inproj_kernel(x_ref, sh_ref, sc_ref, g_ref, wrow_ref, wt_ref, c_ref, s_ref, ct_ref, st_ref,
                   glat_ref, wuk_ref, wuvt_ref, wpool_ref, pscale_ref, wbrb_ref,
                   qat_ref, qit_ref, wit_ref, ka_ref, vat_ref, ki_ref, qct_ref, kc_ref, vct_ref,
                   ga_ref, gc_ref, mb_ref,
                   h_scr, ht_scr, ext_scr, prev_scr):
    i = pl.program_id(1)
    tm = x_ref.shape[1]
    x = x_ref[0]
    h = _rms(x, g_ref[...]) * (1.0 + sc_ref[0]) + sh_ref[0]
    h_scr[...] = h.astype(BF16)
    ht_scr[...] = h.T.astype(BF16)

    def rowdot(a, width):
        return jnp.dot(h_scr[...], wrow_ref[:, a:a + width], preferred_element_type=F32)

    def tdot(a, height):
        return jnp.dot(wt_ref[a:a + height, :], ht_scr[...], preferred_element_type=F32)

    cos_r, sin_r = c_ref[0], s_ref[0]
    cos_t, sin_t = ct_ref[0], st_ref[0]

    latn = _rms(rowdot(R_LAT, KV_LATENT), glat_ref[...])
    ka = (rowdot(R_KR, LANES) * cos_r + rowdot(R_KRR, LANES) * sin_r
          + jnp.dot(latn.astype(BF16), wuk_ref[...], preferred_element_type=F32))
    ka_ref[0] = ka[:, :HEAD_DIM].astype(BF16)
    vat = jnp.dot(wuvt_ref[...], latn.T.astype(BF16), preferred_element_type=F32)
    for j in range(tm // VB_DSA):
        vat_ref[0, j] = vat[:, j * VB_DSA:(j + 1) * VB_DSA].astype(BF16)

    ki =rowdot(R_KI, LANES) * cos_r + rowdot(R_KIR, LANES) * sin_r
    ki_ref[0] = ki[:, :IDX_DIM].astype(BF16)

    qa, qar = tdot(T_QA, A_WIDTH), tdot(T_QAR, A_WIDTH)
    for hh in range(N_HEADS_A):
        r = slice(hh * HEAD_DIM, (hh + 1) * HEAD_DIM)
        qat_ref[0, r, :] = (qa[r] * cos_t + qar[r] * sin_t).astype(BF16)
    qi, qir = tdot(T_QI, N_IDX_HEADS * IDX_DIM), tdot(T_QIR, N_IDX_HEADS * IDX_DIM)
    for hh in range(N_IDX_HEADS):
        r = slice(hh * IDX_DIM, (hh + 1) * IDX_DIM)
        qit_ref[0, r, :] = (qi[r] * cos_t + qir[r] * sin_t).astype(BF16)
    wit_ref[0] = tdot(T_WI, T_WI_ROWS)

    qct_ref[0] = tdot(T_QC, 2 * C_WIDTH).astype(BF16)
    kc_ref[0] = rowdot(R_KC, C_WIDTH).astype(BF16)
    vct = tdot(T_VC, C_WIDTH)
    for j in range(tm // QB):
        vct_ref[0, j] = vct[:, j * QB:(j + 1) * QB].astype(BF16)

    up = rowdot(R_UP, POOL_WIDTH)

    @pl.when(i == 0)
    def _():
        prev_scr[...] = jnp.zeros_like(prev_scr)

    ext_scr[0:POOL_HALO, :] = prev_scr[...]
    ext_scr[POOL_HALO:POOL_HALO + tm, :] = up
    prev_scr[...] = up[tm - POOL_HALO:, :]
    lag = [ext_scr[POOL_HALO - j:POOL_HALO - j + tm, :] for j in range(POOL_HALO)]
    sums = {}
    run = lag[0]
    for j in range(1, POOL_HALO):
        run = run + lag[j]
        if j + 1 in POOL_WINDOWS:
            sums[j + 1] = run
    lane = lax.broadcasted_iota(I32, (tm, POOL_WIDTH), 1)
    pos = i * tm + lax.broadcasted_iota(I32, (tm, POOL_WIDTH), 0)
    pooled_sum = sums[POOL_WINDOWS[-1]]
    win = jnp.full((tm, POOL_WIDTH), POOL_WINDOWS[-1], I32)
    for g in range(N_POOL_GROUPS - 2, -1, -1):
        in_group = lane < (g + 1) * POOL_GROUP_DIM
        pooled_sum = jnp.where(in_group, sums[POOL_WINDOWS[g]], pooled_sum)
        win = jnp.where(in_group, POOL_WINDOWS[g], win)
    cnt = jnp.minimum(pos + 1, win).astype(F32)
    pooled = pooled_sum / cnt - up
    yb = jnp.dot(pooled.astype(BF16), wpool_ref[...], preferred_element_type=F32) * pscale_ref[...]

    ga_ref[0] = _sigmoid(rowdot(R_GATE, D_MODEL)).astype(BF16)
    gb = _sigmoid(rowdot(R_GATE + D_MODEL, D_MODEL))
    mb_ref[0] = (gb * jnp.dot(yb.astype(BF16), wbrb_ref[...], preferred_element_type=F32)).astype(BF16)
    gc_ref[0] = _sigmoid(rowdot(R_GATE + 2 * D_MODEL, D_MODEL)).astype(BF16)


def _inproj_call(x, sh, sc, gain, wrow, wt, cos_r, sin_r, cos_t, sin_t, glat, wuk, wuvt, wpool, pscale, wbrb):
    b, l, d = x.shape
    tm = min(TM_IN, l)
    nq = l // QB
    tok = lambda w: pl.BlockSpec((1, tm, w), lambda bi, i: (bi, i, 0))
    feat = lambda hgt: pl.BlockSpec((1, hgt, tm), lambda bi, i: (bi, 0, i))
    blk = lambda hgt, w: pl.BlockSpec((1, tm // w, hgt, w), lambda bi, i: (bi, i, 0, 0))
    full = lambda a: pl.BlockSpec(a.shape, lambda bi, i: (0,) * a.ndim)
    vec = pl.BlockSpec((1, 1, d), lambda bi, i: (bi, 0, 0))
    out_shape = (
        jax.ShapeDtypeStruct((b, A_WIDTH, l), BF16),
        jax.ShapeDtypeStruct((b, N_IDX_HEADS * IDX_DIM, l), BF16),
        jax.ShapeDtypeStruct((b, T_WI_ROWS, l), F32),
        jax.ShapeDtypeStruct((b, l, HEAD_DIM), BF16),
        jax.ShapeDtypeStruct((b, l // VB_DSA, HEAD_DIM, VB_DSA), BF16),
        jax.ShapeDtypeStruct((b, l, IDX_DIM), BF16),
        jax.ShapeDtypeStruct((b, 2 * C_WIDTH, l), BF16),
        jax.ShapeDtypeStruct((b, l, C_WIDTH), BF16),
        jax.ShapeDtypeStruct((b, nq, C_WIDTH, QB), BF16),
        jax.ShapeDtypeStruct((b, l, d), BF16),
        jax.ShapeDtypeStruct((b, l, d), BF16),
        jax.ShapeDtypeStruct((b, l, d), BF16),
    )
    out_specs = (feat(A_WIDTH), feat(N_IDX_HEADS * IDX_DIM), feat(T_WI_ROWS), tok(HEAD_DIM), blk(HEAD_DIM, VB_DSA),
                 tok(IDX_DIM), feat(2 * C_WIDTH), tok(C_WIDTH), blk(C_WIDTH, QB), tok(d), tok(d), tok(d))
    in_specs = [tok(d), vec, vec, full(gain), full(wrow), full(wt), tok(LANES), tok(LANES),
                feat(HEAD_DIM), feat(HEAD_DIM), full(glat), full(wuk), full(wuvt), full(wpool),
                full(pscale), full(wbrb)]
    return pl.pallas_call(
        _inproj_kernel,
        grid=(b, l // tm),
        in_specs=in_specs,
        out_specs=out_specs,
        out_shape=out_shape,
        scratch_shapes=[pltpu.VMEM((tm, d), BF16), pltpu.VMEM((d, tm), BF16),
                        pltpu.VMEM((tm + POOL_HALO, POOL_WIDTH), F32), pltpu.VMEM((POOL_HALO, POOL_WIDTH), F32)],
        compiler_params=_params(("arbitrary", "arbitrary")),
        name="inproj",
    )(x, sh, sc, gain, wrow, wt, cos_r, sin_r, cos_t, sin_t, glat, wuk, wuvt, wpool, pscale, wbrb)


def _dsa_kernel(qit_ref, wit_ref, qat_ref, ki_ref, ka_ref, vat_ref, tri_ref, ya_ref, keys_scr, top_scr, acc_scr, *,
                k_sel):
    i = pl.program_id(1)
    qs = qat_ref.shape[2]
    n_blocks = (i + 1) * (qs // QB)
    n_chunks = (n_blocks * QB + KC - 1) // KC
    qpos = i * qs + lax.broadcasted_iota(I32, (1, qs), 1)
    w_idx = wit_ref[0]

    def score_chunk(c, carry):
        r0 = pl.multiple_of(c * KC, KC)
        kblk = ki_ref[0, pl.ds(r0, KC), :]
        parts = [jnp.dot(kblk, qit_ref[0, hh * IDX_DIM:(hh + 1) * IDX_DIM, :], preferred_element_type=F32)
                 for hh in range(N_IDX_HEADS)]
        score = jnp.zeros((KC, qs), F32)
        for hh in range(N_IDX_HEADS):
            score = score + jnp.maximum(parts[hh], 0.0) * w_idx[hh:hh + 1, :]
        bits = lax.bitcast_convert_type(score, I32)
        key = jnp.where(bits < 0, INT_MIN - bits, bits)
        kpos = r0 + lax.broadcasted_iota(I32, (KC, qs), 0)
        key = jnp.where(kpos <= qpos, key, INT_MIN)
        keys_scr[pl.ds(r0, KC), :] = key
        top_scr[pl.ds(r0, KC), :] = lax.shift_right_arithmetic(key, 16).astype(I16)
        return carry

    lax.fori_loop(0, n_chunks, score_chunk, 0)

    def count_rows(src, rows_per_vreg, trial):
        n_acc = 4
        groups = KC // rows_per_vreg

        def body(c, accs):
            r0 = pl.multiple_of(c * KC, KC)
            rows = src[pl.ds(r0, KC), :].reshape(groups, rows_per_vreg, qs)
            accs = list(accs)
            for j in range(groups):
                a = accs[j % n_acc]
                accs[j % n_acc] = jnp.where(rows[j] >= trial, a + 1, a)
            return tuple(accs)

        zero = jnp.zeros((rows_per_vreg, qs), src.dtype)
        accs = lax.fori_loop(0, n_chunks, body, tuple(zero for _ in range(n_acc)))
        total = (accs[0] + accs[1]) + (accs[2] + accs[3])
        return jnp.sum(total.astype(I32), axis=0, keepdims=True)

    def count_ge(trial):
        return count_rows(keys_scr, 8, trial)

    def count_ge_top(trial):
        return count_rows(top_scr, 16, lax.shift_right_arithmetic(trial, 16).astype(I16))

    c_zero = count_ge(jnp.zeros((1, qs), I32))
    c_pos = count_ge(jnp.ones((1, qs), I32))
    nonneg = c_zero >= k_sel
    tie_at_zero = jnp.logical_and(nonneg, c_pos < k_sel)
    ans0 = jnp.where(nonneg, 0, INT_MIN)
    c_ans0 = jnp.where(nonneg, c_zero, jnp.int32(2 ** 30))

    def unsettled(c_ans):
        settled = jnp.logical_or(tie_at_zero, c_ans == k_sel)
        return jnp.max(jnp.where(settled, 0, 1))

    def refine(counter, carry):
        bit, ans, c_ans, _ = carry
        trial = ans + lax.shift_left(jnp.int32(1), bit)
        c = counter(trial)
        ok = c >= k_sel
        c_ans = jnp.where(ok, c, c_ans)
        return bit - 1, jnp.where(ok, trial, ans), c_ans, unsettled(c_ans)

    state = (jnp.int32(30), ans0, c_ans0, unsettled(c_ans0))
    state = lax.while_loop(lambda cr: jnp.logical_and(cr[0] >= 16, cr[3] > 0),
                           functools.partial(refine, count_ge_top), state)
    state = lax.while_loop(lambda cr: jnp.logical_and(cr[0] >= 0, cr[3] > 0),
                           functools.partial(refine, count_ge), (jnp.int32(15),) + state[1:])
    thr = state[1]
    n_above = count_ge(thr + 1)
    n_ties = jnp.where(thr == INT_MIN, 0, k_sel - n_above).astype(F32)

    acc_scr[...] = jnp.zeros_like(acc_scr)
    heads = range(N_HEADS_A)

    kb_rows = tri_ref.shape[0]
    vb_rows = vat_ref.shape[3]
    v_per_k = kb_rows // vb_rows

    def attend(kb, carry):
        seen, ms, ls = carry
        r0 = pl.multiple_of(kb * kb_rows, kb_rows)
        keyb = keys_scr[pl.ds(r0, kb_rows), :]
        tied = keyb == thr
        tied_f = jnp.where(tied, 1.0, 0.0)
        rank = seen + jnp.dot(tri_ref[...], tied_f.astype(BF16), preferred_element_type=F32)
        keep = jnp.logical_or(keyb > thr, jnp.logical_and(tied, rank <= n_ties))
        bias = jnp.where(keep, 0.0, NEG_MASK)
        seen = seen + jnp.sum(tied_f, axis=0, keepdims=True)
        kblk = ka_ref[0, pl.ds(r0, kb_rows), :]
        logits = [jnp.dot(kblk, qat_ref[0, hh * HEAD_DIM:(hh + 1) * HEAD_DIM, :], preferred_element_type=F32) + bias
                  for hh in heads]
        new_ms = [jnp.maximum(ms[hh], jnp.max(logits[hh], axis=0, keepdims=True)) for hh in heads]
        probs = [jnp.exp2(logits[hh] - new_ms[hh]) for hh in heads]
        alphas = [jnp.exp2(ms[hh] - new_ms[hh]) for hh in heads]
        new_ls = [alphas[hh] * ls[hh] + jnp.sum(probs[hh], axis=0, keepdims=True) for hh in heads]
        outs = []
        for hh in heads:
            weights = probs[hh].astype(BF16)
            out = jnp.dot(vat_ref[0, kb * v_per_k], weights[:vb_rows, :], preferred_element_type=F32)
            for j in range(1, v_per_k):
                out = out + jnp.dot(vat_ref[0, kb * v_per_k + j], weights[j * vb_rows:(j + 1) * vb_rows, :],
                                    preferred_element_type=F32)
            outs.append(out)
        for hh in heads:
            r = slice(hh * HEAD_DIM, (hh + 1) * HEAD_DIM)
            acc_scr[r, :] = acc_scr[r, :] * alphas[hh] + outs[hh]
        return seen, tuple(new_ms), tuple(new_ls)

    zero = jnp.zeros((1, qs), F32)
    init = (zero, tuple(jnp.full((1, qs), NEG_BIG, F32) for _ in heads), tuple(zero for _ in heads))
    _, _, ls = lax.fori_loop(0, ((i + 1) * qs + kb_rows - 1) // kb_rows, attend, init)
    for hh in heads:
        r = slice(hh * HEAD_DIM, (hh + 1) * HEAD_DIM)
        acc_scr[r, :] = acc_scr[r, :] / ls[hh]
    ya_ref[0] = acc_scr[...].T.astype(BF16)


def _dsa_call(qit, wit, qat, ki, ka, vat, tri):
    b, l, _ = ki.shape
    nq = l // QB
    qs = min(QS_DSA, l)
    k_sel = min(TOPK_MAX, l // 4)
    return pl.pallas_call(
        functools.partial(_dsa_kernel, k_sel=k_sel),
        grid=(b, l // qs),
        in_specs=[
            pl.BlockSpec((1, N_IDX_HEADS * IDX_DIM, qs), lambda bi, i: (bi, 0, i)),
            pl.BlockSpec((1, T_WI_ROWS, qs), lambda bi, i: (bi, 0, i)),
            pl.BlockSpec((1, A_WIDTH, qs), lambda bi, i: (bi, 0, i)),
            pl.BlockSpec((1, l, IDX_DIM), lambda bi, i: (bi, 0, 0)),
            pl.BlockSpec((1, l, HEAD_DIM), lambda bi, i: (bi, 0, 0)),
            pl.BlockSpec((1, l // VB_DSA, HEAD_DIM, VB_DSA), lambda bi, i: (bi, 0, 0, 0)),
            pl.BlockSpec((KB_DSA, KB_DSA), lambda bi, i: (0, 0)),
        ],
        out_specs=pl.BlockSpec((1, qs, A_WIDTH), lambda bi, i: (bi, i, 0)),
        out_shape=jax.ShapeDtypeStruct((b, l, A_WIDTH), BF16),
        scratch_shapes=[pltpu.VMEM((l, qs), I32), pltpu.VMEM((l, qs), I16), pltpu.VMEM((A_WIDTH, qs), F32)],
        compiler_params=_params(("arbitrary", "arbitrary")),
        name="dsa_attention",
    )(qit, wit, qat, ki, ka, vat, tri)


def _stick_kernel(qct_ref, kc_ref, vct_ref, later_ref, yc_ref, acc_scr):
    i = pl.program_id(1)
    qs = qct_ref.shape[2]
    diag_blocks = qs // QB
    qpos = i * qs + lax.broadcasted_iota(I32, (1, qs), 1)
    acc_scr[...] = jnp.zeros_like(acc_scr)

    def block(kb, tail, masked):
        r0 = pl.multiple_of(kb * QB, QB)
        kfull = kc_ref[0, pl.ds(r0, QB), :]
        vt = vct_ref[0, kb]
        heads = range(N_HEADS_C)
        if masked:
            mask = (r0 + lax.broadcasted_iota(I32, (QB, qs), 0)) < qpos
        zs = [jnp.dot(kfull[:, (hh // 2) * LANES:(hh // 2 + 1) * LANES], qct_ref[0, hh * LANES:(hh + 1) * LANES, :],
                      preferred_element_type=F32) for hh in heads]
        log_betas, splits, new_tail = [], [], []
        for hh in heads:
            z = zs[hh]
            log_beta = jnp.minimum(z, 0.0) - jnp.log(1.0 + jnp.exp(-jnp.abs(z)))
            log_keep = log_beta - z
            if masked:
                log_keep = jnp.where(mask, log_keep, 0.0)
            hi = log_keep.astype(BF16)
            lo = (log_keep - hi.astype(F32)).astype(BF16)
            log_betas.append(log_beta)
            splits.append(jnp.concatenate([hi, lo], axis=0))
            new_tail.append(tail[hh] + jnp.sum(log_keep, axis=0, keepdims=True))
        withins = [jnp.dot(later_ref[...], splits[hh], preferred_element_type=F32) for hh in heads]
        weights = []
        for hh in heads:
            a = jnp.exp(log_betas[hh] + withins[hh] + tail[hh])
            if masked:
                a = jnp.where(mask, a, 0.0)
            weights.append(a.astype(BF16))
        for hh in heads:
            r = slice(hh * HEAD_DIM, (hh + 1) * HEAD_DIM)
            acc_scr[r, :] = acc_scr[r, :] + jnp.dot(vt[r, :], weights[hh], preferred_element_type=F32)
        return tuple(new_tail)

    def largest(tail):
        worst = tail[0]
        for hh in range(1, N_HEADS_C):
            worst = jnp.maximum(worst, tail[hh])
        return jnp.max(worst)

    first = (i + 1) * diag_blocks - 1
    n_rest = i * diag_blocks
    tail = tuple(jnp.zeros((1, qs), F32) for _ in range(N_HEADS_C))
    tail = lax.fori_loop(0, diag_blocks, lambda t, tl: block(first - t, tl, True), tail)

    def live(cr):
        return jnp.logical_and(cr[0] < n_rest, cr[2] > STICK_DEAD)

    def step(cr):
        t, tl, _ = cr
        tl = block(n_rest - 1 - t, tl, False)
        return t + 1, tl, largest(tl)

    lax.while_loop(live, step, (jnp.int32(0), tail, largest(tail)))
    yc_ref[0] = acc_scr[...].T.astype(BF16)


def _stick_call(qct, kc, vct, later2):
    b, l, _ = kc.shape
    nq = l // QB
    qs = min(QS_STICK, l)
    return pl.pallas_call(
        _stick_kernel,
        grid=(b, l // qs),
        in_specs=[
            pl.BlockSpec((1, 2 * C_WIDTH, qs), lambda bi, i: (bi, 0, i)),
            pl.BlockSpec((1, l, C_WIDTH), lambda bi, i: (bi, 0, 0)),
            pl.BlockSpec((1, nq, C_WIDTH, QB), lambda bi, i: (bi, 0, 0, 0)),
            pl.BlockSpec((QB, 2 * QB), lambda bi, i: (0, 0)),
        ],
        out_specs=pl.BlockSpec((1, qs, C_WIDTH), lambda bi, i: (bi, i, 0)),
        out_shape=jax.ShapeDtypeStruct((b, l, C_WIDTH), BF16),
        scratch_shapes=[pltpu.VMEM((C_WIDTH, qs), F32)],
        compiler_params=_params(("arbitrary", "arbitrary")),
        name="stick_attention",
    )(qct, kc, vct, later2)


def _merge_kernel(x_ref, ya_ref, yc_ref, mb_ref, ga_ref, gc_ref, gt_ref, sh_ref, sc_ref, g1_ref, g2_ref,
                  wbra_ref, wbrc_ref, wout_ref, *rest, with_router):
    if with_router:
        wr_ref, x1_ref, h2_ref, comb_ref = rest
    else:
        x1_ref, h2_ref = rest
    merged = (ga_ref[0].astype(F32) * jnp.dot(ya_ref[0], wbra_ref[...], preferred_element_type=F32)
              + mb_ref[0].astype(F32)
              + gc_ref[0].astype(F32) * jnp.dot(yc_ref[0], wbrc_ref[...], preferred_element_type=F32))
    y = jnp.dot(merged.astype(BF16), wout_ref[...], preferred_element_type=F32)
    x1 = x_ref[0] + gt_ref[0] * _rms(y, g1_ref[...])
    x1_ref[0] = x1
    h2 = _rms(x1, g2_ref[...]) * (1.0 + sc_ref[0]) + sh_ref[0]
    h2_ref[0] = h2.astype(BF16)
    if with_router:
        logits = jnp.dot(h2, wr_ref[...], preferred_element_type=F32)
        lane = lax.broadcasted_iota(I32, logits.shape, 1)
        valid = lane < N_EXPERTS
        l1 = jnp.where(valid, logits, -jnp.inf)
        v1 = jnp.max(l1, axis=-1, keepdims=True)
        i1 = jnp.min(jnp.where(l1 == v1, lane, LANES), axis=-1, keepdims=True)
        l2 = jnp.where(lane == i1, -jnp.inf, l1)
        v2 = jnp.max(l2, axis=-1, keepdims=True)
        i2 = jnp.min(jnp.where(l2 == v2, lane, LANES), axis=-1, keepdims=True)
        e2 = jnp.exp(v2 - v1)
        p1 = 1.0 / (1.0 + e2)
        p2 = e2 / (1.0 + e2)
        comb_ref[0] = jnp.where(lane == i1, p1, 0.0) + jnp.where(lane == i2, p2, 0.0)


def _merge_call(x, ya, yc, mb, ga, gc, gt, sh, sc, g1, g2, wbra, wbrc, wout, wr=None):
    b, l, d = x.shape
    tm = min(TM_MERGE, l)
    tok = lambda w: pl.BlockSpec((1, tm, w), lambda bi, i: (bi, i, 0))
    full = lambda a: pl.BlockSpec(a.shape, lambda bi, i: (0,) * a.ndim)
    vec = pl.BlockSpec((1, 1, d), lambda bi, i: (bi, 0, 0))
    in_specs = [tok(d), tok(A_WIDTH), tok(C_WIDTH), tok(d), tok(d), tok(d), vec, vec, vec,
                full(g1), full(g2), full(wbra), full(wbrc), full(wout)]
    args = [x, ya, yc, mb, ga, gc, gt, sh, sc, g1, g2, wbra, wbrc, wout]
    out_shape = [jax.ShapeDtypeStruct((b, l, d), F32), jax.ShapeDtypeStruct((b, l, d), BF16)]
    out_specs = [tok(d), tok(d)]
    if wr is not None:
        in_specs.append(full(wr))
        args.append(wr)
        out_shape.append(jax.ShapeDtypeStruct((b, l, LANES), F32))
        out_specs.append(tok(LANES))
    return pl.pallas_call(
        functools.partial(_merge_kernel, with_router=wr is not None),
        grid=(b, l // tm),
        in_specs=in_specs,
        out_specs=out_specs,
        out_shape=out_shape,
        compiler_params=_params(("arbitrary", "arbitrary")),
        name="merge_router" if wr is not None else "merge",
    )(*args)


def _ffn_kernel(x1_ref, h2_ref, gt_ref, g3_ref, wg_ref, wu_ref, wd_ref, o_ref):
    h2 = h2_ref[0]
    gate = jnp.dot(h2, wg_ref[...], preferred_element_type=F32)
    up = jnp.dot(h2, wu_ref[...], preferred_element_type=F32)
    act = (gate * _sigmoid(gate) * up).astype(BF16)
    y = jnp.dot(act, wd_ref[...], preferred_element_type=F32)
    o_ref[0] = x1_ref[0] + gt_ref[0] * _rms(y, g3_ref[...])


def _ffn_call(x1, h2, gt, g3, wg, wu, wd):
    b, l, d = x1.shape
    tm = min(TM_FFN, l)
    tok = pl.BlockSpec((1, tm, d), lambda bi, i: (bi, i, 0))
    full = lambda a: pl.BlockSpec(a.shape, lambda bi, i: (0,) * a.ndim)
    vec = pl.BlockSpec((1, 1, d), lambda bi, i: (bi, 0, 0))
    return pl.pallas_call(
        _ffn_kernel,
        grid=(b, l // tm),
        in_specs=[tok, tok, vec, full(g3), full(wg), full(wu), full(wd)],
        out_specs=tok,
        out_shape=jax.ShapeDtypeStruct((b, l, d), F32),
        compiler_params=_params(("arbitrary", "arbitrary")),
        name="ffn_dense",
    )(x1, h2, gt, g3, wg, wu, wd)


def _moe_kernel(x1_ref, h2_ref, comb_ref, gt_ref, g3_ref, wg_ref, wu_ref, wd_ref, o_ref, acc_scr):
    e = pl.program_id(2)

    @pl.when(e == 0)
    def _():
        acc_scr[...] = jnp.zeros_like(acc_scr)

    h2 = h2_ref[0]
    gate = jnp.dot(h2, wg_ref[0], preferred_element_type=F32)
    up = jnp.dot(h2, wu_ref[0], preferred_element_type=F32)
    act = (gate * _sigmoid(gate) * up).astype(BF16)
    y = jnp.dot(act, wd_ref[0], preferred_element_type=F32)
    comb = comb_ref[0]
    lane = lax.broadcasted_iota(I32, comb.shape, 1)
    weight = jnp.sum(jnp.where(lane == e, comb, 0.0), axis=-1, keepdims=True)
    acc_scr[...] = acc_scr[...] + weight * y

    @pl.when(e == pl.num_programs(2) - 1)
    def _():
        o_ref[0] = x1_ref[0] + gt_ref[0] * _rms(acc_scr[...], g3_ref[...])


def _moe_call(x1, h2, comb, gt, g3, wg, wu, wd):
    b, l, d = x1.shape
    tm = min(TM_FFN, l)
    n_e, _, ff = wg.shape
    tok = lambda w: pl.BlockSpec((1, tm, w), lambda bi, i, e: (bi, i, 0))
    vec = pl.BlockSpec((1, 1, d), lambda bi, i, e: (bi, 0, 0))
    return pl.pallas_call(
        _moe_kernel,
        grid=(b, l // tm, n_e),
        in_specs=[tok(d), tok(d), tok(LANES), vec, pl.BlockSpec(g3.shape, lambda bi, i, e: (0, 0)),
                  pl.BlockSpec((1, d, ff), lambda bi, i, e: (e, 0, 0)),
                  pl.BlockSpec((1, d, ff), lambda bi, i, e: (e, 0, 0)),
                  pl.BlockSpec((1, ff, d), lambda bi, i, e: (e, 0, 0))],
        out_specs=tok(d),
        out_shape=jax.ShapeDtypeStruct((b, l, d), F32),
        scratch_shapes=[pltpu.VMEM((tm, d), F32)],
        compiler_params=_params(("arbitrary", "arbitrary", "arbitrary")),
        name="ffn_moe",
    )(x1, h2, comb, gt, g3, wg, wu, wd)


def _rot_cols(w, n_heads, head_dim):
    d = w.shape[0]
    w = w.reshape(d, n_heads, head_dim)
    half = ROPE_DIM // 2
    rot = jnp.concatenate([-w[..., half:ROPE_DIM], w[..., :half],
                           jnp.zeros((d, n_heads, head_dim - ROPE_DIM), w.dtype)], axis=-1)
    return rot.reshape(d, n_heads * head_dim)


def _pad_cols(w, width):
    return jnp.pad(w, ((0, 0), (0, width - w.shape[1])))


def _layer_weights(w_in, w_uk, w_uv, w_pool):
    d = w_in.shape[0]
    offs, o = [], 0
    for s in IN_SIZES:
        offs.append(o)
        o += s
    piece = lambda k: w_in[:, offs[k]:offs[k] + IN_SIZES[k]]
    w_qa, w_lat, w_kr, w_qi, w_ki, w_wi, w_up, w_qc, w_kc, w_vc, w_gate = [piece(k) for k in range(len(IN_SIZES))]
    attn_scale = HEAD_DIM ** -0.5
    idx_scale = IDX_DIM ** -0.5 * N_IDX_HEADS ** -0.5
    wrow = jnp.concatenate([
        w_lat,
        _pad_cols(w_kr, LANES), _pad_cols(_rot_cols(w_kr, 1, ROPE_DIM), LANES),
        _pad_cols(w_ki, LANES), _pad_cols(_rot_cols(w_ki, 1, IDX_DIM), LANES),
        w_up, w_kc, w_gate], axis=1).astype(BF16)
    w_qc_t = (w_qc * attn_scale).T.reshape(N_HEADS_C, HEAD_DIM, d)
    zeros = jnp.zeros_like(w_qc_t)
    even = (jnp.arange(N_HEADS_C) % 2 == 0)[:, None, None]
    w_qc_pad = jnp.concatenate([jnp.where(even, w_qc_t, zeros), jnp.where(even, zeros, w_qc_t)], axis=1)
    w_qc_pad = w_qc_pad.reshape(2 * C_WIDTH, d)
    softmax_scale = attn_scale * LOG2_E
    wt = jnp.concatenate([
        (w_qa * softmax_scale).T, (_rot_cols(w_qa, N_HEADS_A, HEAD_DIM) * softmax_scale).T,
        w_qi.T, _rot_cols(w_qi, N_IDX_HEADS, IDX_DIM).T,
        w_qc_pad, w_vc.T,
        jnp.pad((w_wi * idx_scale).T, ((0, T_WI_ROWS - N_IDX_HEADS), (0, 0)))], axis=0).astype(BF16)
    wuk = jnp.zeros((KV_LATENT, LANES), F32).at[:, ROPE_DIM:HEAD_DIM].set(w_uk).astype(BF16)
    wuvt = w_uv.T.astype(BF16)
    wpool = jnp.zeros((POOL_WIDTH, POOL_WIDTH), F32)
    for g in range(N_POOL_GROUPS):
        sl = slice(g * POOL_GROUP_DIM, (g + 1) * POOL_GROUP_DIM)
        wpool = wpool.at[sl, sl].set(w_pool[g])
    return wrow, wt, wuk, wuvt, wpool.astype(BF16)


def _rope_tables(positions):
    inv = ROPE_THETA ** (-jnp.arange(0, ROPE_DIM, 2, dtype=F32) / ROPE_DIM)
    ang = positions.astype(F32)[..., None] * inv
    cos, sin = jnp.cos(ang), jnp.sin(ang)
    b, l = positions.shape
    ones = jnp.ones((b, l, HEAD_DIM - ROPE_DIM), F32)
    cos_h = jnp.concatenate([cos, cos, ones], axis=-1)
    sin_h = jnp.concatenate([sin, sin, jnp.zeros_like(ones)], axis=-1)
    cos_r = jnp.concatenate([cos_h, cos_h], axis=-1)
    sin_r = jnp.concatenate([sin_h, sin_h], axis=-1)
    return cos_r, sin_r, cos_h.transpose(0, 2, 1), sin_h.transpose(0, 2, 1)


def kernel(x, c, positions, w_ada, b_ada, norm_gains, w_in, g_kv_latent, w_uk, w_uv, w_pool, pool_scale,
           w_br_a, w_br_b, w_br_c, w_out, w_gate_dense, w_up_dense, w_down_dense,
           w_router, w_gate_moe, w_up_moe, w_down_moe):
    b, l, d = x.shape
    depth = w_in.shape[0]
    assert d == D_MODEL and l % KC == 0 and l % TM_FFN == 0
    cos_r, sin_r, cos_t, sin_t = _rope_tables(positions)
    c_pad = jnp.pad(c, ((0, 8 - b), (0, 0)))
    mod = _mod_call(c_pad, w_ada, b_ada)[:, :b]
    idx = lax.broadcasted_iota(I32, (KB_DSA, KB_DSA), 0)
    jdx = lax.broadcasted_iota(I32, (KB_DSA, KB_DSA), 1)
    tri = (jdx <= idx).astype(BF16)
    later = (jdx > idx)[:QB, :QB].astype(BF16)
    later2 = jnp.concatenate([later, later], axis=1)
    for layer in range(depth):
        sh1, sc1, gt1, sh2, sc2, gt2 = [m.reshape(b, 1, d) for m in jnp.split(mod[layer], 6, axis=-1)]
        gains = norm_gains[layer].reshape(4, 1, d)
        wrow, wt, wuk, wuvt, wpool = _layer_weights(w_in[layer], w_uk[layer], w_uv[layer], w_pool[layer])
        (qat, qit, wit, ka, vat, ki, qct, kc, vct, ga, gc, mb) = _inproj_call(
            x, sh1, sc1, gains[0], wrow, wt, cos_r, sin_r, cos_t, sin_t,
            g_kv_latent[layer].reshape(1, KV_LATENT), wuk, wuvt, wpool,
            pool_scale[layer].reshape(1, POOL_WIDTH), w_br_b[layer].astype(BF16))
        ya = _dsa_call(qit, wit, qat, ki, ka, vat, tri)
        yc = _stick_call(qct, kc, vct, later2)
        i = layer // 2
        merge_args = (x, ya, yc, mb, ga, gc, gt1, sh2, sc2, gains[1], gains[2],
                      w_br_a[layer].astype(BF16), w_br_c[layer].astype(BF16), w_out[layer].astype(BF16))
        if layer % 2 == 0:
            x1, h2 = _merge_call(*merge_args)
            x = _ffn_call(x1, h2, gt2, gains[3], w_gate_dense[i].astype(BF16), w_up_dense[i].astype(BF16),
                          w_down_dense[i].astype(BF16))
        else:
            x1, h2, comb = _merge_call(*merge_args, wr=_pad_cols(w_router[i], LANES))
            x = _moe_call(x1, h2, comb, gt2, gains[3], w_gate_moe[i].astype(BF16), w_up_moe[i].astype(BF16),
                          w_down_moe[i].astype(BF16))
    return x
```

```python
import functools

import jax
import jax.numpy as jnp
from jax import lax
from jax.experimental import pallas as pl
from jax.experimental.pallas import tpu as pltpu

F32 = jnp.float32
BF16 = jnp.bfloat16
I32 = jnp.int32
I16 = jnp.int16

D_MODEL = 1024
HEAD_DIM = 64
ROPE_DIM = HEAD_DIM // 4
NOPE_DIM = HEAD_DIM - ROPE_DIM
ROPE_THETA = 500000.0
N_HEADS_A = (3 * D_MODEL // 8) // HEAD_DIM
A_WIDTH = N_HEADS_A * HEAD_DIM
KV_LATENT = D_MODEL // 8
N_IDX_HEADS = 4
IDX_DIM = 64
TOPK_MAX = 256
N_POOL_GROUPS = 4
POOL_WINDOWS = (2, 4, 8, 16)
POOL_WIDTH = D_MODEL // 4
POOL_GROUP_DIM = POOL_WIDTH // N_POOL_GROUPS
N_HEADS_C = (D_MODEL // 4) // HEAD_DIM
C_WIDTH = N_HEADS_C * HEAD_DIM
N_BRANCHES = 3
IN_SIZES = (A_WIDTH, KV_LATENT, ROPE_DIM, N_IDX_HEADS * IDX_DIM, IDX_DIM, N_IDX_HEADS,
            POOL_WIDTH, C_WIDTH, C_WIDTH, C_WIDTH, N_BRANCHES * D_MODEL)
D_FF = 2816
N_EXPERTS = 8
TOP_K = 2
D_FF_EXPERT = D_FF // TOP_K
RMS_EPS = 1e-6

LANES = 128
QB = 128
KC = 512
QS_STICK = 512
QS_DSA = 256
KB_DSA = 512
VB_DSA = 256
STICK_DEAD = -106.0
TM_IN = 256
TM_MERGE = 256
TM_FFN = 512
POOL_HALO = 16
VMEM_LIMIT = 56 * 1024 * 1024
INT_MIN = -2147483648
LOG2_E = 1.4426950408889634
NEG_BIG = -1e30
NEG_MASK = -2e30

R_LAT, R_KR, R_KRR, R_KI, R_KIR, R_UP, R_KC, R_GATE = 0, 128, 256, 384, 512, 640, 896, 1152
R_WIDTH = R_GATE + N_BRANCHES * D_MODEL
T_QA, T_QAR, T_QI, T_QIR, T_QC, T_VC, T_WI = 0, 384, 768, 1024, 1280, 1792, 2048
T_WI_ROWS = 16
T_HEIGHT = T_WI + T_WI_ROWS


def _params(sem):
    return pltpu.CompilerParams(dimension_semantics=sem, vmem_limit_bytes=VMEM_LIMIT)


def _sigmoid(v):
    return 1.0 / (1.0 + jnp.exp(-v))


def _rms(v, gain):
    return v * lax.rsqrt(jnp.mean(v * v, axis=-1, keepdims=True) + RMS_EPS) * gain


def _mod_kernel(c_ref, w_ref, b_ref, o_ref):
    c = c_ref[...]
    cond = c * _sigmoid(c)
    o_ref[0] = jnp.dot(cond, w_ref[0], preferred_element_type=F32) + b_ref[0]


def _mod_call(c_pad, w_ada, b_ada):
    depth, d, n = w_ada.shape
    tn = 1024
    return pl.pallas_call(
        _mod_kernel,
        grid=(depth, n // tn),
        in_specs=[
            pl.BlockSpec((8, d), lambda l, j: (0, 0)),
            pl.BlockSpec((1, d, tn), lambda l, j: (l, 0, j)),
            pl.BlockSpec((1, 1, tn), lambda l, j: (l, 0, j)),
        ],
        out_specs=pl.BlockSpec((1, 8, tn), lambda l, j: (l, 0, j)),
        out_shape=jax.ShapeDtypeStruct((depth, 8, n), F32),
        compiler_params=_params(("arbitrary", "arbitrary")),
        name="adaln_mod",
    )(c_pad, w_ada, b_ada.reshape(depth, 1, n))


def _inproj_kernel(x_ref, sh_ref, sc_ref, g_ref, wrow_ref, wt_ref, c_ref, s_ref, ct_ref, st_ref,
                   glat_ref, wuk_ref, wuvt_ref, wpool_ref, pscale_ref, wbrb_ref,
                   qat_ref, qit_ref, wit_ref, ka_ref, vat_ref, ki_ref, qct_ref, kc_ref, vct_ref,
                   ga_ref, gc_ref, mb_ref,
                   h_scr, ht_scr, ext_scr, prev_scr):
    i = pl.program_id(1)
    tm = x_ref.shape[1]
    x = x_ref[0]
    h = _rms(x, g_ref[...]) * (1.0 + sc_ref[0]) + sh_ref[0]
    h_scr[...] = h.astype(BF16)
    ht_scr[...] = h.T.astype(BF16)

    def rowdot(a, width):
        return jnp.dot(h_scr[...], wrow_ref[:, a:a + width], preferred_element_type=F32)

    def tdot(a, height):
        return jnp.dot(wt_ref[a:a + height, :], ht_scr[...], preferred_element_type=F32)

    cos_r, sin_r = c_ref[0], s_ref[0]
    cos_t, sin_t = ct_ref[0], st_ref[0]

    latn = _rms(rowdot(R_LAT, KV_LATENT), glat_ref[...])
    ka = (rowdot(R_KR, LANES) * cos_r + rowdot(R_KRR, LANES) * sin_r
          + jnp.dot(latn.astype(BF16), wuk_ref[...], preferred_element_type=F32))
    ka_ref[0] = ka[:, :HEAD_DIM].astype(BF16)
    vat = jnp.dot(wuvt_ref[...], latn.T.astype(BF16), preferred_element_type=F32)
    for j in range(tm // VB_DSA):
        vat_ref[0, j] = vat[:, j * VB_DSA:(j + 1) * VB_DSA].astype(BF16)

    ki =rowdot(R_KI, LANES) * cos_r + rowdot(R_KIR, LANES) * sin_r
    ki_ref[0] = ki[:, :IDX_DIM].astype(BF16)

    qa, qar = tdot(T_QA, A_WIDTH), tdot(T_QAR, A_WIDTH)
    for hh in range(N_HEADS_A):
        r = slice(hh * HEAD_DIM, (hh + 1) * HEAD_DIM)
        qat_ref[0, r, :] = (qa[r] * cos_t + qar[r] * sin_t).astype(BF16)
    qi, qir = tdot(T_QI, N_IDX_HEADS * IDX_DIM), tdot(T_QIR, N_IDX_HEADS * IDX_DIM)
    for hh in range(N_IDX_HEADS):
        r = slice(hh * IDX_DIM, (hh + 1) * IDX_DIM)
        qit_ref[0, r, :] = (qi[r] * cos_t + qir[r] * sin_t).astype(BF16)
    wit_ref[0] = tdot(T_WI, T_WI_ROWS)

    qct_ref[0] = tdot(T_QC, 2 * C_WIDTH).astype(BF16)
    kc_ref[0] = rowdot(R_KC, C_WIDTH).astype(BF16)
    vct = tdot(T_VC, C_WIDTH)
    for j in range(tm // QB):
        vct_ref[0, j] = vct[:, j * QB:(j + 1) * QB].astype(BF16)

    up = rowdot(R_UP, POOL_WIDTH)

    @pl.when(i == 0)
    def _():
        prev_scr[...] = jnp.zeros_like(prev_scr)

    ext_scr[0:POOL_HALO, :] = prev_scr[...]
    ext_scr[POOL_HALO:POOL_HALO + tm, :] = up
    prev_scr[...] = up[tm - POOL_HALO:, :]
    lag = [ext_scr[POOL_HALO - j:POOL_HALO - j + tm, :] for j in range(POOL_HALO)]
    sums = {}
    run = lag[0]
    for j in range(1, POOL_HALO):
        run = run + lag[j]
        if j + 1 in POOL_WINDOWS:
            sums[j + 1] = run
    lane = lax.broadcasted_iota(I32, (tm, POOL_WIDTH), 1)
    pos = i * tm + lax.broadcasted_iota(I32, (tm, POOL_WIDTH), 0)
    pooled_sum = sums[POOL_WINDOWS[-1]]
    win = jnp.full((tm, POOL_WIDTH), POOL_WINDOWS[-1], I32)
    for g in range(N_POOL_GROUPS - 2, -1, -1):
        in_group = lane < (g + 1) * POOL_GROUP_DIM
        pooled_sum = jnp.where(in_group, sums[POOL_WINDOWS[g]], pooled_sum)
        win = jnp.where(in_group, POOL_WINDOWS[g], win)
    cnt = jnp.minimum(pos + 1, win).astype(F32)
    pooled = pooled_sum / cnt - up
    yb = jnp.dot(pooled.astype(BF16), wpool_ref[...], preferred_element_type=F32) * pscale_ref[...]

    ga_ref[0] = _sigmoid(rowdot(R_GATE, D_MODEL)).astype(BF16)
    gb = _sigmoid(rowdot(R_GATE + D_MODEL, D_MODEL))
    mb_ref[0] = (gb * jnp.dot(yb.astype(BF16), wbrb_ref[...], preferred_element_type=F32)).astype(BF16)
    gc_ref[0] = _sigmoid(rowdot(R_GATE + 2 * D_MODEL, D_MODEL)).astype(BF16)


def _inproj_call(x, sh, sc, gain, wrow, wt, cos_r, sin_r, cos_t, sin_t, glat, wuk, wuvt, wpool, pscale, wbrb):
    b, l, d = x.shape
    tm = min(TM_IN, l)
    nq = l // QB
    tok = lambda w: pl.BlockSpec((1, tm, w), lambda bi, i: (bi, i, 0))
    feat = lambda hgt: pl.BlockSpec((1, hgt, tm), lambda bi, i: (bi, 0, i))
    blk = lambda hgt, w: pl.BlockSpec((1, tm // w, hgt, w), lambda bi, i: (bi, i, 0, 0))
    full = lambda a: pl.BlockSpec(a.shape, lambda bi, i: (0,) * a.ndim)
    vec = pl.BlockSpec((1, 1, d), lambda bi, i: (bi, 0, 0))
    out_shape = (
        jax.ShapeDtypeStruct((b, A_WIDTH, l), BF16),
        jax.ShapeDtypeStruct((b, N_IDX_HEADS * IDX_DIM, l), BF16),
        jax.ShapeDtypeStruct((b, T_WI_ROWS, l), F32),
        jax.ShapeDtypeStruct((b, l, HEAD_DIM), BF16),
        jax.ShapeDtypeStruct((b, l // VB_DSA, HEAD_DIM, VB_DSA), BF16),
        jax.ShapeDtypeStruct((b, l, IDX_DIM), BF16),
        jax.ShapeDtypeStruct((b, 2 * C_WIDTH, l), BF16),
        jax.ShapeDtypeStruct((b, l, C_WIDTH), BF16),
        jax.ShapeDtypeStruct((b, nq, C_WIDTH, QB), BF16),
        jax.ShapeDtypeStruct((b, l, d), BF16),
        jax.ShapeDtypeStruct((b, l, d), BF16),
        jax.ShapeDtypeStruct((b, l, d), BF16),
    )
    out_specs = (feat(A_WIDTH), feat(N_IDX_HEADS * IDX_DIM), feat(T_WI_ROWS), tok(HEAD_DIM), blk(HEAD_DIM, VB_DSA),
                 tok(IDX_DIM), feat(2 * C_WIDTH), tok(C_WIDTH), blk(C_WIDTH, QB), tok(d), tok(d), tok(d))
    in_specs = [tok(d), vec, vec, full(gain), full(wrow), full(wt), tok(LANES), tok(LANES),
                feat(HEAD_DIM), feat(HEAD_DIM), full(glat), full(wuk), full(wuvt), full(wpool),
                full(pscale), full(wbrb)]
    return pl.pallas_call(
        _inproj_kernel,
        grid=(b, l // tm),
        in_specs=in_specs,
        out_specs=out_specs,
        out_shape=out_shape,
        scratch_shapes=[pltpu.VMEM((tm, d), BF16), pltpu.VMEM((d, tm), BF16),
                        pltpu.VMEM((tm + POOL_HALO, POOL_WIDTH), F32), pltpu.VMEM((POOL_HALO, POOL_WIDTH), F32)],
        compiler_params=_params(("arbitrary", "arbitrary")),
        name="inproj",
    )(x, sh, sc, gain, wrow, wt, cos_r, sin_r, cos_t, sin_t, glat, wuk, wuvt, wpool, pscale, wbrb)


def _dsa_kernel(qit_ref, wit_ref, qat_ref, ki_ref, ka_ref, vat_ref, tri_ref, ya_ref,
                keys_scr, top_scr, acc_scr, *, k_sel):
    i = pl.program_id(1)
    qs = qat_ref.shape[2]
    n_blocks = (i + 1) * (qs // QB)
    n_chunks = (n_blocks * QB + KC - 1) // KC
    qpos = i * qs + lax.broadcasted_iota(I32, (1, qs), 1)
    w_idx = wit_ref[0]

    def score_chunk(c, carry):
        r0 = pl.multiple_of(c * KC, KC)
        kblk = ki_ref[0, pl.ds(r0, KC), :]
        parts = [jnp.dot(kblk, qit_ref[0, hh * IDX_DIM:(hh + 1) * IDX_DIM, :], preferred_element_type=F32)
                 for hh in range(N_IDX_HEADS)]
        score = jnp.zeros((KC, qs), F32)
        for hh in range(N_IDX_HEADS):
            score = score + jnp.maximum(parts[hh], 0.0) * w_idx[hh:hh + 1, :]
        bits = lax.bitcast_convert_type(score, I32)
        key = jnp.where(bits < 0, INT_MIN - bits, bits)
        kpos = r0 + lax.broadcasted_iota(I32, (KC, qs), 0)
        key = jnp.where(kpos <= qpos, key, INT_MIN)
        keys_scr[pl.ds(r0, KC), :] = key
        top_scr[pl.ds(r0, KC), :] = lax.shift_right_arithmetic(key, 16).astype(I16)
        return carry

    lax.fori_loop(0, n_chunks, score_chunk, 0)

    def count_rows(src, rows_per_vreg, trial):
        n_acc = 4
        groups = KC // rows_per_vreg

        def body(c, accs):
            r0 = pl.multiple_of(c * KC, KC)
            rows = src[pl.ds(r0, KC), :].reshape(groups, rows_per_vreg, qs)
            accs = list(accs)
            for j in range(groups):
                a = accs[j % n_acc]
                accs[j % n_acc] = jnp.where(rows[j] >= trial, a + 1, a)
            return tuple(accs)

        zero = jnp.zeros((rows_per_vreg, qs), src.dtype)
        accs = lax.fori_loop(0, n_chunks, body, tuple(zero for _ in range(n_acc)))
        total = (accs[0] + accs[1]) + (accs[2] + accs[3])
        return jnp.sum(total.astype(I32), axis=0, keepdims=True)

    def count_ge(trial):
        return count_rows(keys_scr, 8, trial)

    def count_ge_top(trial):
        return count_rows(top_scr, 16, lax.shift_right_arithmetic(trial, 16).astype(I16))

    c_zero = count_ge_top(jnp.zeros((1, qs), I32))
    c_pos = count_ge(jnp.ones((1, qs), I32))
    nonneg = c_zero >= k_sel
    tie_at_zero = jnp.logical_and(nonneg, c_pos < k_sel)
    ans0 = jnp.where(nonneg, 0, INT_MIN)
    c_ans0 = jnp.where(nonneg, c_zero, jnp.int32(2 ** 30))

    def unsettled(c_ans):
        settled = jnp.logical_or(tie_at_zero, c_ans == k_sel)
        return jnp.max(jnp.where(settled, 0, 1))

    def search(counter, lowest_bit, group, state):
        def refine(carry):
            bit, ans, c_ans, _ = carry
            for g in range(group):
                trial = ans + lax.shift_left(jnp.int32(1), bit - g)
                c = counter(trial)
                ok = c >= k_sel
                c_ans = jnp.where(ok, c, c_ans)
                ans = jnp.where(ok, trial, ans)
            return bit - group, ans, c_ans, unsettled(c_ans)

        return lax.while_loop(lambda cr: jnp.logical_and(cr[0] >= lowest_bit, cr[3] > 0), refine, state)

    _, top, c_top, alive = search(count_ge_top, 16, 5, (jnp.int32(30), ans0, c_ans0, unsettled(c_ans0)))
    bucket = lax.shift_right_arithmetic(top, 16)
    last_bucket = bucket >= 2 ** 15 - 1
    c_over = jnp.where(last_bucket, 0,
                       count_rows(top_scr, 16, jnp.where(last_bucket, bucket, bucket + 1).astype(I16)))

    def pack_low(c, carry):
        r0 = pl.multiple_of(c * KC, KC)
        low = (keys_scr[pl.ds(r0, KC), :] ^ 0x8000).astype(I16)
        top_scr[pl.ds(r0, KC), :] = jnp.where(top_scr[pl.ds(r0, KC), :] == bucket.astype(I16), low, -2 ** 15)
        return carry

    lax.fori_loop(0, n_chunks, pack_low, 0)

    def count_ge_low(low_trial):
        return c_over + count_rows(top_scr, 16, (low_trial - 2 ** 15).astype(I16))

    zero_row = jnp.zeros((1, qs), I32)
    _, low, _, _ = search(count_ge_low, 0, 4, (jnp.int32(15), zero_row, c_top, alive))
    thr = top + low
    full_low = low >= 2 ** 16 - 1
    n_above = jnp.where(full_low, c_over, count_ge_low(jnp.where(full_low, low, low + 1)))
    n_ties = jnp.where(thr == INT_MIN, 0, k_sel - n_above).astype(F32)

    acc_scr[...] = jnp.zeros_like(acc_scr)
    heads = range(N_HEADS_A)

    kb_rows = tri_ref.shape[0]
    vb_rows = vat_ref.shape[3]
    v_per_k = kb_rows // vb_rows

    n_steps = ((i + 1) * qs + kb_rows - 1) // kb_rows

    def attend(kb, carry):
        seen, ms, ls = carry
        r0 = pl.multiple_of(kb * kb_rows, kb_rows)
        keyb = keys_scr[pl.ds(r0, kb_rows), :]
        tied = keyb == thr
        tied_f = jnp.where(tied, 1.0, 0.0)
        rank = seen + jnp.dot(tri_ref[...], tied_f.astype(BF16), preferred_element_type=F32)
        keep = jnp.logical_or(keyb > thr, jnp.logical_and(tied, rank <= n_ties))
        bias = jnp.where(keep, 0.0, NEG_MASK)
        seen = seen + jnp.sum(tied_f, axis=0, keepdims=True)
        kblk = ka_ref[0, pl.ds(r0, kb_rows), :]
        logits = [jnp.dot(kblk, qat_ref[0, hh * HEAD_DIM:(hh + 1) * HEAD_DIM, :], preferred_element_type=F32) + bias
                  for hh in heads]
        new_ms = [jnp.maximum(ms[hh], jnp.max(logits[hh], axis=0, keepdims=True)) for hh in heads]
        probs = [jnp.exp2(logits[hh] - new_ms[hh]) for hh in heads]
        alphas = [jnp.exp2(ms[hh] - new_ms[hh]) for hh in heads]
        new_ls = [alphas[hh] * ls[hh] + jnp.sum(probs[hh], axis=0, keepdims=True) for hh in heads]
        outs = []
        for hh in heads:
            weights = probs[hh].astype(BF16)
            out = jnp.dot(vat_ref[0, kb * v_per_k], weights[:vb_rows, :], preferred_element_type=F32)
            for j in range(1, v_per_k):
                out = out + jnp.dot(vat_ref[0, kb * v_per_k + j], weights[j * vb_rows:(j + 1) * vb_rows, :],
                                    preferred_element_type=F32)
            outs.append(out)
        for hh in heads:
            r = slice(hh * HEAD_DIM, (hh + 1) * HEAD_DIM)
            acc_scr[r, :] = acc_scr[r, :] * alphas[hh] + outs[hh]
        return seen, tuple(new_ms), tuple(new_ls)

    zero = jnp.zeros((1, qs), F32)
    init = (zero, tuple(jnp.full((1, qs), NEG_BIG, F32) for _ in heads), tuple(zero for _ in heads))
    _, _, ls = lax.fori_loop(0, n_steps, attend, init)
    for hh in heads:
        r = slice(hh * HEAD_DIM, (hh + 1) * HEAD_DIM)
        acc_scr[r, :] = acc_scr[r, :] / ls[hh]
    ya_ref[0] = acc_scr[...].T.astype(BF16)


def _dsa_call(qit, wit, qat, ki, ka, vat, tri):
    b, l, _ = ki.shape
    nq = l // QB
    qs = min(QS_DSA, l)
    k_sel = min(TOPK_MAX, l // 4)
    return pl.pallas_call(
        functools.partial(_dsa_kernel, k_sel=k_sel),
        grid=(b, l // qs),
        in_specs=[
            pl.BlockSpec((1, N_IDX_HEADS * IDX_DIM, qs), lambda bi, i: (bi, 0, i)),
            pl.BlockSpec((1, T_WI_ROWS, qs), lambda bi, i: (bi, 0, i)),
            pl.BlockSpec((1, A_WIDTH, qs), lambda bi, i: (bi, 0, i)),
            pl.BlockSpec((1, l, IDX_DIM), lambda bi, i: (bi, 0, 0), pipeline_mode=pl.Buffered(1)),
            pl.BlockSpec((1, l, HEAD_DIM), lambda bi, i: (bi, 0, 0), pipeline_mode=pl.Buffered(1)),
            pl.BlockSpec((1, l // VB_DSA, HEAD_DIM, VB_DSA), lambda bi, i: (bi, 0, 0, 0),
                         pipeline_mode=pl.Buffered(1)),
            pl.BlockSpec((KB_DSA, KB_DSA), lambda bi, i: (0, 0)),
        ],
        out_specs=pl.BlockSpec((1, qs, A_WIDTH), lambda bi, i: (bi, i, 0)),
        out_shape=jax.ShapeDtypeStruct((b, l, A_WIDTH), BF16),
        scratch_shapes=[pltpu.VMEM((l, qs), I32), pltpu.VMEM((l, qs), I16), pltpu.VMEM((A_WIDTH, qs), F32)],
        compiler_params=_params(("arbitrary", "arbitrary")),
        name="dsa_attention",
    )(qit, wit, qat, ki, ka, vat, tri)


def _stick_kernel(qct_ref, kc_ref, vct_ref, later_ref, yc_ref, acc_scr):
    i = pl.program_id(1)
    qs = qct_ref.shape[2]
    diag_blocks = qs // QB
    qpos = i * qs + lax.broadcasted_iota(I32, (1, qs), 1)
    acc_scr[...] = jnp.zeros_like(acc_scr)

    def block(kb, tail, masked):
        r0 = pl.multiple_of(kb * QB, QB)
        kfull = kc_ref[0, pl.ds(r0, QB), :]
        vt = vct_ref[0, kb]
        heads = range(N_HEADS_C)
        if masked:
            mask = (r0 + lax.broadcasted_iota(I32, (QB, qs), 0)) < qpos
        zs = [jnp.dot(kfull[:, (hh // 2) * LANES:(hh // 2 + 1) * LANES], qct_ref[0, hh * LANES:(hh + 1) * LANES, :],
                      preferred_element_type=F32) for hh in heads]
        log_betas, splits, new_tail = [], [], []
        for hh in heads:
            z = zs[hh]
            log_beta = jnp.minimum(z, 0.0) - jnp.log(1.0 + jnp.exp(-jnp.abs(z)))
            log_keep = log_beta - z
            if masked:
                log_keep = jnp.where(mask, log_keep, 0.0)
            hi = log_keep.astype(BF16)
            lo = (log_keep - hi.astype(F32)).astype(BF16)
            log_betas.append(log_beta)
            splits.append(jnp.concatenate([hi, lo], axis=0))
            new_tail.append(tail[hh] + jnp.sum(log_keep, axis=0, keepdims=True))
        withins = [jnp.dot(later_ref[...], splits[hh], preferred_element_type=F32) for hh in heads]
        weights = []
        for hh in heads:
            a = jnp.exp(log_betas[hh] + withins[hh] + tail[hh])
            if masked:
                a = jnp.where(mask, a, 0.0)
            weights.append(a.astype(BF16))
        for hh in heads:
            r = slice(hh * HEAD_DIM, (hh + 1) * HEAD_DIM)
            acc_scr[r, :] = acc_scr[r, :] + jnp.dot(vt[r, :], weights[hh], preferred_element_type=F32)
        return tuple(new_tail)

    def largest(tail):
        worst = tail[0]
        for hh in range(1, N_HEADS_C):
            worst = jnp.maximum(worst, tail[hh])
        return jnp.max(worst)

    first = (i + 1) * diag_blocks - 1
    n_rest = i * diag_blocks
    tail = tuple(jnp.zeros((1, qs), F32) for _ in range(N_HEADS_C))
    tail = lax.fori_loop(0, diag_blocks, lambda t, tl: block(first - t, tl, True), tail)

    def live(cr):
        return jnp.logical_and(cr[0] < n_rest, cr[2] > STICK_DEAD)

    def step(cr):
        t, tl, _ = cr
        tl = block(n_rest - 1 - t, tl, False)
        return t + 1, tl, largest(tl)

    lax.while_loop(live, step, (jnp.int32(0), tail, largest(tail)))
    yc_ref[0] = acc_scr[...].T.astype(BF16)


def _stick_call(qct, kc, vct, later2):
    b, l, _ = kc.shape
    nq = l // QB
    qs = min(QS_STICK, l)
    return pl.pallas_call(
        _stick_kernel,
        grid=(b, l // qs),
        in_specs=[
            pl.BlockSpec((1, 2 * C_WIDTH, qs), lambda bi, i: (bi, 0, i)),
            pl.BlockSpec((1, l, C_WIDTH), lambda bi, i: (bi, 0, 0)),
            pl.BlockSpec((1, nq, C_WIDTH, QB), lambda bi, i: (bi, 0, 0, 0)),
            pl.BlockSpec((QB, 2 * QB), lambda bi, i: (0, 0)),
        ],
        out_specs=pl.BlockSpec((1, qs, C_WIDTH), lambda bi, i: (bi, i, 0)),
        out_shape=jax.ShapeDtypeStruct((b, l, C_WIDTH), BF16),
        scratch_shapes=[pltpu.VMEM((C_WIDTH, qs), F32)],
        compiler_params=_params(("arbitrary", "arbitrary")),
        name="stick_attention",
    )(qct, kc, vct, later2)


def _merge_kernel(x_ref, ya_ref, yc_ref, mb_ref, ga_ref, gc_ref, gt_ref, sh_ref, sc_ref, g1_ref, g2_ref,
                  wbra_ref, wbrc_ref, wout_ref, *rest, with_router):
    if with_router:
        wr_ref, x1_ref, h2_ref, comb_ref = rest
    else:
        x1_ref, h2_ref = rest
    merged = (ga_ref[0].astype(F32) * jnp.dot(ya_ref[0], wbra_ref[...], preferred_element_type=F32)
              + mb_ref[0].astype(F32)
              + gc_ref[0].astype(F32) * jnp.dot(yc_ref[0], wbrc_ref[...], preferred_element_type=F32))
    y = jnp.dot(merged.astype(BF16), wout_ref[...], preferred_element_type=F32)
    x1 = x_ref[0] + gt_ref[0] * _rms(y, g1_ref[...])
    x1_ref[0] = x1
    h2 = _rms(x1, g2_ref[...]) * (1.0 + sc_ref[0]) + sh_ref[0]
    h2_ref[0] = h2.astype(BF16)
    if with_router:
        logits = jnp.dot(h2, wr_ref[...], preferred_element_type=F32)
        lane = lax.broadcasted_iota(I32, logits.shape, 1)
        valid = lane < N_EXPERTS
        l1 = jnp.where(valid, logits, -jnp.inf)
        v1 = jnp.max(l1, axis=-1, keepdims=True)
        i1 = jnp.min(jnp.where(l1 == v1, lane, LANES), axis=-1, keepdims=True)
        l2 = jnp.where(lane == i1, -jnp.inf, l1)
        v2 = jnp.max(l2, axis=-1, keepdims=True)
        i2 = jnp.min(jnp.where(l2 == v2, lane, LANES), axis=-1, keepdims=True)
        e2 = jnp.exp(v2 - v1)
        p1 = 1.0 / (1.0 + e2)
        p2 = e2 / (1.0 + e2)
        comb_ref[0] = jnp.where(lane == i1, p1, 0.0) + jnp.where(lane == i2, p2, 0.0)


def _merge_call(x, ya, yc, mb, ga, gc, gt, sh, sc, g1, g2, wbra, wbrc, wout, wr=None):
    b, l, d = x.shape
    tm = min(TM_MERGE, l)
    tok = lambda w: pl.BlockSpec((1, tm, w), lambda bi, i: (bi, i, 0))
    full = lambda a: pl.BlockSpec(a.shape, lambda bi, i: (0,) * a.ndim)
    vec = pl.BlockSpec((1, 1, d), lambda bi, i: (bi, 0, 0))
    in_specs = [tok(d), tok(A_WIDTH), tok(C_WIDTH), tok(d), tok(d), tok(d), vec, vec, vec,
                full(g1), full(g2), full(wbra), full(wbrc), full(wout)]
    args = [x, ya, yc, mb, ga, gc, gt, sh, sc, g1, g2, wbra, wbrc, wout]
    out_shape = [jax.ShapeDtypeStruct((b, l, d), F32), jax.ShapeDtypeStruct((b, l, d), BF16)]
    out_specs = [tok(d), tok(d)]
    if wr is not None:
        in_specs.append(full(wr))
        args.append(wr)
        out_shape.append(jax.ShapeDtypeStruct((b, l, LANES), F32))
        out_specs.append(tok(LANES))
    return pl.pallas_call(
        functools.partial(_merge_kernel, with_router=wr is not None),
        grid=(b, l // tm),
        in_specs=in_specs,
        out_specs=out_specs,
        out_shape=out_shape,
        compiler_params=_params(("arbitrary", "arbitrary")),
        name="merge_router" if wr is not None else "merge",
    )(*args)


def _ffn_kernel(x1_ref, h2_ref, gt_ref, g3_ref, wg_ref, wu_ref, wd_ref, o_ref):
    h2 = h2_ref[0]
    gate = jnp.dot(h2, wg_ref[...], preferred_element_type=F32)
    up = jnp.dot(h2, wu_ref[...], preferred_element_type=F32)
    act = (gate * _sigmoid(gate) * up).astype(BF16)
    y = jnp.dot(act, wd_ref[...], preferred_element_type=F32)
    o_ref[0] = x1_ref[0] + gt_ref[0] * _rms(y, g3_ref[...])


def _ffn_call(x1, h2, gt, g3, wg, wu, wd):
    b, l, d = x1.shape
    tm = min(TM_FFN, l)
    tok = pl.BlockSpec((1, tm, d), lambda bi, i: (bi, i, 0))
    full = lambda a: pl.BlockSpec(a.shape, lambda bi, i: (0,) * a.ndim)
    vec = pl.BlockSpec((1, 1, d), lambda bi, i: (bi, 0, 0))
    return pl.pallas_call(
        _ffn_kernel,
        grid=(b, l // tm),
        in_specs=[tok, tok, vec, full(g3), full(wg), full(wu), full(wd)],
        out_specs=tok,
        out_shape=jax.ShapeDtypeStruct((b, l, d), F32),
        compiler_params=_params(("arbitrary", "arbitrary")),
        name="ffn_dense",
    )(x1, h2, gt, g3, wg, wu, wd)


def _moe_kernel(x1_ref, h2_ref, comb_ref, gt_ref, g3_ref, wg_ref, wu_ref, wd_ref, o_ref, acc_scr):
    e = pl.program_id(2)

    @pl.when(e == 0)
    def _():
        acc_scr[...] = jnp.zeros_like(acc_scr)

    h2 = h2_ref[0]
    gate = jnp.dot(h2, wg_ref[0], preferred_element_type=F32)
    up = jnp.dot(h2, wu_ref[0], preferred_element_type=F32)
    act = (gate * _sigmoid(gate) * up).astype(BF16)
    y = jnp.dot(act, wd_ref[0], preferred_element_type=F32)
    comb = comb_ref[0]
    lane = lax.broadcasted_iota(I32, comb.shape, 1)
    weight = jnp.sum(jnp.where(lane == e, comb, 0.0), axis=-1, keepdims=True)
    acc_scr[...] = acc_scr[...] + weight * y

    @pl.when(e == pl.num_programs(2) - 1)
    def _():
        o_ref[0] = x1_ref[0] + gt_ref[0] * _rms(acc_scr[...], g3_ref[...])


def _moe_call(x1, h2, comb, gt, g3, wg, wu, wd):
    b, l, d = x1.shape
    tm = min(TM_FFN, l)
    n_e, _, ff = wg.shape
    tok = lambda w: pl.BlockSpec((1, tm, w), lambda bi, i, e: (bi, i, 0))
    vec = pl.BlockSpec((1, 1, d), lambda bi, i, e: (bi, 0, 0))
    return pl.pallas_call(
        _moe_kernel,
        grid=(b, l // tm, n_e),
        in_specs=[tok(d), tok(d), tok(LANES), vec, pl.BlockSpec(g3.shape, lambda bi, i, e: (0, 0)),
                  pl.BlockSpec((1, d, ff), lambda bi, i, e: (e, 0, 0)),
                  pl.BlockSpec((1, d, ff), lambda bi, i, e: (e, 0, 0)),
                  pl.BlockSpec((1, ff, d), lambda bi, i, e: (e, 0, 0))],
        out_specs=tok(d),
        out_shape=jax.ShapeDtypeStruct((b, l, d), F32),
        scratch_shapes=[pltpu.VMEM((tm, d), F32)],
        compiler_params=_params(("arbitrary", "arbitrary", "arbitrary")),
        name="ffn_moe",
    )(x1, h2, comb, gt, g3, wg, wu, wd)


def _rot_cols(w, n_heads, head_dim):
    d = w.shape[0]
    w = w.reshape(d, n_heads, head_dim)
    half = ROPE_DIM // 2
    rot = jnp.concatenate([-w[..., half:ROPE_DIM], w[..., :half],
                           jnp.zeros((d, n_heads, head_dim - ROPE_DIM), w.dtype)], axis=-1)
    return rot.reshape(d, n_heads * head_dim)


def _pad_cols(w, width):
    return jnp.pad(w, ((0, 0), (0, width - w.shape[1])))


def _layer_weights(w_in, w_uk, w_uv, w_pool):
    d = w_in.shape[0]
    offs, o = [], 0
    for s in IN_SIZES:
        offs.append(o)
        o += s
    piece = lambda k: w_in[:, offs[k]:offs[k] + IN_SIZES[k]]
    w_qa, w_lat, w_kr, w_qi, w_ki, w_wi, w_up, w_qc, w_kc, w_vc, w_gate = [piece(k) for k in range(len(IN_SIZES))]
    attn_scale = HEAD_DIM ** -0.5
    idx_scale = IDX_DIM ** -0.5 * N_IDX_HEADS ** -0.5
    wrow = jnp.concatenate([
        w_lat,
        _pad_cols(w_kr, LANES), _pad_cols(_rot_cols(w_kr, 1, ROPE_DIM), LANES),
        _pad_cols(w_ki, LANES), _pad_cols(_rot_cols(w_ki, 1, IDX_DIM), LANES),
        w_up, w_kc, w_gate], axis=1).astype(BF16)
    w_qc_t = (w_qc * attn_scale).T.reshape(N_HEADS_C, HEAD_DIM, d)
    zeros = jnp.zeros_like(w_qc_t)
    even = (jnp.arange(N_HEADS_C) % 2 == 0)[:, None, None]
    w_qc_pad = jnp.concatenate([jnp.where(even, w_qc_t, zeros), jnp.where(even, zeros, w_qc_t)], axis=1)
    w_qc_pad = w_qc_pad.reshape(2 * C_WIDTH, d)
    softmax_scale = attn_scale * LOG2_E
    wt = jnp.concatenate([
        (w_qa * softmax_scale).T, (_rot_cols(w_qa, N_HEADS_A, HEAD_DIM) * softmax_scale).T,
        w_qi.T, _rot_cols(w_qi, N_IDX_HEADS, IDX_DIM).T,
        w_qc_pad, w_vc.T,
        jnp.pad((w_wi * idx_scale).T, ((0, T_WI_ROWS - N_IDX_HEADS), (0, 0)))], axis=0).astype(BF16)
    wuk = jnp.zeros((KV_LATENT, LANES), F32).at[:, ROPE_DIM:HEAD_DIM].set(w_uk).astype(BF16)
    wuvt = w_uv.T.astype(BF16)
    wpool = jnp.zeros((POOL_WIDTH, POOL_WIDTH), F32)
    for g in range(N_POOL_GROUPS):
        sl = slice(g * POOL_GROUP_DIM, (g + 1) * POOL_GROUP_DIM)
        wpool = wpool.at[sl, sl].set(w_pool[g])
    return wrow, wt, wuk, wuvt, wpool.astype(BF16)


def _rope_tables(positions):
    inv = ROPE_THETA ** (-jnp.arange(0, ROPE_DIM, 2, dtype=F32) / ROPE_DIM)
    ang = positions.astype(F32)[..., None] * inv
    cos, sin = jnp.cos(ang), jnp.sin(ang)
    b, l = positions.shape
    ones = jnp.ones((b, l, HEAD_DIM - ROPE_DIM), F32)
    cos_h = jnp.concatenate([cos, cos, ones], axis=-1)
    sin_h = jnp.concatenate([sin, sin, jnp.zeros_like(ones)], axis=-1)
    cos_r = jnp.concatenate([cos_h, cos_h], axis=-1)
    sin_r = jnp.concatenate([sin_h, sin_h], axis=-1)
    return cos_r, sin_r, cos_h.transpose(0, 2, 1), sin_h.transpose(0, 2, 1)


def kernel(x, c, positions, w_ada, b_ada, norm_gains, w_in, g_kv_latent, w_uk, w_uv, w_pool, pool_scale,
           w_br_a, w_br_b, w_br_c, w_out, w_gate_dense, w_up_dense, w_down_dense,
           w_router, w_gate_moe, w_up_moe, w_down_moe):
    b, l, d = x.shape
    depth = w_in.shape[0]
    assert d == D_MODEL and l % KC == 0 and l % TM_FFN == 0
    cos_r, sin_r, cos_t, sin_t = _rope_tables(positions)
    c_pad = jnp.pad(c, ((0, 8 - b), (0, 0)))
    mod = _mod_call(c_pad, w_ada, b_ada)[:, :b]
    idx = lax.broadcasted_iota(I32, (KB_DSA, KB_DSA), 0)
    jdx = lax.broadcasted_iota(I32, (KB_DSA, KB_DSA), 1)
    tri = (jdx <= idx).astype(BF16)
    later = (jdx > idx)[:QB, :QB].astype(BF16)
    later2 = jnp.concatenate([later, later], axis=1)
    for layer in range(depth):
        sh1, sc1, gt1, sh2, sc2, gt2 = [m.reshape(b, 1, d) for m in jnp.split(mod[layer], 6, axis=-1)]
        gains = norm_gains[layer].reshape(4, 1, d)
        wrow, wt, wuk, wuvt, wpool = _layer_weights(w_in[layer], w_uk[layer], w_uv[layer], w_pool[layer])
        (qat, qit, wit, ka, vat, ki, qct, kc, vct, ga, gc, mb) = _inproj_call(
            x, sh1, sc1, gains[0], wrow, wt, cos_r, sin_r, cos_t, sin_t,
            g_kv_latent[layer].reshape(1, KV_LATENT), wuk, wuvt, wpool,
            pool_scale[layer].reshape(1, POOL_WIDTH), w_br_b[layer].astype(BF16))
        ya = _dsa_call(qit, wit, qat, ki, ka, vat, tri)
        yc = _stick_call(qct, kc, vct, later2)
        i = layer // 2
        merge_args = (x, ya, yc, mb, ga, gc, gt1, sh2, sc2, gains[1], gains[2],
                      w_br_a[layer].astype(BF16), w_br_c[layer].astype(BF16), w_out[layer].astype(BF16))
        if layer % 2 == 0:
            x1, h2 = _merge_call(*merge_args)
            x = _ffn_call(x1, h2, gt2, gains[3], w_gate_dense[i].astype(BF16), w_up_dense[i].astype(BF16),
                          w_down_dense[i].astype(BF16))
        else:
            x1, h2, comb = _merge_call(*merge_args, wr=_pad_cols(w_router[i], LANES))
            x = _moe_call(x1, h2, comb, gt2, gains[3], w_gate_moe[i].astype(BF16), w_up_moe[i].astype(BF16),
                          w_down_moe[i].astype(BF16))
    return x
```

```python
import functools

import jax
import jax.numpy as jnp
from jax import lax
from jax.experimental import pallas as pl
from jax.experimental.pallas import tpu as pltpu

F32 = jnp.float32
BF16 = jnp.bfloat16
I32 = jnp.int32
I16 = jnp.int16

D_MODEL = 1024
HEAD_DIM = 64
ROPE_DIM = HEAD_DIM // 4
NOPE_DIM = HEAD_DIM - ROPE_DIM
ROPE_THETA = 500000.0
N_HEADS_A = (3 * D_MODEL // 8) // HEAD_DIM
A_WIDTH = N_HEADS_A * HEAD_DIM
KV_LATENT = D_MODEL // 8
N_IDX_HEADS = 4
IDX_DIM = 64
TOPK_MAX = 256
N_POOL_GROUPS = 4
POOL_WINDOWS = (2, 4, 8, 16)
POOL_WIDTH = D_MODEL // 4
POOL_GROUP_DIM = POOL_WIDTH // N_POOL_GROUPS
N_HEADS_C = (D_MODEL // 4) // HEAD_DIM
C_WIDTH = N_HEADS_C * HEAD_DIM
N_BRANCHES = 3
IN_SIZES = (A_WIDTH, KV_LATENT, ROPE_DIM, N_IDX_HEADS * IDX_DIM, IDX_DIM, N_IDX_HEADS,
            POOL_WIDTH, C_WIDTH, C_WIDTH, C_WIDTH, N_BRANCHES * D_MODEL)
D_FF = 2816
N_EXPERTS = 8
TOP_K = 2
D_FF_EXPERT = D_FF // TOP_K
RMS_EPS = 1e-6

LANES = 128
QB = 128
KC = 512
QS_STICK = 512
QS_DSA = 256
KB_DSA = 512
VB_DSA = 256
V_ROWS = 80
STICK_DEAD = -106.0
TM_IN = 256
TM_MERGE = 256
TM_FFN = 512
POOL_HALO = 16
VMEM_LIMIT = 56 * 1024 * 1024
INT_MIN = -2147483648
LOG2_E = 1.4426950408889634
NEG_BIG = -1e30
NEG_MASK = -2e30

R_LAT, R_KR, R_KRR, R_KI, R_KIR, R_UP, R_KC, R_GATE = 0, 128, 256, 384, 512, 640, 896, 1152
R_WIDTH = R_GATE + N_BRANCHES * D_MODEL
T_QA, T_QAR, T_QI, T_QIR, T_QC, T_VC, T_WI = 0, 384, 768, 1024, 1280, 1792, 2048
T_WI_ROWS = 16
T_HEIGHT = T_WI + T_WI_ROWS


def _params(sem):
    return pltpu.CompilerParams(dimension_semantics=sem, vmem_limit_bytes=VMEM_LIMIT)


def _sigmoid(v):
    return 1.0 / (1.0 + jnp.exp(-v))


def _rms(v, gain):
    return v * lax.rsqrt(jnp.mean(v * v, axis=-1, keepdims=True) + RMS_EPS) * gain


def _mod_kernel(c_ref, w_ref, b_ref, o_ref):
    c = c_ref[...]
    cond = c * _sigmoid(c)
    o_ref[0] = jnp.dot(cond, w_ref[0], preferred_element_type=F32) + b_ref[0]


def _mod_call(c_pad, w_ada, b_ada):
    depth, d, n = w_ada.shape
    tn = 1024
    return pl.pallas_call(
        _mod_kernel,
        grid=(depth, n // tn),
        in_specs=[
            pl.BlockSpec((8, d), lambda l, j: (0, 0)),
            pl.BlockSpec((1, d, tn), lambda l, j: (l, 0, j)),
            pl.BlockSpec((1, 1, tn), lambda l, j: (l, 0, j)),
        ],
        out_specs=pl.BlockSpec((1, 8, tn), lambda l, j: (l, 0, j)),
        out_shape=jax.ShapeDtypeStruct((depth, 8, n), F32),
        compiler_params=_params(("arbitrary", "arbitrary")),
        name="adaln_mod",
    )(c_pad, w_ada, b_ada.reshape(depth, 1, n))


def _inproj_kernel(x_ref, sh_ref, sc_ref, g_ref, wrow_ref, wt_ref, c_ref, s_ref, ct_ref, st_ref,
                   glat_ref, wuk_ref, wuvt_ref, wpool_ref, pscale_ref, wbrb_ref,
                   qat_ref, qit_ref, wit_ref, ka_ref, vat_ref, ki_ref, qct_ref, kc_ref, vct_ref,
                   ga_ref, gc_ref, mb_ref,
                   h_scr, ht_scr, ext_scr, prev_scr):
    i = pl.program_id(1)
    tm = x_ref.shape[1]
    x = x_ref[0]
    h = _rms(x, g_ref[...]) * (1.0 + sc_ref[0]) + sh_ref[0]
    h_scr[...] = h.astype(BF16)
    ht_scr[...] = h.T.astype(BF16)

    def rowdot(a, width):
        return jnp.dot(h_scr[...], wrow_ref[:, a:a + width], preferred_element_type=F32)

    def tdot(a, height):
        return jnp.dot(wt_ref[a:a + height, :], ht_scr[...], preferred_element_type=F32)

    cos_r, sin_r = c_ref[0], s_ref[0]
    cos_t, sin_t = ct_ref[0], st_ref[0]

    latn = _rms(rowdot(R_LAT, KV_LATENT), glat_ref[...])
    ka = (rowdot(R_KR, LANES) * cos_r + rowdot(R_KRR, LANES) * sin_r
          + jnp.dot(latn.astype(BF16), wuk_ref[...], preferred_element_type=F32))
    ka_ref[0] = ka[:, :HEAD_DIM].astype(BF16)
    vat = jnp.dot(wuvt_ref[...], latn.T.astype(BF16), preferred_element_type=F32)
    pad_row = lax.broadcasted_iota(I32, (V_ROWS - HEAD_DIM, tm), 0)
    vat = jnp.concatenate([vat, jnp.where(pad_row == 0, 1.0, 0.0)], axis=0)
    for j in range(tm // VB_DSA):
        vat_ref[0, j] = vat[:, j * VB_DSA:(j + 1) * VB_DSA].astype(BF16)

    ki =rowdot(R_KI, LANES) * cos_r + rowdot(R_KIR, LANES) * sin_r
    ki_ref[0] = ki[:, :IDX_DIM].astype(BF16)

    qa, qar = tdot(T_QA, A_WIDTH), tdot(T_QAR, A_WIDTH)
    for hh in range(N_HEADS_A):
        r = slice(hh * HEAD_DIM, (hh + 1) * HEAD_DIM)
        qat_ref[0, r, :] = (qa[r] * cos_t + qar[r] * sin_t).astype(BF16)
    qi, qir = tdot(T_QI, N_IDX_HEADS * IDX_DIM), tdot(T_QIR, N_IDX_HEADS * IDX_DIM)
    for hh in range(N_IDX_HEADS):
        r = slice(hh * IDX_DIM, (hh + 1) * IDX_DIM)
        qit_ref[0, r, :] = (qi[r] * cos_t + qir[r] * sin_t).astype(BF16)
    wit_ref[0] = tdot(T_WI, T_WI_ROWS)

    qct_ref[0] = tdot(T_QC, 2 * C_WIDTH).astype(BF16)
    kc_ref[0] = rowdot(R_KC, C_WIDTH).astype(BF16)
    vct = tdot(T_VC, C_WIDTH)
    for j in range(tm // QB):
        vct_ref[0, j] = vct[:, j * QB:(j + 1) * QB].astype(BF16)

    up = rowdot(R_UP, POOL_WIDTH)

    @pl.when(i == 0)
    def _():
        prev_scr[...] = jnp.zeros_like(prev_scr)

    ext_scr[0:POOL_HALO, :] = prev_scr[...]
    ext_scr[POOL_HALO:POOL_HALO + tm, :] = up
    prev_scr[...] = up[tm - POOL_HALO:, :]
    lag = [ext_scr[POOL_HALO - j:POOL_HALO - j + tm, :] for j in range(POOL_HALO)]
    sums = {}
    run = lag[0]
    for j in range(1, POOL_HALO):
        run = run + lag[j]
        if j + 1 in POOL_WINDOWS:
            sums[j + 1] = run
    lane = lax.broadcasted_iota(I32, (tm, POOL_WIDTH), 1)
    pos = i * tm + lax.broadcasted_iota(I32, (tm, POOL_WIDTH), 0)
    pooled_sum = sums[POOL_WINDOWS[-1]]
    win = jnp.full((tm, POOL_WIDTH), POOL_WINDOWS[-1], I32)
    for g in range(N_POOL_GROUPS - 2, -1, -1):
        in_group = lane < (g + 1) * POOL_GROUP_DIM
        pooled_sum = jnp.where(in_group, sums[POOL_WINDOWS[g]], pooled_sum)
        win = jnp.where(in_group, POOL_WINDOWS[g], win)
    cnt = jnp.minimum(pos + 1, win).astype(F32)
    pooled = pooled_sum / cnt - up
    yb = jnp.dot(pooled.astype(BF16), wpool_ref[...], preferred_element_type=F32) * pscale_ref[...]

    ga_ref[0] = _sigmoid(rowdot(R_GATE, D_MODEL)).astype(BF16)
    gb = _sigmoid(rowdot(R_GATE + D_MODEL, D_MODEL))
    mb_ref[0] = (gb * jnp.dot(yb.astype(BF16), wbrb_ref[...], preferred_element_type=F32)).astype(BF16)
    gc_ref[0] = _sigmoid(rowdot(R_GATE + 2 * D_MODEL, D_MODEL)).astype(BF16)


def _inproj_call(x, sh, sc, gain, wrow, wt, cos_r, sin_r, cos_t, sin_t, glat, wuk, wuvt, wpool, pscale, wbrb):
    b, l, d = x.shape
    tm = min(TM_IN, l)
    nq = l // QB
    tok = lambda w: pl.BlockSpec((1, tm, w), lambda bi, i: (bi, i, 0))
    feat = lambda hgt: pl.BlockSpec((1, hgt, tm), lambda bi, i: (bi, 0, i))
    blk = lambda hgt, w: pl.BlockSpec((1, tm // w, hgt, w), lambda bi, i: (bi, i, 0, 0))
    full = lambda a: pl.BlockSpec(a.shape, lambda bi, i: (0,) * a.ndim)
    vec = pl.BlockSpec((1, 1, d), lambda bi, i: (bi, 0, 0))
    out_shape = (
        jax.ShapeDtypeStruct((b, A_WIDTH, l), BF16),
        jax.ShapeDtypeStruct((b, N_IDX_HEADS * IDX_DIM, l), BF16),
        jax.ShapeDtypeStruct((b, T_WI_ROWS, l), F32),
        jax.ShapeDtypeStruct((b, l, HEAD_DIM), BF16),
        jax.ShapeDtypeStruct((b, l // VB_DSA, V_ROWS, VB_DSA), BF16),
        jax.ShapeDtypeStruct((b, l, IDX_DIM), BF16),
        jax.ShapeDtypeStruct((b, 2 * C_WIDTH, l), BF16),
        jax.ShapeDtypeStruct((b, l, C_WIDTH), BF16),
        jax.ShapeDtypeStruct((b, nq, C_WIDTH, QB), BF16),
        jax.ShapeDtypeStruct((b, l, d), BF16),
        jax.ShapeDtypeStruct((b, l, d), BF16),
        jax.ShapeDtypeStruct((b, l, d), BF16),
    )
    out_specs = (feat(A_WIDTH), feat(N_IDX_HEADS * IDX_DIM), feat(T_WI_ROWS), tok(HEAD_DIM), blk(V_ROWS, VB_DSA),
                 tok(IDX_DIM), feat(2 * C_WIDTH), tok(C_WIDTH), blk(C_WIDTH, QB), tok(d), tok(d), tok(d))
    in_specs = [tok(d), vec, vec, full(gain), full(wrow), full(wt), tok(LANES), tok(LANES),
                feat(HEAD_DIM), feat(HEAD_DIM), full(glat), full(wuk), full(wuvt), full(wpool),
                full(pscale), full(wbrb)]
    return pl.pallas_call(
        _inproj_kernel,
        grid=(b, l // tm),
        in_specs=in_specs,
        out_specs=out_specs,
        out_shape=out_shape,
        scratch_shapes=[pltpu.VMEM((tm, d), BF16), pltpu.VMEM((d, tm), BF16),
                        pltpu.VMEM((tm + POOL_HALO, POOL_WIDTH), F32), pltpu.VMEM((POOL_HALO, POOL_WIDTH), F32)],
        compiler_params=_params(("arbitrary", "arbitrary")),
        name="inproj",
    )(x, sh, sc, gain, wrow, wt, cos_r, sin_r, cos_t, sin_t, glat, wuk, wuvt, wpool, pscale, wbrb)


def _dsa_kernel(qit_ref, wit_ref, qat_ref, ki_ref, ka_ref, vat_ref, tri_ref, ya_ref,
                keys_scr, top_scr, acc_scr, *, k_sel):
    i = pl.program_id(1)
    qs = qat_ref.shape[2]
    n_blocks = (i + 1) * (qs // QB)
    n_chunks = (n_blocks * QB + KC - 1) // KC
    qpos = i * qs + lax.broadcasted_iota(I32, (1, qs), 1)
    w_idx = wit_ref[0]

    def score_chunk(c, carry):
        r0 = pl.multiple_of(c * KC, KC)
        kblk = ki_ref[0, pl.ds(r0, KC), :]
        parts = [jnp.dot(kblk, qit_ref[0, hh * IDX_DIM:(hh + 1) * IDX_DIM, :], preferred_element_type=F32)
                 for hh in range(N_IDX_HEADS)]
        score = jnp.zeros((KC, qs), F32)
        for hh in range(N_IDX_HEADS):
            score = score + jnp.maximum(parts[hh], 0.0) * w_idx[hh:hh + 1, :]
        bits = lax.bitcast_convert_type(score, I32)
        key = jnp.where(bits < 0, INT_MIN - bits, bits)
        kpos = r0 + lax.broadcasted_iota(I32, (KC, qs), 0)
        key = jnp.where(kpos <= qpos, key, INT_MIN)
        keys_scr[pl.ds(r0, KC), :] = key
        top_scr[pl.ds(r0, KC), :] = lax.shift_right_arithmetic(key, 16).astype(I16)
        return carry

    lax.fori_loop(0, n_chunks, score_chunk, 0)

    def count_rows(src, rows_per_vreg, trial):
        n_acc = 4
        groups = KC // rows_per_vreg

        def body(c, accs):
            r0 = pl.multiple_of(c * KC, KC)
            rows = src[pl.ds(r0, KC), :].reshape(groups, rows_per_vreg, qs)
            accs = list(accs)
            for j in range(groups):
                a = accs[j % n_acc]
                accs[j % n_acc] = jnp.where(rows[j] >= trial, a + 1, a)
            return tuple(accs)

        zero = jnp.zeros((rows_per_vreg, qs), src.dtype)
        accs = lax.fori_loop(0, n_chunks, body, tuple(zero for _ in range(n_acc)))
        total = (accs[0] + accs[1]) + (accs[2] + accs[3])
        return jnp.sum(total.astype(I32), axis=0, keepdims=True)

    def count_ge(trial):
        return count_rows(keys_scr, 8, trial)

    def count_ge_top(trial):
        return count_rows(top_scr, 16, lax.shift_right_arithmetic(trial, 16).astype(I16))

    c_zero = count_ge_top(jnp.zeros((1, qs), I32))
    c_pos = count_ge(jnp.ones((1, qs), I32))
    nonneg = c_zero >= k_sel
    tie_at_zero = jnp.logical_and(nonneg, c_pos < k_sel)
    ans0 = jnp.where(nonneg, 0, INT_MIN)
    c_ans0 = jnp.where(nonneg, c_zero, jnp.int32(2 ** 30))

    def unsettled(c_ans):
        settled = jnp.logical_or(tie_at_zero, c_ans == k_sel)
        return jnp.max(jnp.where(settled, 0, 1))

    def search(counter, lowest_bit, group, state):
        def refine(carry):
            bit, ans, c_ans, _ = carry
            for g in range(group):
                trial = ans + lax.shift_left(jnp.int32(1), bit - g)
                c = counter(trial)
                ok = c >= k_sel
                c_ans = jnp.where(ok, c, c_ans)
                ans = jnp.where(ok, trial, ans)
            return bit - group, ans, c_ans, unsettled(c_ans)

        return lax.while_loop(lambda cr: jnp.logical_and(cr[0] >= lowest_bit, cr[3] > 0), refine, state)

    _, top, c_top, alive = search(count_ge_top, 16, 5, (jnp.int32(30), ans0, c_ans0, unsettled(c_ans0)))
    bucket = lax.shift_right_arithmetic(top, 16)
    last_bucket = bucket >= 2 ** 15 - 1
    c_over = jnp.where(last_bucket, 0,
                       count_rows(top_scr, 16, jnp.where(last_bucket, bucket, bucket + 1).astype(I16)))

    def pack_low(c, carry):
        r0 = pl.multiple_of(c * KC, KC)
        low = (keys_scr[pl.ds(r0, KC), :] ^ 0x8000).astype(I16)
        top_scr[pl.ds(r0, KC), :] = jnp.where(top_scr[pl.ds(r0, KC), :] == bucket.astype(I16), low, -2 ** 15)
        return carry

    lax.fori_loop(0, n_chunks, pack_low, 0)

    def count_ge_low(low_trial):
        return c_over + count_rows(top_scr, 16, (low_trial - 2 ** 15).astype(I16))

    zero_row = jnp.zeros((1, qs), I32)
    _, low, _, _ = search(count_ge_low, 0, 4, (jnp.int32(15), zero_row, c_top, alive))
    thr = top + low
    full_low = low >= 2 ** 16 - 1
    n_above = jnp.where(full_low, c_over, count_ge_low(jnp.where(full_low, low, low + 1)))
    n_ties = jnp.where(thr == INT_MIN, 0, k_sel - n_above).astype(F32)

    acc_scr[...] = jnp.zeros_like(acc_scr)
    heads = range(N_HEADS_A)

    kb_rows = KB_DSA
    vb_rows = vat_ref.shape[3]
    v_per_k = kb_rows // vb_rows

    n_steps = ((i + 1) * qs + kb_rows - 1) // kb_rows

    v_rows = vat_ref.shape[2]

    def attend(kb, carry):
        seen, ms = carry
        r0 = pl.multiple_of(kb * kb_rows, kb_rows)
        keyb = keys_scr[pl.ds(r0, kb_rows), :]
        tied = keyb == thr
        tied_b = jnp.where(tied, 1.0, 0.0).astype(BF16)
        ranks = []
        for j in range(kb_rows // QB):
            within = jnp.dot(tri_ref[...], tied_b[j * QB:(j + 1) * QB, :], preferred_element_type=F32)
            ranks.append(seen + within)
            seen = seen + within[QB - 1:QB, :]
        rank = jnp.concatenate(ranks, axis=0)
        keep = jnp.logical_or(keyb > thr, jnp.logical_and(tied, rank <= n_ties))
        bias = jnp.where(keep, 0.0, NEG_MASK).astype(BF16)
        kblk = ka_ref[0, pl.ds(r0, kb_rows), :]
        logits = [jnp.dot(kblk, qat_ref[0, hh * HEAD_DIM:(hh + 1) * HEAD_DIM, :],
                          preferred_element_type=F32).astype(BF16) + bias for hh in heads]
        new_ms = [jnp.maximum(ms[hh], jnp.max(logits[hh], axis=0, keepdims=True).astype(F32)) for hh in heads]
        probs = [jnp.exp2(logits[hh] - new_ms[hh].astype(BF16)) for hh in heads]
        alphas = [jnp.exp2(ms[hh] - new_ms[hh]) for hh in heads]
        outs = []
        for hh in heads:
            out = jnp.dot(vat_ref[0, kb * v_per_k], probs[hh][:vb_rows, :], preferred_element_type=F32)
            for j in range(1, v_per_k):
                out = out + jnp.dot(vat_ref[0, kb * v_per_k + j], probs[hh][j * vb_rows:(j + 1) * vb_rows, :],
                                    preferred_element_type=F32)
            outs.append(out)
        for hh in heads:
            r = slice(hh * v_rows, (hh + 1) * v_rows)
            acc_scr[r, :] = acc_scr[r, :] * alphas[hh] + outs[hh]
        return seen, tuple(new_ms)

    init = (jnp.zeros((1, qs), F32), tuple(jnp.full((1, qs), NEG_BIG, F32) for _ in heads))
    lax.fori_loop(0, n_steps, attend, init)
    normed = [acc_scr[hh * v_rows:hh * v_rows + HEAD_DIM, :] / acc_scr[hh * v_rows + HEAD_DIM:hh * v_rows + HEAD_DIM + 1, :]
              for hh in heads]
    ya_ref[0] = jnp.concatenate(normed, axis=0).T.astype(BF16)


def _dsa_call(qit, wit, qat, ki, ka, vat, tri):
    b, l, _ = ki.shape
    nq = l // QB
    qs = min(QS_DSA, l)
    k_sel = min(TOPK_MAX, l // 4)
    return pl.pallas_call(
        functools.partial(_dsa_kernel, k_sel=k_sel),
        grid=(b, l // qs),
        in_specs=[
            pl.BlockSpec((1, N_IDX_HEADS * IDX_DIM, qs), lambda bi, i: (bi, 0, i)),
            pl.BlockSpec((1, T_WI_ROWS, qs), lambda bi, i: (bi, 0, i)),
            pl.BlockSpec((1, A_WIDTH, qs), lambda bi, i: (bi, 0, i)),
            pl.BlockSpec((1, l, IDX_DIM), lambda bi, i: (bi, 0, 0), pipeline_mode=pl.Buffered(1)),
            pl.BlockSpec((1, l, HEAD_DIM), lambda bi, i: (bi, 0, 0), pipeline_mode=pl.Buffered(1)),
            pl.BlockSpec((1, l // VB_DSA, V_ROWS, VB_DSA), lambda bi, i: (bi, 0, 0, 0),
                         pipeline_mode=pl.Buffered(1)),
            pl.BlockSpec((QB, QB), lambda bi, i: (0, 0)),
        ],
        out_specs=pl.BlockSpec((1, qs, A_WIDTH), lambda bi, i: (bi, i, 0)),
        out_shape=jax.ShapeDtypeStruct((b, l, A_WIDTH), BF16),
        scratch_shapes=[pltpu.VMEM((l, qs), I32), pltpu.VMEM((l, qs), I16),
                        pltpu.VMEM((N_HEADS_A * V_ROWS, qs), F32)],
        compiler_params=_params(("arbitrary", "arbitrary")),
        name="dsa_attention",
    )(qit, wit, qat, ki, ka, vat, tri)


def _stick_kernel(qct_ref, kc_ref, vct_ref, later_ref, yc_ref, acc_scr):
    i = pl.program_id(1)
    qs = qct_ref.shape[2]
    diag_blocks = qs // QB
    qpos = i * qs + lax.broadcasted_iota(I32, (1, qs), 1)
    acc_scr[...] = jnp.zeros_like(acc_scr)

    def block(kb, tail, masked):
        r0 = pl.multiple_of(kb * QB, QB)
        kfull = kc_ref[0, pl.ds(r0, QB), :]
        vt = vct_ref[0, kb]
        heads = range(N_HEADS_C)
        if masked:
            mask = (r0 + lax.broadcasted_iota(I32, (QB, qs), 0)) < qpos
        zs = [jnp.dot(kfull[:, (hh // 2) * LANES:(hh // 2 + 1) * LANES], qct_ref[0, hh * LANES:(hh + 1) * LANES, :],
                      preferred_element_type=F32) for hh in heads]
        log_betas, splits, new_tail = [], [], []
        for hh in heads:
            z = zs[hh]
            log_beta = jnp.minimum(z, 0.0) - jnp.log(1.0 + jnp.exp(-jnp.abs(z)))
            log_keep = log_beta - z
            if masked:
                log_keep = jnp.where(mask, log_keep, 0.0)
            hi = log_keep.astype(BF16)
            lo = (log_keep - hi.astype(F32)).astype(BF16)
            log_betas.append(log_beta)
            splits.append(jnp.concatenate([hi, lo], axis=0))
            new_tail.append(tail[hh] + jnp.sum(log_keep, axis=0, keepdims=True))
        withins = [jnp.dot(later_ref[...], splits[hh], preferred_element_type=F32) for hh in heads]
        weights = []
        for hh in heads:
            a = jnp.exp(log_betas[hh] + withins[hh] + tail[hh])
            if masked:
                a = jnp.where(mask, a, 0.0)
            weights.append(a.astype(BF16))
        for hh in heads:
            r = slice(hh * HEAD_DIM, (hh + 1) * HEAD_DIM)
            acc_scr[r, :] = acc_scr[r, :] + jnp.dot(vt[r, :], weights[hh], preferred_element_type=F32)
        return tuple(new_tail)

    def largest(tail):
        worst = tail[0]
        for hh in range(1, N_HEADS_C):
            worst = jnp.maximum(worst, tail[hh])
        return jnp.max(worst)

    first = (i + 1) * diag_blocks - 1
    n_rest = i * diag_blocks
    tail = tuple(jnp.zeros((1, qs), F32) for _ in range(N_HEADS_C))
    tail = lax.fori_loop(0, diag_blocks, lambda t, tl: block(first - t, tl, True), tail)

    def live(cr):
        return jnp.logical_and(cr[0] < n_rest, cr[2] > STICK_DEAD)

    def step(cr):
        t, tl, _ = cr
        tl = block(n_rest - 1 - t, tl, False)
        return t + 1, tl, largest(tl)

    lax.while_loop(live, step, (jnp.int32(0), tail, largest(tail)))
    yc_ref[0] = acc_scr[...].T.astype(BF16)


def _stick_call(qct, kc, vct, later2):
    b, l, _ = kc.shape
    nq = l // QB
    qs = min(QS_STICK, l)
    return pl.pallas_call(
        _stick_kernel,
        grid=(b, l // qs),
        in_specs=[
            pl.BlockSpec((1, 2 * C_WIDTH, qs), lambda bi, i: (bi, 0, i)),
            pl.BlockSpec((1, l, C_WIDTH), lambda bi, i: (bi, 0, 0)),
            pl.BlockSpec((1, nq, C_WIDTH, QB), lambda bi, i: (bi, 0, 0, 0)),
            pl.BlockSpec((QB, 2 * QB), lambda bi, i: (0, 0)),
        ],
        out_specs=pl.BlockSpec((1, qs, C_WIDTH), lambda bi, i: (bi, i, 0)),
        out_shape=jax.ShapeDtypeStruct((b, l, C_WIDTH), BF16),
        scratch_shapes=[pltpu.VMEM((C_WIDTH, qs), F32)],
        compiler_params=_params(("arbitrary", "arbitrary")),
        name="stick_attention",
    )(qct, kc, vct, later2)


def _merge_kernel(x_ref, ya_ref, yc_ref, mb_ref, ga_ref, gc_ref, gt_ref, sh_ref, sc_ref, g1_ref, g2_ref,
                  wbra_ref, wbrc_ref, wout_ref, *rest, with_router):
    if with_router:
        wr_ref, x1_ref, h2_ref, comb_ref = rest
    else:
        x1_ref, h2_ref = rest
    merged = (ga_ref[0].astype(F32) * jnp.dot(ya_ref[0], wbra_ref[...], preferred_element_type=F32)
              + mb_ref[0].astype(F32)
              + gc_ref[0].astype(F32) * jnp.dot(yc_ref[0], wbrc_ref[...], preferred_element_type=F32))
    y = jnp.dot(merged.astype(BF16), wout_ref[...], preferred_element_type=F32)
    x1 = x_ref[0] + gt_ref[0] * _rms(y, g1_ref[...])
    x1_ref[0] = x1
    h2 = _rms(x1, g2_ref[...]) * (1.0 + sc_ref[0]) + sh_ref[0]
    h2_ref[0] = h2.astype(BF16)
    if with_router:
        logits = jnp.dot(h2, wr_ref[...], preferred_element_type=F32)
        lane = lax.broadcasted_iota(I32, logits.shape, 1)
        valid = lane < N_EXPERTS
        l1 = jnp.where(valid, logits, -jnp.inf)
        v1 = jnp.max(l1, axis=-1, keepdims=True)
        i1 = jnp.min(jnp.where(l1 == v1, lane, LANES), axis=-1, keepdims=True)
        l2 = jnp.where(lane == i1, -jnp.inf, l1)
        v2 = jnp.max(l2, axis=-1, keepdims=True)
        i2 = jnp.min(jnp.where(l2 == v2, lane, LANES), axis=-1, keepdims=True)
        e2 = jnp.exp(v2 - v1)
        p1 = 1.0 / (1.0 + e2)
        p2 = e2 / (1.0 + e2)
        comb_ref[0] = jnp.where(lane == i1, p1, 0.0) + jnp.where(lane == i2, p2, 0.0)


def _merge_call(x, ya, yc, mb, ga, gc, gt, sh, sc, g1, g2, wbra, wbrc, wout, wr=None):
    b, l, d = x.shape
    tm = min(TM_MERGE, l)
    tok = lambda w: pl.BlockSpec((1, tm, w), lambda bi, i: (bi, i, 0))
    full = lambda a: pl.BlockSpec(a.shape, lambda bi, i: (0,) * a.ndim)
    vec = pl.BlockSpec((1, 1, d), lambda bi, i: (bi, 0, 0))
    in_specs = [tok(d), tok(A_WIDTH), tok(C_WIDTH), tok(d), tok(d), tok(d), vec, vec, vec,
                full(g1), full(g2), full(wbra), full(wbrc), full(wout)]
    args = [x, ya, yc, mb, ga, gc, gt, sh, sc, g1, g2, wbra, wbrc, wout]
    out_shape = [jax.ShapeDtypeStruct((b, l, d), F32), jax.ShapeDtypeStruct((b, l, d), BF16)]
    out_specs = [tok(d), tok(d)]
    if wr is not None:
        in_specs.append(full(wr))
        args.append(wr)
        out_shape.append(jax.ShapeDtypeStruct((b, l, LANES), F32))
        out_specs.append(tok(LANES))
    return pl.pallas_call(
        functools.partial(_merge_kernel, with_router=wr is not None),
        grid=(b, l // tm),
        in_specs=in_specs,
        out_specs=out_specs,
        out_shape=out_shape,
        compiler_params=_params(("arbitrary", "arbitrary")),
        name="merge_router" if wr is not None else "merge",
    )(*args)


def _ffn_kernel(x1_ref, h2_ref, gt_ref, g3_ref, wg_ref, wu_ref, wd_ref, o_ref):
    h2 = h2_ref[0]
    gate = jnp.dot(h2, wg_ref[...], preferred_element_type=F32)
    up = jnp.dot(h2, wu_ref[...], preferred_element_type=F32)
    act = (gate * _sigmoid(gate) * up).astype(BF16)
    y = jnp.dot(act, wd_ref[...], preferred_element_type=F32)
    o_ref[0] = x1_ref[0] + gt_ref[0] * _rms(y, g3_ref[...])


def _ffn_call(x1, h2, gt, g3, wg, wu, wd):
    b, l, d = x1.shape
    tm = min(TM_FFN, l)
    tok = pl.BlockSpec((1, tm, d), lambda bi, i: (bi, i, 0))
    full = lambda a: pl.BlockSpec(a.shape, lambda bi, i: (0,) * a.ndim)
    vec = pl.BlockSpec((1, 1, d), lambda bi, i: (bi, 0, 0))
    return pl.pallas_call(
        _ffn_kernel,
        grid=(b, l // tm),
        in_specs=[tok, tok, vec, full(g3), full(wg), full(wu), full(wd)],
        out_specs=tok,
        out_shape=jax.ShapeDtypeStruct((b, l, d), F32),
        compiler_params=_params(("arbitrary", "arbitrary")),
        name="ffn_dense",
    )(x1, h2, gt, g3, wg, wu, wd)


def _moe_kernel(x1_ref, h2_ref, comb_ref, gt_ref, g3_ref, wg_ref, wu_ref, wd_ref, o_ref, acc_scr):
    e = pl.program_id(2)

    @pl.when(e == 0)
    def _():
        acc_scr[...] = jnp.zeros_like(acc_scr)

    h2 = h2_ref[0]
    gate = jnp.dot(h2, wg_ref[0], preferred_element_type=F32)
    up = jnp.dot(h2, wu_ref[0], preferred_element_type=F32)
    act = (gate * _sigmoid(gate) * up).astype(BF16)
    y = jnp.dot(act, wd_ref[0], preferred_element_type=F32)
    comb = comb_ref[0]
    lane = lax.broadcasted_iota(I32, comb.shape, 1)
    weight = jnp.sum(jnp.where(lane == e, comb, 0.0), axis=-1, keepdims=True)
    acc_scr[...] = acc_scr[...] + weight * y

    @pl.when(e == pl.num_programs(2) - 1)
    def _():
        o_ref[0] = x1_ref[0] + gt_ref[0] * _rms(acc_scr[...], g3_ref[...])


def _moe_call(x1, h2, comb, gt, g3, wg, wu, wd):
    b, l, d = x1.shape
    tm = min(TM_FFN, l)
    n_e, _, ff = wg.shape
    tok = lambda w: pl.BlockSpec((1, tm, w), lambda bi, i, e: (bi, i, 0))
    vec = pl.BlockSpec((1, 1, d), lambda bi, i, e: (bi, 0, 0))
    return pl.pallas_call(
        _moe_kernel,
        grid=(b, l // tm, n_e),
        in_specs=[tok(d), tok(d), tok(LANES), vec, pl.BlockSpec(g3.shape, lambda bi, i, e: (0, 0)),
                  pl.BlockSpec((1, d, ff), lambda bi, i, e: (e, 0, 0)),
                  pl.BlockSpec((1, d, ff), lambda bi, i, e: (e, 0, 0)),
                  pl.BlockSpec((1, ff, d), lambda bi, i, e: (e, 0, 0))],
        out_specs=tok(d),
        out_shape=jax.ShapeDtypeStruct((b, l, d), F32),
        scratch_shapes=[pltpu.VMEM((tm, d), F32)],
        compiler_params=_params(("arbitrary", "arbitrary", "arbitrary")),
        name="ffn_moe",
    )(x1, h2, comb, gt, g3, wg, wu, wd)


def _rot_cols(w, n_heads, head_dim):
    d = w.shape[0]
    w = w.reshape(d, n_heads, head_dim)
    half = ROPE_DIM // 2
    rot = jnp.concatenate([-w[..., half:ROPE_DIM], w[..., :half],
                           jnp.zeros((d, n_heads, head_dim - ROPE_DIM), w.dtype)], axis=-1)
    return rot.reshape(d, n_heads * head_dim)


def _pad_cols(w, width):
    return jnp.pad(w, ((0, 0), (0, width - w.shape[1])))


def _layer_weights(w_in, w_uk, w_uv, w_pool):
    d = w_in.shape[0]
    offs, o = [], 0
    for s in IN_SIZES:
        offs.append(o)
        o += s
    piece = lambda k: w_in[:, offs[k]:offs[k] + IN_SIZES[k]]
    w_qa, w_lat, w_kr, w_qi, w_ki, w_wi, w_up, w_qc, w_kc, w_vc, w_gate = [piece(k) for k in range(len(IN_SIZES))]
    attn_scale = HEAD_DIM ** -0.5
    idx_scale = IDX_DIM ** -0.5 * N_IDX_HEADS ** -0.5
    wrow = jnp.concatenate([
        w_lat,
        _pad_cols(w_kr, LANES), _pad_cols(_rot_cols(w_kr, 1, ROPE_DIM), LANES),
        _pad_cols(w_ki, LANES), _pad_cols(_rot_cols(w_ki, 1, IDX_DIM), LANES),
        w_up, w_kc, w_gate], axis=1).astype(BF16)
    w_qc_t = (w_qc * attn_scale).T.reshape(N_HEADS_C, HEAD_DIM, d)
    zeros = jnp.zeros_like(w_qc_t)
    even = (jnp.arange(N_HEADS_C) % 2 == 0)[:, None, None]
    w_qc_pad = jnp.concatenate([jnp.where(even, w_qc_t, zeros), jnp.where(even, zeros, w_qc_t)], axis=1)
    w_qc_pad = w_qc_pad.reshape(2 * C_WIDTH, d)
    softmax_scale = attn_scale * LOG2_E
    wt = jnp.concatenate([
        (w_qa * softmax_scale).T, (_rot_cols(w_qa, N_HEADS_A, HEAD_DIM) * softmax_scale).T,
        w_qi.T, _rot_cols(w_qi, N_IDX_HEADS, IDX_DIM).T,
        w_qc_pad, w_vc.T,
        jnp.pad((w_wi * idx_scale).T, ((0, T_WI_ROWS - N_IDX_HEADS), (0, 0)))], axis=0).astype(BF16)
    wuk = jnp.zeros((KV_LATENT, LANES), F32).at[:, ROPE_DIM:HEAD_DIM].set(w_uk).astype(BF16)
    wuvt = w_uv.T.astype(BF16)
    wpool = jnp.zeros((POOL_WIDTH, POOL_WIDTH), F32)
    for g in range(N_POOL_GROUPS):
        sl = slice(g * POOL_GROUP_DIM, (g + 1) * POOL_GROUP_DIM)
        wpool = wpool.at[sl, sl].set(w_pool[g])
    return wrow, wt, wuk, wuvt, wpool.astype(BF16)


def _rope_tables(positions):
    inv = ROPE_THETA ** (-jnp.arange(0, ROPE_DIM, 2, dtype=F32) / ROPE_DIM)
    ang = positions.astype(F32)[..., None] * inv
    cos, sin = jnp.cos(ang), jnp.sin(ang)
    b, l = positions.shape
    ones = jnp.ones((b, l, HEAD_DIM - ROPE_DIM), F32)
    cos_h = jnp.concatenate([cos, cos, ones], axis=-1)
    sin_h = jnp.concatenate([sin, sin, jnp.zeros_like(ones)], axis=-1)
    cos_r = jnp.concatenate([cos_h, cos_h], axis=-1)
    sin_r = jnp.concatenate([sin_h, sin_h], axis=-1)
    return cos_r, sin_r, cos_h.transpose(0, 2, 1), sin_h.transpose(0, 2, 1)


def kernel(x, c, positions, w_ada, b_ada, norm_gains, w_in, g_kv_latent, w_uk, w_uv, w_pool, pool_scale,
           w_br_a, w_br_b, w_br_c, w_out, w_gate_dense, w_up_dense, w_down_dense,
           w_router, w_gate_moe, w_up_moe, w_down_moe):
    b, l, d = x.shape
    depth = w_in.shape[0]
    assert d == D_MODEL and l % KC == 0 and l % TM_FFN == 0
    cos_r, sin_r, cos_t, sin_t = _rope_tables(positions)
    c_pad = jnp.pad(c, ((0, 8 - b), (0, 0)))
    mod = _mod_call(c_pad, w_ada, b_ada)[:, :b]
    idx = lax.broadcasted_iota(I32, (QB, QB), 0)
    jdx = lax.broadcasted_iota(I32, (QB, QB), 1)
    tri = (jdx <= idx).astype(BF16)
    later = (jdx > idx).astype(BF16)
    later2 = jnp.concatenate([later, later], axis=1)
    for layer in range(depth):
        sh1, sc1, gt1, sh2, sc2, gt2 = [m.reshape(b, 1, d) for m in jnp.split(mod[layer], 6, axis=-1)]
        gains = norm_gains[layer].reshape(4, 1, d)
        wrow, wt, wuk, wuvt, wpool = _layer_weights(w_in[layer], w_uk[layer], w_uv[layer], w_pool[layer])
        (qat, qit, wit, ka, vat, ki, qct, kc, vct, ga, gc, mb) = _inproj_call(
            x, sh1, sc1, gains[0], wrow, wt, cos_r, sin_r, cos_t, sin_t,
            g_kv_latent[layer].reshape(1, KV_LATENT), wuk, wuvt, wpool,
            pool_scale[layer].reshape(1, POOL_WIDTH), w_br_b[layer].astype(BF16))
        ya = _dsa_call(qit, wit, qat, ki, ka, vat, tri)
        yc = _stick_call(qct, kc, vct, later2)
        i = layer // 2
        merge_args = (x, ya, yc, mb, ga, gc, gt1, sh2, sc2, gains[1], gains[2],
                      w_br_a[layer].astype(BF16), w_br_c[layer].astype(BF16), w_out[layer].astype(BF16))
        if layer % 2 == 0:
            x1, h2 = _merge_call(*merge_args)
            x = _ffn_call(x1, h2, gt2, gains[3], w_gate_dense[i].astype(BF16), w_up_dense[i].astype(BF16),
                          w_down_dense[i].astype(BF16))
        else:
            x1, h2, comb = _merge_call(*merge_args, wr=_pad_cols(w_router[i], LANES))
            x = _moe_call(x1, h2, comb, gt2, gains[3], w_gate_moe[i].astype(BF16), w_up_moe[i].astype(BF16),
                          w_down_moe[i].astype(BF16))
    return x
```

```python
import functools

import jax
import jax.numpy as jnp
from jax import lax
from jax.experimental import pallas as pl
from jax.experimental.pallas import tpu as pltpu

F32 = jnp.float32
BF16 = jnp.bfloat16
I32 = jnp.int32
I16 = jnp.int16

D_MODEL = 1024
HEAD_DIM = 64
ROPE_DIM = HEAD_DIM // 4
NOPE_DIM = HEAD_DIM - ROPE_DIM
ROPE_THETA = 500000.0
N_HEADS_A = (3 * D_MODEL // 8) // HEAD_DIM
A_WIDTH = N_HEADS_A * HEAD_DIM
KV_LATENT = D_MODEL // 8
N_IDX_HEADS = 4
IDX_DIM = 64
TOPK_MAX = 256
N_POOL_GROUPS = 4
POOL_WINDOWS = (2, 4, 8, 16)
POOL_WIDTH = D_MODEL // 4
POOL_GROUP_DIM = POOL_WIDTH // N_POOL_GROUPS
N_HEADS_C = (D_MODEL // 4) // HEAD_DIM
C_WIDTH = N_HEADS_C * HEAD_DIM
N_BRANCHES = 3
IN_SIZES = (A_WIDTH, KV_LATENT, ROPE_DIM, N_IDX_HEADS * IDX_DIM, IDX_DIM, N_IDX_HEADS,
            POOL_WIDTH, C_WIDTH, C_WIDTH, C_WIDTH, N_BRANCHES * D_MODEL)
D_FF = 2816
N_EXPERTS = 8
TOP_K = 2
D_FF_EXPERT = D_FF // TOP_K
RMS_EPS = 1e-6

LANES = 128
QB = 128
KC = 512
QS_STICK = 512
QS_DSA = 256
KB_DSA = 512
VB_DSA = 256
V_ROWS = 80
STICK_DEAD = -106.0
TM_IN = 256
TM_MERGE = 256
TM_FFN = 512
POOL_HALO = 16
VMEM_LIMIT = 56 * 1024 * 1024
INT_MIN = -2147483648
LOG2_E = 1.4426950408889634
NEG_BIG = -1e30
NEG_MASK = -2e30

R_LAT, R_KR, R_KRR, R_KI, R_KIR, R_UP, R_KC, R_GATE = 0, 128, 256, 384, 512, 640, 896, 1152
R_WIDTH = R_GATE + N_BRANCHES * D_MODEL
T_QA, T_QAR, T_QI, T_QIR, T_QC, T_VC, T_WI = 0, 384, 768, 1024, 1280, 1792, 2048
T_WI_ROWS = 16
T_HEIGHT = T_WI + T_WI_ROWS


def _params(sem):
    return pltpu.CompilerParams(dimension_semantics=sem, vmem_limit_bytes=VMEM_LIMIT)


def _sigmoid(v):
    return 1.0 / (1.0 + jnp.exp(-v))


def _rms(v, gain):
    return v * lax.rsqrt(jnp.mean(v * v, axis=-1, keepdims=True) + RMS_EPS) * gain


def _mod_kernel(c_ref, w_ref, b_ref, o_ref):
    c = c_ref[...]
    cond = c * _sigmoid(c)
    o_ref[0] = jnp.dot(cond, w_ref[0], preferred_element_type=F32) + b_ref[0]


def _mod_call(c_pad, w_ada, b_ada):
    depth, d, n = w_ada.shape
    tn = 1024
    return pl.pallas_call(
        _mod_kernel,
        grid=(depth, n // tn),
        in_specs=[
            pl.BlockSpec((8, d), lambda l, j: (0, 0)),
            pl.BlockSpec((1, d, tn), lambda l, j: (l, 0, j)),
            pl.BlockSpec((1, 1, tn), lambda l, j: (l, 0, j)),
        ],
        out_specs=pl.BlockSpec((1, 8, tn), lambda l, j: (l, 0, j)),
        out_shape=jax.ShapeDtypeStruct((depth, 8, n), F32),
        compiler_params=_params(("arbitrary", "arbitrary")),
        name="adaln_mod",
    )(c_pad, w_ada, b_ada.reshape(depth, 1, n))


def _inproj_kernel(x_ref, sh_ref, sc_ref, g_ref, wrow_ref, wt_ref, c_ref, s_ref, ct_ref, st_ref,
                   glat_ref, wuk_ref, wuvt_ref, wpool_ref, pscale_ref, wbrb_ref,
                   qat_ref, qit_ref, wit_ref, ka_ref, vat_ref, ki_ref, qct_ref, kc_ref, vct_ref,
                   ga_ref, gc_ref, mb_ref,
                   h_scr, ht_scr, ext_scr, prev_scr):
    i = pl.program_id(1)
    tm = x_ref.shape[1]
    x = x_ref[0]
    h = _rms(x, g_ref[...]) * (1.0 + sc_ref[0]) + sh_ref[0]
    h_scr[...] = h.astype(BF16)
    ht_scr[...] = h.T.astype(BF16)

    def rowdot(a, width):
        return jnp.dot(h_scr[...], wrow_ref[:, a:a + width], preferred_element_type=F32)

    def tdot(a, height):
        return jnp.dot(wt_ref[a:a + height, :], ht_scr[...], preferred_element_type=F32)

    cos_r, sin_r = c_ref[0], s_ref[0]
    cos_t, sin_t = ct_ref[0], st_ref[0]

    latn = _rms(rowdot(R_LAT, KV_LATENT), glat_ref[...])
    ka = (rowdot(R_KR, LANES) * cos_r + rowdot(R_KRR, LANES) * sin_r
          + jnp.dot(latn.astype(BF16), wuk_ref[...], preferred_element_type=F32))
    ka_ref[0] = ka[:, :HEAD_DIM].astype(BF16)
    vat = jnp.dot(wuvt_ref[...], latn.T.astype(BF16), preferred_element_type=F32)
    pad_row = lax.broadcasted_iota(I32, (V_ROWS - HEAD_DIM, tm), 0)
    vat = jnp.concatenate([vat, jnp.where(pad_row == 0, 1.0, 0.0)], axis=0)
    for j in range(tm // VB_DSA):
        vat_ref[0, j] = vat[:, j * VB_DSA:(j + 1) * VB_DSA].astype(BF16)

    ki =rowdot(R_KI, LANES) * cos_r + rowdot(R_KIR, LANES) * sin_r
    ki_ref[0] = ki[:, :IDX_DIM].astype(BF16)

    qa, qar = tdot(T_QA, A_WIDTH), tdot(T_QAR, A_WIDTH)
    for hh in range(N_HEADS_A):
        r = slice(hh * HEAD_DIM, (hh + 1) * HEAD_DIM)
        qat_ref[0, r, :] = (qa[r] * cos_t + qar[r] * sin_t).astype(BF16)
    qi, qir = tdot(T_QI, N_IDX_HEADS * IDX_DIM), tdot(T_QIR, N_IDX_HEADS * IDX_DIM)
    for hh in range(N_IDX_HEADS):
        r = slice(hh * IDX_DIM, (hh + 1) * IDX_DIM)
        qit_ref[0, r, :] = (qi[r] * cos_t + qir[r] * sin_t).astype(BF16)
    wit_ref[0] = tdot(T_WI, T_WI_ROWS)

    qct_ref[0] = tdot(T_QC, 2 * C_WIDTH).astype(BF16)
    kc_ref[0] = rowdot(R_KC, C_WIDTH).astype(BF16)
    vct = tdot(T_VC, C_WIDTH)
    for j in range(tm // QB):
        vct_ref[0, j] = vct[:, j * QB:(j + 1) * QB].astype(BF16)

    up = rowdot(R_UP, POOL_WIDTH)

    @pl.when(i == 0)
    def _():
        prev_scr[...] = jnp.zeros_like(prev_scr)

    ext_scr[0:POOL_HALO, :] = prev_scr[...]
    ext_scr[POOL_HALO:POOL_HALO + tm, :] = up
    prev_scr[...] = up[tm - POOL_HALO:, :]
    lag = [ext_scr[POOL_HALO - j:POOL_HALO - j + tm, :] for j in range(POOL_HALO)]
    sums = {}
    run = lag[0]
    for j in range(1, POOL_HALO):
        run = run + lag[j]
        if j + 1 in POOL_WINDOWS:
            sums[j + 1] = run
    lane = lax.broadcasted_iota(I32, (tm, POOL_WIDTH), 1)
    pos = i * tm + lax.broadcasted_iota(I32, (tm, POOL_WIDTH), 0)
    pooled_sum = sums[POOL_WINDOWS[-1]]
    win = jnp.full((tm, POOL_WIDTH), POOL_WINDOWS[-1], I32)
    for g in range(N_POOL_GROUPS - 2, -1, -1):
        in_group = lane < (g + 1) * POOL_GROUP_DIM
        pooled_sum = jnp.where(in_group, sums[POOL_WINDOWS[g]], pooled_sum)
        win = jnp.where(in_group, POOL_WINDOWS[g], win)
    cnt = jnp.minimum(pos + 1, win).astype(F32)
    pooled = pooled_sum / cnt - up
    yb = jnp.dot(pooled.astype(BF16), wpool_ref[...], preferred_element_type=F32) * pscale_ref[...]

    ga_ref[0] = _sigmoid(rowdot(R_GATE, D_MODEL)).astype(BF16)
    gb = _sigmoid(rowdot(R_GATE + D_MODEL, D_MODEL))
    mb_ref[0] = (gb * jnp.dot(yb.astype(BF16), wbrb_ref[...], preferred_element_type=F32)).astype(BF16)
    gc_ref[0] = _sigmoid(rowdot(R_GATE + 2 * D_MODEL, D_MODEL)).astype(BF16)


def _inproj_call(x, sh, sc, gain, wrow, wt, cos_r, sin_r, cos_t, sin_t, glat, wuk, wuvt, wpool, pscale, wbrb):
    b, l, d = x.shape
    tm = min(TM_IN, l)
    nq = l // QB
    tok = lambda w: pl.BlockSpec((1, tm, w), lambda bi, i: (bi, i, 0))
    feat = lambda hgt: pl.BlockSpec((1, hgt, tm), lambda bi, i: (bi, 0, i))
    blk = lambda hgt, w: pl.BlockSpec((1, tm // w, hgt, w), lambda bi, i: (bi, i, 0, 0))
    full = lambda a: pl.BlockSpec(a.shape, lambda bi, i: (0,) * a.ndim)
    vec = pl.BlockSpec((1, 1, d), lambda bi, i: (bi, 0, 0))
    out_shape = (
        jax.ShapeDtypeStruct((b, A_WIDTH, l), BF16),
        jax.ShapeDtypeStruct((b, N_IDX_HEADS * IDX_DIM, l), BF16),
        jax.ShapeDtypeStruct((b, T_WI_ROWS, l), F32),
        jax.ShapeDtypeStruct((b, l, HEAD_DIM), BF16),
        jax.ShapeDtypeStruct((b, l // VB_DSA, V_ROWS, VB_DSA), BF16),
        jax.ShapeDtypeStruct((b, l, IDX_DIM), BF16),
        jax.ShapeDtypeStruct((b, 2 * C_WIDTH, l), BF16),
        jax.ShapeDtypeStruct((b, l, C_WIDTH), BF16),
        jax.ShapeDtypeStruct((b, nq, C_WIDTH, QB), BF16),
        jax.ShapeDtypeStruct((b, l, d), BF16),
        jax.ShapeDtypeStruct((b, l, d), BF16),
        jax.ShapeDtypeStruct((b, l, d), BF16),
    )
    out_specs = (feat(A_WIDTH), feat(N_IDX_HEADS * IDX_DIM), feat(T_WI_ROWS), tok(HEAD_DIM), blk(V_ROWS, VB_DSA),
                 tok(IDX_DIM), feat(2 * C_WIDTH), tok(C_WIDTH), blk(C_WIDTH, QB), tok(d), tok(d), tok(d))
    in_specs = [tok(d), vec, vec, full(gain), full(wrow), full(wt), tok(LANES), tok(LANES),
                feat(HEAD_DIM), feat(HEAD_DIM), full(glat), full(wuk), full(wuvt), full(wpool),
                full(pscale), full(wbrb)]
    return pl.pallas_call(
        _inproj_kernel,
        grid=(b, l // tm),
        in_specs=in_specs,
        out_specs=out_specs,
        out_shape=out_shape,
        scratch_shapes=[pltpu.VMEM((tm, d), BF16), pltpu.VMEM((d, tm), BF16),
                        pltpu.VMEM((tm + POOL_HALO, POOL_WIDTH), F32), pltpu.VMEM((POOL_HALO, POOL_WIDTH), F32)],
        compiler_params=_params(("arbitrary", "arbitrary")),
        name="inproj",
    )(x, sh, sc, gain, wrow, wt, cos_r, sin_r, cos_t, sin_t, glat, wuk, wuvt, wpool, pscale, wbrb)


def _dsa_kernel(qit_ref, wit_ref, qat_ref, ki_ref, ka_ref, vat_ref, tri_ref, ya_ref,
                keys_scr, top_scr, acc_scr, *, k_sel):
    i = pl.program_id(1)
    qs = qat_ref.shape[2]
    n_blocks = (i + 1) * (qs // QB)
    n_chunks = (n_blocks * QB + KC - 1) // KC
    qpos = i * qs + lax.broadcasted_iota(I32, (1, qs), 1)
    w_idx = wit_ref[0]

    def score_chunk(c, carry, masked):
        r0 = pl.multiple_of(c * KC, KC)
        kblk = ki_ref[0, pl.ds(r0, KC), :]
        parts = [jnp.dot(kblk, qit_ref[0, hh * IDX_DIM:(hh + 1) * IDX_DIM, :], preferred_element_type=F32)
                 for hh in range(N_IDX_HEADS)]
        score = jnp.maximum(parts[0], 0.0) * w_idx[0:1, :]
        for hh in range(1, N_IDX_HEADS):
            score = score + jnp.maximum(parts[hh], 0.0) * w_idx[hh:hh + 1, :]
        bits = lax.bitcast_convert_type(score, I32)
        key = jnp.where(bits < 0, INT_MIN - bits, bits)
        if masked:
            kpos = r0 + lax.broadcasted_iota(I32, (KC, qs), 0)
            key = jnp.where(kpos <= qpos, key, INT_MIN)
        keys_scr[pl.ds(r0, KC), :] = key
        top_scr[pl.ds(r0, KC), :] = lax.shift_right_arithmetic(key, 16).astype(I16)
        return carry

    n_open = (i * qs) // KC
    lax.fori_loop(0, n_open, functools.partial(score_chunk, masked=False), 0)
    lax.fori_loop(n_open, n_chunks, functools.partial(score_chunk, masked=True), 0)

    def count_rows(src, rows_per_vreg, trial):
        n_acc = 4
        groups = KC // rows_per_vreg

        def body(c, accs):
            r0 = pl.multiple_of(c * KC, KC)
            rows = src[pl.ds(r0, KC), :].reshape(groups, rows_per_vreg, qs)
            accs = list(accs)
            for j in range(groups):
                a = accs[j % n_acc]
                accs[j % n_acc] = jnp.where(rows[j] >= trial, a + 1, a)
            return tuple(accs)

        zero = jnp.zeros((rows_per_vreg, qs), src.dtype)
        accs = lax.fori_loop(0, n_chunks, body, tuple(zero for _ in range(n_acc)))
        total = (accs[0] + accs[1]) + (accs[2] + accs[3])
        return jnp.sum(total.astype(I32), axis=0, keepdims=True)

    def count_ge(trial):
        return count_rows(keys_scr, 8, trial)

    def count_ge_top(trial):
        return count_rows(top_scr, 16, lax.shift_right_arithmetic(trial, 16).astype(I16))

    c_zero = count_ge_top(jnp.zeros((1, qs), I32))
    c_pos = count_ge(jnp.ones((1, qs), I32))
    nonneg = c_zero >= k_sel
    tie_at_zero = jnp.logical_and(nonneg, c_pos < k_sel)
    ans0 = jnp.where(nonneg, 0, INT_MIN)
    c_ans0 = jnp.where(nonneg, c_zero, jnp.int32(2 ** 30))

    def unsettled(c_ans):
        settled = jnp.logical_or(tie_at_zero, c_ans == k_sel)
        return jnp.max(jnp.where(settled, 0, 1))

    def search(counter, lowest_bit, group, state):
        def refine(carry):
            bit, ans, c_ans, _ = carry
            for g in range(group):
                trial = ans + lax.shift_left(jnp.int32(1), bit - g)
                c = counter(trial)
                ok = c >= k_sel
                c_ans = jnp.where(ok, c, c_ans)
                ans = jnp.where(ok, trial, ans)
            return bit - group, ans, c_ans, unsettled(c_ans)

        return lax.while_loop(lambda cr: jnp.logical_and(cr[0] >= lowest_bit, cr[3] > 0), refine, state)

    _, top, c_top, alive = search(count_ge_top, 16, 5, (jnp.int32(30), ans0, c_ans0, unsettled(c_ans0)))
    bucket = lax.shift_right_arithmetic(top, 16)
    last_bucket = bucket >= 2 ** 15 - 1
    c_over = jnp.where(last_bucket, 0,
                       count_rows(top_scr, 16, jnp.where(last_bucket, bucket, bucket + 1).astype(I16)))

    def pack_low(c, carry):
        r0 = pl.multiple_of(c * KC, KC)
        low = (keys_scr[pl.ds(r0, KC), :] ^ 0x8000).astype(I16)
        top_scr[pl.ds(r0, KC), :] = jnp.where(top_scr[pl.ds(r0, KC), :] == bucket.astype(I16), low, -2 ** 15)
        return carry

    lax.fori_loop(0, n_chunks, pack_low, 0)

    def count_ge_low(low_trial):
        return c_over + count_rows(top_scr, 16, (low_trial - 2 ** 15).astype(I16))

    zero_row = jnp.zeros((1, qs), I32)
    _, low, _, _ = search(count_ge_low, 0, 4, (jnp.int32(15), zero_row, c_top, alive))
    thr = top + low
    full_low = low >= 2 ** 16 - 1
    n_above = jnp.where(full_low, c_over, count_ge_low(jnp.where(full_low, low, low + 1)))
    n_ties = jnp.where(thr == INT_MIN, 0, k_sel - n_above).astype(F32)

    acc_scr[...] = jnp.zeros_like(acc_scr)
    heads = range(N_HEADS_A)

    kb_rows = KB_DSA
    vb_rows = vat_ref.shape[3]
    v_per_k = kb_rows // vb_rows

    n_steps = ((i + 1) * qs + kb_rows - 1) // kb_rows

    v_rows = vat_ref.shape[2]

    def attend(kb, carry):
        seen, ms = carry
        r0 = pl.multiple_of(kb * kb_rows, kb_rows)
        keyb = keys_scr[pl.ds(r0, kb_rows), :]
        tied = keyb == thr
        tied_b = jnp.where(tied, 1.0, 0.0).astype(BF16)
        ranks = []
        for j in range(kb_rows // QB):
            within = jnp.dot(tri_ref[...], tied_b[j * QB:(j + 1) * QB, :], preferred_element_type=F32)
            ranks.append(seen + within)
            seen = seen + within[QB - 1:QB, :]
        rank = jnp.concatenate(ranks, axis=0)
        keep = jnp.logical_or(keyb > thr, jnp.logical_and(tied, rank <= n_ties))
        bias = jnp.where(keep, 0.0, NEG_MASK).astype(BF16)
        kblk = ka_ref[0, pl.ds(r0, kb_rows), :]
        logits = [jnp.dot(kblk, qat_ref[0, hh * HEAD_DIM:(hh + 1) * HEAD_DIM, :],
                          preferred_element_type=F32).astype(BF16) + bias for hh in heads]
        new_ms = [jnp.maximum(ms[hh], jnp.max(logits[hh], axis=0, keepdims=True).astype(F32)) for hh in heads]
        probs = [jnp.exp2(logits[hh] - new_ms[hh].astype(BF16)) for hh in heads]
        alphas = [jnp.exp2(ms[hh] - new_ms[hh]) for hh in heads]
        outs = []
        for hh in heads:
            out = jnp.dot(vat_ref[0, kb * v_per_k], probs[hh][:vb_rows, :], preferred_element_type=F32)
            for j in range(1, v_per_k):
                out = out + jnp.dot(vat_ref[0, kb * v_per_k + j], probs[hh][j * vb_rows:(j + 1) * vb_rows, :],
                                    preferred_element_type=F32)
            outs.append(out)
        for hh in heads:
            r = slice(hh * v_rows, (hh + 1) * v_rows)
            acc_scr[r, :] = acc_scr[r, :] * alphas[hh] + outs[hh]
        return seen, tuple(new_ms)

    init = (jnp.zeros((1, qs), F32), tuple(jnp.full((1, qs), NEG_BIG, F32) for _ in heads))
    lax.fori_loop(0, n_steps, attend, init)
    normed = [acc_scr[hh * v_rows:hh * v_rows + HEAD_DIM, :] / acc_scr[hh * v_rows + HEAD_DIM:hh * v_rows + HEAD_DIM + 1, :]
              for hh in heads]
    ya_ref[0] = jnp.concatenate(normed, axis=0).T.astype(BF16)


def _dsa_call(qit, wit, qat, ki, ka, vat, tri):
    b, l, _ = ki.shape
    nq = l // QB
    qs = min(QS_DSA, l)
    k_sel = min(TOPK_MAX, l // 4)
    return pl.pallas_call(
        functools.partial(_dsa_kernel, k_sel=k_sel),
        grid=(b, l // qs),
        in_specs=[
            pl.BlockSpec((1, N_IDX_HEADS * IDX_DIM, qs), lambda bi, i: (bi, 0, i)),
            pl.BlockSpec((1, T_WI_ROWS, qs), lambda bi, i: (bi, 0, i)),
            pl.BlockSpec((1, A_WIDTH, qs), lambda bi, i: (bi, 0, i)),
            pl.BlockSpec((1, l, IDX_DIM), lambda bi, i: (bi, 0, 0), pipeline_mode=pl.Buffered(1)),
            pl.BlockSpec((1, l, HEAD_DIM), lambda bi, i: (bi, 0, 0), pipeline_mode=pl.Buffered(1)),
            pl.BlockSpec((1, l // VB_DSA, V_ROWS, VB_DSA), lambda bi, i: (bi, 0, 0, 0),
                         pipeline_mode=pl.Buffered(1)),
            pl.BlockSpec((QB, QB), lambda bi, i: (0, 0)),
        ],
        out_specs=pl.BlockSpec((1, qs, A_WIDTH), lambda bi, i: (bi, i, 0)),
        out_shape=jax.ShapeDtypeStruct((b, l, A_WIDTH), BF16),
        scratch_shapes=[pltpu.VMEM((l, qs), I32), pltpu.VMEM((l, qs), I16),
                        pltpu.VMEM((N_HEADS_A * V_ROWS, qs), F32)],
        compiler_params=_params(("arbitrary", "arbitrary")),
        name="dsa_attention",
    )(qit, wit, qat, ki, ka, vat, tri)


def _stick_kernel(qct_ref, kc_ref, vct_ref, later_ref, yc_ref, acc_scr):
    i = pl.program_id(1)
    qs = qct_ref.shape[2]
    diag_blocks = qs // QB
    qpos = i * qs + lax.broadcasted_iota(I32, (1, qs), 1)
    acc_scr[...] = jnp.zeros_like(acc_scr)

    def block(kb, tail, masked, lane0=0):
        r0 = pl.multiple_of(kb * QB, QB)
        kfull = kc_ref[0, pl.ds(r0, QB), :]
        vt = vct_ref[0, kb]
        heads = range(N_HEADS_C)
        old = [tail[hh][:, lane0:] for hh in heads]
        if masked:
            mask = (r0 + lax.broadcasted_iota(I32, (QB, qs - lane0), 0)) < qpos[:, lane0:]
        zs = [jnp.dot(kfull[:, (hh // 2) * LANES:(hh // 2 + 1) * LANES],
                      qct_ref[0, hh * LANES:(hh + 1) * LANES, lane0:], preferred_element_type=F32) for hh in heads]
        log_betas, splits, new_tail = [], [], []
        for hh in heads:
            z = zs[hh]
            log_beta = jnp.minimum(z, 0.0) - jnp.log(1.0 + jnp.exp(-jnp.abs(z)))
            log_keep = log_beta - z
            if masked:
                log_keep = jnp.where(mask, log_keep, 0.0)
            hi = log_keep.astype(BF16)
            lo = (log_keep - hi.astype(F32)).astype(BF16)
            log_betas.append(log_beta)
            splits.append(jnp.concatenate([hi, lo], axis=0))
            new = old[hh] + jnp.sum(log_keep, axis=0, keepdims=True)
            new_tail.append(new if lane0 == 0 else jnp.concatenate([tail[hh][:, :lane0], new], axis=1))
        withins = [jnp.dot(later_ref[...], splits[hh], preferred_element_type=F32) for hh in heads]
        weights = []
        for hh in heads:
            a = jnp.exp(log_betas[hh] + withins[hh] + old[hh])
            if masked:
                a = jnp.where(mask, a, 0.0)
            weights.append(a.astype(BF16))
        for hh in heads:
            r = slice(hh * HEAD_DIM, (hh + 1) * HEAD_DIM)
            acc_scr[r, lane0:] = acc_scr[r, lane0:] + jnp.dot(vt[r, :], weights[hh], preferred_element_type=F32)
        return tuple(new_tail)

    def largest(tail):
        worst = tail[0]
        for hh in range(1, N_HEADS_C):
            worst = jnp.maximum(worst, tail[hh])
        return jnp.max(worst)

    n_rest = i * diag_blocks
    tail = tuple(jnp.zeros((1, qs), F32) for _ in range(N_HEADS_C))
    for d in reversed(range(diag_blocks)):
        tail = block(n_rest + d, tail, True, lane0=d * QB)

    def live(cr):
        return jnp.logical_and(cr[0] < n_rest, cr[2] > STICK_DEAD)

    def step(cr):
        t, tl, _ = cr
        tl = block(n_rest - 1 - t, tl, False)
        return t + 1, tl, largest(tl)

    lax.while_loop(live, step, (jnp.int32(0), tail, largest(tail)))
    yc_ref[0] = acc_scr[...].T.astype(BF16)


def _stick_call(qct, kc, vct, later2):
    b, l, _ = kc.shape
    nq = l // QB
    qs = min(QS_STICK, l)
    return pl.pallas_call(
        _stick_kernel,
        grid=(b, l // qs),
        in_specs=[
            pl.BlockSpec((1, 2 * C_WIDTH, qs), lambda bi, i: (bi, 0, i)),
            pl.BlockSpec((1, l, C_WIDTH), lambda bi, i: (bi, 0, 0)),
            pl.BlockSpec((1, nq, C_WIDTH, QB), lambda bi, i: (bi, 0, 0, 0)),
            pl.BlockSpec((QB, 2 * QB), lambda bi, i: (0, 0)),
        ],
        out_specs=pl.BlockSpec((1, qs, C_WIDTH), lambda bi, i: (bi, i, 0)),
        out_shape=jax.ShapeDtypeStruct((b, l, C_WIDTH), BF16),
        scratch_shapes=[pltpu.VMEM((C_WIDTH, qs), F32)],
        compiler_params=_params(("arbitrary", "arbitrary")),
        name="stick_attention",
    )(qct, kc, vct, later2)


def _merge_kernel(x_ref, ya_ref, yc_ref, mb_ref, ga_ref, gc_ref, gt_ref, sh_ref, sc_ref, g1_ref, g2_ref,
                  wbra_ref, wbrc_ref, wout_ref, *rest, with_router):
    if with_router:
        wr_ref, x1_ref, h2_ref, comb_ref = rest
    else:
        x1_ref, h2_ref = rest
    merged = (ga_ref[0].astype(F32) * jnp.dot(ya_ref[0], wbra_ref[...], preferred_element_type=F32)
              + mb_ref[0].astype(F32)
              + gc_ref[0].astype(F32) * jnp.dot(yc_ref[0], wbrc_ref[...], preferred_element_type=F32))
    y = jnp.dot(merged.astype(BF16), wout_ref[...], preferred_element_type=F32)
    x1 = x_ref[0] + gt_ref[0] * _rms(y, g1_ref[...])
    x1_ref[0] = x1
    h2 = _rms(x1, g2_ref[...]) * (1.0 + sc_ref[0]) + sh_ref[0]
    h2_ref[0] = h2.astype(BF16)
    if with_router:
        logits = jnp.dot(h2, wr_ref[...], preferred_element_type=F32)
        lane = lax.broadcasted_iota(I32, logits.shape, 1)
        valid = lane < N_EXPERTS
        l1 = jnp.where(valid, logits, -jnp.inf)
        v1 = jnp.max(l1, axis=-1, keepdims=True)
        i1 = jnp.min(jnp.where(l1 == v1, lane, LANES), axis=-1, keepdims=True)
        l2 = jnp.where(lane == i1, -jnp.inf, l1)
        v2 = jnp.max(l2, axis=-1, keepdims=True)
        i2 = jnp.min(jnp.where(l2 == v2, lane, LANES), axis=-1, keepdims=True)
        e2 = jnp.exp(v2 - v1)
        p1 = 1.0 / (1.0 + e2)
        p2 = e2 / (1.0 + e2)
        comb_ref[0] = jnp.where(lane == i1, p1, 0.0) + jnp.where(lane == i2, p2, 0.0)


def _merge_call(x, ya, yc, mb, ga, gc, gt, sh, sc, g1, g2, wbra, wbrc, wout, wr=None):
    b, l, d = x.shape
    tm = min(TM_MERGE, l)
    tok = lambda w: pl.BlockSpec((1, tm, w), lambda bi, i: (bi, i, 0))
    full = lambda a: pl.BlockSpec(a.shape, lambda bi, i: (0,) * a.ndim)
    vec = pl.BlockSpec((1, 1, d), lambda bi, i: (bi, 0, 0))
    in_specs = [tok(d), tok(A_WIDTH), tok(C_WIDTH), tok(d), tok(d), tok(d), vec, vec, vec,
                full(g1), full(g2), full(wbra), full(wbrc), full(wout)]
    args = [x, ya, yc, mb, ga, gc, gt, sh, sc, g1, g2, wbra, wbrc, wout]
    out_shape = [jax.ShapeDtypeStruct((b, l, d), F32), jax.ShapeDtypeStruct((b, l, d), BF16)]
    out_specs = [tok(d), tok(d)]
    if wr is not None:
        in_specs.append(full(wr))
        args.append(wr)
        out_shape.append(jax.ShapeDtypeStruct((b, l, LANES), F32))
        out_specs.append(tok(LANES))
    return pl.pallas_call(
        functools.partial(_merge_kernel, with_router=wr is not None),
        grid=(b, l // tm),
        in_specs=in_specs,
        out_specs=out_specs,
        out_shape=out_shape,
        compiler_params=_params(("arbitrary", "arbitrary")),
        name="merge_router" if wr is not None else "merge",
    )(*args)


def _ffn_kernel(x1_ref, h2_ref, gt_ref, g3_ref, wg_ref, wu_ref, wd_ref, o_ref):
    h2 = h2_ref[0]
    gate = jnp.dot(h2, wg_ref[...], preferred_element_type=F32)
    up = jnp.dot(h2, wu_ref[...], preferred_element_type=F32)
    act = (gate * _sigmoid(gate) * up).astype(BF16)
    y = jnp.dot(act, wd_ref[...], preferred_element_type=F32)
    o_ref[0] = x1_ref[0] + gt_ref[0] * _rms(y, g3_ref[...])


def _ffn_call(x1, h2, gt, g3, wg, wu, wd):
    b, l, d = x1.shape
    tm = min(TM_FFN, l)
    tok = pl.BlockSpec((1, tm, d), lambda bi, i: (bi, i, 0))
    full = lambda a: pl.BlockSpec(a.shape, lambda bi, i: (0,) * a.ndim)
    vec = pl.BlockSpec((1, 1, d), lambda bi, i: (bi, 0, 0))
    return pl.pallas_call(
        _ffn_kernel,
        grid=(b, l // tm),
        in_specs=[tok, tok, vec, full(g3), full(wg), full(wu), full(wd)],
        out_specs=tok,
        out_shape=jax.ShapeDtypeStruct((b, l, d), F32),
        compiler_params=_params(("arbitrary", "arbitrary")),
        name="ffn_dense",
    )(x1, h2, gt, g3, wg, wu, wd)


def _moe_kernel(x1_ref, h2_ref, comb_ref, gt_ref, g3_ref, wg_ref, wu_ref, wd_ref, o_ref, acc_scr):
    e = pl.program_id(2)

    @pl.when(e == 0)
    def _():
        acc_scr[...] = jnp.zeros_like(acc_scr)

    h2 = h2_ref[0]
    gate = jnp.dot(h2, wg_ref[0], preferred_element_type=F32)
    up = jnp.dot(h2, wu_ref[0], preferred_element_type=F32)
    act = (gate * _sigmoid(gate) * up).astype(BF16)
    y = jnp.dot(act, wd_ref[0], preferred_element_type=F32)
    comb = comb_ref[0]
    lane = lax.broadcasted_iota(I32, comb.shape, 1)
    weight = jnp.sum(jnp.where(lane == e, comb, 0.0), axis=-1, keepdims=True)
    acc_scr[...] = acc_scr[...] + weight * y

    @pl.when(e == pl.num_programs(2) - 1)
    def _():
        o_ref[0] = x1_ref[0] + gt_ref[0] * _rms(acc_scr[...], g3_ref[...])


def _moe_call(x1, h2, comb, gt, g3, wg, wu, wd):
    b, l, d = x1.shape
    tm = min(TM_FFN, l)
    n_e, _, ff = wg.shape
    tok = lambda w: pl.BlockSpec((1, tm, w), lambda bi, i, e: (bi, i, 0))
    vec = pl.BlockSpec((1, 1, d), lambda bi, i, e: (bi, 0, 0))
    return pl.pallas_call(
        _moe_kernel,
        grid=(b, l // tm, n_e),
        in_specs=[tok(d), tok(d), tok(LANES), vec, pl.BlockSpec(g3.shape, lambda bi, i, e: (0, 0)),
                  pl.BlockSpec((1, d, ff), lambda bi, i, e: (e, 0, 0)),
                  pl.BlockSpec((1, d, ff), lambda bi, i, e: (e, 0, 0)),
                  pl.BlockSpec((1, ff, d), lambda bi, i, e: (e, 0, 0))],
        out_specs=tok(d),
        out_shape=jax.ShapeDtypeStruct((b, l, d), F32),
        scratch_shapes=[pltpu.VMEM((tm, d), F32)],
        compiler_params=_params(("arbitrary", "arbitrary", "arbitrary")),
        name="ffn_moe",
    )(x1, h2, comb, gt, g3, wg, wu, wd)


def _rot_cols(w, n_heads, head_dim):
    d = w.shape[0]
    w = w.reshape(d, n_heads, head_dim)
    half = ROPE_DIM // 2
    rot = jnp.concatenate([-w[..., half:ROPE_DIM], w[..., :half],
                           jnp.zeros((d, n_heads, head_dim - ROPE_DIM), w.dtype)], axis=-1)
    return rot.reshape(d, n_heads * head_dim)


def _pad_cols(w, width):
    return jnp.pad(w, ((0, 0), (0, width - w.shape[1])))


def _layer_weights(w_in, w_uk, w_uv, w_pool):
    d = w_in.shape[0]
    offs, o = [], 0
    for s in IN_SIZES:
        offs.append(o)
        o += s
    piece = lambda k: w_in[:, offs[k]:offs[k] + IN_SIZES[k]]
    w_qa, w_lat, w_kr, w_qi, w_ki, w_wi, w_up, w_qc, w_kc, w_vc, w_gate = [piece(k) for k in range(len(IN_SIZES))]
    attn_scale = HEAD_DIM ** -0.5
    idx_scale = IDX_DIM ** -0.5 * N_IDX_HEADS ** -0.5
    wrow = jnp.concatenate([
        w_lat,
        _pad_cols(w_kr, LANES), _pad_cols(_rot_cols(w_kr, 1, ROPE_DIM), LANES),
        _pad_cols(w_ki, LANES), _pad_cols(_rot_cols(w_ki, 1, IDX_DIM), LANES),
        w_up, w_kc, w_gate], axis=1).astype(BF16)
    w_qc_t = (w_qc * attn_scale).T.reshape(N_HEADS_C, HEAD_DIM, d)
    zeros = jnp.zeros_like(w_qc_t)
    even = (jnp.arange(N_HEADS_C) % 2 == 0)[:, None, None]
    w_qc_pad = jnp.concatenate([jnp.where(even, w_qc_t, zeros), jnp.where(even, zeros, w_qc_t)], axis=1)
    w_qc_pad = w_qc_pad.reshape(2 * C_WIDTH, d)
    softmax_scale = attn_scale * LOG2_E
    wt = jnp.concatenate([
        (w_qa * softmax_scale).T, (_rot_cols(w_qa, N_HEADS_A, HEAD_DIM) * softmax_scale).T,
        w_qi.T, _rot_cols(w_qi, N_IDX_HEADS, IDX_DIM).T,
        w_qc_pad, w_vc.T,
        jnp.pad((w_wi * idx_scale).T, ((0, T_WI_ROWS - N_IDX_HEADS), (0, 0)))], axis=0).astype(BF16)
    wuk = jnp.zeros((KV_LATENT, LANES), F32).at[:, ROPE_DIM:HEAD_DIM].set(w_uk).astype(BF16)
    wuvt = w_uv.T.astype(BF16)
    wpool = jnp.zeros((POOL_WIDTH, POOL_WIDTH), F32)
    for g in range(N_POOL_GROUPS):
        sl = slice(g * POOL_GROUP_DIM, (g + 1) * POOL_GROUP_DIM)
        wpool = wpool.at[sl, sl].set(w_pool[g])
    return wrow, wt, wuk, wuvt, wpool.astype(BF16)


def _rope_tables(positions):
    inv = ROPE_THETA ** (-jnp.arange(0, ROPE_DIM, 2, dtype=F32) / ROPE_DIM)
    ang = positions.astype(F32)[..., None] * inv
    cos, sin = jnp.cos(ang), jnp.sin(ang)
    b, l = positions.shape
    ones = jnp.ones((b, l, HEAD_DIM - ROPE_DIM), F32)
    cos_h = jnp.concatenate([cos, cos, ones], axis=-1)
    sin_h = jnp.concatenate([sin, sin, jnp.zeros_like(ones)], axis=-1)
    cos_r = jnp.concatenate([cos_h, cos_h], axis=-1)
    sin_r = jnp.concatenate([sin_h, sin_h], axis=-1)
    return cos_r, sin_r, cos_h.transpose(0, 2, 1), sin_h.transpose(0, 2, 1)


def kernel(x, c, positions, w_ada, b_ada, norm_gains, w_in, g_kv_latent, w_uk, w_uv, w_pool, pool_scale,
           w_br_a, w_br_b, w_br_c, w_out, w_gate_dense, w_up_dense, w_down_dense,
           w_router, w_gate_moe, w_up_moe, w_down_moe):
    b, l, d = x.shape
    depth = w_in.shape[0]
    assert d == D_MODEL and l % KC == 0 and l % TM_FFN == 0
    cos_r, sin_r, cos_t, sin_t = _rope_tables(positions)
    c_pad = jnp.pad(c, ((0, 8 - b), (0, 0)))
    mod = _mod_call(c_pad, w_ada, b_ada)[:, :b]
    idx = lax.broadcasted_iota(I32, (QB, QB), 0)
    jdx = lax.broadcasted_iota(I32, (QB, QB), 1)
    tri = (jdx <= idx).astype(BF16)
    later = (jdx > idx).astype(BF16)
    later2 = jnp.concatenate([later, later], axis=1)
    for layer in range(depth):
        sh1, sc1, gt1, sh2, sc2, gt2 = [m.reshape(b, 1, d) for m in jnp.split(mod[layer], 6, axis=-1)]
        gains = norm_gains[layer].reshape(4, 1, d)
        wrow, wt, wuk, wuvt, wpool = _layer_weights(w_in[layer], w_uk[layer], w_uv[layer], w_pool[layer])
        (qat, qit, wit, ka, vat, ki, qct, kc, vct, ga, gc, mb) = _inproj_call(
            x, sh1, sc1, gains[0], wrow, wt, cos_r, sin_r, cos_t, sin_t,
            g_kv_latent[layer].reshape(1, KV_LATENT), wuk, wuvt, wpool,
            pool_scale[layer].reshape(1, POOL_WIDTH), w_br_b[layer].astype(BF16))
        ya = _dsa_call(qit, wit, qat, ki, ka, vat, tri)
        yc = _stick_call(qct, kc, vct, later2)
        i = layer // 2
        merge_args = (x, ya, yc, mb, ga, gc, gt1, sh2, sc2, gains[1], gains[2],
                      w_br_a[layer].astype(BF16), w_br_c[layer].astype(BF16), w_out[layer].astype(BF16))
        if layer % 2 == 0:
            x1, h2 = _merge_call(*merge_args)
            x = _ffn_call(x1, h2, gt2, gains[3], w_gate_dense[i].astype(BF16), w_up_dense[i].astype(BF16),
                          w_down_dense[i].astype(BF16))
        else:
            x1, h2, comb = _merge_call(*merge_args, wr=_pad_cols(w_router[i], LANES))
            x = _moe_call(x1, h2, comb, gt2, gains[3], w_gate_moe[i].astype(BF16), w_up_moe[i].astype(BF16),
                          w_down_moe[i].astype(BF16))
    return x
```

```python
import functools

import jax
import jax.numpy as jnp
from jax import lax
from jax.experimental import pallas as pl
from jax.experimental.pallas import tpu as pltpu

F32 = jnp.float32
BF16 = jnp.bfloat16
I32 = jnp.int32
I16 = jnp.int16

D_MODEL = 1024
HEAD_DIM = 64
ROPE_DIM = HEAD_DIM // 4
NOPE_DIM = HEAD_DIM - ROPE_DIM
ROPE_THETA = 500000.0
N_HEADS_A = (3 * D_MODEL // 8) // HEAD_DIM
A_WIDTH = N_HEADS_A * HEAD_DIM
KV_LATENT = D_MODEL // 8
N_IDX_HEADS = 4
IDX_DIM = 64
TOPK_MAX = 256
N_POOL_GROUPS = 4
POOL_WINDOWS = (2, 4, 8, 16)
POOL_WIDTH = D_MODEL // 4
POOL_GROUP_DIM = POOL_WIDTH // N_POOL_GROUPS
N_HEADS_C = (D_MODEL // 4) // HEAD_DIM
C_WIDTH = N_HEADS_C * HEAD_DIM
N_BRANCHES = 3
IN_SIZES = (A_WIDTH, KV_LATENT, ROPE_DIM, N_IDX_HEADS * IDX_DIM, IDX_DIM, N_IDX_HEADS,
            POOL_WIDTH, C_WIDTH, C_WIDTH, C_WIDTH, N_BRANCHES * D_MODEL)
D_FF = 2816
N_EXPERTS = 8
TOP_K = 2
D_FF_EXPERT = D_FF // TOP_K
RMS_EPS = 1e-6

LANES = 128
QB = 128
KC = 512
QS_STICK = 512
QS_DSA = 256
KB_DSA = 512
VB_DSA = 256
V_ROWS = 80
STICK_DEAD = -106.0
TM_IN = 256
TM_MERGE = 256
TM_FFN = 512
ROW_CHUNK = 128
POOL_HALO = 16
VMEM_LIMIT = 56 * 1024 * 1024
INT_MIN = -2147483648
LOG2_E = 1.4426950408889634
NEG_BIG = -1e30
NEG_MASK = -2e30

R_LAT, R_KR, R_KRR, R_KI, R_KIR, R_UP, R_KC, R_GATE = 0, 128, 256, 384, 512, 640, 896, 1152
R_WIDTH = R_GATE + N_BRANCHES * D_MODEL
T_QA, T_QAR, T_QI, T_QIR, T_QC, T_VC, T_WI = 0, 384, 768, 1024, 1280, 1792, 2048
T_WI_ROWS = 16
T_HEIGHT = T_WI + T_WI_ROWS


def _params(sem):
    return pltpu.CompilerParams(dimension_semantics=sem, vmem_limit_bytes=VMEM_LIMIT)


def _sigmoid(v):
    return 1.0 / (1.0 + jnp.exp(-v))


def _rms(v, gain):
    return v * lax.rsqrt(jnp.mean(v * v, axis=-1, keepdims=True) + RMS_EPS) * gain


def _mod_kernel(c_ref, w_ref, b_ref, o_ref):
    c = c_ref[...]
    cond = c * _sigmoid(c)
    o_ref[0] = jnp.dot(cond, w_ref[0], preferred_element_type=F32) + b_ref[0]


def _mod_call(c_pad, w_ada, b_ada):
    depth, d, n = w_ada.shape
    tn = 1024
    return pl.pallas_call(
        _mod_kernel,
        grid=(depth, n // tn),
        in_specs=[
            pl.BlockSpec((8, d), lambda l, j: (0, 0)),
            pl.BlockSpec((1, d, tn), lambda l, j: (l, 0, j)),
            pl.BlockSpec((1, 1, tn), lambda l, j: (l, 0, j)),
        ],
        out_specs=pl.BlockSpec((1, 8, tn), lambda l, j: (l, 0, j)),
        out_shape=jax.ShapeDtypeStruct((depth, 8, n), F32),
        compiler_params=_params(("arbitrary", "arbitrary")),
        name="adaln_mod",
    )(c_pad, w_ada, b_ada.reshape(depth, 1, n))


def _inproj_kernel(x_ref, sh_ref, sc_ref, g_ref, wrow_ref, wt_ref, c_ref, s_ref, ct_ref, st_ref,
                   glat_ref, wuk_ref, wuvt_ref, wpool_ref, pscale_ref, wbrb_ref,
                   qat_ref, qit_ref, wit_ref, ka_ref, vat_ref, ki_ref, qct_ref, kc_ref, vct_ref,
                   ga_ref, gc_ref, mb_ref,
                   h_scr, ht_scr, ext_scr, prev_scr):
    i = pl.program_id(1)
    tm = x_ref.shape[1]
    x = x_ref[0]
    h = _rms(x, g_ref[...]) * (1.0 + sc_ref[0]) + sh_ref[0]
    h_scr[...] = h.astype(BF16)
    ht_scr[...] = h.T.astype(BF16)

    def rowdot(a, width):
        return jnp.dot(h_scr[...], wrow_ref[:, a:a + width], preferred_element_type=F32)

    def tdot(a, height):
        return jnp.dot(wt_ref[a:a + height, :], ht_scr[...], preferred_element_type=F32)

    cos_r, sin_r = c_ref[0], s_ref[0]
    cos_t, sin_t = ct_ref[0], st_ref[0]

    latn = _rms(rowdot(R_LAT, KV_LATENT), glat_ref[...])
    ka = (rowdot(R_KR, LANES) * cos_r + rowdot(R_KRR, LANES) * sin_r
          + jnp.dot(latn.astype(BF16), wuk_ref[...], preferred_element_type=F32))
    ka_ref[0] = ka[:, :HEAD_DIM].astype(BF16)
    vat = jnp.dot(wuvt_ref[...], latn.T.astype(BF16), preferred_element_type=F32)
    pad_row = lax.broadcasted_iota(I32, (V_ROWS - HEAD_DIM, tm), 0)
    vat = jnp.concatenate([vat, jnp.where(pad_row == 0, 1.0, 0.0)], axis=0)
    for j in range(tm // VB_DSA):
        vat_ref[0, j] = vat[:, j * VB_DSA:(j + 1) * VB_DSA].astype(BF16)

    ki =rowdot(R_KI, LANES) * cos_r + rowdot(R_KIR, LANES) * sin_r
    ki_ref[0] = ki[:, :IDX_DIM].astype(BF16)

    qa, qar = tdot(T_QA, A_WIDTH), tdot(T_QAR, A_WIDTH)
    for hh in range(N_HEADS_A):
        r = slice(hh * HEAD_DIM, (hh + 1) * HEAD_DIM)
        qat_ref[0, r, :] = (qa[r] * cos_t + qar[r] * sin_t).astype(BF16)
    qi, qir = tdot(T_QI, N_IDX_HEADS * IDX_DIM), tdot(T_QIR, N_IDX_HEADS * IDX_DIM)
    for hh in range(N_IDX_HEADS):
        r = slice(hh * IDX_DIM, (hh + 1) * IDX_DIM)
        qit_ref[0, r, :] = (qi[r] * cos_t + qir[r] * sin_t).astype(BF16)
    wit_ref[0] = tdot(T_WI, T_WI_ROWS)

    qct_ref[0] = tdot(T_QC, 2 * C_WIDTH).astype(BF16)
    kc_ref[0] = rowdot(R_KC, C_WIDTH).astype(BF16)
    vct = tdot(T_VC, C_WIDTH)
    for j in range(tm // QB):
        vct_ref[0, j] = vct[:, j * QB:(j + 1) * QB].astype(BF16)

    up = rowdot(R_UP, POOL_WIDTH)

    @pl.when(i == 0)
    def _():
        prev_scr[...] = jnp.zeros_like(prev_scr)

    ext_scr[0:POOL_HALO, :] = prev_scr[...]
    ext_scr[POOL_HALO:POOL_HALO + tm, :] = up
    prev_scr[...] = up[tm - POOL_HALO:, :]
    lag = [ext_scr[POOL_HALO - j:POOL_HALO - j + tm, :] for j in range(POOL_HALO)]
    sums = {}
    run = lag[0]
    for j in range(1, POOL_HALO):
        run = run + lag[j]
        if j + 1 in POOL_WINDOWS:
            sums[j + 1] = run
    lane = lax.broadcasted_iota(I32, (tm, POOL_WIDTH), 1)
    pos = i * tm + lax.broadcasted_iota(I32, (tm, POOL_WIDTH), 0)
    pooled_sum = sums[POOL_WINDOWS[-1]]
    win = jnp.full((tm, POOL_WIDTH), POOL_WINDOWS[-1], I32)
    for g in range(N_POOL_GROUPS - 2, -1, -1):
        in_group = lane < (g + 1) * POOL_GROUP_DIM
        pooled_sum = jnp.where(in_group, sums[POOL_WINDOWS[g]], pooled_sum)
        win = jnp.where(in_group, POOL_WINDOWS[g], win)
    cnt = jnp.minimum(pos + 1, win).astype(F32)
    pooled = pooled_sum / cnt - up
    yb = jnp.dot(pooled.astype(BF16), wpool_ref[...], preferred_element_type=F32) * pscale_ref[...]

    ga_ref[0] = _sigmoid(rowdot(R_GATE, D_MODEL)).astype(BF16)
    gb = _sigmoid(rowdot(R_GATE + D_MODEL, D_MODEL))
    mb_ref[0] = (gb * jnp.dot(yb.astype(BF16), wbrb_ref[...], preferred_element_type=F32)).astype(BF16)
    gc_ref[0] = _sigmoid(rowdot(R_GATE + 2 * D_MODEL, D_MODEL)).astype(BF16)


def _inproj_call(x, sh, sc, gain, wrow, wt, cos_r, sin_r, cos_t, sin_t, glat, wuk, wuvt, wpool, pscale, wbrb):
    b, l, d = x.shape
    tm = min(TM_IN, l)
    nq = l // QB
    tok = lambda w: pl.BlockSpec((1, tm, w), lambda bi, i: (bi, i, 0))
    feat = lambda hgt: pl.BlockSpec((1, hgt, tm), lambda bi, i: (bi, 0, i))
    blk = lambda hgt, w: pl.BlockSpec((1, tm // w, hgt, w), lambda bi, i: (bi, i, 0, 0))
    full = lambda a: pl.BlockSpec(a.shape, lambda bi, i: (0,) * a.ndim)
    vec = pl.BlockSpec((1, 1, d), lambda bi, i: (bi, 0, 0))
    out_shape = (
        jax.ShapeDtypeStruct((b, A_WIDTH, l), BF16),
        jax.ShapeDtypeStruct((b, N_IDX_HEADS * IDX_DIM, l), BF16),
        jax.ShapeDtypeStruct((b, T_WI_ROWS, l), F32),
        jax.ShapeDtypeStruct((b, l, HEAD_DIM), BF16),
        jax.ShapeDtypeStruct((b, l // VB_DSA, V_ROWS, VB_DSA), BF16),
        jax.ShapeDtypeStruct((b, l, IDX_DIM), BF16),
        jax.ShapeDtypeStruct((b, 2 * C_WIDTH, l), BF16),
        jax.ShapeDtypeStruct((b, l, C_WIDTH), BF16),
        jax.ShapeDtypeStruct((b, nq, C_WIDTH, QB), BF16),
        jax.ShapeDtypeStruct((b, l, d), BF16),
        jax.ShapeDtypeStruct((b, l, d), BF16),
        jax.ShapeDtypeStruct((b, l, d), BF16),
    )
    out_specs = (feat(A_WIDTH), feat(N_IDX_HEADS * IDX_DIM), feat(T_WI_ROWS), tok(HEAD_DIM), blk(V_ROWS, VB_DSA),
                 tok(IDX_DIM), feat(2 * C_WIDTH), tok(C_WIDTH), blk(C_WIDTH, QB), tok(d), tok(d), tok(d))
    in_specs = [tok(d), vec, vec, full(gain), full(wrow), full(wt), tok(LANES), tok(LANES),
                feat(HEAD_DIM), feat(HEAD_DIM), full(glat), full(wuk), full(wuvt), full(wpool),
                full(pscale), full(wbrb)]
    return pl.pallas_call(
        _inproj_kernel,
        grid=(b, l // tm),
        in_specs=in_specs,
        out_specs=out_specs,
        out_shape=out_shape,
        scratch_shapes=[pltpu.VMEM((tm, d), BF16), pltpu.VMEM((d, tm), BF16),
                        pltpu.VMEM((tm + POOL_HALO, POOL_WIDTH), F32), pltpu.VMEM((POOL_HALO, POOL_WIDTH), F32)],
        compiler_params=_params(("arbitrary", "arbitrary")),
        name="inproj",
    )(x, sh, sc, gain, wrow, wt, cos_r, sin_r, cos_t, sin_t, glat, wuk, wuvt, wpool, pscale, wbrb)


def _dsa_kernel(qit_ref, wit_ref, qat_ref, ki_ref, ka_ref, vat_ref, tri_ref, ya_ref,
                keys_scr, top_scr, acc_scr, *, k_sel):
    i = pl.program_id(1)
    qs = qat_ref.shape[2]
    n_blocks = (i + 1) * (qs // QB)
    n_chunks = (n_blocks * QB + KC - 1) // KC
    qpos = i * qs + lax.broadcasted_iota(I32, (1, qs), 1)
    w_idx = wit_ref[0]

    def score_chunk(c, carry, masked):
        r0 = pl.multiple_of(c * KC, KC)
        kblk = ki_ref[0, pl.ds(r0, KC), :]
        parts = [jnp.dot(kblk, qit_ref[0, hh * IDX_DIM:(hh + 1) * IDX_DIM, :], preferred_element_type=F32)
                 for hh in range(N_IDX_HEADS)]
        score = jnp.maximum(parts[0], 0.0) * w_idx[0:1, :]
        for hh in range(1, N_IDX_HEADS):
            score = score + jnp.maximum(parts[hh], 0.0) * w_idx[hh:hh + 1, :]
        bits = lax.bitcast_convert_type(score, I32)
        key = jnp.where(bits < 0, INT_MIN - bits, bits)
        if masked:
            kpos = r0 + lax.broadcasted_iota(I32, (KC, qs), 0)
            key = jnp.where(kpos <= qpos, key, INT_MIN)
        keys_scr[pl.ds(r0, KC), :] = key
        top_scr[pl.ds(r0, KC), :] = lax.shift_right_arithmetic(key, 16).astype(I16)
        return carry

    n_open = (i * qs) // KC
    lax.fori_loop(0, n_open, functools.partial(score_chunk, masked=False), 0)
    lax.fori_loop(n_open, n_chunks, functools.partial(score_chunk, masked=True), 0)

    def count_rows(src, rows_per_vreg, trial):
        n_acc = 4
        groups = KC // rows_per_vreg

        def body(c, accs):
            r0 = pl.multiple_of(c * KC, KC)
            rows = src[pl.ds(r0, KC), :].reshape(groups, rows_per_vreg, qs)
            accs = list(accs)
            for j in range(groups):
                a = accs[j % n_acc]
                accs[j % n_acc] = jnp.where(rows[j] >= trial, a + 1, a)
            return tuple(accs)

        zero = jnp.zeros((rows_per_vreg, qs), src.dtype)
        accs = lax.fori_loop(0, n_chunks, body, tuple(zero for _ in range(n_acc)))
        total = (accs[0] + accs[1]) + (accs[2] + accs[3])
        return jnp.sum(total.astype(I32), axis=0, keepdims=True)

    def count_ge(trial):
        return count_rows(keys_scr, 8, trial)

    def count_ge_top(trial):
        return count_rows(top_scr, 16, lax.shift_right_arithmetic(trial, 16).astype(I16))

    c_zero = count_ge_top(jnp.zeros((1, qs), I32))
    c_pos = count_ge(jnp.ones((1, qs), I32))
    nonneg = c_zero >= k_sel
    tie_at_zero = jnp.logical_and(nonneg, c_pos < k_sel)
    ans0 = jnp.where(nonneg, 0, INT_MIN)
    c_ans0 = jnp.where(nonneg, c_zero, jnp.int32(2 ** 30))

    def unsettled(c_ans):
        settled = jnp.logical_or(tie_at_zero, c_ans == k_sel)
        return jnp.max(jnp.where(settled, 0, 1))

    def search(counter, lowest_bit, group, state):
        def refine(carry):
            bit, ans, c_ans, _ = carry
            for g in range(group):
                trial = ans + lax.shift_left(jnp.int32(1), bit - g)
                c = counter(trial)
                ok = c >= k_sel
                c_ans = jnp.where(ok, c, c_ans)
                ans = jnp.where(ok, trial, ans)
            return bit - group, ans, c_ans, unsettled(c_ans)

        return lax.while_loop(lambda cr: jnp.logical_and(cr[0] >= lowest_bit, cr[3] > 0), refine, state)

    _, top, c_top, alive = search(count_ge_top, 16, 5, (jnp.int32(30), ans0, c_ans0, unsettled(c_ans0)))
    bucket = lax.shift_right_arithmetic(top, 16)
    last_bucket = bucket >= 2 ** 15 - 1
    c_over = jnp.where(last_bucket, 0,
                       count_rows(top_scr, 16, jnp.where(last_bucket, bucket, bucket + 1).astype(I16)))

    def pack_low(c, carry):
        r0 = pl.multiple_of(c * KC, KC)
        low = (keys_scr[pl.ds(r0, KC), :] ^ 0x8000).astype(I16)
        top_scr[pl.ds(r0, KC), :] = jnp.where(top_scr[pl.ds(r0, KC), :] == bucket.astype(I16), low, -2 ** 15)
        return carry

    lax.fori_loop(0, n_chunks, pack_low, 0)

    def count_ge_low(low_trial):
        return c_over + count_rows(top_scr, 16, (low_trial - 2 ** 15).astype(I16))

    zero_row = jnp.zeros((1, qs), I32)
    _, low, _, _ = search(count_ge_low, 0, 4, (jnp.int32(15), zero_row, c_top, alive))
    thr = top + low
    full_low = low >= 2 ** 16 - 1
    n_above = jnp.where(full_low, c_over, count_ge_low(jnp.where(full_low, low, low + 1)))
    n_ties = jnp.where(thr == INT_MIN, 0, k_sel - n_above).astype(F32)

    acc_scr[...] = jnp.zeros_like(acc_scr)
    heads = range(N_HEADS_A)

    kb_rows = KB_DSA
    vb_rows = vat_ref.shape[3]
    v_per_k = kb_rows // vb_rows

    n_steps = ((i + 1) * qs + kb_rows - 1) // kb_rows

    v_rows = vat_ref.shape[2]

    def attend(kb, carry):
        seen, ms = carry
        r0 = pl.multiple_of(kb * kb_rows, kb_rows)
        keyb = keys_scr[pl.ds(r0, kb_rows), :]
        tied = keyb == thr
        tied_b = jnp.where(tied, 1.0, 0.0).astype(BF16)
        ranks = []
        for j in range(kb_rows // QB):
            within = jnp.dot(tri_ref[...], tied_b[j * QB:(j + 1) * QB, :], preferred_element_type=F32)
            ranks.append(seen + within)
            seen = seen + within[QB - 1:QB, :]
        rank = jnp.concatenate(ranks, axis=0)
        keep = jnp.logical_or(keyb > thr, jnp.logical_and(tied, rank <= n_ties))
        bias = jnp.where(keep, 0.0, NEG_MASK).astype(BF16)
        kblk = ka_ref[0, pl.ds(r0, kb_rows), :]
        logits = [jnp.dot(kblk, qat_ref[0, hh * HEAD_DIM:(hh + 1) * HEAD_DIM, :],
                          preferred_element_type=F32).astype(BF16) + bias for hh in heads]
        new_ms = [jnp.maximum(ms[hh], jnp.max(logits[hh], axis=0, keepdims=True).astype(F32)) for hh in heads]
        probs = [jnp.exp2(logits[hh] - new_ms[hh].astype(BF16)) for hh in heads]
        alphas = [jnp.exp2(ms[hh] - new_ms[hh]) for hh in heads]
        outs = []
        for hh in heads:
            out = jnp.dot(vat_ref[0, kb * v_per_k], probs[hh][:vb_rows, :], preferred_element_type=F32)
            for j in range(1, v_per_k):
                out = out + jnp.dot(vat_ref[0, kb * v_per_k + j], probs[hh][j * vb_rows:(j + 1) * vb_rows, :],
                                    preferred_element_type=F32)
            outs.append(out)
        for hh in heads:
            r = slice(hh * v_rows, (hh + 1) * v_rows)
            acc_scr[r, :] = acc_scr[r, :] * alphas[hh] + outs[hh]
        return seen, tuple(new_ms)

    init = (jnp.zeros((1, qs), F32), tuple(jnp.full((1, qs), NEG_BIG, F32) for _ in heads))
    lax.fori_loop(0, n_steps, attend, init)
    normed = [acc_scr[hh * v_rows:hh * v_rows + HEAD_DIM, :] / acc_scr[hh * v_rows + HEAD_DIM:hh * v_rows + HEAD_DIM + 1, :]
              for hh in heads]
    ya_ref[0] = jnp.concatenate(normed, axis=0).T.astype(BF16)


def _dsa_call(qit, wit, qat, ki, ka, vat, tri):
    b, l, _ = ki.shape
    nq = l // QB
    qs = min(QS_DSA, l)
    k_sel = min(TOPK_MAX, l // 4)
    return pl.pallas_call(
        functools.partial(_dsa_kernel, k_sel=k_sel),
        grid=(b, l // qs),
        in_specs=[
            pl.BlockSpec((1, N_IDX_HEADS * IDX_DIM, qs), lambda bi, i: (bi, 0, i)),
            pl.BlockSpec((1, T_WI_ROWS, qs), lambda bi, i: (bi, 0, i)),
            pl.BlockSpec((1, A_WIDTH, qs), lambda bi, i: (bi, 0, i)),
            pl.BlockSpec((1, l, IDX_DIM), lambda bi, i: (bi, 0, 0), pipeline_mode=pl.Buffered(1)),
            pl.BlockSpec((1, l, HEAD_DIM), lambda bi, i: (bi, 0, 0), pipeline_mode=pl.Buffered(1)),
            pl.BlockSpec((1, l // VB_DSA, V_ROWS, VB_DSA), lambda bi, i: (bi, 0, 0, 0),
                         pipeline_mode=pl.Buffered(1)),
            pl.BlockSpec((QB, QB), lambda bi, i: (0, 0)),
        ],
        out_specs=pl.BlockSpec((1, qs, A_WIDTH), lambda bi, i: (bi, i, 0)),
        out_shape=jax.ShapeDtypeStruct((b, l, A_WIDTH), BF16),
        scratch_shapes=[pltpu.VMEM((l, qs), I32), pltpu.VMEM((l, qs), I16),
                        pltpu.VMEM((N_HEADS_A * V_ROWS, qs), F32)],
        compiler_params=_params(("arbitrary", "arbitrary")),
        name="dsa_attention",
    )(qit, wit, qat, ki, ka, vat, tri)


def _stick_kernel(qct_ref, kc_ref, vct_ref, later_ref, yc_ref, acc_scr):
    i = pl.program_id(1)
    qs = qct_ref.shape[2]
    diag_blocks = qs // QB
    qpos = i * qs + lax.broadcasted_iota(I32, (1, qs), 1)
    acc_scr[...] = jnp.zeros_like(acc_scr)

    def block(kb, tail, masked, lane0=0):
        r0 = pl.multiple_of(kb * QB, QB)
        kfull = kc_ref[0, pl.ds(r0, QB), :]
        vt = vct_ref[0, kb]
        heads = range(N_HEADS_C)
        old = [tail[hh][:, lane0:] for hh in heads]
        if masked:
            mask = (r0 + lax.broadcasted_iota(I32, (QB, qs - lane0), 0)) < qpos[:, lane0:]
        zs = [jnp.dot(kfull[:, (hh // 2) * LANES:(hh // 2 + 1) * LANES],
                      qct_ref[0, hh * LANES:(hh + 1) * LANES, lane0:], preferred_element_type=F32) for hh in heads]
        log_betas, splits, new_tail = [], [], []
        for hh in heads:
            z = zs[hh]
            log_beta = jnp.minimum(z, 0.0) - jnp.log(1.0 + jnp.exp(-jnp.abs(z)))
            log_keep = log_beta - z
            if masked:
                log_keep = jnp.where(mask, log_keep, 0.0)
            hi = log_keep.astype(BF16)
            lo = (log_keep - hi.astype(F32)).astype(BF16)
            log_betas.append(log_beta)
            splits.append(jnp.concatenate([hi, lo], axis=0))
            new = old[hh] + jnp.sum(log_keep, axis=0, keepdims=True)
            new_tail.append(new if lane0 == 0 else jnp.concatenate([tail[hh][:, :lane0], new], axis=1))
        withins = [jnp.dot(later_ref[...], splits[hh], preferred_element_type=F32) for hh in heads]
        weights = []
        for hh in heads:
            a = jnp.exp(log_betas[hh] + withins[hh] + old[hh])
            if masked:
                a = jnp.where(mask, a, 0.0)
            weights.append(a.astype(BF16))
        for hh in heads:
            r = slice(hh * HEAD_DIM, (hh + 1) * HEAD_DIM)
            acc_scr[r, lane0:] = acc_scr[r, lane0:] + jnp.dot(vt[r, :], weights[hh], preferred_element_type=F32)
        return tuple(new_tail)

    def largest(tail):
        worst = tail[0]
        for hh in range(1, N_HEADS_C):
            worst = jnp.maximum(worst, tail[hh])
        return jnp.max(worst)

    n_rest = i * diag_blocks
    tail = tuple(jnp.zeros((1, qs), F32) for _ in range(N_HEADS_C))
    for d in reversed(range(diag_blocks)):
        tail = block(n_rest + d, tail, True, lane0=d * QB)

    def live(cr):
        return jnp.logical_and(cr[0] < n_rest, cr[2] > STICK_DEAD)

    def step(cr):
        t, tl, _ = cr
        tl = block(n_rest - 1 - t, tl, False)
        return t + 1, tl, largest(tl)

    lax.while_loop(live, step, (jnp.int32(0), tail, largest(tail)))
    yc_ref[0] = acc_scr[...].T.astype(BF16)


def _stick_call(qct, kc, vct, later2):
    b, l, _ = kc.shape
    nq = l // QB
    qs = min(QS_STICK, l)
    return pl.pallas_call(
        _stick_kernel,
        grid=(b, l // qs),
        in_specs=[
            pl.BlockSpec((1, 2 * C_WIDTH, qs), lambda bi, i: (bi, 0, i)),
            pl.BlockSpec((1, l, C_WIDTH), lambda bi, i: (bi, 0, 0)),
            pl.BlockSpec((1, nq, C_WIDTH, QB), lambda bi, i: (bi, 0, 0, 0)),
            pl.BlockSpec((QB, 2 * QB), lambda bi, i: (0, 0)),
        ],
        out_specs=pl.BlockSpec((1, qs, C_WIDTH), lambda bi, i: (bi, i, 0)),
        out_shape=jax.ShapeDtypeStruct((b, l, C_WIDTH), BF16),
        scratch_shapes=[pltpu.VMEM((C_WIDTH, qs), F32)],
        compiler_params=_params(("arbitrary", "arbitrary")),
        name="stick_attention",
    )(qct, kc, vct, later2)


def _merge_kernel(x_ref, ya_ref, yc_ref, mb_ref, ga_ref, gc_ref, gt_ref, sh_ref, sc_ref, g1_ref, g2_ref,
                  wbra_ref, wbrc_ref, wout_ref, *rest, with_router):
    if with_router:
        wr_ref, x1_ref, h2_ref, route_ref, cnt_ref = rest
    else:
        x1_ref, h2_ref = rest
    merged = (ga_ref[0].astype(F32) * jnp.dot(ya_ref[0], wbra_ref[...], preferred_element_type=F32)
              + mb_ref[0].astype(F32)
              + gc_ref[0].astype(F32) * jnp.dot(yc_ref[0], wbrc_ref[...], preferred_element_type=F32))
    y = jnp.dot(merged.astype(BF16), wout_ref[...], preferred_element_type=F32)
    x1 = x_ref[0] + gt_ref[0] * _rms(y, g1_ref[...])
    x1_ref[0] = x1
    h2 = _rms(x1, g2_ref[...]) * (1.0 + sc_ref[0]) + sh_ref[0]
    h2_ref[0] = h2.astype(BF16)
    if with_router:
        logits = jnp.dot(h2, wr_ref[...], preferred_element_type=F32)
        lane = lax.broadcasted_iota(I32, logits.shape, 1)
        valid = lane < N_EXPERTS
        l1 = jnp.where(valid, logits, -jnp.inf)
        v1 = jnp.max(l1, axis=-1, keepdims=True)
        i1 = jnp.min(jnp.where(l1 == v1, lane, LANES), axis=-1, keepdims=True)
        l2 = jnp.where(lane == i1, -jnp.inf, l1)
        v2 = jnp.max(l2, axis=-1, keepdims=True)
        i2 = jnp.min(jnp.where(l2 == v2, lane, LANES), axis=-1, keepdims=True)
        e2 = jnp.exp(v2 - v1)
        p1 = 1.0 / (1.0 + e2)
        p2 = e2 / (1.0 + e2)
        route = jnp.where(lane == 0, i1.astype(F32), jnp.where(lane == 1, i2.astype(F32),
                          jnp.where(lane == 2, p1, jnp.where(lane == 3, p2, 0.0))))
        route_ref[0] = route
        chosen = jnp.where(jnp.logical_or(lane == i1, lane == i2), 1.0, 0.0)
        cnt_ref[0, 0] = jnp.broadcast_to(jnp.sum(chosen, axis=0, keepdims=True), (8, LANES))


def _merge_call(x, ya, yc, mb, ga, gc, gt, sh, sc, g1, g2, wbra, wbrc, wout, wr=None):
    b, l, d = x.shape
    tm = min(TM_MERGE if wr is None else TM_FFN, l)
    tok = lambda w: pl.BlockSpec((1, tm, w), lambda bi, i: (bi, i, 0))
    full = lambda a: pl.BlockSpec(a.shape, lambda bi, i: (0,) * a.ndim)
    vec = pl.BlockSpec((1, 1, d), lambda bi, i: (bi, 0, 0))
    in_specs = [tok(d), tok(A_WIDTH), tok(C_WIDTH), tok(d), tok(d), tok(d), vec, vec, vec,
                full(g1), full(g2), full(wbra), full(wbrc), full(wout)]
    args = [x, ya, yc, mb, ga, gc, gt, sh, sc, g1, g2, wbra, wbrc, wout]
    out_shape = [jax.ShapeDtypeStruct((b, l, d), F32), jax.ShapeDtypeStruct((b, l, d), BF16)]
    out_specs = [tok(d), tok(d)]
    if wr is not None:
        in_specs.append(full(wr))
        args.append(wr)
        out_shape += [jax.ShapeDtypeStruct((b, l, LANES), F32), jax.ShapeDtypeStruct((b, l // tm, 8, LANES), F32)]
        out_specs += [tok(LANES), pl.BlockSpec((1, 1, 8, LANES), lambda bi, i: (bi, i, 0, 0))]
    return pl.pallas_call(
        functools.partial(_merge_kernel, with_router=wr is not None),
        grid=(b, l // tm),
        in_specs=in_specs,
        out_specs=out_specs,
        out_shape=out_shape,
        compiler_params=_params(("arbitrary", "arbitrary")),
        name="merge_router" if wr is not None else "merge",
    )(*args)


def _ffn_kernel(x1_ref, h2_ref, gt_ref, g3_ref, wg_ref, wu_ref, wd_ref, o_ref):
    h2 = h2_ref[0]
    gate = jnp.dot(h2, wg_ref[...], preferred_element_type=F32)
    up = jnp.dot(h2, wu_ref[...], preferred_element_type=F32)
    act = (gate * _sigmoid(gate) * up).astype(BF16)
    y = jnp.dot(act, wd_ref[...], preferred_element_type=F32)
    o_ref[0] = x1_ref[0] + gt_ref[0] * _rms(y, g3_ref[...])


def _ffn_call(x1, h2, gt, g3, wg, wu, wd):
    b, l, d = x1.shape
    tm = min(TM_FFN, l)
    tok = pl.BlockSpec((1, tm, d), lambda bi, i: (bi, i, 0))
    full = lambda a: pl.BlockSpec(a.shape, lambda bi, i: (0,) * a.ndim)
    vec = pl.BlockSpec((1, 1, d), lambda bi, i: (bi, 0, 0))
    return pl.pallas_call(
        _ffn_kernel,
        grid=(b, l // tm),
        in_specs=[tok, tok, vec, full(g3), full(wg), full(wu), full(wd)],
        out_specs=tok,
        out_shape=jax.ShapeDtypeStruct((b, l, d), F32),
        compiler_params=_params(("arbitrary", "arbitrary")),
        name="ffn_dense",
    )(x1, h2, gt, g3, wg, wu, wd)


def _moe_kernel(nchunk_ref, first_ref, total_ref,
                x1_ref, h2_ref, route_ref, before_ref, gt_ref, g3_ref, wg_ref, wu_ref, wd_ref, o_ref,
                xs_scr, ys_scr, wrow_scr, dest_scr, acc_scr):
    tile = pl.program_id(0) * pl.num_programs(1) + pl.program_id(1)
    e = pl.program_id(2)
    n_e = pl.num_programs(2)
    tm = h2_ref.shape[1]
    total = total_ref[tile]

    @pl.when(e == 0)
    def _():
        route = route_ref[0]
        lane = lax.broadcasted_iota(I32, route.shape, 1).astype(F32)
        hot = [jnp.where(lane == route[:, s:s + 1], 1.0, 0.0) for s in range(TOP_K)]
        earlier = jnp.dot(before_ref[...], (hot[0] + hot[1]).astype(BF16), preferred_element_type=F32)
        start = jnp.zeros((1, LANES), F32)
        for ee in range(N_EXPERTS):
            start = jnp.where(lane[:1, :] == ee, (first_ref[tile * N_EXPERTS + ee] * ROW_CHUNK).astype(F32), start)
        where_to = earlier + start
        dest = [hot[s] * where_to for s in range(TOP_K)]
        prob = [hot[s] * route[:, TOP_K + s:TOP_K + s + 1] for s in range(TOP_K)]
        for s in range(TOP_K):
            dest_scr[s] = jnp.broadcast_to(jnp.sum(dest[s], axis=1, keepdims=True), (tm, LANES))
        dest_row = [jnp.sum(dest[s].T, axis=0, keepdims=True) for s in range(TOP_K)]
        prob_row = [jnp.sum(prob[s].T, axis=0, keepdims=True) for s in range(TOP_K)]
        h2 = h2_ref[0]

        def place(c, carry):
            r0 = pl.multiple_of(c * ROW_CHUNK, ROW_CHUNK)
            rows = (r0 + lax.broadcasted_iota(I32, (ROW_CHUNK, tm), 0)).astype(F32)
            here = [rows == dest_row[s] for s in range(TOP_K)]
            pick = jnp.where(jnp.logical_or(here[0], here[1]), 1.0, 0.0).astype(BF16)
            xs_scr[pl.ds(r0, ROW_CHUNK), :] = jnp.dot(pick, h2, preferred_element_type=F32).astype(BF16)
            weight = jnp.where(here[0], prob_row[0], 0.0) + jnp.where(here[1], prob_row[1], 0.0)
            wrow_scr[pl.ds(r0, ROW_CHUNK), :] = jnp.broadcast_to(jnp.sum(weight, axis=1, keepdims=True),
                                                                  (ROW_CHUNK, LANES))
            return carry

        lax.fori_loop(0, total, place, 0)

    def expert_chunk(c, carry):
        r0 = pl.multiple_of((first_ref[tile * n_e + e] + c) * ROW_CHUNK, ROW_CHUNK)
        xs = xs_scr[pl.ds(r0, ROW_CHUNK), :]
        gate = jnp.dot(xs, wg_ref[0], preferred_element_type=F32)
        up = jnp.dot(xs, wu_ref[0], preferred_element_type=F32)
        act = (gate * _sigmoid(gate) * up).astype(BF16)
        y = jnp.dot(act, wd_ref[0], preferred_element_type=F32)
        ys_scr[pl.ds(r0, ROW_CHUNK), :] = (y * wrow_scr[pl.ds(r0, ROW_CHUNK), 0:1]).astype(BF16)
        return carry

    lax.fori_loop(0, nchunk_ref[tile * n_e + e], expert_chunk, 0)

    @pl.when(e == n_e - 1)
    def _():
        acc_scr[...] = jnp.zeros_like(acc_scr)

        def collect(c, carry):
            r0 = pl.multiple_of(c * ROW_CHUNK, ROW_CHUNK)
            cols = (r0 + lax.broadcasted_iota(I32, (tm, ROW_CHUNK), 1)).astype(F32)
            mine = jnp.logical_or(cols == dest_scr[0], cols == dest_scr[1])
            acc_scr[...] = acc_scr[...] + jnp.dot(jnp.where(mine, 1.0, 0.0).astype(BF16),
                                                  ys_scr[pl.ds(r0, ROW_CHUNK), :], preferred_element_type=F32)
            return carry

        lax.fori_loop(0, total, collect, 0)
        o_ref[0] = x1_ref[0] + gt_ref[0] * _rms(acc_scr[...], g3_ref[...])


def _moe_call(x1, h2, route, counts, before, gt, g3, wg, wu, wd):
    b, l, d = x1.shape
    tm = min(TM_FFN, l)
    n_e, _, ff = wg.shape
    cnt = counts[:, :, 0, :n_e].astype(I32).reshape(-1, n_e)
    nchunk = (cnt + ROW_CHUNK - 1) // ROW_CHUNK
    first = jnp.cumsum(nchunk, axis=1) - nchunk
    total = jnp.sum(nchunk, axis=1)
    rows = TOP_K * tm + n_e * ROW_CHUNK
    tok = lambda w: pl.BlockSpec((1, tm, w), lambda bi, i, e, *_: (bi, i, 0))
    vec = pl.BlockSpec((1, 1, d), lambda bi, i, e, *_: (bi, 0, 0))
    grid_spec = pltpu.PrefetchScalarGridSpec(
        num_scalar_prefetch=3,
        grid=(b, l // tm, n_e),
        in_specs=[tok(d), tok(d), tok(LANES), pl.BlockSpec(before.shape, lambda bi, i, e, *_: (0, 0)), vec,
                  pl.BlockSpec(g3.shape, lambda bi, i, e, *_: (0, 0)),
                  pl.BlockSpec((1, d, ff), lambda bi, i, e, *_: (e, 0, 0)),
                  pl.BlockSpec((1, d, ff), lambda bi, i, e, *_: (e, 0, 0)),
                  pl.BlockSpec((1, ff, d), lambda bi, i, e, *_: (e, 0, 0))],
        out_specs=tok(d),
        scratch_shapes=[pltpu.VMEM((rows, d), BF16), pltpu.VMEM((rows, d), BF16), pltpu.VMEM((rows, LANES), F32),
                        pltpu.VMEM((TOP_K, tm, LANES), F32), pltpu.VMEM((tm, d), F32)],
    )
    return pl.pallas_call(
        _moe_kernel,
        grid_spec=grid_spec,
        out_shape=jax.ShapeDtypeStruct((b, l, d), F32),
        compiler_params=_params(("arbitrary", "arbitrary", "arbitrary")),
        name="ffn_moe",
    )(nchunk.reshape(-1), first.reshape(-1), total, x1, h2, route, before, gt, g3, wg, wu, wd)


def _rot_cols(w, n_heads, head_dim):
    d = w.shape[0]
    w = w.reshape(d, n_heads, head_dim)
    half = ROPE_DIM // 2
    rot = jnp.concatenate([-w[..., half:ROPE_DIM], w[..., :half],
                           jnp.zeros((d, n_heads, head_dim - ROPE_DIM), w.dtype)], axis=-1)
    return rot.reshape(d, n_heads * head_dim)


def _pad_cols(w, width):
    return jnp.pad(w, ((0, 0), (0, width - w.shape[1])))


def _layer_weights(w_in, w_uk, w_uv, w_pool):
    d = w_in.shape[0]
    offs, o = [], 0
    for s in IN_SIZES:
        offs.append(o)
        o += s
    piece = lambda k: w_in[:, offs[k]:offs[k] + IN_SIZES[k]]
    w_qa, w_lat, w_kr, w_qi, w_ki, w_wi, w_up, w_qc, w_kc, w_vc, w_gate = [piece(k) for k in range(len(IN_SIZES))]
    attn_scale = HEAD_DIM ** -0.5
    idx_scale = IDX_DIM ** -0.5 * N_IDX_HEADS ** -0.5
    wrow = jnp.concatenate([
        w_lat,
        _pad_cols(w_kr, LANES), _pad_cols(_rot_cols(w_kr, 1, ROPE_DIM), LANES),
        _pad_cols(w_ki, LANES), _pad_cols(_rot_cols(w_ki, 1, IDX_DIM), LANES),
        w_up, w_kc, w_gate], axis=1).astype(BF16)
    w_qc_t = (w_qc * attn_scale).T.reshape(N_HEADS_C, HEAD_DIM, d)
    zeros = jnp.zeros_like(w_qc_t)
    even = (jnp.arange(N_HEADS_C) % 2 == 0)[:, None, None]
    w_qc_pad = jnp.concatenate([jnp.where(even, w_qc_t, zeros), jnp.where(even, zeros, w_qc_t)], axis=1)
    w_qc_pad = w_qc_pad.reshape(2 * C_WIDTH, d)
    softmax_scale = attn_scale * LOG2_E
    wt = jnp.concatenate([
        (w_qa * softmax_scale).T, (_rot_cols(w_qa, N_HEADS_A, HEAD_DIM) * softmax_scale).T,
        w_qi.T, _rot_cols(w_qi, N_IDX_HEADS, IDX_DIM).T,
        w_qc_pad, w_vc.T,
        jnp.pad((w_wi * idx_scale).T, ((0, T_WI_ROWS - N_IDX_HEADS), (0, 0)))], axis=0).astype(BF16)
    wuk = jnp.zeros((KV_LATENT, LANES), F32).at[:, ROPE_DIM:HEAD_DIM].set(w_uk).astype(BF16)
    wuvt = w_uv.T.astype(BF16)
    wpool = jnp.zeros((POOL_WIDTH, POOL_WIDTH), F32)
    for g in range(N_POOL_GROUPS):
        sl = slice(g * POOL_GROUP_DIM, (g + 1) * POOL_GROUP_DIM)
        wpool = wpool.at[sl, sl].set(w_pool[g])
    return wrow, wt, wuk, wuvt, wpool.astype(BF16)


def _rope_tables(positions):
    inv = ROPE_THETA ** (-jnp.arange(0, ROPE_DIM, 2, dtype=F32) / ROPE_DIM)
    ang = positions.astype(F32)[..., None] * inv
    cos, sin = jnp.cos(ang), jnp.sin(ang)
    b, l = positions.shape
    ones = jnp.ones((b, l, HEAD_DIM - ROPE_DIM), F32)
    cos_h = jnp.concatenate([cos, cos, ones], axis=-1)
    sin_h = jnp.concatenate([sin, sin, jnp.zeros_like(ones)], axis=-1)
    cos_r = jnp.concatenate([cos_h, cos_h], axis=-1)
    sin_r = jnp.concatenate([sin_h, sin_h], axis=-1)
    return cos_r, sin_r, cos_h.transpose(0, 2, 1), sin_h.transpose(0, 2, 1)


def kernel(x, c, positions, w_ada, b_ada, norm_gains, w_in, g_kv_latent, w_uk, w_uv, w_pool, pool_scale,
           w_br_a, w_br_b, w_br_c, w_out, w_gate_dense, w_up_dense, w_down_dense,
           w_router, w_gate_moe, w_up_moe, w_down_moe):
    b, l, d = x.shape
    depth = w_in.shape[0]
    assert d == D_MODEL and l % KC == 0 and l % TM_FFN == 0
    cos_r, sin_r, cos_t, sin_t = _rope_tables(positions)
    c_pad = jnp.pad(c, ((0, 8 - b), (0, 0)))
    mod = _mod_call(c_pad, w_ada, b_ada)[:, :b]
    idx = lax.broadcasted_iota(I32, (QB, QB), 0)
    jdx = lax.broadcasted_iota(I32, (QB, QB), 1)
    tri = (jdx <= idx).astype(BF16)
    later = (jdx > idx).astype(BF16)
    later2 = jnp.concatenate([later, later], axis=1)
    tm_ffn = min(TM_FFN, l)
    before = (lax.broadcasted_iota(I32, (tm_ffn, tm_ffn), 1)
              < lax.broadcasted_iota(I32, (tm_ffn, tm_ffn), 0)).astype(BF16)
    for layer in range(depth):
        sh1, sc1, gt1, sh2, sc2, gt2 = [m.reshape(b, 1, d) for m in jnp.split(mod[layer], 6, axis=-1)]
        gains = norm_gains[layer].reshape(4, 1, d)
        wrow, wt, wuk, wuvt, wpool = _layer_weights(w_in[layer], w_uk[layer], w_uv[layer], w_pool[layer])
        (qat, qit, wit, ka, vat, ki, qct, kc, vct, ga, gc, mb) = _inproj_call(
            x, sh1, sc1, gains[0], wrow, wt, cos_r, sin_r, cos_t, sin_t,
            g_kv_latent[layer].reshape(1, KV_LATENT), wuk, wuvt, wpool,
            pool_scale[layer].reshape(1, POOL_WIDTH), w_br_b[layer].astype(BF16))
        ya = _dsa_call(qit, wit, qat, ki, ka, vat, tri)
        yc = _stick_call(qct, kc, vct, later2)
        i = layer // 2
        merge_args = (x, ya, yc, mb, ga, gc, gt1, sh2, sc2, gains[1], gains[2],
                      w_br_a[layer].astype(BF16), w_br_c[layer].astype(BF16), w_out[layer].astype(BF16))
        if layer % 2 == 0:
            x1, h2 = _merge_call(*merge_args)
            x = _ffn_call(x1, h2, gt2, gains[3], w_gate_dense[i].astype(BF16), w_up_dense[i].astype(BF16),
                          w_down_dense[i].astype(BF16))
        else:
            x1, h2, route, counts = _merge_call(*merge_args, wr=_pad_cols(w_router[i], LANES))
            x = _moe_call(x1, h2, route, counts, before, gt2, gains[3], w_gate_moe[i].astype(BF16),
                          w_up_moe[i].astype(BF16), w_down_moe[i].astype(BF16))
    return x
```

```python
import functools

import jax
import jax.numpy as jnp
from jax import lax
from jax.experimental import pallas as pl
from jax.experimental.pallas import tpu as pltpu

F32 = jnp.float32
BF16 = jnp.bfloat16
I32 = jnp.int32
I16 = jnp.int16

D_MODEL = 1024
HEAD_DIM = 64
ROPE_DIM = HEAD_DIM // 4
NOPE_DIM = HEAD_DIM - ROPE_DIM
ROPE_THETA = 500000.0
N_HEADS_A = (3 * D_MODEL // 8) // HEAD_DIM
A_WIDTH = N_HEADS_A * HEAD_DIM
KV_LATENT = D_MODEL // 8
N_IDX_HEADS = 4
IDX_DIM = 64
TOPK_MAX = 256
N_POOL_GROUPS = 4
POOL_WINDOWS = (2, 4, 8, 16)
POOL_WIDTH = D_MODEL // 4
POOL_GROUP_DIM = POOL_WIDTH // N_POOL_GROUPS
N_HEADS_C = (D_MODEL // 4) // HEAD_DIM
C_WIDTH = N_HEADS_C * HEAD_DIM
N_BRANCHES = 3
IN_SIZES = (A_WIDTH, KV_LATENT, ROPE_DIM, N_IDX_HEADS * IDX_DIM, IDX_DIM, N_IDX_HEADS,
            POOL_WIDTH, C_WIDTH, C_WIDTH, C_WIDTH, N_BRANCHES * D_MODEL)
D_FF = 2816
N_EXPERTS = 8
TOP_K = 2
D_FF_EXPERT = D_FF // TOP_K
RMS_EPS = 1e-6

LANES = 128
QB = 128
KC = 512
QS_STICK = 512
QS_DSA = 256
KB_DSA = 512
VB_DSA = 256
V_ROWS = 80
STICK_DEAD = -106.0
TM_IN = 256
TM_MERGE = 256
TM_FFN = 512
TM_MOE = 1024
ROW_CHUNK = 128
POOL_HALO = 16
VMEM_LIMIT = 56 * 1024 * 1024
INT_MIN = -2147483648
LOG2_E = 1.4426950408889634
NEG_BIG = -1e30
NEG_MASK = -2e30

R_LAT, R_KR, R_KRR, R_KI, R_KIR, R_UP, R_KC, R_GATE = 0, 128, 256, 384, 512, 640, 896, 1152
R_WIDTH = R_GATE + N_BRANCHES * D_MODEL
T_QA, T_QAR, T_QI, T_QIR, T_QC, T_VC, T_WI = 0, 384, 768, 1024, 1280, 1792, 2048
T_WI_ROWS = 16
T_HEIGHT = T_WI + T_WI_ROWS


def _params(sem):
    return pltpu.CompilerParams(dimension_semantics=sem, vmem_limit_bytes=VMEM_LIMIT)


def _sigmoid(v):
    return 1.0 / (1.0 + jnp.exp(-v))


def _rms(v, gain):
    return v * lax.rsqrt(jnp.mean(v * v, axis=-1, keepdims=True) + RMS_EPS) * gain


def _mod_kernel(c_ref, w_ref, b_ref, o_ref):
    c = c_ref[...]
    cond = c * _sigmoid(c)
    o_ref[0] = jnp.dot(cond, w_ref[0], preferred_element_type=F32) + b_ref[0]


def _mod_call(c_pad, w_ada, b_ada):
    depth, d, n = w_ada.shape
    tn = 1024
    return pl.pallas_call(
        _mod_kernel,
        grid=(depth, n // tn),
        in_specs=[
            pl.BlockSpec((8, d), lambda l, j: (0, 0)),
            pl.BlockSpec((1, d, tn), lambda l, j: (l, 0, j)),
            pl.BlockSpec((1, 1, tn), lambda l, j: (l, 0, j)),
        ],
        out_specs=pl.BlockSpec((1, 8, tn), lambda l, j: (l, 0, j)),
        out_shape=jax.ShapeDtypeStruct((depth, 8, n), F32),
        compiler_params=_params(("arbitrary", "arbitrary")),
        name="adaln_mod",
    )(c_pad, w_ada, b_ada.reshape(depth, 1, n))


def _inproj_kernel(x_ref, sh_ref, sc_ref, g_ref, wrow_ref, wt_ref, c_ref, s_ref, ct_ref, st_ref,
                   glat_ref, wuk_ref, wuvt_ref, wpool_ref, pscale_ref, wbrb_ref,
                   qat_ref, qit_ref, wit_ref, ka_ref, vat_ref, ki_ref, qct_ref, kc_ref, vct_ref,
                   ga_ref, gc_ref, mb_ref,
                   h_scr, ht_scr, ext_scr, prev_scr):
    i = pl.program_id(1)
    tm = x_ref.shape[1]
    x = x_ref[0]
    h = _rms(x, g_ref[...]) * (1.0 + sc_ref[0]) + sh_ref[0]
    h_scr[...] = h.astype(BF16)
    ht_scr[...] = h.T.astype(BF16)

    def rowdot(a, width):
        return jnp.dot(h_scr[...], wrow_ref[:, a:a + width], preferred_element_type=F32)

    def tdot(a, height):
        return jnp.dot(wt_ref[a:a + height, :], ht_scr[...], preferred_element_type=F32)

    cos_r, sin_r = c_ref[0], s_ref[0]
    cos_t, sin_t = ct_ref[0], st_ref[0]

    latn = _rms(rowdot(R_LAT, KV_LATENT), glat_ref[...])
    ka = (rowdot(R_KR, LANES) * cos_r + rowdot(R_KRR, LANES) * sin_r
          + jnp.dot(latn.astype(BF16), wuk_ref[...], preferred_element_type=F32))
    ka_ref[0] = ka[:, :HEAD_DIM].astype(BF16)
    vat = jnp.dot(wuvt_ref[...], latn.T.astype(BF16), preferred_element_type=F32)
    pad_row = lax.broadcasted_iota(I32, (V_ROWS - HEAD_DIM, tm), 0)
    vat = jnp.concatenate([vat, jnp.where(pad_row == 0, 1.0, 0.0)], axis=0)
    for j in range(tm // VB_DSA):
        vat_ref[0, j] = vat[:, j * VB_DSA:(j + 1) * VB_DSA].astype(BF16)

    ki =rowdot(R_KI, LANES) * cos_r + rowdot(R_KIR, LANES) * sin_r
    ki_ref[0] = ki[:, :IDX_DIM].astype(BF16)

    qa, qar = tdot(T_QA, A_WIDTH), tdot(T_QAR, A_WIDTH)
    for hh in range(N_HEADS_A):
        r = slice(hh * HEAD_DIM, (hh + 1) * HEAD_DIM)
        qat_ref[0, r, :] = (qa[r] * cos_t + qar[r] * sin_t).astype(BF16)
    qi, qir = tdot(T_QI, N_IDX_HEADS * IDX_DIM), tdot(T_QIR, N_IDX_HEADS * IDX_DIM)
    for hh in range(N_IDX_HEADS):
        r = slice(hh * IDX_DIM, (hh + 1) * IDX_DIM)
        qit_ref[0, r, :] = (qi[r] * cos_t + qir[r] * sin_t).astype(BF16)
    wit_ref[0] = tdot(T_WI, T_WI_ROWS)

    qct_ref[0] = tdot(T_QC, 2 * C_WIDTH).astype(BF16)
    kc_ref[0] = rowdot(R_KC, C_WIDTH).astype(BF16)
    vct = tdot(T_VC, C_WIDTH)
    for j in range(tm // QB):
        vct_ref[0, j] = vct[:, j * QB:(j + 1) * QB].astype(BF16)

    up = rowdot(R_UP, POOL_WIDTH)

    @pl.when(i == 0)
    def _():
        prev_scr[...] = jnp.zeros_like(prev_scr)

    ext_scr[0:POOL_HALO, :] = prev_scr[...]
    ext_scr[POOL_HALO:POOL_HALO + tm, :] = up
    prev_scr[...] = up[tm - POOL_HALO:, :]
    lag = [ext_scr[POOL_HALO - j:POOL_HALO - j + tm, :] for j in range(POOL_HALO)]
    sums = {}
    run = lag[0]
    for j in range(1, POOL_HALO):
        run = run + lag[j]
        if j + 1 in POOL_WINDOWS:
            sums[j + 1] = run
    lane = lax.broadcasted_iota(I32, (tm, POOL_WIDTH), 1)
    pos = i * tm + lax.broadcasted_iota(I32, (tm, POOL_WIDTH), 0)
    pooled_sum = sums[POOL_WINDOWS[-1]]
    win = jnp.full((tm, POOL_WIDTH), POOL_WINDOWS[-1], I32)
    for g in range(N_POOL_GROUPS - 2, -1, -1):
        in_group = lane < (g + 1) * POOL_GROUP_DIM
        pooled_sum = jnp.where(in_group, sums[POOL_WINDOWS[g]], pooled_sum)
        win = jnp.where(in_group, POOL_WINDOWS[g], win)
    cnt = jnp.minimum(pos + 1, win).astype(F32)
    pooled = pooled_sum / cnt - up
    yb = jnp.dot(pooled.astype(BF16), wpool_ref[...], preferred_element_type=F32) * pscale_ref[...]

    ga_ref[0] = _sigmoid(rowdot(R_GATE, D_MODEL)).astype(BF16)
    gb = _sigmoid(rowdot(R_GATE + D_MODEL, D_MODEL))
    mb_ref[0] = (gb * jnp.dot(yb.astype(BF16), wbrb_ref[...], preferred_element_type=F32)).astype(BF16)
    gc_ref[0] = _sigmoid(rowdot(R_GATE + 2 * D_MODEL, D_MODEL)).astype(BF16)


def _inproj_call(x, sh, sc, gain, wrow, wt, cos_r, sin_r, cos_t, sin_t, glat, wuk, wuvt, wpool, pscale, wbrb):
    b, l, d = x.shape
    tm = min(TM_IN, l)
    nq = l // QB
    tok = lambda w: pl.BlockSpec((1, tm, w), lambda bi, i: (bi, i, 0))
    feat = lambda hgt: pl.BlockSpec((1, hgt, tm), lambda bi, i: (bi, 0, i))
    blk = lambda hgt, w: pl.BlockSpec((1, tm // w, hgt, w), lambda bi, i: (bi, i, 0, 0))
    full = lambda a: pl.BlockSpec(a.shape, lambda bi, i: (0,) * a.ndim)
    vec = pl.BlockSpec((1, 1, d), lambda bi, i: (bi, 0, 0))
    out_shape = (
        jax.ShapeDtypeStruct((b, A_WIDTH, l), BF16),
        jax.ShapeDtypeStruct((b, N_IDX_HEADS * IDX_DIM, l), BF16),
        jax.ShapeDtypeStruct((b, T_WI_ROWS, l), F32),
        jax.ShapeDtypeStruct((b, l, HEAD_DIM), BF16),
        jax.ShapeDtypeStruct((b, l // VB_DSA, V_ROWS, VB_DSA), BF16),
        jax.ShapeDtypeStruct((b, l, IDX_DIM), BF16),
        jax.ShapeDtypeStruct((b, 2 * C_WIDTH, l), BF16),
        jax.ShapeDtypeStruct((b, l, C_WIDTH), BF16),
        jax.ShapeDtypeStruct((b, nq, C_WIDTH, QB), BF16),
        jax.ShapeDtypeStruct((b, l, d), BF16),
        jax.ShapeDtypeStruct((b, l, d), BF16),
        jax.ShapeDtypeStruct((b, l, d), BF16),
    )
    out_specs = (feat(A_WIDTH), feat(N_IDX_HEADS * IDX_DIM), feat(T_WI_ROWS), tok(HEAD_DIM), blk(V_ROWS, VB_DSA),
                 tok(IDX_DIM), feat(2 * C_WIDTH), tok(C_WIDTH), blk(C_WIDTH, QB), tok(d), tok(d), tok(d))
    in_specs = [tok(d), vec, vec, full(gain), full(wrow), full(wt), tok(LANES), tok(LANES),
                feat(HEAD_DIM), feat(HEAD_DIM), full(glat), full(wuk), full(wuvt), full(wpool),
                full(pscale), full(wbrb)]
    return pl.pallas_call(
        _inproj_kernel,
        grid=(b, l // tm),
        in_specs=in_specs,
        out_specs=out_specs,
        out_shape=out_shape,
        scratch_shapes=[pltpu.VMEM((tm, d), BF16), pltpu.VMEM((d, tm), BF16),
                        pltpu.VMEM((tm + POOL_HALO, POOL_WIDTH), F32), pltpu.VMEM((POOL_HALO, POOL_WIDTH), F32)],
        compiler_params=_params(("arbitrary", "arbitrary")),
        name="inproj",
    )(x, sh, sc, gain, wrow, wt, cos_r, sin_r, cos_t, sin_t, glat, wuk, wuvt, wpool, pscale, wbrb)


def _dsa_kernel(qit_ref, wit_ref, qat_ref, ki_ref, ka_ref, vat_ref, tri_ref, ya_ref,
                keys_scr, top_scr, acc_scr, *, k_sel):
    i = pl.program_id(1)
    qs = qat_ref.shape[2]
    n_blocks = (i + 1) * (qs // QB)
    n_chunks = (n_blocks * QB + KC - 1) // KC
    qpos = i * qs + lax.broadcasted_iota(I32, (1, qs), 1)
    w_idx = wit_ref[0]

    def score_chunk(c, carry, masked):
        r0 = pl.multiple_of(c * KC, KC)
        kblk = ki_ref[0, pl.ds(r0, KC), :]
        parts = [jnp.dot(kblk, qit_ref[0, hh * IDX_DIM:(hh + 1) * IDX_DIM, :], preferred_element_type=F32)
                 for hh in range(N_IDX_HEADS)]
        score = jnp.maximum(parts[0], 0.0) * w_idx[0:1, :]
        for hh in range(1, N_IDX_HEADS):
            score = score + jnp.maximum(parts[hh], 0.0) * w_idx[hh:hh + 1, :]
        bits = lax.bitcast_convert_type(score, I32)
        key = jnp.where(bits < 0, INT_MIN - bits, bits)
        if masked:
            kpos = r0 + lax.broadcasted_iota(I32, (KC, qs), 0)
            key = jnp.where(kpos <= qpos, key, INT_MIN)
        keys_scr[pl.ds(r0, KC), :] = key
        top_scr[pl.ds(r0, KC), :] = lax.shift_right_arithmetic(key, 16).astype(I16)
        return carry

    n_open = (i * qs) // KC
    lax.fori_loop(0, n_open, functools.partial(score_chunk, masked=False), 0)
    lax.fori_loop(n_open, n_chunks, functools.partial(score_chunk, masked=True), 0)

    def count_rows(src, rows_per_vreg, trial):
        n_acc = 4
        groups = KC // rows_per_vreg

        def body(c, accs):
            r0 = pl.multiple_of(c * KC, KC)
            rows = src[pl.ds(r0, KC), :].reshape(groups, rows_per_vreg, qs)
            accs = list(accs)
            for j in range(groups):
                a = accs[j % n_acc]
                accs[j % n_acc] = jnp.where(rows[j] >= trial, a + 1, a)
            return tuple(accs)

        zero = jnp.zeros((rows_per_vreg, qs), src.dtype)
        accs = lax.fori_loop(0, n_chunks, body, tuple(zero for _ in range(n_acc)))
        total = (accs[0] + accs[1]) + (accs[2] + accs[3])
        return jnp.sum(total.astype(I32), axis=0, keepdims=True)

    def count_ge(trial):
        return count_rows(keys_scr, 8, trial)

    def count_ge_top(trial):
        return count_rows(top_scr, 16, lax.shift_right_arithmetic(trial, 16).astype(I16))

    c_zero = count_ge_top(jnp.zeros((1, qs), I32))
    c_pos = count_ge(jnp.ones((1, qs), I32))
    nonneg = c_zero >= k_sel
    tie_at_zero = jnp.logical_and(nonneg, c_pos < k_sel)
    ans0 = jnp.where(nonneg, 0, INT_MIN)
    c_ans0 = jnp.where(nonneg, c_zero, jnp.int32(2 ** 30))

    def unsettled(c_ans):
        settled = jnp.logical_or(tie_at_zero, c_ans == k_sel)
        return jnp.max(jnp.where(settled, 0, 1))

    def search(counter, lowest_bit, group, state):
        def refine(carry):
            bit, ans, c_ans, _ = carry
            for g in range(group):
                trial = ans + lax.shift_left(jnp.int32(1), bit - g)
                c = counter(trial)
                ok = c >= k_sel
                c_ans = jnp.where(ok, c, c_ans)
                ans = jnp.where(ok, trial, ans)
            return bit - group, ans, c_ans, unsettled(c_ans)

        return lax.while_loop(lambda cr: jnp.logical_and(cr[0] >= lowest_bit, cr[3] > 0), refine, state)

    _, top, c_top, alive = search(count_ge_top, 16, 5, (jnp.int32(30), ans0, c_ans0, unsettled(c_ans0)))
    bucket = lax.shift_right_arithmetic(top, 16)
    last_bucket = bucket >= 2 ** 15 - 1
    c_over = jnp.where(last_bucket, 0,
                       count_rows(top_scr, 16, jnp.where(last_bucket, bucket, bucket + 1).astype(I16)))

    def pack_low(c, carry):
        r0 = pl.multiple_of(c * KC, KC)
        low = (keys_scr[pl.ds(r0, KC), :] ^ 0x8000).astype(I16)
        top_scr[pl.ds(r0, KC), :] = jnp.where(top_scr[pl.ds(r0, KC), :] == bucket.astype(I16), low, -2 ** 15)
        return carry

    lax.fori_loop(0, n_chunks, pack_low, 0)

    def count_ge_low(low_trial):
        return c_over + count_rows(top_scr, 16, (low_trial - 2 ** 15).astype(I16))

    zero_row = jnp.zeros((1, qs), I32)
    _, low, _, _ = search(count_ge_low, 0, 4, (jnp.int32(15), zero_row, c_top, alive))
    thr = top + low
    full_low = low >= 2 ** 16 - 1
    n_above = jnp.where(full_low, c_over, count_ge_low(jnp.where(full_low, low, low + 1)))
    n_ties = jnp.where(thr == INT_MIN, 0, k_sel - n_above).astype(F32)

    acc_scr[...] = jnp.zeros_like(acc_scr)
    heads = range(N_HEADS_A)

    kb_rows = KB_DSA
    vb_rows = vat_ref.shape[3]
    v_per_k = kb_rows // vb_rows

    n_steps = ((i + 1) * qs + kb_rows - 1) // kb_rows

    v_rows = vat_ref.shape[2]

    def attend(kb, carry):
        seen, ms = carry
        r0 = pl.multiple_of(kb * kb_rows, kb_rows)
        keyb = keys_scr[pl.ds(r0, kb_rows), :]
        tied = keyb == thr
        tied_b = jnp.where(tied, 1.0, 0.0).astype(BF16)
        ranks = []
        for j in range(kb_rows // QB):
            within = jnp.dot(tri_ref[...], tied_b[j * QB:(j + 1) * QB, :], preferred_element_type=F32)
            ranks.append(seen + within)
            seen = seen + within[QB - 1:QB, :]
        rank = jnp.concatenate(ranks, axis=0)
        keep = jnp.logical_or(keyb > thr, jnp.logical_and(tied, rank <= n_ties))
        bias = jnp.where(keep, 0.0, NEG_MASK).astype(BF16)
        kblk = ka_ref[0, pl.ds(r0, kb_rows), :]
        logits = [jnp.dot(kblk, qat_ref[0, hh * HEAD_DIM:(hh + 1) * HEAD_DIM, :],
                          preferred_element_type=F32).astype(BF16) + bias for hh in heads]
        new_ms = [jnp.maximum(ms[hh], jnp.max(logits[hh], axis=0, keepdims=True).astype(F32)) for hh in heads]
        probs = [jnp.exp2(logits[hh] - new_ms[hh].astype(BF16)) for hh in heads]
        alphas = [jnp.exp2(ms[hh] - new_ms[hh]) for hh in heads]
        outs = []
        for hh in heads:
            out = jnp.dot(vat_ref[0, kb * v_per_k], probs[hh][:vb_rows, :], preferred_element_type=F32)
            for j in range(1, v_per_k):
                out = out + jnp.dot(vat_ref[0, kb * v_per_k + j], probs[hh][j * vb_rows:(j + 1) * vb_rows, :],
                                    preferred_element_type=F32)
            outs.append(out)
        for hh in heads:
            r = slice(hh * v_rows, (hh + 1) * v_rows)
            acc_scr[r, :] = acc_scr[r, :] * alphas[hh] + outs[hh]
        return seen, tuple(new_ms)

    init = (jnp.zeros((1, qs), F32), tuple(jnp.full((1, qs), NEG_BIG, F32) for _ in heads))
    lax.fori_loop(0, n_steps, attend, init)
    normed = [acc_scr[hh * v_rows:hh * v_rows + HEAD_DIM, :] / acc_scr[hh * v_rows + HEAD_DIM:hh * v_rows + HEAD_DIM + 1, :]
              for hh in heads]
    ya_ref[0] = jnp.concatenate(normed, axis=0).T.astype(BF16)


def _dsa_call(qit, wit, qat, ki, ka, vat, tri):
    b, l, _ = ki.shape
    nq = l // QB
    qs = min(QS_DSA, l)
    k_sel = min(TOPK_MAX, l // 4)
    return pl.pallas_call(
        functools.partial(_dsa_kernel, k_sel=k_sel),
        grid=(b, l // qs),
        in_specs=[
            pl.BlockSpec((1, N_IDX_HEADS * IDX_DIM, qs), lambda bi, i: (bi, 0, i)),
            pl.BlockSpec((1, T_WI_ROWS, qs), lambda bi, i: (bi, 0, i)),
            pl.BlockSpec((1, A_WIDTH, qs), lambda bi, i: (bi, 0, i)),
            pl.BlockSpec((1, l, IDX_DIM), lambda bi, i: (bi, 0, 0), pipeline_mode=pl.Buffered(1)),
            pl.BlockSpec((1, l, HEAD_DIM), lambda bi, i: (bi, 0, 0), pipeline_mode=pl.Buffered(1)),
            pl.BlockSpec((1, l // VB_DSA, V_ROWS, VB_DSA), lambda bi, i: (bi, 0, 0, 0),
                         pipeline_mode=pl.Buffered(1)),
            pl.BlockSpec((QB, QB), lambda bi, i: (0, 0)),
        ],
        out_specs=pl.BlockSpec((1, qs, A_WIDTH), lambda bi, i: (bi, i, 0)),
        out_shape=jax.ShapeDtypeStruct((b, l, A_WIDTH), BF16),
        scratch_shapes=[pltpu.VMEM((l, qs), I32), pltpu.VMEM((l, qs), I16),
                        pltpu.VMEM((N_HEADS_A * V_ROWS, qs), F32)],
        compiler_params=_params(("arbitrary", "arbitrary")),
        name="dsa_attention",
    )(qit, wit, qat, ki, ka, vat, tri)


def _stick_kernel(qct_ref, kc_ref, vct_ref, later_ref, yc_ref, acc_scr):
    i = pl.program_id(1)
    qs = qct_ref.shape[2]
    diag_blocks = qs // QB
    qpos = i * qs + lax.broadcasted_iota(I32, (1, qs), 1)
    acc_scr[...] = jnp.zeros_like(acc_scr)

    def block(kb, tail, masked, lane0=0):
        r0 = pl.multiple_of(kb * QB, QB)
        kfull = kc_ref[0, pl.ds(r0, QB), :]
        vt = vct_ref[0, kb]
        heads = range(N_HEADS_C)
        old = [tail[hh][:, lane0:] for hh in heads]
        if masked:
            mask = (r0 + lax.broadcasted_iota(I32, (QB, qs - lane0), 0)) < qpos[:, lane0:]
        zs = [jnp.dot(kfull[:, (hh // 2) * LANES:(hh // 2 + 1) * LANES],
                      qct_ref[0, hh * LANES:(hh + 1) * LANES, lane0:], preferred_element_type=F32) for hh in heads]
        log_betas, splits, new_tail = [], [], []
        for hh in heads:
            z = zs[hh]
            log_beta = jnp.minimum(z, 0.0) - jnp.log(1.0 + jnp.exp(-jnp.abs(z)))
            log_keep = log_beta - z
            if masked:
                log_keep = jnp.where(mask, log_keep, 0.0)
            hi = log_keep.astype(BF16)
            lo = (log_keep - hi.astype(F32)).astype(BF16)
            log_betas.append(log_beta)
            splits.append(jnp.concatenate([hi, lo], axis=0))
            new = old[hh] + jnp.sum(log_keep, axis=0, keepdims=True)
            new_tail.append(new if lane0 == 0 else jnp.concatenate([tail[hh][:, :lane0], new], axis=1))
        withins = [jnp.dot(later_ref[...], splits[hh], preferred_element_type=F32) for hh in heads]
        weights = []
        for hh in heads:
            a = jnp.exp(log_betas[hh] + withins[hh] + old[hh])
            if masked:
                a = jnp.where(mask, a, 0.0)
            weights.append(a.astype(BF16))
        for hh in heads:
            r = slice(hh * HEAD_DIM, (hh + 1) * HEAD_DIM)
            acc_scr[r, lane0:] = acc_scr[r, lane0:] + jnp.dot(vt[r, :], weights[hh], preferred_element_type=F32)
        return tuple(new_tail)

    def largest(tail):
        worst = tail[0]
        for hh in range(1, N_HEADS_C):
            worst = jnp.maximum(worst, tail[hh])
        return jnp.max(worst)

    n_rest = i * diag_blocks
    tail = tuple(jnp.zeros((1, qs), F32) for _ in range(N_HEADS_C))
    for d in reversed(range(diag_blocks)):
        tail = block(n_rest + d, tail, True, lane0=d * QB)

    def live(cr):
        return jnp.logical_and(cr[0] < n_rest, cr[2] > STICK_DEAD)

    def step(cr):
        t, tl, _ = cr
        tl = block(n_rest - 1 - t, tl, False)
        return t + 1, tl, largest(tl)

    lax.while_loop(live, step, (jnp.int32(0), tail, largest(tail)))
    yc_ref[0] = acc_scr[...].T.astype(BF16)


def _stick_call(qct, kc, vct, later2):
    b, l, _ = kc.shape
    nq = l // QB
    qs = min(QS_STICK, l)
    return pl.pallas_call(
        _stick_kernel,
        grid=(b, l // qs),
        in_specs=[
            pl.BlockSpec((1, 2 * C_WIDTH, qs), lambda bi, i: (bi, 0, i)),
            pl.BlockSpec((1, l, C_WIDTH), lambda bi, i: (bi, 0, 0)),
            pl.BlockSpec((1, nq, C_WIDTH, QB), lambda bi, i: (bi, 0, 0, 0)),
            pl.BlockSpec((QB, 2 * QB), lambda bi, i: (0, 0)),
        ],
        out_specs=pl.BlockSpec((1, qs, C_WIDTH), lambda bi, i: (bi, i, 0)),
        out_shape=jax.ShapeDtypeStruct((b, l, C_WIDTH), BF16),
        scratch_shapes=[pltpu.VMEM((C_WIDTH, qs), F32)],
        compiler_params=_params(("arbitrary", "arbitrary")),
        name="stick_attention",
    )(qct, kc, vct, later2)


def _merge_kernel(x_ref, ya_ref, yc_ref, mb_ref, ga_ref, gc_ref, gt_ref, sh_ref, sc_ref, g1_ref, g2_ref,
                  wbra_ref, wbrc_ref, wout_ref, *rest, with_router):
    if with_router:
        wr_ref, x1_ref, h2_ref, route_ref, cnt_ref = rest
    else:
        x1_ref, h2_ref = rest
    merged = (ga_ref[0].astype(F32) * jnp.dot(ya_ref[0], wbra_ref[...], preferred_element_type=F32)
              + mb_ref[0].astype(F32)
              + gc_ref[0].astype(F32) * jnp.dot(yc_ref[0], wbrc_ref[...], preferred_element_type=F32))
    y = jnp.dot(merged.astype(BF16), wout_ref[...], preferred_element_type=F32)
    x1 = x_ref[0] + gt_ref[0] * _rms(y, g1_ref[...])
    x1_ref[0] = x1
    h2 = _rms(x1, g2_ref[...]) * (1.0 + sc_ref[0]) + sh_ref[0]
    h2_ref[0] = h2.astype(BF16)
    if with_router:
        logits = jnp.dot(h2, wr_ref[...], preferred_element_type=F32)
        lane = lax.broadcasted_iota(I32, logits.shape, 1)
        valid = lane < N_EXPERTS
        l1 = jnp.where(valid, logits, -jnp.inf)
        v1 = jnp.max(l1, axis=-1, keepdims=True)
        i1 = jnp.min(jnp.where(l1 == v1, lane, LANES), axis=-1, keepdims=True)
        l2 = jnp.where(lane == i1, -jnp.inf, l1)
        v2 = jnp.max(l2, axis=-1, keepdims=True)
        i2 = jnp.min(jnp.where(l2 == v2, lane, LANES), axis=-1, keepdims=True)
        e2 = jnp.exp(v2 - v1)
        p1 = 1.0 / (1.0 + e2)
        p2 = e2 / (1.0 + e2)
        route = jnp.where(lane == 0, i1.astype(F32), jnp.where(lane == 1, i2.astype(F32),
                          jnp.where(lane == 2, p1, jnp.where(lane == 3, p2, 0.0))))
        route_ref[0] = route
        chosen = jnp.where(jnp.logical_or(lane == i1, lane == i2), 1.0, 0.0)
        cnt_ref[0, 0] = jnp.broadcast_to(jnp.sum(chosen, axis=0, keepdims=True), (8, LANES))


def _merge_call(x, ya, yc, mb, ga, gc, gt, sh, sc, g1, g2, wbra, wbrc, wout, wr=None):
    b, l, d = x.shape
    tm = min(TM_MERGE if wr is None else TM_FFN, l)
    tok = lambda w: pl.BlockSpec((1, tm, w), lambda bi, i: (bi, i, 0))
    full = lambda a: pl.BlockSpec(a.shape, lambda bi, i: (0,) * a.ndim)
    vec = pl.BlockSpec((1, 1, d), lambda bi, i: (bi, 0, 0))
    in_specs = [tok(d), tok(A_WIDTH), tok(C_WIDTH), tok(d), tok(d), tok(d), vec, vec, vec,
                full(g1), full(g2), full(wbra), full(wbrc), full(wout)]
    args = [x, ya, yc, mb, ga, gc, gt, sh, sc, g1, g2, wbra, wbrc, wout]
    out_shape = [jax.ShapeDtypeStruct((b, l, d), F32), jax.ShapeDtypeStruct((b, l, d), BF16)]
    out_specs = [tok(d), tok(d)]
    if wr is not None:
        in_specs.append(full(wr))
        args.append(wr)
        out_shape += [jax.ShapeDtypeStruct((b, l, LANES), F32), jax.ShapeDtypeStruct((b, l // tm, 8, LANES), F32)]
        out_specs += [tok(LANES), pl.BlockSpec((1, 1, 8, LANES), lambda bi, i: (bi, i, 0, 0))]
    return pl.pallas_call(
        functools.partial(_merge_kernel, with_router=wr is not None),
        grid=(b, l // tm),
        in_specs=in_specs,
        out_specs=out_specs,
        out_shape=out_shape,
        compiler_params=_params(("arbitrary", "arbitrary")),
        name="merge_router" if wr is not None else "merge",
    )(*args)


def _ffn_kernel(x1_ref, h2_ref, gt_ref, g3_ref, wg_ref, wu_ref, wd_ref, o_ref):
    h2 = h2_ref[0]
    gate = jnp.dot(h2, wg_ref[...], preferred_element_type=F32)
    up = jnp.dot(h2, wu_ref[...], preferred_element_type=F32)
    act = (gate * _sigmoid(gate) * up).astype(BF16)
    y = jnp.dot(act, wd_ref[...], preferred_element_type=F32)
    o_ref[0] = x1_ref[0] + gt_ref[0] * _rms(y, g3_ref[...])


def _ffn_call(x1, h2, gt, g3, wg, wu, wd):
    b, l, d = x1.shape
    tm = min(TM_FFN, l)
    tok = pl.BlockSpec((1, tm, d), lambda bi, i: (bi, i, 0))
    full = lambda a: pl.BlockSpec(a.shape, lambda bi, i: (0,) * a.ndim)
    vec = pl.BlockSpec((1, 1, d), lambda bi, i: (bi, 0, 0))
    return pl.pallas_call(
        _ffn_kernel,
        grid=(b, l // tm),
        in_specs=[tok, tok, vec, full(g3), full(wg), full(wu), full(wd)],
        out_specs=tok,
        out_shape=jax.ShapeDtypeStruct((b, l, d), F32),
        compiler_params=_params(("arbitrary", "arbitrary")),
        name="ffn_dense",
    )(x1, h2, gt, g3, wg, wu, wd)


def _moe_kernel(nchunk_ref, first_ref, total_ref,
                x1_ref, h2_ref, route_ref, before_ref, gt_ref, g3_ref, wg_ref, wu_ref, wd_ref, o_ref,
                xs_scr, wrow_scr, dest_scr, acc_scr):
    tile = pl.program_id(0) * pl.num_programs(1) + pl.program_id(1)
    e = pl.program_id(2)
    n_e = pl.num_programs(2)
    tm = h2_ref.shape[1]
    total = total_ref[tile]

    @pl.when(e == 0)
    def _():
        route = route_ref[0]
        lane = lax.broadcasted_iota(I32, route.shape, 1).astype(F32)
        hot = [jnp.where(lane == route[:, s:s + 1], 1.0, 0.0) for s in range(TOP_K)]
        earlier = jnp.dot(before_ref[...], (hot[0] + hot[1]).astype(BF16), preferred_element_type=F32)
        start = jnp.zeros((1, LANES), F32)
        for ee in range(N_EXPERTS):
            start = jnp.where(lane[:1, :] == ee, (first_ref[tile * N_EXPERTS + ee] * ROW_CHUNK).astype(F32), start)
        where_to = earlier + start
        dest = [hot[s] * where_to for s in range(TOP_K)]
        prob = [hot[s] * route[:, TOP_K + s:TOP_K + s + 1] for s in range(TOP_K)]
        for s in range(TOP_K):
            dest_scr[s] = jnp.broadcast_to(jnp.sum(dest[s], axis=1, keepdims=True), (tm, LANES))
        dest_row = [jnp.sum(dest[s].T, axis=0, keepdims=True) for s in range(TOP_K)]
        prob_row = [jnp.sum(prob[s].T, axis=0, keepdims=True) for s in range(TOP_K)]
        h2 = h2_ref[0]

        def place(c, carry):
            r0 = pl.multiple_of(c * ROW_CHUNK, ROW_CHUNK)
            rows = (r0 + lax.broadcasted_iota(I32, (ROW_CHUNK, tm), 0)).astype(F32)
            here = [rows == dest_row[s] for s in range(TOP_K)]
            pick = jnp.where(jnp.logical_or(here[0], here[1]), 1.0, 0.0).astype(BF16)
            xs_scr[pl.ds(r0, ROW_CHUNK), :] = jnp.dot(pick, h2, preferred_element_type=F32).astype(BF16)
            weight = jnp.where(here[0], prob_row[0], 0.0) + jnp.where(here[1], prob_row[1], 0.0)
            wrow_scr[pl.ds(r0, ROW_CHUNK), :] = jnp.broadcast_to(jnp.sum(weight, axis=1, keepdims=True),
                                                                  (ROW_CHUNK, LANES))
            return carry

        lax.fori_loop(0, total, place, 0)

    def expert_chunk(c, carry):
        r0 = pl.multiple_of((first_ref[tile * n_e + e] + c) * ROW_CHUNK, ROW_CHUNK)
        xs = xs_scr[pl.ds(r0, ROW_CHUNK), :]
        gate = jnp.dot(xs, wg_ref[0], preferred_element_type=F32)
        up = jnp.dot(xs, wu_ref[0], preferred_element_type=F32)
        act = (gate * _sigmoid(gate) * up).astype(BF16)
        y = jnp.dot(act, wd_ref[0], preferred_element_type=F32)
        xs_scr[pl.ds(r0, ROW_CHUNK), :] = (y * wrow_scr[pl.ds(r0, ROW_CHUNK), 0:1]).astype(BF16)
        return carry

    lax.fori_loop(0, nchunk_ref[tile * n_e + e], expert_chunk, 0)

    @pl.when(e == n_e - 1)
    def _():
        acc_scr[...] = jnp.zeros_like(acc_scr)

        def collect(c, carry):
            r0 = pl.multiple_of(c * ROW_CHUNK, ROW_CHUNK)
            cols = (r0 + lax.broadcasted_iota(I32, (tm, ROW_CHUNK), 1)).astype(F32)
            mine = jnp.logical_or(cols == dest_scr[0], cols == dest_scr[1])
            acc_scr[...] = acc_scr[...] + jnp.dot(jnp.where(mine, 1.0, 0.0).astype(BF16),
                                                  xs_scr[pl.ds(r0, ROW_CHUNK), :], preferred_element_type=F32)
            return carry

        lax.fori_loop(0, total, collect, 0)
        o_ref[0] = x1_ref[0] + gt_ref[0] * _rms(acc_scr[...], g3_ref[...])


def _moe_call(x1, h2, route, counts, before, gt, g3, wg, wu, wd):
    b, l, d = x1.shape
    tm = before.shape[0]
    n_e, _, ff = wg.shape
    cnt = counts[:, :, 0, :n_e].astype(I32).reshape(b * (l // tm), -1, n_e).sum(axis=1)
    nchunk = (cnt + ROW_CHUNK - 1) // ROW_CHUNK
    first = jnp.cumsum(nchunk, axis=1) - nchunk
    total = jnp.sum(nchunk, axis=1)
    rows = TOP_K * tm + n_e * ROW_CHUNK
    tok = lambda w: pl.BlockSpec((1, tm, w), lambda bi, i, e, *_: (bi, i, 0))
    tok_once = lambda w: pl.BlockSpec((1, tm, w), lambda bi, i, e, *_: (bi, i, 0), pipeline_mode=pl.Buffered(1))
    vec = pl.BlockSpec((1, 1, d), lambda bi, i, e, *_: (bi, 0, 0))
    grid_spec = pltpu.PrefetchScalarGridSpec(
        num_scalar_prefetch=3,
        grid=(b, l // tm, n_e),
        in_specs=[tok_once(d), tok_once(d), tok(LANES),
                  pl.BlockSpec(before.shape, lambda bi, i, e, *_: (0, 0), pipeline_mode=pl.Buffered(1)), vec,
                  pl.BlockSpec(g3.shape, lambda bi, i, e, *_: (0, 0)),
                  pl.BlockSpec((1, d, ff), lambda bi, i, e, *_: (e, 0, 0)),
                  pl.BlockSpec((1, d, ff), lambda bi, i, e, *_: (e, 0, 0)),
                  pl.BlockSpec((1, ff, d), lambda bi, i, e, *_: (e, 0, 0))],
        out_specs=tok(d),
        scratch_shapes=[pltpu.VMEM((rows, d), BF16), pltpu.VMEM((rows, LANES), F32),
                        pltpu.VMEM((TOP_K, tm, LANES), F32), pltpu.VMEM((tm, d), F32)],
    )
    return pl.pallas_call(
        _moe_kernel,
        grid_spec=grid_spec,
        out_shape=jax.ShapeDtypeStruct((b, l, d), F32),
        compiler_params=_params(("arbitrary", "arbitrary", "arbitrary")),
        name="ffn_moe",
    )(nchunk.reshape(-1), first.reshape(-1), total, x1, h2, route, before, gt, g3, wg, wu, wd)


def _rot_cols(w, n_heads, head_dim):
    d = w.shape[0]
    w = w.reshape(d, n_heads, head_dim)
    half = ROPE_DIM // 2
    rot = jnp.concatenate([-w[..., half:ROPE_DIM], w[..., :half],
                           jnp.zeros((d, n_heads, head_dim - ROPE_DIM), w.dtype)], axis=-1)
    return rot.reshape(d, n_heads * head_dim)


def _pad_cols(w, width):
    return jnp.pad(w, ((0, 0), (0, width - w.shape[1])))


def _layer_weights(w_in, w_uk, w_uv, w_pool):
    d = w_in.shape[0]
    offs, o = [], 0
    for s in IN_SIZES:
        offs.append(o)
        o += s
    piece = lambda k: w_in[:, offs[k]:offs[k] + IN_SIZES[k]]
    w_qa, w_lat, w_kr, w_qi, w_ki, w_wi, w_up, w_qc, w_kc, w_vc, w_gate = [piece(k) for k in range(len(IN_SIZES))]
    attn_scale = HEAD_DIM ** -0.5
    idx_scale = IDX_DIM ** -0.5 * N_IDX_HEADS ** -0.5
    wrow = jnp.concatenate([
        w_lat,
        _pad_cols(w_kr, LANES), _pad_cols(_rot_cols(w_kr, 1, ROPE_DIM), LANES),
        _pad_cols(w_ki, LANES), _pad_cols(_rot_cols(w_ki, 1, IDX_DIM), LANES),
        w_up, w_kc, w_gate], axis=1).astype(BF16)
    w_qc_t = (w_qc * attn_scale).T.reshape(N_HEADS_C, HEAD_DIM, d)
    zeros = jnp.zeros_like(w_qc_t)
    even = (jnp.arange(N_HEADS_C) % 2 == 0)[:, None, None]
    w_qc_pad = jnp.concatenate([jnp.where(even, w_qc_t, zeros), jnp.where(even, zeros, w_qc_t)], axis=1)
    w_qc_pad = w_qc_pad.reshape(2 * C_WIDTH, d)
    softmax_scale = attn_scale * LOG2_E
    wt = jnp.concatenate([
        (w_qa * softmax_scale).T, (_rot_cols(w_qa, N_HEADS_A, HEAD_DIM) * softmax_scale).T,
        w_qi.T, _rot_cols(w_qi, N_IDX_HEADS, IDX_DIM).T,
        w_qc_pad, w_vc.T,
        jnp.pad((w_wi * idx_scale).T, ((0, T_WI_ROWS - N_IDX_HEADS), (0, 0)))], axis=0).astype(BF16)
    wuk = jnp.zeros((KV_LATENT, LANES), F32).at[:, ROPE_DIM:HEAD_DIM].set(w_uk).astype(BF16)
    wuvt = w_uv.T.astype(BF16)
    wpool = jnp.zeros((POOL_WIDTH, POOL_WIDTH), F32)
    for g in range(N_POOL_GROUPS):
        sl = slice(g * POOL_GROUP_DIM, (g + 1) * POOL_GROUP_DIM)
        wpool = wpool.at[sl, sl].set(w_pool[g])
    return wrow, wt, wuk, wuvt, wpool.astype(BF16)


def _rope_tables(positions):
    inv = ROPE_THETA ** (-jnp.arange(0, ROPE_DIM, 2, dtype=F32) / ROPE_DIM)
    ang = positions.astype(F32)[..., None] * inv
    cos, sin = jnp.cos(ang), jnp.sin(ang)
    b, l = positions.shape
    ones = jnp.ones((b, l, HEAD_DIM - ROPE_DIM), F32)
    cos_h = jnp.concatenate([cos, cos, ones], axis=-1)
    sin_h = jnp.concatenate([sin, sin, jnp.zeros_like(ones)], axis=-1)
    cos_r = jnp.concatenate([cos_h, cos_h], axis=-1)
    sin_r = jnp.concatenate([sin_h, sin_h], axis=-1)
    return cos_r, sin_r, cos_h.transpose(0, 2, 1), sin_h.transpose(0, 2, 1)


def kernel(x, c, positions, w_ada, b_ada, norm_gains, w_in, g_kv_latent, w_uk, w_uv, w_pool, pool_scale,
           w_br_a, w_br_b, w_br_c, w_out, w_gate_dense, w_up_dense, w_down_dense,
           w_router, w_gate_moe, w_up_moe, w_down_moe):
    b, l, d = x.shape
    depth = w_in.shape[0]
    assert d == D_MODEL and l % max(KC, TM_FFN, TM_MOE) == 0
    cos_r, sin_r, cos_t, sin_t = _rope_tables(positions)
    c_pad = jnp.pad(c, ((0, 8 - b), (0, 0)))
    mod = _mod_call(c_pad, w_ada, b_ada)[:, :b]
    idx = lax.broadcasted_iota(I32, (QB, QB), 0)
    jdx = lax.broadcasted_iota(I32, (QB, QB), 1)
    tri = (jdx <= idx).astype(BF16)
    later = (jdx > idx).astype(BF16)
    later2 = jnp.concatenate([later, later], axis=1)
    tm_moe = min(TM_MOE, l)
    before = (lax.broadcasted_iota(I32, (tm_moe, tm_moe), 1)
              < lax.broadcasted_iota(I32, (tm_moe, tm_moe), 0)).astype(BF16)
    for layer in range(depth):
        sh1, sc1, gt1, sh2, sc2, gt2 = [m.reshape(b, 1, d) for m in jnp.split(mod[layer], 6, axis=-1)]
        gains = norm_gains[layer].reshape(4, 1, d)
        wrow, wt, wuk, wuvt, wpool = _layer_weights(w_in[layer], w_uk[layer], w_uv[layer], w_pool[layer])
        (qat, qit, wit, ka, vat, ki, qct, kc, vct, ga, gc, mb) = _inproj_call(
            x, sh1, sc1, gains[0], wrow, wt, cos_r, sin_r, cos_t, sin_t,
            g_kv_latent[layer].reshape(1, KV_LATENT), wuk, wuvt, wpool,
            pool_scale[layer].reshape(1, POOL_WIDTH), w_br_b[layer].astype(BF16))
        ya = _dsa_call(qit, wit, qat, ki, ka, vat, tri)
        yc = _stick_call(qct, kc, vct, later2)
        i = layer // 2
        merge_args = (x, ya, yc, mb, ga, gc, gt1, sh2, sc2, gains[1], gains[2],
                      w_br_a[layer].astype(BF16), w_br_c[layer].astype(BF16), w_out[layer].astype(BF16))
        if layer % 2 == 0:
            x1, h2 = _merge_call(*merge_args)
            x = _ffn_call(x1, h2, gt2, gains[3], w_gate_dense[i].astype(BF16), w_up_dense[i].astype(BF16),
                          w_down_dense[i].astype(BF16))
        else:
            x1, h2, route, counts = _merge_call(*merge_args, wr=_pad_cols(w_router[i], LANES))
            x = _moe_call(x1, h2, route, counts, before, gt2, gains[3], w_gate_moe[i].astype(BF16),
                          w_up_moe[i].astype(BF16), w_down_moe[i].astype(BF16))
    return x
```

```python
import functools

import jax
import jax.numpy as jnp
from jax import lax
from jax.experimental import pallas as pl
from jax.experimental.pallas import tpu as pltpu

F32 = jnp.float32
BF16 = jnp.bfloat16
I32 = jnp.int32
I16 = jnp.int16

D_MODEL = 1024
HEAD_DIM = 64
ROPE_DIM = HEAD_DIM // 4
NOPE_DIM = HEAD_DIM - ROPE_DIM
ROPE_THETA = 500000.0
N_HEADS_A = (3 * D_MODEL // 8) // HEAD_DIM
A_WIDTH = N_HEADS_A * HEAD_DIM
KV_LATENT = D_MODEL // 8
N_IDX_HEADS = 4
IDX_DIM = 64
TOPK_MAX = 256
N_POOL_GROUPS = 4
POOL_WINDOWS = (2, 4, 8, 16)
POOL_WIDTH = D_MODEL // 4
POOL_GROUP_DIM = POOL_WIDTH // N_POOL_GROUPS
N_HEADS_C = (D_MODEL // 4) // HEAD_DIM
C_WIDTH = N_HEADS_C * HEAD_DIM
N_BRANCHES = 3
IN_SIZES = (A_WIDTH, KV_LATENT, ROPE_DIM, N_IDX_HEADS * IDX_DIM, IDX_DIM, N_IDX_HEADS,
            POOL_WIDTH, C_WIDTH, C_WIDTH, C_WIDTH, N_BRANCHES * D_MODEL)
D_FF = 2816
N_EXPERTS = 8
TOP_K = 2
D_FF_EXPERT = D_FF // TOP_K
RMS_EPS = 1e-6

LANES = 128
QB = 128
KC = 512
QS_STICK = 512
QS_DSA = 256
KB_DSA = 512
VB_DSA = 256
V_ROWS = 80
STICK_DEAD = -106.0
TM_IN = 256
TM_MERGE = 256
TM_FFN = 512
TM_MOE = 1024
ROW_CHUNK = 128
POOL_HALO = 16
VMEM_LIMIT = 56 * 1024 * 1024
INT_MIN = -2147483648
LOG2_E = 1.4426950408889634
NEG_BIG = -1e30
NEG_MASK = -2e30

R_LAT, R_KR, R_KRR, R_KI, R_KIR, R_UP, R_KC, R_GATE = 0, 128, 256, 384, 512, 640, 896, 1152
R_WIDTH = R_GATE + N_BRANCHES * D_MODEL
T_QA, T_QAR, T_QI, T_QIR, T_QC, T_VC, T_WI = 0, 384, 768, 1024, 1280, 1792, 2048
T_WI_ROWS = 16
T_HEIGHT = T_WI + T_WI_ROWS


def _params(sem):
    return pltpu.CompilerParams(dimension_semantics=sem, vmem_limit_bytes=VMEM_LIMIT)


def _sigmoid(v):
    return 1.0 / (1.0 + jnp.exp(-v))


def _rms(v, gain):
    return v * lax.rsqrt(jnp.mean(v * v, axis=-1, keepdims=True) + RMS_EPS) * gain


def _mod_kernel(c_ref, w_ref, b_ref, o_ref):
    c = c_ref[...]
    cond = c * _sigmoid(c)
    o_ref[0] = jnp.dot(cond, w_ref[0], preferred_element_type=F32) + b_ref[0]


def _mod_call(c_pad, w_ada, b_ada):
    depth, d, n = w_ada.shape
    tn = 1024
    return pl.pallas_call(
        _mod_kernel,
        grid=(depth, n // tn),
        in_specs=[
            pl.BlockSpec((8, d), lambda l, j: (0, 0)),
            pl.BlockSpec((1, d, tn), lambda l, j: (l, 0, j)),
            pl.BlockSpec((1, 1, tn), lambda l, j: (l, 0, j)),
        ],
        out_specs=pl.BlockSpec((1, 8, tn), lambda l, j: (l, 0, j)),
        out_shape=jax.ShapeDtypeStruct((depth, 8, n), F32),
        compiler_params=_params(("arbitrary", "arbitrary")),
        name="adaln_mod",
    )(c_pad, w_ada, b_ada.reshape(depth, 1, n))


def _inproj_kernel(x_ref, sh_ref, sc_ref, g_ref, wrow_ref, wt_ref, c_ref, s_ref, ct_ref, st_ref,
                   glat_ref, wuk_ref, wuvt_ref, wpool_ref, pscale_ref, wbrb_ref,
                   qat_ref, qit_ref, wit_ref, ka_ref, vat_ref, ki_ref, qct_ref, kc_ref, vct_ref,
                   ga_ref, gc_ref, mb_ref,
                   h_scr, ht_scr, ext_scr, prev_scr):
    i = pl.program_id(1)
    tm = x_ref.shape[1]
    x = x_ref[0]
    h = _rms(x, g_ref[...]) * (1.0 + sc_ref[0]) + sh_ref[0]
    h_scr[...] = h.astype(BF16)
    ht_scr[...] = h.T.astype(BF16)

    def rowdot(a, width):
        return jnp.dot(h_scr[...], wrow_ref[:, a:a + width], preferred_element_type=F32)

    def tdot(a, height):
        return jnp.dot(wt_ref[a:a + height, :], ht_scr[...], preferred_element_type=F32)

    cos_r, sin_r = c_ref[0], s_ref[0]
    cos_t, sin_t = ct_ref[0], st_ref[0]

    latn = _rms(rowdot(R_LAT, KV_LATENT), glat_ref[...])
    ka = (rowdot(R_KR, LANES) * cos_r + rowdot(R_KRR, LANES) * sin_r
          + jnp.dot(latn.astype(BF16), wuk_ref[...], preferred_element_type=F32))
    ka_ref[0] = ka[:, :HEAD_DIM].astype(BF16)
    vat = jnp.dot(wuvt_ref[...], latn.T.astype(BF16), preferred_element_type=F32)
    pad_row = lax.broadcasted_iota(I32, (V_ROWS - HEAD_DIM, tm), 0)
    vat = jnp.concatenate([vat, jnp.where(pad_row == 0, 1.0, 0.0)], axis=0)
    for j in range(tm // VB_DSA):
        vat_ref[0, j] = vat[:, j * VB_DSA:(j + 1) * VB_DSA].astype(BF16)

    ki =rowdot(R_KI, LANES) * cos_r + rowdot(R_KIR, LANES) * sin_r
    ki_ref[0] = ki[:, :IDX_DIM].astype(BF16)

    qa, qar = tdot(T_QA, A_WIDTH), tdot(T_QAR, A_WIDTH)
    for hh in range(N_HEADS_A):
        r = slice(hh * HEAD_DIM, (hh + 1) * HEAD_DIM)
        qat_ref[0, r, :] = (qa[r] * cos_t + qar[r] * sin_t).astype(BF16)
    qi, qir = tdot(T_QI, N_IDX_HEADS * IDX_DIM), tdot(T_QIR, N_IDX_HEADS * IDX_DIM)
    for hh in range(N_IDX_HEADS):
        r = slice(hh * IDX_DIM, (hh + 1) * IDX_DIM)
        qit_ref[0, r, :] = (qi[r] * cos_t + qir[r] * sin_t).astype(BF16)
    wit_ref[0] = tdot(T_WI, T_WI_ROWS)

    qct_ref[0] = tdot(T_QC, 2 * C_WIDTH).astype(BF16)
    kc_ref[0] = rowdot(R_KC, C_WIDTH).astype(BF16)
    vct = tdot(T_VC, C_WIDTH)
    for j in range(tm // QB):
        vct_ref[0, j] = vct[:, j * QB:(j + 1) * QB].astype(BF16)

    up = rowdot(R_UP, POOL_WIDTH)

    @pl.when(i == 0)
    def _():
        prev_scr[...] = jnp.zeros_like(prev_scr)

    ext_scr[0:POOL_HALO, :] = prev_scr[...]
    ext_scr[POOL_HALO:POOL_HALO + tm, :] = up
    prev_scr[...] = up[tm - POOL_HALO:, :]
    lag = [ext_scr[POOL_HALO - j:POOL_HALO - j + tm, :] for j in range(POOL_HALO)]
    sums = {}
    run = lag[0]
    for j in range(1, POOL_HALO):
        run = run + lag[j]
        if j + 1 in POOL_WINDOWS:
            sums[j + 1] = run
    lane = lax.broadcasted_iota(I32, (tm, POOL_WIDTH), 1)
    pos = i * tm + lax.broadcasted_iota(I32, (tm, POOL_WIDTH), 0)
    pooled_sum = sums[POOL_WINDOWS[-1]]
    win = jnp.full((tm, POOL_WIDTH), POOL_WINDOWS[-1], I32)
    for g in range(N_POOL_GROUPS - 2, -1, -1):
        in_group = lane < (g + 1) * POOL_GROUP_DIM
        pooled_sum = jnp.where(in_group, sums[POOL_WINDOWS[g]], pooled_sum)
        win = jnp.where(in_group, POOL_WINDOWS[g], win)
    cnt = jnp.minimum(pos + 1, win).astype(F32)
    pooled = pooled_sum / cnt - up
    yb = jnp.dot(pooled.astype(BF16), wpool_ref[...], preferred_element_type=F32) * pscale_ref[...]

    ga_ref[0] = _sigmoid(rowdot(R_GATE, D_MODEL)).astype(BF16)
    gb = _sigmoid(rowdot(R_GATE + D_MODEL, D_MODEL))
    mb_ref[0] = (gb * jnp.dot(yb.astype(BF16), wbrb_ref[...], preferred_element_type=F32)).astype(BF16)
    gc_ref[0] = _sigmoid(rowdot(R_GATE + 2 * D_MODEL, D_MODEL)).astype(BF16)


def _inproj_call(x, sh, sc, gain, wrow, wt, cos_r, sin_r, cos_t, sin_t, glat, wuk, wuvt, wpool, pscale, wbrb):
    b, l, d = x.shape
    tm = min(TM_IN, l)
    nq = l // QB
    tok = lambda w: pl.BlockSpec((1, tm, w), lambda bi, i: (bi, i, 0))
    feat = lambda hgt: pl.BlockSpec((1, hgt, tm), lambda bi, i: (bi, 0, i))
    blk = lambda hgt, w: pl.BlockSpec((1, tm // w, hgt, w), lambda bi, i: (bi, i, 0, 0))
    full = lambda a: pl.BlockSpec(a.shape, lambda bi, i: (0,) * a.ndim)
    vec = pl.BlockSpec((1, 1, d), lambda bi, i: (bi, 0, 0))
    out_shape = (
        jax.ShapeDtypeStruct((b, A_WIDTH, l), BF16),
        jax.ShapeDtypeStruct((b, N_IDX_HEADS * IDX_DIM, l), BF16),
        jax.ShapeDtypeStruct((b, T_WI_ROWS, l), F32),
        jax.ShapeDtypeStruct((b, l, HEAD_DIM), BF16),
        jax.ShapeDtypeStruct((b, l // VB_DSA, V_ROWS, VB_DSA), BF16),
        jax.ShapeDtypeStruct((b, l, IDX_DIM), BF16),
        jax.ShapeDtypeStruct((b, 2 * C_WIDTH, l), BF16),
        jax.ShapeDtypeStruct((b, l, C_WIDTH), BF16),
        jax.ShapeDtypeStruct((b, nq, C_WIDTH, QB), BF16),
        jax.ShapeDtypeStruct((b, l, d), BF16),
        jax.ShapeDtypeStruct((b, l, d), BF16),
        jax.ShapeDtypeStruct((b, l, d), BF16),
    )
    out_specs = (feat(A_WIDTH), feat(N_IDX_HEADS * IDX_DIM), feat(T_WI_ROWS), tok(HEAD_DIM), blk(V_ROWS, VB_DSA),
                 tok(IDX_DIM), feat(2 * C_WIDTH), tok(C_WIDTH), blk(C_WIDTH, QB), tok(d), tok(d), tok(d))
    in_specs = [tok(d), vec, vec, full(gain), full(wrow), full(wt), tok(LANES), tok(LANES),
                feat(HEAD_DIM), feat(HEAD_DIM), full(glat), full(wuk), full(wuvt), full(wpool),
                full(pscale), full(wbrb)]
    return pl.pallas_call(
        _inproj_kernel,
        grid=(b, l // tm),
        in_specs=in_specs,
        out_specs=out_specs,
        out_shape=out_shape,
        scratch_shapes=[pltpu.VMEM((tm, d), BF16), pltpu.VMEM((d, tm), BF16),
                        pltpu.VMEM((tm + POOL_HALO, POOL_WIDTH), F32), pltpu.VMEM((POOL_HALO, POOL_WIDTH), F32)],
        compiler_params=_params(("arbitrary", "arbitrary")),
        name="inproj",
    )(x, sh, sc, gain, wrow, wt, cos_r, sin_r, cos_t, sin_t, glat, wuk, wuvt, wpool, pscale, wbrb)


def _dsa_kernel(qit_ref, wit_ref, qat_ref, ki_ref, ka_ref, vat_ref, tri_ref, ya_ref,
                keys_scr, top_scr, acc_scr, *, k_sel):
    i = pl.program_id(1)
    qs = qat_ref.shape[2]
    n_blocks = (i + 1) * (qs // QB)
    n_chunks = (n_blocks * QB + KC - 1) // KC
    qpos = i * qs + lax.broadcasted_iota(I32, (1, qs), 1)
    w_idx = wit_ref[0]

    def score_chunk(c, carry, masked):
        r0 = pl.multiple_of(c * KC, KC)
        kblk = ki_ref[0, pl.ds(r0, KC), :]
        parts = [jnp.dot(kblk, qit_ref[0, hh * IDX_DIM:(hh + 1) * IDX_DIM, :], preferred_element_type=F32)
                 for hh in range(N_IDX_HEADS)]
        score = jnp.maximum(parts[0], 0.0) * w_idx[0:1, :]
        for hh in range(1, N_IDX_HEADS):
            score = score + jnp.maximum(parts[hh], 0.0) * w_idx[hh:hh + 1, :]
        bits = lax.bitcast_convert_type(score, I32)
        key = jnp.where(bits < 0, INT_MIN - bits, bits)
        if masked:
            kpos = r0 + lax.broadcasted_iota(I32, (KC, qs), 0)
            key = jnp.where(kpos <= qpos, key, INT_MIN)
        keys_scr[pl.ds(r0, KC), :] = key
        top_scr[pl.ds(r0, KC), :] = lax.shift_right_arithmetic(key, 16).astype(I16)
        return carry

    n_open = (i * qs) // KC
    lax.fori_loop(0, n_open, functools.partial(score_chunk, masked=False), 0)
    lax.fori_loop(n_open, n_chunks, functools.partial(score_chunk, masked=True), 0)

    def count_rows(src, rows_per_vreg, trial):
        n_acc = 4
        groups = KC // rows_per_vreg

        def body(c, accs):
            r0 = pl.multiple_of(c * KC, KC)
            rows = src[pl.ds(r0, KC), :].reshape(groups, rows_per_vreg, qs)
            accs = list(accs)
            for j in range(groups):
                a = accs[j % n_acc]
                accs[j % n_acc] = jnp.where(rows[j] >= trial, a + 1, a)
            return tuple(accs)

        zero = jnp.zeros((rows_per_vreg, qs), src.dtype)
        accs = lax.fori_loop(0, n_chunks, body, tuple(zero for _ in range(n_acc)))
        total = (accs[0] + accs[1]) + (accs[2] + accs[3])
        return jnp.sum(total.astype(I32), axis=0, keepdims=True)

    def count_ge(trial):
        return count_rows(keys_scr, 8, trial)

    def count_ge_top(trial):
        return count_rows(top_scr, 16, lax.shift_right_arithmetic(trial, 16).astype(I16))

    c_zero = count_ge_top(jnp.zeros((1, qs), I32))
    c_pos = count_ge(jnp.ones((1, qs), I32))
    nonneg = c_zero >= k_sel
    tie_at_zero = jnp.logical_and(nonneg, c_pos < k_sel)
    ans0 = jnp.where(nonneg, 0, INT_MIN)
    c_ans0 = jnp.where(nonneg, c_zero, jnp.int32(2 ** 30))

    def unsettled(c_ans):
        settled = jnp.logical_or(tie_at_zero, c_ans == k_sel)
        return jnp.max(jnp.where(settled, 0, 1))

    def search(counter, lowest_bit, group, state):
        def refine(carry):
            bit, ans, c_ans, _ = carry
            for g in range(group):
                trial = ans + lax.shift_left(jnp.int32(1), bit - g)
                c = counter(trial)
                ok = c >= k_sel
                c_ans = jnp.where(ok, c, c_ans)
                ans = jnp.where(ok, trial, ans)
            return bit - group, ans, c_ans, unsettled(c_ans)

        return lax.while_loop(lambda cr: jnp.logical_and(cr[0] >= lowest_bit, cr[3] > 0), refine, state)

    _, top, c_top, alive = search(count_ge_top, 16, 5, (jnp.int32(30), ans0, c_ans0, unsettled(c_ans0)))
    bucket = lax.shift_right_arithmetic(top, 16)
    last_bucket = bucket >= 2 ** 15 - 1
    c_over = jnp.where(last_bucket, 0,
                       count_rows(top_scr, 16, jnp.where(last_bucket, bucket, bucket + 1).astype(I16)))

    def pack_low(c, carry):
        r0 = pl.multiple_of(c * KC, KC)
        low = (keys_scr[pl.ds(r0, KC), :] ^ 0x8000).astype(I16)
        top_scr[pl.ds(r0, KC), :] = jnp.where(top_scr[pl.ds(r0, KC), :] == bucket.astype(I16), low, -2 ** 15)
        return carry

    lax.fori_loop(0, n_chunks, pack_low, 0)

    def count_ge_low(low_trial):
        return c_over + count_rows(top_scr, 16, (low_trial - 2 ** 15).astype(I16))

    zero_row = jnp.zeros((1, qs), I32)
    _, low, _, _ = search(count_ge_low, 0, 4, (jnp.int32(15), zero_row, c_top, alive))
    thr = top + low
    full_low = low >= 2 ** 16 - 1
    n_above = jnp.where(full_low, c_over, count_ge_low(jnp.where(full_low, low, low + 1)))
    n_ties = jnp.where(thr == INT_MIN, 0, k_sel - n_above).astype(F32)

    acc_scr[...] = jnp.zeros_like(acc_scr)
    heads = range(N_HEADS_A)

    kb_rows = KB_DSA
    vb_rows = vat_ref.shape[3]
    v_per_k = kb_rows // vb_rows

    n_steps = ((i + 1) * qs + kb_rows - 1) // kb_rows

    v_rows = vat_ref.shape[2]

    def attend(kb, carry):
        seen, ms = carry
        r0 = pl.multiple_of(kb * kb_rows, kb_rows)
        keyb = keys_scr[pl.ds(r0, kb_rows), :]
        tied = keyb == thr
        tied_b = jnp.where(tied, 1.0, 0.0).astype(BF16)
        ranks = []
        for j in range(kb_rows // QB):
            within = jnp.dot(tri_ref[...], tied_b[j * QB:(j + 1) * QB, :], preferred_element_type=F32)
            ranks.append(seen + within)
            seen = seen + within[QB - 1:QB, :]
        rank = jnp.concatenate(ranks, axis=0)
        keep = jnp.logical_or(keyb > thr, jnp.logical_and(tied, rank <= n_ties))
        bias = jnp.where(keep, 0.0, NEG_MASK).astype(BF16)
        kblk = ka_ref[0, pl.ds(r0, kb_rows), :]
        logits = [jnp.dot(kblk, qat_ref[0, hh * HEAD_DIM:(hh + 1) * HEAD_DIM, :],
                          preferred_element_type=F32).astype(BF16) + bias for hh in heads]
        new_ms = [jnp.maximum(ms[hh], jnp.max(logits[hh], axis=0, keepdims=True).astype(F32)) for hh in heads]
        probs = [jnp.exp2(logits[hh] - new_ms[hh].astype(BF16)) for hh in heads]
        alphas = [jnp.exp2(ms[hh] - new_ms[hh]) for hh in heads]
        outs = []
        for hh in heads:
            out = jnp.dot(vat_ref[0, kb * v_per_k], probs[hh][:vb_rows, :], preferred_element_type=F32)
            for j in range(1, v_per_k):
                out = out + jnp.dot(vat_ref[0, kb * v_per_k + j], probs[hh][j * vb_rows:(j + 1) * vb_rows, :],
                                    preferred_element_type=F32)
            outs.append(out)
        for hh in heads:
            r = slice(hh * v_rows, (hh + 1) * v_rows)
            acc_scr[r, :] = acc_scr[r, :] * alphas[hh] + outs[hh]
        return seen, tuple(new_ms)

    init = (jnp.zeros((1, qs), F32), tuple(jnp.full((1, qs), NEG_BIG, F32) for _ in heads))
    lax.fori_loop(0, n_steps, attend, init)
    normed = [acc_scr[hh * v_rows:hh * v_rows + HEAD_DIM, :] / acc_scr[hh * v_rows + HEAD_DIM:hh * v_rows + HEAD_DIM + 1, :]
              for hh in heads]
    ya_ref[0] = jnp.concatenate(normed, axis=0).T.astype(BF16)


def _dsa_call(qit, wit, qat, ki, ka, vat, tri):
    b, l, _ = ki.shape
    nq = l // QB
    qs = min(QS_DSA, l)
    k_sel = min(TOPK_MAX, l // 4)
    return pl.pallas_call(
        functools.partial(_dsa_kernel, k_sel=k_sel),
        grid=(b, l // qs),
        in_specs=[
            pl.BlockSpec((1, N_IDX_HEADS * IDX_DIM, qs), lambda bi, i: (bi, 0, i)),
            pl.BlockSpec((1, T_WI_ROWS, qs), lambda bi, i: (bi, 0, i)),
            pl.BlockSpec((1, A_WIDTH, qs), lambda bi, i: (bi, 0, i)),
            pl.BlockSpec((1, l, IDX_DIM), lambda bi, i: (bi, 0, 0), pipeline_mode=pl.Buffered(1)),
            pl.BlockSpec((1, l, HEAD_DIM), lambda bi, i: (bi, 0, 0), pipeline_mode=pl.Buffered(1)),
            pl.BlockSpec((1, l // VB_DSA, V_ROWS, VB_DSA), lambda bi, i: (bi, 0, 0, 0),
                         pipeline_mode=pl.Buffered(1)),
            pl.BlockSpec((QB, QB), lambda bi, i: (0, 0)),
        ],
        out_specs=pl.BlockSpec((1, qs, A_WIDTH), lambda bi, i: (bi, i, 0)),
        out_shape=jax.ShapeDtypeStruct((b, l, A_WIDTH), BF16),
        scratch_shapes=[pltpu.VMEM((l, qs), I32), pltpu.VMEM((l, qs), I16),
                        pltpu.VMEM((N_HEADS_A * V_ROWS, qs), F32)],
        compiler_params=_params(("arbitrary", "arbitrary")),
        name="dsa_attention",
    )(qit, wit, qat, ki, ka, vat, tri)


def _stick_kernel(qct_ref, kc_ref, vct_ref, later_ref, yc_ref, acc_scr):
    i = pl.program_id(1)
    qs = qct_ref.shape[2]
    diag_blocks = qs // QB
    qpos = i * qs + lax.broadcasted_iota(I32, (1, qs), 1)
    acc_scr[...] = jnp.zeros_like(acc_scr)

    def block(kb, tail, masked, lane0=0):
        r0 = pl.multiple_of(kb * QB, QB)
        kfull = kc_ref[0, pl.ds(r0, QB), :]
        vt = vct_ref[0, kb]
        heads = range(N_HEADS_C)
        old = [tail[hh][:, lane0:] for hh in heads]
        if masked:
            mask = (r0 + lax.broadcasted_iota(I32, (QB, qs - lane0), 0)) < qpos[:, lane0:]
        zs = [jnp.dot(kfull[:, (hh // 2) * LANES:(hh // 2 + 1) * LANES],
                      qct_ref[0, hh * LANES:(hh + 1) * LANES, lane0:], preferred_element_type=F32) for hh in heads]
        log_betas, splits, new_tail = [], [], []
        for hh in heads:
            z = zs[hh]
            log_beta = jnp.minimum(z, 0.0) - jnp.log(1.0 + jnp.exp(-jnp.abs(z)))
            log_keep = log_beta - z
            if masked:
                log_keep = jnp.where(mask, log_keep, 0.0)
            hi = log_keep.astype(BF16)
            lo = (log_keep - hi.astype(F32)).astype(BF16)
            log_betas.append(log_beta)
            splits.append(jnp.concatenate([hi, lo], axis=0))
            new = old[hh] + jnp.sum(log_keep, axis=0, keepdims=True)
            new_tail.append(new if lane0 == 0 else jnp.concatenate([tail[hh][:, :lane0], new], axis=1))
        withins = [jnp.dot(later_ref[...], splits[hh], preferred_element_type=F32) for hh in heads]
        weights = []
        for hh in heads:
            a = jnp.exp(log_betas[hh] + withins[hh] + old[hh])
            if masked:
                a = jnp.where(mask, a, 0.0)
            weights.append(a.astype(BF16))
        for hh in heads:
            r = slice(hh * HEAD_DIM, (hh + 1) * HEAD_DIM)
            acc_scr[r, lane0:] = acc_scr[r, lane0:] + jnp.dot(vt[r, :], weights[hh], preferred_element_type=F32)
        return tuple(new_tail)

    def largest(tail):
        worst = tail[0]
        for hh in range(1, N_HEADS_C):
            worst = jnp.maximum(worst, tail[hh])
        return jnp.max(worst)

    n_rest = i * diag_blocks
    tail = tuple(jnp.zeros((1, qs), F32) for _ in range(N_HEADS_C))
    for d in reversed(range(diag_blocks)):
        tail = block(n_rest + d, tail, True, lane0=d * QB)

    def live(cr):
        return jnp.logical_and(cr[0] < n_rest, cr[2] > STICK_DEAD)

    def step(cr):
        t, tl, _ = cr
        tl = block(n_rest - 1 - t, tl, False)
        return t + 1, tl, largest(tl)

    lax.while_loop(live, step, (jnp.int32(0), tail, largest(tail)))
    yc_ref[0] = acc_scr[...].T.astype(BF16)


def _stick_call(qct, kc, vct, later2):
    b, l, _ = kc.shape
    nq = l // QB
    qs = min(QS_STICK, l)
    return pl.pallas_call(
        _stick_kernel,
        grid=(b, l // qs),
        in_specs=[
            pl.BlockSpec((1, 2 * C_WIDTH, qs), lambda bi, i: (bi, 0, i)),
            pl.BlockSpec((1, l, C_WIDTH), lambda bi, i: (bi, 0, 0)),
            pl.BlockSpec((1, nq, C_WIDTH, QB), lambda bi, i: (bi, 0, 0, 0)),
            pl.BlockSpec((QB, 2 * QB), lambda bi, i: (0, 0)),
        ],
        out_specs=pl.BlockSpec((1, qs, C_WIDTH), lambda bi, i: (bi, i, 0)),
        out_shape=jax.ShapeDtypeStruct((b, l, C_WIDTH), BF16),
        scratch_shapes=[pltpu.VMEM((C_WIDTH, qs), F32)],
        compiler_params=_params(("arbitrary", "arbitrary")),
        name="stick_attention",
    )(qct, kc, vct, later2)


def _merge_kernel(x_ref, ya_ref, yc_ref, mb_ref, ga_ref, gc_ref, gt_ref, sh_ref, sc_ref, g1_ref, g2_ref,
                  wbra_ref, wbrc_ref, wout_ref, *rest, with_router):
    if with_router:
        wr_ref, x1_ref, h2_ref, route_ref, cnt_ref = rest
    else:
        x1_ref, h2_ref = rest
    merged = (ga_ref[0].astype(F32) * jnp.dot(ya_ref[0], wbra_ref[...], preferred_element_type=F32)
              + mb_ref[0].astype(F32)
              + gc_ref[0].astype(F32) * jnp.dot(yc_ref[0], wbrc_ref[...], preferred_element_type=F32))
    y = jnp.dot(merged.astype(BF16), wout_ref[...], preferred_element_type=F32)
    x1 = x_ref[0] + gt_ref[0] * _rms(y, g1_ref[...])
    x1_ref[0] = x1
    h2 = _rms(x1, g2_ref[...]) * (1.0 + sc_ref[0]) + sh_ref[0]
    h2_ref[0] = h2.astype(BF16)
    if with_router:
        logits = jnp.dot(h2, wr_ref[...], preferred_element_type=F32)
        lane = lax.broadcasted_iota(I32, logits.shape, 1)
        valid = lane < N_EXPERTS
        l1 = jnp.where(valid, logits, -jnp.inf)
        v1 = jnp.max(l1, axis=-1, keepdims=True)
        i1 = jnp.min(jnp.where(l1 == v1, lane, LANES), axis=-1, keepdims=True)
        l2 = jnp.where(lane == i1, -jnp.inf, l1)
        v2 = jnp.max(l2, axis=-1, keepdims=True)
        i2 = jnp.min(jnp.where(l2 == v2, lane, LANES), axis=-1, keepdims=True)
        e2 = jnp.exp(v2 - v1)
        p1 = 1.0 / (1.0 + e2)
        p2 = e2 / (1.0 + e2)
        route = jnp.where(lane == 0, i1.astype(F32), jnp.where(lane == 1, i2.astype(F32),
                          jnp.where(lane == 2, p1, jnp.where(lane == 3, p2, 0.0))))
        route_ref[0] = route
        chosen = jnp.where(jnp.logical_or(lane == i1, lane == i2), 1.0, 0.0)
        cnt_ref[0, 0] = jnp.broadcast_to(jnp.sum(chosen, axis=0, keepdims=True), (8, LANES))


def _merge_call(x, ya, yc, mb, ga, gc, gt, sh, sc, g1, g2, wbra, wbrc, wout, wr=None):
    b, l, d = x.shape
    tm = min(TM_MERGE if wr is None else TM_FFN, l)
    tok = lambda w: pl.BlockSpec((1, tm, w), lambda bi, i: (bi, i, 0))
    full = lambda a: pl.BlockSpec(a.shape, lambda bi, i: (0,) * a.ndim)
    vec = pl.BlockSpec((1, 1, d), lambda bi, i: (bi, 0, 0))
    in_specs = [tok(d), tok(A_WIDTH), tok(C_WIDTH), tok(d), tok(d), tok(d), vec, vec, vec,
                full(g1), full(g2), full(wbra), full(wbrc), full(wout)]
    args = [x, ya, yc, mb, ga, gc, gt, sh, sc, g1, g2, wbra, wbrc, wout]
    out_shape = [jax.ShapeDtypeStruct((b, l, d), F32), jax.ShapeDtypeStruct((b, l, d), BF16)]
    out_specs = [tok(d), tok(d)]
    if wr is not None:
        in_specs.append(full(wr))
        args.append(wr)
        out_shape += [jax.ShapeDtypeStruct((b, l, LANES), F32), jax.ShapeDtypeStruct((b, l // tm, 8, LANES), F32)]
        out_specs += [tok(LANES), pl.BlockSpec((1, 1, 8, LANES), lambda bi, i: (bi, i, 0, 0))]
    return pl.pallas_call(
        functools.partial(_merge_kernel, with_router=wr is not None),
        grid=(b, l // tm),
        in_specs=in_specs,
        out_specs=out_specs,
        out_shape=out_shape,
        compiler_params=_params(("arbitrary", "arbitrary")),
        name="merge_router" if wr is not None else "merge",
    )(*args)


def _ffn_kernel(x1_ref, h2_ref, gt_ref, g3_ref, wg_ref, wu_ref, wd_ref, o_ref):
    h2 = h2_ref[0]
    gate = jnp.dot(h2, wg_ref[...], preferred_element_type=F32)
    up = jnp.dot(h2, wu_ref[...], preferred_element_type=F32)
    act = (gate * _sigmoid(gate) * up).astype(BF16)
    y = jnp.dot(act, wd_ref[...], preferred_element_type=F32)
    o_ref[0] = x1_ref[0] + gt_ref[0] * _rms(y, g3_ref[...])


def _ffn_call(x1, h2, gt, g3, wg, wu, wd):
    b, l, d = x1.shape
    tm = min(TM_FFN, l)
    tok = pl.BlockSpec((1, tm, d), lambda bi, i: (bi, i, 0))
    full = lambda a: pl.BlockSpec(a.shape, lambda bi, i: (0,) * a.ndim)
    vec = pl.BlockSpec((1, 1, d), lambda bi, i: (bi, 0, 0))
    return pl.pallas_call(
        _ffn_kernel,
        grid=(b, l // tm),
        in_specs=[tok, tok, vec, full(g3), full(wg), full(wu), full(wd)],
        out_specs=tok,
        out_shape=jax.ShapeDtypeStruct((b, l, d), F32),
        compiler_params=_params(("arbitrary", "arbitrary")),
        name="ffn_dense",
    )(x1, h2, gt, g3, wg, wu, wd)


def _moe_kernel(nchunk_ref, first_ref, total_ref,
                x1_ref, h2_ref, route_ref, before_ref, gt_ref, g3_ref, wg_ref, wu_ref, wd_ref, o_ref,
                xs_scr, wrow_scr, dest_scr, acc_scr):
    tile = pl.program_id(0) * pl.num_programs(1) + pl.program_id(1)
    e = pl.program_id(2)
    n_e = pl.num_programs(2)
    tm = h2_ref.shape[1]
    total = total_ref[tile]

    @pl.when(e == 0)
    def _():
        route = route_ref[0]
        lane = lax.broadcasted_iota(I32, route.shape, 1).astype(F32)
        hot = [jnp.where(lane == route[:, s:s + 1], 1.0, 0.0) for s in range(TOP_K)]
        earlier = jnp.dot(before_ref[...], (hot[0] + hot[1]).astype(BF16), preferred_element_type=F32)
        start = jnp.zeros((1, LANES), F32)
        for ee in range(N_EXPERTS):
            start = jnp.where(lane[:1, :] == ee, (first_ref[tile * N_EXPERTS + ee] * ROW_CHUNK).astype(F32), start)
        where_to = earlier + start
        dest = [hot[s] * where_to for s in range(TOP_K)]
        prob = [hot[s] * route[:, TOP_K + s:TOP_K + s + 1] for s in range(TOP_K)]
        for s in range(TOP_K):
            dest_scr[s] = jnp.broadcast_to(jnp.sum(dest[s], axis=1, keepdims=True), (tm, LANES))
        dest_row = [jnp.sum(dest[s].T, axis=0, keepdims=True) for s in range(TOP_K)]
        prob_row = [jnp.sum(prob[s].T, axis=0, keepdims=True) for s in range(TOP_K)]
        h2 = h2_ref[0]

        def place(c, carry):
            r0 = pl.multiple_of(c * ROW_CHUNK, ROW_CHUNK)
            rows = (r0 + lax.broadcasted_iota(I32, (ROW_CHUNK, tm), 0)).astype(F32)
            here = [rows == dest_row[s] for s in range(TOP_K)]
            pick = jnp.where(jnp.logical_or(here[0], here[1]), 1.0, 0.0).astype(BF16)
            xs_scr[pl.ds(r0, ROW_CHUNK), :] = jnp.dot(pick, h2, preferred_element_type=F32).astype(BF16)
            weight = jnp.where(here[0], prob_row[0], 0.0) + jnp.where(here[1], prob_row[1], 0.0)
            wrow_scr[pl.ds(r0, ROW_CHUNK), :] = jnp.broadcast_to(jnp.sum(weight, axis=1, keepdims=True),
                                                                  (ROW_CHUNK, LANES))
            return carry

        lax.fori_loop(0, total + total % 2, place, 0)

    def expert_chunk(c, carry):
        r0 = pl.multiple_of((first_ref[tile * n_e + e] + c) * ROW_CHUNK, ROW_CHUNK)
        xs = xs_scr[pl.ds(r0, ROW_CHUNK), :]
        gate = jnp.dot(xs, wg_ref[0], preferred_element_type=F32)
        up = jnp.dot(xs, wu_ref[0], preferred_element_type=F32)
        act = (gate * _sigmoid(gate) * up).astype(BF16)
        y = jnp.dot(act, wd_ref[0], preferred_element_type=F32)
        xs_scr[pl.ds(r0, ROW_CHUNK), :] = (y * wrow_scr[pl.ds(r0, ROW_CHUNK), 0:1]).astype(BF16)
        return carry

    lax.fori_loop(0, nchunk_ref[tile * n_e + e], expert_chunk, 0)

    @pl.when(e == n_e - 1)
    def _():
        acc_scr[...] = jnp.zeros_like(acc_scr)

        def collect(c, carry):
            r0 = pl.multiple_of(c * 2 * ROW_CHUNK, 2 * ROW_CHUNK)
            halves = []
            for j in range(2):
                cols = (r0 + j * ROW_CHUNK + lax.broadcasted_iota(I32, (tm, ROW_CHUNK), 1)).astype(F32)
                mine = jnp.logical_or(cols == dest_scr[0], cols == dest_scr[1])
                halves.append(jnp.where(mine, 1.0, 0.0).astype(BF16))
            acc_scr[...] = acc_scr[...] + jnp.dot(jnp.concatenate(halves, axis=1),
                                                  xs_scr[pl.ds(r0, 2 * ROW_CHUNK), :], preferred_element_type=F32)
            return carry

        lax.fori_loop(0, (total + 1) // 2, collect, 0)
        o_ref[0] = x1_ref[0] + gt_ref[0] * _rms(acc_scr[...], g3_ref[...])


def _moe_call(x1, h2, route, counts, before, gt, g3, wg, wu, wd):
    b, l, d = x1.shape
    tm = before.shape[0]
    n_e, _, ff = wg.shape
    cnt = counts[:, :, 0, :n_e].astype(I32).reshape(b * (l // tm), -1, n_e).sum(axis=1)
    nchunk = (cnt + ROW_CHUNK - 1) // ROW_CHUNK
    first = jnp.cumsum(nchunk, axis=1) - nchunk
    total = jnp.sum(nchunk, axis=1)
    rows = TOP_K * tm + n_e * ROW_CHUNK
    tok = lambda w: pl.BlockSpec((1, tm, w), lambda bi, i, e, *_: (bi, i, 0))
    tok_once = lambda w: pl.BlockSpec((1, tm, w), lambda bi, i, e, *_: (bi, i, 0), pipeline_mode=pl.Buffered(1))
    vec = pl.BlockSpec((1, 1, d), lambda bi, i, e, *_: (bi, 0, 0))
    grid_spec = pltpu.PrefetchScalarGridSpec(
        num_scalar_prefetch=3,
        grid=(b, l // tm, n_e),
        in_specs=[tok_once(d), tok_once(d), tok(LANES),
                  pl.BlockSpec(before.shape, lambda bi, i, e, *_: (0, 0), pipeline_mode=pl.Buffered(1)), vec,
                  pl.BlockSpec(g3.shape, lambda bi, i, e, *_: (0, 0)),
                  pl.BlockSpec((1, d, ff), lambda bi, i, e, *_: (e, 0, 0)),
                  pl.BlockSpec((1, d, ff), lambda bi, i, e, *_: (e, 0, 0)),
                  pl.BlockSpec((1, ff, d), lambda bi, i, e, *_: (e, 0, 0))],
        out_specs=tok(d),
        scratch_shapes=[pltpu.VMEM((rows, d), BF16), pltpu.VMEM((rows, LANES), F32),
                        pltpu.VMEM((TOP_K, tm, LANES), F32), pltpu.VMEM((tm, d), F32)],
    )
    return pl.pallas_call(
        _moe_kernel,
        grid_spec=grid_spec,
        out_shape=jax.ShapeDtypeStruct((b, l, d), F32),
        compiler_params=_params(("arbitrary", "arbitrary", "arbitrary")),
        name="ffn_moe",
    )(nchunk.reshape(-1), first.reshape(-1), total, x1, h2, route, before, gt, g3, wg, wu, wd)


def _rot_cols(w, n_heads, head_dim):
    d = w.shape[0]
    w = w.reshape(d, n_heads, head_dim)
    half = ROPE_DIM // 2
    rot = jnp.concatenate([-w[..., half:ROPE_DIM], w[..., :half],
                           jnp.zeros((d, n_heads, head_dim - ROPE_DIM), w.dtype)], axis=-1)
    return rot.reshape(d, n_heads * head_dim)


def _pad_cols(w, width):
    return jnp.pad(w, ((0, 0), (0, width - w.shape[1])))


def _layer_weights(w_in, w_uk, w_uv, w_pool):
    d = w_in.shape[0]
    offs, o = [], 0
    for s in IN_SIZES:
        offs.append(o)
        o += s
    piece = lambda k: w_in[:, offs[k]:offs[k] + IN_SIZES[k]]
    w_qa, w_lat, w_kr, w_qi, w_ki, w_wi, w_up, w_qc, w_kc, w_vc, w_gate = [piece(k) for k in range(len(IN_SIZES))]
    attn_scale = HEAD_DIM ** -0.5
    idx_scale = IDX_DIM ** -0.5 * N_IDX_HEADS ** -0.5
    wrow = jnp.concatenate([
        w_lat,
        _pad_cols(w_kr, LANES), _pad_cols(_rot_cols(w_kr, 1, ROPE_DIM), LANES),
        _pad_cols(w_ki, LANES), _pad_cols(_rot_cols(w_ki, 1, IDX_DIM), LANES),
        w_up, w_kc, w_gate], axis=1).astype(BF16)
    w_qc_t = (w_qc * attn_scale).T.reshape(N_HEADS_C, HEAD_DIM, d)
    zeros = jnp.zeros_like(w_qc_t)
    even = (jnp.arange(N_HEADS_C) % 2 == 0)[:, None, None]
    w_qc_pad = jnp.concatenate([jnp.where(even, w_qc_t, zeros), jnp.where(even, zeros, w_qc_t)], axis=1)
    w_qc_pad = w_qc_pad.reshape(2 * C_WIDTH, d)
    softmax_scale = attn_scale * LOG2_E
    wt = jnp.concatenate([
        (w_qa * softmax_scale).T, (_rot_cols(w_qa, N_HEADS_A, HEAD_DIM) * softmax_scale).T,
        w_qi.T, _rot_cols(w_qi, N_IDX_HEADS, IDX_DIM).T,
        w_qc_pad, w_vc.T,
        jnp.pad((w_wi * idx_scale).T, ((0, T_WI_ROWS - N_IDX_HEADS), (0, 0)))], axis=0).astype(BF16)
    wuk = jnp.zeros((KV_LATENT, LANES), F32).at[:, ROPE_DIM:HEAD_DIM].set(w_uk).astype(BF16)
    wuvt = w_uv.T.astype(BF16)
    wpool = jnp.zeros((POOL_WIDTH, POOL_WIDTH), F32)
    for g in range(N_POOL_GROUPS):
        sl = slice(g * POOL_GROUP_DIM, (g + 1) * POOL_GROUP_DIM)
        wpool = wpool.at[sl, sl].set(w_pool[g])
    return wrow, wt, wuk, wuvt, wpool.astype(BF16)


def _rope_tables(positions):
    inv = ROPE_THETA ** (-jnp.arange(0, ROPE_DIM, 2, dtype=F32) / ROPE_DIM)
    ang = positions.astype(F32)[..., None] * inv
    cos, sin = jnp.cos(ang), jnp.sin(ang)
    b, l = positions.shape
    ones = jnp.ones((b, l, HEAD_DIM - ROPE_DIM), F32)
    cos_h = jnp.concatenate([cos, cos, ones], axis=-1)
    sin_h = jnp.concatenate([sin, sin, jnp.zeros_like(ones)], axis=-1)
    cos_r = jnp.concatenate([cos_h, cos_h], axis=-1)
    sin_r = jnp.concatenate([sin_h, sin_h], axis=-1)
    return cos_r, sin_r, cos_h.transpose(0, 2, 1), sin_h.transpose(0, 2, 1)


def kernel(x, c, positions, w_ada, b_ada, norm_gains, w_in, g_kv_latent, w_uk, w_uv, w_pool, pool_scale,
           w_br_a, w_br_b, w_br_c, w_out, w_gate_dense, w_up_dense, w_down_dense,
           w_router, w_gate_moe, w_up_moe, w_down_moe):
    b, l, d = x.shape
    depth = w_in.shape[0]
    assert d == D_MODEL and l % max(KC, TM_FFN, TM_MOE) == 0
    cos_r, sin_r, cos_t, sin_t = _rope_tables(positions)
    c_pad = jnp.pad(c, ((0, 8 - b), (0, 0)))
    mod = _mod_call(c_pad, w_ada, b_ada)[:, :b]
    idx = lax.broadcasted_iota(I32, (QB, QB), 0)
    jdx = lax.broadcasted_iota(I32, (QB, QB), 1)
    tri = (jdx <= idx).astype(BF16)
    later = (jdx > idx).astype(BF16)
    later2 = jnp.concatenate([later, later], axis=1)
    tm_moe = min(TM_MOE, l)
    before = (lax.broadcasted_iota(I32, (tm_moe, tm_moe), 1)
              < lax.broadcasted_iota(I32, (tm_moe, tm_moe), 0)).astype(BF16)
    for layer in range(depth):
        sh1, sc1, gt1, sh2, sc2, gt2 = [m.reshape(b, 1, d) for m in jnp.split(mod[layer], 6, axis=-1)]
        gains = norm_gains[layer].reshape(4, 1, d)
        wrow, wt, wuk, wuvt, wpool = _layer_weights(w_in[layer], w_uk[layer], w_uv[layer], w_pool[layer])
        (qat, qit, wit, ka, vat, ki, qct, kc, vct, ga, gc, mb) = _inproj_call(
            x, sh1, sc1, gains[0], wrow, wt, cos_r, sin_r, cos_t, sin_t,
            g_kv_latent[layer].reshape(1, KV_LATENT), wuk, wuvt, wpool,
            pool_scale[layer].reshape(1, POOL_WIDTH), w_br_b[layer].astype(BF16))
        ya = _dsa_call(qit, wit, qat, ki, ka, vat, tri)
        yc = _stick_call(qct, kc, vct, later2)
        i = layer // 2
        merge_args = (x, ya, yc, mb, ga, gc, gt1, sh2, sc2, gains[1], gains[2],
                      w_br_a[layer].astype(BF16), w_br_c[layer].astype(BF16), w_out[layer].astype(BF16))
        if layer % 2 == 0:
            x1, h2 = _merge_call(*merge_args)
            x = _ffn_call(x1, h2, gt2, gains[3], w_gate_dense[i].astype(BF16), w_up_dense[i].astype(BF16),
                          w_down_dense[i].astype(BF16))
        else:
            x1, h2, route, counts = _merge_call(*merge_args, wr=_pad_cols(w_router[i], LANES))
            x = _moe_call(x1, h2, route, counts, before, gt2, gains[3], w_gate_moe[i].astype(BF16),
                          w_up_moe[i].astype(BF16), w_down_moe[i].astype(BF16))
    return x
```

```python
import functools

import jax
import jax.numpy as jnp
from jax import lax
from jax.experimental import pallas as pl
from jax.experimental.pallas import tpu as pltpu

F32 = jnp.float32
BF16 = jnp.bfloat16
I32 = jnp.int32
I16 = jnp.int16

D_MODEL = 1024
HEAD_DIM = 64
ROPE_DIM = HEAD_DIM // 4
NOPE_DIM = HEAD_DIM - ROPE_DIM
ROPE_THETA = 500000.0
N_HEADS_A = (3 * D_MODEL // 8) // HEAD_DIM
A_WIDTH = N_HEADS_A * HEAD_DIM
KV_LATENT = D_MODEL // 8
N_IDX_HEADS = 4
IDX_DIM = 64
TOPK_MAX = 256
N_POOL_GROUPS = 4
POOL_WINDOWS = (2, 4, 8, 16)
POOL_WIDTH = D_MODEL // 4
POOL_GROUP_DIM = POOL_WIDTH // N_POOL_GROUPS
N_HEADS_C = (D_MODEL // 4) // HEAD_DIM
C_WIDTH = N_HEADS_C * HEAD_DIM
N_BRANCHES = 3
IN_SIZES = (A_WIDTH, KV_LATENT, ROPE_DIM, N_IDX_HEADS * IDX_DIM, IDX_DIM, N_IDX_HEADS,
            POOL_WIDTH, C_WIDTH, C_WIDTH, C_WIDTH, N_BRANCHES * D_MODEL)
D_FF = 2816
N_EXPERTS = 8
TOP_K = 2
D_FF_EXPERT = D_FF // TOP_K
RMS_EPS = 1e-6

LANES = 128
QB = 128
KC = 512
QS_STICK = 512
QS_DSA = 256
KB_DSA = 512
VB_DSA = 256
V_ROWS = 80
STICK_DEAD = -106.0
TM_IN = 256
TM_MERGE = 256
TM_FFN = 512
TM_MOE = 1024
ROW_CHUNK = 128
POOL_HALO = 16
VMEM_LIMIT = 56 * 1024 * 1024
INT_MIN = -2147483648
LOG2_E = 1.4426950408889634
NEG_BIG = -1e30
NEG_MASK = -2e30

R_LAT, R_KR, R_KRR, R_KI, R_KIR, R_UP, R_KC, R_GATE = 0, 128, 256, 384, 512, 640, 896, 1152
R_WIDTH = R_GATE + N_BRANCHES * D_MODEL
T_QA, T_QAR, T_QI, T_QIR, T_QC, T_VC, T_WI = 0, 384, 768, 1024, 1280, 1792, 2048
T_WI_ROWS = 16
T_HEIGHT = T_WI + T_WI_ROWS


def _params(sem):
    return pltpu.CompilerParams(dimension_semantics=sem, vmem_limit_bytes=VMEM_LIMIT)


def _sigmoid(v):
    return 1.0 / (1.0 + jnp.exp(-v))


def _rms(v, gain):
    return v * lax.rsqrt(jnp.mean(v * v, axis=-1, keepdims=True) + RMS_EPS) * gain


def _mod_kernel(c_ref, w_ref, b_ref, o_ref):
    c = c_ref[...]
    cond = c * _sigmoid(c)
    o_ref[0] = jnp.dot(cond, w_ref[0], preferred_element_type=F32) + b_ref[0]


def _mod_call(c_pad, w_ada, b_ada):
    depth, d, n = w_ada.shape
    tn = 1024
    return pl.pallas_call(
        _mod_kernel,
        grid=(depth, n // tn),
        in_specs=[
            pl.BlockSpec((8, d), lambda l, j: (0, 0)),
            pl.BlockSpec((1, d, tn), lambda l, j: (l, 0, j)),
            pl.BlockSpec((1, 1, tn), lambda l, j: (l, 0, j)),
        ],
        out_specs=pl.BlockSpec((1, 8, tn), lambda l, j: (l, 0, j)),
        out_shape=jax.ShapeDtypeStruct((depth, 8, n), F32),
        compiler_params=_params(("arbitrary", "arbitrary")),
        name="adaln_mod",
    )(c_pad, w_ada, b_ada.reshape(depth, 1, n))


def _inproj_kernel(x_ref, sh_ref, sc_ref, g_ref, wrow_ref, wt_ref, c_ref, s_ref, ct_ref, st_ref,
                   glat_ref, wuk_ref, wuvt_ref, wpool_ref, pscale_ref, wbrb_ref,
                   qat_ref, qit_ref, wit_ref, ka_ref, vat_ref, ki_ref, qct_ref, kc_ref, vct_ref,
                   ga_ref, gc_ref, mb_ref,
                   h_scr, ht_scr, ext_scr, prev_scr):
    i = pl.program_id(1)
    tm = x_ref.shape[1]
    x = x_ref[0]
    h = _rms(x, g_ref[...]) * (1.0 + sc_ref[0]) + sh_ref[0]
    h_scr[...] = h.astype(BF16)
    ht_scr[...] = h.T.astype(BF16)

    def rowdot(a, width):
        return jnp.dot(h_scr[...], wrow_ref[:, a:a + width], preferred_element_type=F32)

    def tdot(a, height):
        return jnp.dot(wt_ref[a:a + height, :], ht_scr[...], preferred_element_type=F32)

    cos_r, sin_r = c_ref[0], s_ref[0]
    cos_t, sin_t = ct_ref[0], st_ref[0]

    latn = _rms(rowdot(R_LAT, KV_LATENT), glat_ref[...])
    ka = (rowdot(R_KR, LANES) * cos_r + rowdot(R_KRR, LANES) * sin_r
          + jnp.dot(latn.astype(BF16), wuk_ref[...], preferred_element_type=F32))
    ka_ref[0] = ka[:, :HEAD_DIM].astype(BF16)
    vat = jnp.dot(wuvt_ref[...], latn.T.astype(BF16), preferred_element_type=F32)
    pad_row = lax.broadcasted_iota(I32, (V_ROWS - HEAD_DIM, tm), 0)
    vat = jnp.concatenate([vat, jnp.where(pad_row == 0, 1.0, 0.0)], axis=0)
    for j in range(tm // VB_DSA):
        vat_ref[0, j] = vat[:, j * VB_DSA:(j + 1) * VB_DSA].astype(BF16)

    ki =rowdot(R_KI, LANES) * cos_r + rowdot(R_KIR, LANES) * sin_r
    ki_ref[0] = ki[:, :IDX_DIM].astype(BF16)

    qa, qar = tdot(T_QA, A_WIDTH), tdot(T_QAR, A_WIDTH)
    for hh in range(N_HEADS_A):
        r = slice(hh * HEAD_DIM, (hh + 1) * HEAD_DIM)
        qat_ref[0, r, :] = (qa[r] * cos_t + qar[r] * sin_t).astype(BF16)
    qi, qir = tdot(T_QI, N_IDX_HEADS * IDX_DIM), tdot(T_QIR, N_IDX_HEADS * IDX_DIM)
    for hh in range(N_IDX_HEADS):
        r = slice(hh * IDX_DIM, (hh + 1) * IDX_DIM)
        qit_ref[0, r, :] = (qi[r] * cos_t + qir[r] * sin_t).astype(BF16)
    wit_ref[0] = tdot(T_WI, T_WI_ROWS)

    qct_ref[0] = tdot(T_QC, 2 * C_WIDTH).astype(BF16)
    kc_ref[0] = rowdot(R_KC, C_WIDTH).astype(BF16)
    vct = tdot(T_VC, C_WIDTH)
    for j in range(tm // QB):
        vct_ref[0, j] = vct[:, j * QB:(j + 1) * QB].astype(BF16)

    up = rowdot(R_UP, POOL_WIDTH)

    @pl.when(i == 0)
    def _():
        prev_scr[...] = jnp.zeros_like(prev_scr)

    ext_scr[0:POOL_HALO, :] = prev_scr[...]
    ext_scr[POOL_HALO:POOL_HALO + tm, :] = up
    prev_scr[...] = up[tm - POOL_HALO:, :]
    lag = [ext_scr[POOL_HALO - j:POOL_HALO - j + tm, :] for j in range(POOL_HALO)]
    sums = {}
    run = lag[0]
    for j in range(1, POOL_HALO):
        run = run + lag[j]
        if j + 1 in POOL_WINDOWS:
            sums[j + 1] = run
    lane = lax.broadcasted_iota(I32, (tm, POOL_WIDTH), 1)
    pos = i * tm + lax.broadcasted_iota(I32, (tm, POOL_WIDTH), 0)
    pooled_sum = sums[POOL_WINDOWS[-1]]
    win = jnp.full((tm, POOL_WIDTH), POOL_WINDOWS[-1], I32)
    for g in range(N_POOL_GROUPS - 2, -1, -1):
        in_group = lane < (g + 1) * POOL_GROUP_DIM
        pooled_sum = jnp.where(in_group, sums[POOL_WINDOWS[g]], pooled_sum)
        win = jnp.where(in_group, POOL_WINDOWS[g], win)
    cnt = jnp.minimum(pos + 1, win).astype(F32)
    pooled = pooled_sum / cnt - up
    yb = jnp.dot(pooled.astype(BF16), wpool_ref[...], preferred_element_type=F32) * pscale_ref[...]

    ga_ref[0] = _sigmoid(rowdot(R_GATE, D_MODEL)).astype(BF16)
    gb = _sigmoid(rowdot(R_GATE + D_MODEL, D_MODEL))
    mb_ref[0] = (gb * jnp.dot(yb.astype(BF16), wbrb_ref[...], preferred_element_type=F32)).astype(BF16)
    gc_ref[0] = _sigmoid(rowdot(R_GATE + 2 * D_MODEL, D_MODEL)).astype(BF16)


def _inproj_call(x, sh, sc, gain, wrow, wt, cos_r, sin_r, cos_t, sin_t, glat, wuk, wuvt, wpool, pscale, wbrb):
    b, l, d = x.shape
    tm = min(TM_IN, l)
    nq = l // QB
    tok = lambda w: pl.BlockSpec((1, tm, w), lambda bi, i: (bi, i, 0))
    feat = lambda hgt: pl.BlockSpec((1, hgt, tm), lambda bi, i: (bi, 0, i))
    blk = lambda hgt, w: pl.BlockSpec((1, tm // w, hgt, w), lambda bi, i: (bi, i, 0, 0))
    full = lambda a: pl.BlockSpec(a.shape, lambda bi, i: (0,) * a.ndim)
    vec = pl.BlockSpec((1, 1, d), lambda bi, i: (bi, 0, 0))
    out_shape = (
        jax.ShapeDtypeStruct((b, A_WIDTH, l), BF16),
        jax.ShapeDtypeStruct((b, N_IDX_HEADS * IDX_DIM, l), BF16),
        jax.ShapeDtypeStruct((b, T_WI_ROWS, l), F32),
        jax.ShapeDtypeStruct((b, l, HEAD_DIM), BF16),
        jax.ShapeDtypeStruct((b, l // VB_DSA, V_ROWS, VB_DSA), BF16),
        jax.ShapeDtypeStruct((b, l, IDX_DIM), BF16),
        jax.ShapeDtypeStruct((b, 2 * C_WIDTH, l), BF16),
        jax.ShapeDtypeStruct((b, l, C_WIDTH), BF16),
        jax.ShapeDtypeStruct((b, nq, C_WIDTH, QB), BF16),
        jax.ShapeDtypeStruct((b, l, d), BF16),
        jax.ShapeDtypeStruct((b, l, d), BF16),
        jax.ShapeDtypeStruct((b, l, d), BF16),
    )
    out_specs = (feat(A_WIDTH), feat(N_IDX_HEADS * IDX_DIM), feat(T_WI_ROWS), tok(HEAD_DIM), blk(V_ROWS, VB_DSA),
                 tok(IDX_DIM), feat(2 * C_WIDTH), tok(C_WIDTH), blk(C_WIDTH, QB), tok(d), tok(d), tok(d))
    in_specs = [tok(d), vec, vec, full(gain), full(wrow), full(wt), tok(LANES), tok(LANES),
                feat(HEAD_DIM), feat(HEAD_DIM), full(glat), full(wuk), full(wuvt), full(wpool),
                full(pscale), full(wbrb)]
    return pl.pallas_call(
        _inproj_kernel,
        grid=(b, l // tm),
        in_specs=in_specs,
        out_specs=out_specs,
        out_shape=out_shape,
        scratch_shapes=[pltpu.VMEM((tm, d), BF16), pltpu.VMEM((d, tm), BF16),
                        pltpu.VMEM((tm + POOL_HALO, POOL_WIDTH), F32), pltpu.VMEM((POOL_HALO, POOL_WIDTH), F32)],
        compiler_params=_params(("arbitrary", "arbitrary")),
        name="inproj",
    )(x, sh, sc, gain, wrow, wt, cos_r, sin_r, cos_t, sin_t, glat, wuk, wuvt, wpool, pscale, wbrb)


def _dsa_kernel(qit_ref, wit_ref, qat_ref, ki_ref, ka_ref, vat_ref, tri_ref, ya_ref,
                keys_scr, top_scr, acc_scr, *, k_sel):
    i = pl.program_id(1)
    qs = qat_ref.shape[2]
    n_blocks = (i + 1) * (qs // QB)
    n_chunks = (n_blocks * QB + KC - 1) // KC
    qpos = i * qs + lax.broadcasted_iota(I32, (1, qs), 1)
    w_idx = wit_ref[0]

    def score_chunk(c, carry, masked):
        r0 = pl.multiple_of(c * KC, KC)
        kblk = ki_ref[0, pl.ds(r0, KC), :]
        parts = [jnp.dot(kblk, qit_ref[0, hh * IDX_DIM:(hh + 1) * IDX_DIM, :], preferred_element_type=F32)
                 for hh in range(N_IDX_HEADS)]
        score = jnp.maximum(parts[0], 0.0) * w_idx[0:1, :]
        for hh in range(1, N_IDX_HEADS):
            score = score + jnp.maximum(parts[hh], 0.0) * w_idx[hh:hh + 1, :]
        bits = lax.bitcast_convert_type(score, I32)
        key = jnp.where(bits < 0, INT_MIN - bits, bits)
        if masked:
            kpos = r0 + lax.broadcasted_iota(I32, (KC, qs), 0)
            key = jnp.where(kpos <= qpos, key, INT_MIN)
        keys_scr[pl.ds(r0, KC), :] = key
        top = lax.shift_right_arithmetic(key, 16).astype(I16)
        top_scr[pl.ds(r0, KC), :] = top
        groups = top.reshape(KC // 16, 16, qs)
        for j in range(KC // 16):
            carry = jnp.where(groups[j] > carry, groups[j], carry)
        return carry

    n_open = (i * qs) // KC
    top_max = lax.fori_loop(0, n_open, functools.partial(score_chunk, masked=False),
                            jnp.full((16, qs), -2 ** 15, I16))
    top_max = lax.fori_loop(n_open, n_chunks, functools.partial(score_chunk, masked=True), top_max)
    top_max = jnp.max(top_max.astype(I32), axis=0, keepdims=True)

    def count_rows(src, rows_per_vreg, trial):
        n_acc = 4
        groups = KC // rows_per_vreg

        def body(c, accs):
            r0 = pl.multiple_of(c * KC, KC)
            rows = src[pl.ds(r0, KC), :].reshape(groups, rows_per_vreg, qs)
            accs = list(accs)
            for j in range(groups):
                a = accs[j % n_acc]
                accs[j % n_acc] = jnp.where(rows[j] >= trial, a + 1, a)
            return tuple(accs)

        zero = jnp.zeros((rows_per_vreg, qs), src.dtype)
        accs = lax.fori_loop(0, n_chunks, body, tuple(zero for _ in range(n_acc)))
        total = (accs[0] + accs[1]) + (accs[2] + accs[3])
        return jnp.sum(total.astype(I32), axis=0, keepdims=True)

    def count_ge(trial):
        return count_rows(keys_scr, 8, trial)

    def count_ge_top(trial):
        return count_rows(top_scr, 16, lax.shift_right_arithmetic(trial, 16).astype(I16))

    c_zero = count_ge_top(jnp.zeros((1, qs), I32))
    c_pos = count_ge(jnp.ones((1, qs), I32))
    tie_at_zero = jnp.logical_and(c_zero >= k_sel, c_pos < k_sel)

    def unsettled(c_ans):
        settled = jnp.logical_or(tie_at_zero, c_ans == k_sel)
        return jnp.max(jnp.where(settled, 0, 1))

    def search(counter, lowest_bit, group, state):
        def refine(carry):
            bit, ans, c_ans, _ = carry
            for g in range(group):
                trial = ans + lax.shift_left(jnp.int32(1), bit - g)
                c = counter(trial)
                ok = c >= k_sel
                c_ans = jnp.where(ok, c, c_ans)
                ans = jnp.where(ok, trial, ans)
            return bit - group, ans, c_ans, unsettled(c_ans)

        return lax.while_loop(lambda cr: jnp.logical_and(cr[0] >= lowest_bit, cr[3] > 0), refine, state)

    def count_within(gap):
        return count_rows(top_scr, 16, jnp.maximum(top_max - gap, -2 ** 15).astype(I16))

    near = 2 ** 8 - 1
    c_near = count_within(near)
    all_near = jnp.min(jnp.where(c_near >= k_sel, 1, 0)) > 0
    first_bit = jnp.where(all_near, 7, 15)

    def widen(carry):
        bit, gap, c_gap = carry
        for g in range(4):
            trial = gap + lax.shift_left(jnp.int32(1), bit - g) - 1
            c = count_within(trial)
            ok = c >= k_sel
            gap = jnp.where(ok, gap, trial + 1)
            c_gap = jnp.where(ok, c, c_gap)
        return bit - 4, gap, c_gap

    c_far = jnp.where(all_near, c_near, jnp.int32(2 ** 30))
    _, gap, c_top = lax.while_loop(lambda cr: cr[0] >= 0, widen, (first_bit, jnp.zeros((1, qs), I32), c_far))
    bucket = jnp.where(qpos + 1 < k_sel, -2 ** 15, jnp.maximum(top_max - gap, -2 ** 15))
    top = lax.shift_left(bucket, 16)
    alive = unsettled(c_top)
    last_bucket = bucket >= 2 ** 15 - 1
    c_over = jnp.where(last_bucket, 0,
                       count_rows(top_scr, 16, jnp.where(last_bucket, bucket, bucket + 1).astype(I16)))

    def pack_low(c, carry):
        r0 = pl.multiple_of(c * KC, KC)
        low = (keys_scr[pl.ds(r0, KC), :] ^ 0x8000).astype(I16)
        top_scr[pl.ds(r0, KC), :] = jnp.where(top_scr[pl.ds(r0, KC), :] == bucket.astype(I16), low, -2 ** 15)
        return carry

    lax.fori_loop(0, n_chunks, pack_low, 0)

    def count_ge_low(low_trial):
        return c_over + count_rows(top_scr, 16, (low_trial - 2 ** 15).astype(I16))

    zero_row = jnp.zeros((1, qs), I32)
    _, low, _, _ = search(count_ge_low, 0, 4, (jnp.int32(15), zero_row, c_top, alive))
    thr = top + low
    full_low = low >= 2 ** 16 - 1
    n_above = jnp.where(full_low, c_over, count_ge_low(jnp.where(full_low, low, low + 1)))
    n_ties = jnp.where(thr == INT_MIN, 0, k_sel - n_above).astype(F32)

    acc_scr[...] = jnp.zeros_like(acc_scr)
    heads = range(N_HEADS_A)

    kb_rows = KB_DSA
    vb_rows = vat_ref.shape[3]
    v_per_k = kb_rows // vb_rows

    n_steps = ((i + 1) * qs + kb_rows - 1) // kb_rows

    v_rows = vat_ref.shape[2]

    def attend(kb, carry):
        seen, ms = carry
        r0 = pl.multiple_of(kb * kb_rows, kb_rows)
        keyb = keys_scr[pl.ds(r0, kb_rows), :]
        tied = keyb == thr
        tied_b = jnp.where(tied, 1.0, 0.0).astype(BF16)
        ranks = []
        for j in range(kb_rows // QB):
            within = jnp.dot(tri_ref[...], tied_b[j * QB:(j + 1) * QB, :], preferred_element_type=F32)
            ranks.append(seen + within)
            seen = seen + within[QB - 1:QB, :]
        rank = jnp.concatenate(ranks, axis=0)
        keep = jnp.logical_or(keyb > thr, jnp.logical_and(tied, rank <= n_ties))
        bias = jnp.where(keep, 0.0, NEG_MASK).astype(BF16)
        kblk = ka_ref[0, pl.ds(r0, kb_rows), :]
        logits = [jnp.dot(kblk, qat_ref[0, hh * HEAD_DIM:(hh + 1) * HEAD_DIM, :],
                          preferred_element_type=F32).astype(BF16) + bias for hh in heads]
        new_ms = [jnp.maximum(ms[hh], jnp.max(logits[hh], axis=0, keepdims=True).astype(F32)) for hh in heads]
        probs = [jnp.exp2(logits[hh] - new_ms[hh].astype(BF16)) for hh in heads]
        alphas = [jnp.exp2(ms[hh] - new_ms[hh]) for hh in heads]
        outs = []
        for hh in heads:
            out = jnp.dot(vat_ref[0, kb * v_per_k], probs[hh][:vb_rows, :], preferred_element_type=F32)
            for j in range(1, v_per_k):
                out = out + jnp.dot(vat_ref[0, kb * v_per_k + j], probs[hh][j * vb_rows:(j + 1) * vb_rows, :],
                                    preferred_element_type=F32)
            outs.append(out)
        for hh in heads:
            r = slice(hh * v_rows, (hh + 1) * v_rows)
            acc_scr[r, :] = acc_scr[r, :] * alphas[hh] + outs[hh]
        return seen, tuple(new_ms)

    init = (jnp.zeros((1, qs), F32), tuple(jnp.full((1, qs), NEG_BIG, F32) for _ in heads))
    lax.fori_loop(0, n_steps, attend, init)
    normed = [acc_scr[hh * v_rows:hh * v_rows + HEAD_DIM, :] / acc_scr[hh * v_rows + HEAD_DIM:hh * v_rows + HEAD_DIM + 1, :]
              for hh in heads]
    ya_ref[0] = jnp.concatenate(normed, axis=0).T.astype(BF16)


def _dsa_call(qit, wit, qat, ki, ka, vat, tri):
    b, l, _ = ki.shape
    nq = l // QB
    qs = min(QS_DSA, l)
    k_sel = min(TOPK_MAX, l // 4)
    return pl.pallas_call(
        functools.partial(_dsa_kernel, k_sel=k_sel),
        grid=(b, l // qs),
        in_specs=[
            pl.BlockSpec((1, N_IDX_HEADS * IDX_DIM, qs), lambda bi, i: (bi, 0, i)),
            pl.BlockSpec((1, T_WI_ROWS, qs), lambda bi, i: (bi, 0, i)),
            pl.BlockSpec((1, A_WIDTH, qs), lambda bi, i: (bi, 0, i)),
            pl.BlockSpec((1, l, IDX_DIM), lambda bi, i: (bi, 0, 0), pipeline_mode=pl.Buffered(1)),
            pl.BlockSpec((1, l, HEAD_DIM), lambda bi, i: (bi, 0, 0), pipeline_mode=pl.Buffered(1)),
            pl.BlockSpec((1, l // VB_DSA, V_ROWS, VB_DSA), lambda bi, i: (bi, 0, 0, 0),
                         pipeline_mode=pl.Buffered(1)),
            pl.BlockSpec((QB, QB), lambda bi, i: (0, 0)),
        ],
        out_specs=pl.BlockSpec((1, qs, A_WIDTH), lambda bi, i: (bi, i, 0)),
        out_shape=jax.ShapeDtypeStruct((b, l, A_WIDTH), BF16),
        scratch_shapes=[pltpu.VMEM((l, qs), I32), pltpu.VMEM((l, qs), I16),
                        pltpu.VMEM((N_HEADS_A * V_ROWS, qs), F32)],
        compiler_params=_params(("arbitrary", "arbitrary")),
        name="dsa_attention",
    )(qit, wit, qat, ki, ka, vat, tri)


def _stick_kernel(qct_ref, kc_ref, vct_ref, later_ref, yc_ref, acc_scr):
    i = pl.program_id(1)
    qs = qct_ref.shape[2]
    diag_blocks = qs // QB
    qpos = i * qs + lax.broadcasted_iota(I32, (1, qs), 1)
    acc_scr[...] = jnp.zeros_like(acc_scr)

    def block(kb, tail, masked, lane0=0):
        r0 = pl.multiple_of(kb * QB, QB)
        kfull = kc_ref[0, pl.ds(r0, QB), :]
        vt = vct_ref[0, kb]
        heads = range(N_HEADS_C)
        old = [tail[hh][:, lane0:] for hh in heads]
        if masked:
            mask = (r0 + lax.broadcasted_iota(I32, (QB, qs - lane0), 0)) < qpos[:, lane0:]
        zs = [jnp.dot(kfull[:, (hh // 2) * LANES:(hh // 2 + 1) * LANES],
                      qct_ref[0, hh * LANES:(hh + 1) * LANES, lane0:], preferred_element_type=F32) for hh in heads]
        log_betas, splits, new_tail = [], [], []
        for hh in heads:
            z = zs[hh]
            log_beta = jnp.minimum(z, 0.0) - jnp.log(1.0 + jnp.exp(-jnp.abs(z)))
            log_keep = log_beta - z
            if masked:
                log_keep = jnp.where(mask, log_keep, 0.0)
            hi = log_keep.astype(BF16)
            lo = (log_keep - hi.astype(F32)).astype(BF16)
            log_betas.append(log_beta)
            splits.append(jnp.concatenate([hi, lo], axis=0))
            new = old[hh] + jnp.sum(log_keep, axis=0, keepdims=True)
            new_tail.append(new if lane0 == 0 else jnp.concatenate([tail[hh][:, :lane0], new], axis=1))
        withins = [jnp.dot(later_ref[...], splits[hh], preferred_element_type=F32) for hh in heads]
        weights = []
        for hh in heads:
            a = jnp.exp(log_betas[hh] + withins[hh] + old[hh])
            if masked:
                a = jnp.where(mask, a, 0.0)
            weights.append(a.astype(BF16))
        for hh in heads:
            r = slice(hh * HEAD_DIM, (hh + 1) * HEAD_DIM)
            acc_scr[r, lane0:] = acc_scr[r, lane0:] + jnp.dot(vt[r, :], weights[hh], preferred_element_type=F32)
        return tuple(new_tail)

    def largest(tail):
        worst = tail[0]
        for hh in range(1, N_HEADS_C):
            worst = jnp.maximum(worst, tail[hh])
        return jnp.max(worst)

    n_rest = i * diag_blocks
    tail = tuple(jnp.zeros((1, qs), F32) for _ in range(N_HEADS_C))
    for d in reversed(range(diag_blocks)):
        tail = block(n_rest + d, tail, True, lane0=d * QB)

    def live(cr):
        return jnp.logical_and(cr[0] < n_rest, cr[2] > STICK_DEAD)

    def step(cr):
        t, tl, _ = cr
        tl = block(n_rest - 1 - t, tl, False)
        return t + 1, tl, largest(tl)

    lax.while_loop(live, step, (jnp.int32(0), tail, largest(tail)))
    yc_ref[0] = acc_scr[...].T.astype(BF16)


def _stick_call(qct, kc, vct, later2):
    b, l, _ = kc.shape
    nq = l // QB
    qs = min(QS_STICK, l)
    return pl.pallas_call(
        _stick_kernel,
        grid=(b, l // qs),
        in_specs=[
            pl.BlockSpec((1, 2 * C_WIDTH, qs), lambda bi, i: (bi, 0, i)),
            pl.BlockSpec((1, l, C_WIDTH), lambda bi, i: (bi, 0, 0)),
            pl.BlockSpec((1, nq, C_WIDTH, QB), lambda bi, i: (bi, 0, 0, 0)),
            pl.BlockSpec((QB, 2 * QB), lambda bi, i: (0, 0)),
        ],
        out_specs=pl.BlockSpec((1, qs, C_WIDTH), lambda bi, i: (bi, i, 0)),
        out_shape=jax.ShapeDtypeStruct((b, l, C_WIDTH), BF16),
        scratch_shapes=[pltpu.VMEM((C_WIDTH, qs), F32)],
        compiler_params=_params(("arbitrary", "arbitrary")),
        name="stick_attention",
    )(qct, kc, vct, later2)


def _merge_kernel(x_ref, ya_ref, yc_ref, mb_ref, ga_ref, gc_ref, gt_ref, sh_ref, sc_ref, g1_ref, g2_ref,
                  wbra_ref, wbrc_ref, wout_ref, *rest, with_router):
    if with_router:
        wr_ref, x1_ref, h2_ref, route_ref, cnt_ref = rest
    else:
        x1_ref, h2_ref = rest
    merged = (ga_ref[0].astype(F32) * jnp.dot(ya_ref[0], wbra_ref[...], preferred_element_type=F32)
              + mb_ref[0].astype(F32)
              + gc_ref[0].astype(F32) * jnp.dot(yc_ref[0], wbrc_ref[...], preferred_element_type=F32))
    y = jnp.dot(merged.astype(BF16), wout_ref[...], preferred_element_type=F32)
    x1 = x_ref[0] + gt_ref[0] * _rms(y, g1_ref[...])
    x1_ref[0] = x1
    h2 = _rms(x1, g2_ref[...]) * (1.0 + sc_ref[0]) + sh_ref[0]
    h2_ref[0] = h2.astype(BF16)
    if with_router:
        logits = jnp.dot(h2, wr_ref[...], preferred_element_type=F32)
        lane = lax.broadcasted_iota(I32, logits.shape, 1)
        valid = lane < N_EXPERTS
        l1 = jnp.where(valid, logits, -jnp.inf)
        v1 = jnp.max(l1, axis=-1, keepdims=True)
        i1 = jnp.min(jnp.where(l1 == v1, lane, LANES), axis=-1, keepdims=True)
        l2 = jnp.where(lane == i1, -jnp.inf, l1)
        v2 = jnp.max(l2, axis=-1, keepdims=True)
        i2 = jnp.min(jnp.where(l2 == v2, lane, LANES), axis=-1, keepdims=True)
        e2 = jnp.exp(v2 - v1)
        p1 = 1.0 / (1.0 + e2)
        p2 = e2 / (1.0 + e2)
        route = jnp.where(lane == 0, i1.astype(F32), jnp.where(lane == 1, i2.astype(F32),
                          jnp.where(lane == 2, p1, jnp.where(lane == 3, p2, 0.0))))
        route_ref[0] = route
        chosen = jnp.where(jnp.logical_or(lane == i1, lane == i2), 1.0, 0.0)
        cnt_ref[0, 0] = jnp.broadcast_to(jnp.sum(chosen, axis=0, keepdims=True), (8, LANES))


def _merge_call(x, ya, yc, mb, ga, gc, gt, sh, sc, g1, g2, wbra, wbrc, wout, wr=None):
    b, l, d = x.shape
    tm = min(TM_MERGE if wr is None else TM_FFN, l)
    tok = lambda w: pl.BlockSpec((1, tm, w), lambda bi, i: (bi, i, 0))
    full = lambda a: pl.BlockSpec(a.shape, lambda bi, i: (0,) * a.ndim)
    vec = pl.BlockSpec((1, 1, d), lambda bi, i: (bi, 0, 0))
    in_specs = [tok(d), tok(A_WIDTH), tok(C_WIDTH), tok(d), tok(d), tok(d), vec, vec, vec,
                full(g1), full(g2), full(wbra), full(wbrc), full(wout)]
    args = [x, ya, yc, mb, ga, gc, gt, sh, sc, g1, g2, wbra, wbrc, wout]
    out_shape = [jax.ShapeDtypeStruct((b, l, d), F32), jax.ShapeDtypeStruct((b, l, d), BF16)]
    out_specs = [tok(d), tok(d)]
    if wr is not None:
        in_specs.append(full(wr))
        args.append(wr)
        out_shape += [jax.ShapeDtypeStruct((b, l, LANES), F32), jax.ShapeDtypeStruct((b, l // tm, 8, LANES), F32)]
        out_specs += [tok(LANES), pl.BlockSpec((1, 1, 8, LANES), lambda bi, i: (bi, i, 0, 0))]
    return pl.pallas_call(
        functools.partial(_merge_kernel, with_router=wr is not None),
        grid=(b, l // tm),
        in_specs=in_specs,
        out_specs=out_specs,
        out_shape=out_shape,
        compiler_params=_params(("arbitrary", "arbitrary")),
        name="merge_router" if wr is not None else "merge",
    )(*args)


def _ffn_kernel(x1_ref, h2_ref, gt_ref, g3_ref, wg_ref, wu_ref, wd_ref, o_ref):
    h2 = h2_ref[0]
    gate = jnp.dot(h2, wg_ref[...], preferred_element_type=F32)
    up = jnp.dot(h2, wu_ref[...], preferred_element_type=F32)
    act = (gate * _sigmoid(gate) * up).astype(BF16)
    y = jnp.dot(act, wd_ref[...], preferred_element_type=F32)
    o_ref[0] = x1_ref[0] + gt_ref[0] * _rms(y, g3_ref[...])


def _ffn_call(x1, h2, gt, g3, wg, wu, wd):
    b, l, d = x1.shape
    tm = min(TM_FFN, l)
    tok = pl.BlockSpec((1, tm, d), lambda bi, i: (bi, i, 0))
    full = lambda a: pl.BlockSpec(a.shape, lambda bi, i: (0,) * a.ndim)
    vec = pl.BlockSpec((1, 1, d), lambda bi, i: (bi, 0, 0))
    return pl.pallas_call(
        _ffn_kernel,
        grid=(b, l // tm),
        in_specs=[tok, tok, vec, full(g3), full(wg), full(wu), full(wd)],
        out_specs=tok,
        out_shape=jax.ShapeDtypeStruct((b, l, d), F32),
        compiler_params=_params(("arbitrary", "arbitrary")),
        name="ffn_dense",
    )(x1, h2, gt, g3, wg, wu, wd)


def _moe_kernel(nchunk_ref, first_ref, total_ref,
                x1_ref, h2_ref, route_ref, before_ref, gt_ref, g3_ref, wg_ref, wu_ref, wd_ref, o_ref,
                xs_scr, wrow_scr, dest_scr, acc_scr):
    tile = pl.program_id(0) * pl.num_programs(1) + pl.program_id(1)
    e = pl.program_id(2)
    n_e = pl.num_programs(2)
    tm = h2_ref.shape[1]
    total = total_ref[tile]

    @pl.when(e == 0)
    def _():
        route = route_ref[0]
        lane = lax.broadcasted_iota(I32, route.shape, 1).astype(F32)
        hot = [jnp.where(lane == route[:, s:s + 1], 1.0, 0.0) for s in range(TOP_K)]
        earlier = jnp.dot(before_ref[...], (hot[0] + hot[1]).astype(BF16), preferred_element_type=F32)
        start = jnp.zeros((1, LANES), F32)
        for ee in range(N_EXPERTS):
            start = jnp.where(lane[:1, :] == ee, (first_ref[tile * N_EXPERTS + ee] * ROW_CHUNK).astype(F32), start)
        where_to = earlier + start
        dest = [hot[s] * where_to for s in range(TOP_K)]
        prob = [hot[s] * route[:, TOP_K + s:TOP_K + s + 1] for s in range(TOP_K)]
        for s in range(TOP_K):
            dest_scr[s] = jnp.broadcast_to(jnp.sum(dest[s], axis=1, keepdims=True), (tm, LANES))
        dest_row = [jnp.sum(dest[s].T, axis=0, keepdims=True) for s in range(TOP_K)]
        prob_row = [jnp.sum(prob[s].T, axis=0, keepdims=True) for s in range(TOP_K)]
        h2 = h2_ref[0]

        def place(c, carry):
            r0 = pl.multiple_of(c * ROW_CHUNK, ROW_CHUNK)
            rows = (r0 + lax.broadcasted_iota(I32, (ROW_CHUNK, tm), 0)).astype(F32)
            here = [rows == dest_row[s] for s in range(TOP_K)]
            pick = jnp.where(jnp.logical_or(here[0], here[1]), 1.0, 0.0).astype(BF16)
            xs_scr[pl.ds(r0, ROW_CHUNK), :] = jnp.dot(pick, h2, preferred_element_type=F32).astype(BF16)
            weight = jnp.where(here[0], prob_row[0], 0.0) + jnp.where(here[1], prob_row[1], 0.0)
            wrow_scr[pl.ds(r0, ROW_CHUNK), :] = jnp.broadcast_to(jnp.sum(weight, axis=1, keepdims=True),
                                                                  (ROW_CHUNK, LANES))
            return carry

        lax.fori_loop(0, total + total % 2, place, 0)

    def expert_chunk(c, carry):
        r0 = pl.multiple_of((first_ref[tile * n_e + e] + c) * ROW_CHUNK, ROW_CHUNK)
        xs = xs_scr[pl.ds(r0, ROW_CHUNK), :]
        gate = jnp.dot(xs, wg_ref[0], preferred_element_type=F32)
        up = jnp.dot(xs, wu_ref[0], preferred_element_type=F32)
        act = (gate * _sigmoid(gate) * up).astype(BF16)
        y = jnp.dot(act, wd_ref[0], preferred_element_type=F32)
        xs_scr[pl.ds(r0, ROW_CHUNK), :] = (y * wrow_scr[pl.ds(r0, ROW_CHUNK), 0:1]).astype(BF16)
        return carry

    lax.fori_loop(0, nchunk_ref[tile * n_e + e], expert_chunk, 0)

    @pl.when(e == n_e - 1)
    def _():
        acc_scr[...] = jnp.zeros_like(acc_scr)

        def collect(c, carry):
            r0 = pl.multiple_of(c * 2 * ROW_CHUNK, 2 * ROW_CHUNK)
            halves = []
            for j in range(2):
                cols = (r0 + j * ROW_CHUNK + lax.broadcasted_iota(I32, (tm, ROW_CHUNK), 1)).astype(F32)
                mine = jnp.logical_or(cols == dest_scr[0], cols == dest_scr[1])
                halves.append(jnp.where(mine, 1.0, 0.0).astype(BF16))
            acc_scr[...] = acc_scr[...] + jnp.dot(jnp.concatenate(halves, axis=1),
                                                  xs_scr[pl.ds(r0, 2 * ROW_CHUNK), :], preferred_element_type=F32)
            return carry

        lax.fori_loop(0, (total + 1) // 2, collect, 0)
        o_ref[0] = x1_ref[0] + gt_ref[0] * _rms(acc_scr[...], g3_ref[...])


def _moe_call(x1, h2, route, counts, before, gt, g3, wg, wu, wd):
    b, l, d = x1.shape
    tm = before.shape[0]
    n_e, _, ff = wg.shape
    cnt = counts[:, :, 0, :n_e].astype(I32).reshape(b * (l // tm), -1, n_e).sum(axis=1)
    nchunk = (cnt + ROW_CHUNK - 1) // ROW_CHUNK
    first = jnp.cumsum(nchunk, axis=1) - nchunk
    total = jnp.sum(nchunk, axis=1)
    rows = TOP_K * tm + n_e * ROW_CHUNK
    tok = lambda w: pl.BlockSpec((1, tm, w), lambda bi, i, e, *_: (bi, i, 0))
    tok_once = lambda w: pl.BlockSpec((1, tm, w), lambda bi, i, e, *_: (bi, i, 0), pipeline_mode=pl.Buffered(1))
    vec = pl.BlockSpec((1, 1, d), lambda bi, i, e, *_: (bi, 0, 0))
    grid_spec = pltpu.PrefetchScalarGridSpec(
        num_scalar_prefetch=3,
        grid=(b, l // tm, n_e),
        in_specs=[tok_once(d), tok_once(d), tok(LANES),
                  pl.BlockSpec(before.shape, lambda bi, i, e, *_: (0, 0), pipeline_mode=pl.Buffered(1)), vec,
                  pl.BlockSpec(g3.shape, lambda bi, i, e, *_: (0, 0)),
                  pl.BlockSpec((1, d, ff), lambda bi, i, e, *_: (e, 0, 0)),
                  pl.BlockSpec((1, d, ff), lambda bi, i, e, *_: (e, 0, 0)),
                  pl.BlockSpec((1, ff, d), lambda bi, i, e, *_: (e, 0, 0))],
        out_specs=tok(d),
        scratch_shapes=[pltpu.VMEM((rows, d), BF16), pltpu.VMEM((rows, LANES), F32),
                        pltpu.VMEM((TOP_K, tm, LANES), F32), pltpu.VMEM((tm, d), F32)],
    )
    return pl.pallas_call(
        _moe_kernel,
        grid_spec=grid_spec,
        out_shape=jax.ShapeDtypeStruct((b, l, d), F32),
        compiler_params=_params(("arbitrary", "arbitrary", "arbitrary")),
        name="ffn_moe",
    )(nchunk.reshape(-1), first.reshape(-1), total, x1, h2, route, before, gt, g3, wg, wu, wd)


def _rot_cols(w, n_heads, head_dim):
    d = w.shape[0]
    w = w.reshape(d, n_heads, head_dim)
    half = ROPE_DIM // 2
    rot = jnp.concatenate([-w[..., half:ROPE_DIM], w[..., :half],
                           jnp.zeros((d, n_heads, head_dim - ROPE_DIM), w.dtype)], axis=-1)
    return rot.reshape(d, n_heads * head_dim)


def _pad_cols(w, width):
    return jnp.pad(w, ((0, 0), (0, width - w.shape[1])))


def _layer_weights(w_in, w_uk, w_uv, w_pool):
    d = w_in.shape[0]
    offs, o = [], 0
    for s in IN_SIZES:
        offs.append(o)
        o += s
    piece = lambda k: w_in[:, offs[k]:offs[k] + IN_SIZES[k]]
    w_qa, w_lat, w_kr, w_qi, w_ki, w_wi, w_up, w_qc, w_kc, w_vc, w_gate = [piece(k) for k in range(len(IN_SIZES))]
    attn_scale = HEAD_DIM ** -0.5
    idx_scale = IDX_DIM ** -0.5 * N_IDX_HEADS ** -0.5
    wrow = jnp.concatenate([
        w_lat,
        _pad_cols(w_kr, LANES), _pad_cols(_rot_cols(w_kr, 1, ROPE_DIM), LANES),
        _pad_cols(w_ki, LANES), _pad_cols(_rot_cols(w_ki, 1, IDX_DIM), LANES),
        w_up, w_kc, w_gate], axis=1).astype(BF16)
    w_qc_t = (w_qc * attn_scale).T.reshape(N_HEADS_C, HEAD_DIM, d)
    zeros = jnp.zeros_like(w_qc_t)
    even = (jnp.arange(N_HEADS_C) % 2 == 0)[:, None, None]
    w_qc_pad = jnp.concatenate([jnp.where(even, w_qc_t, zeros), jnp.where(even, zeros, w_qc_t)], axis=1)
    w_qc_pad = w_qc_pad.reshape(2 * C_WIDTH, d)
    softmax_scale = attn_scale * LOG2_E
    wt = jnp.concatenate([
        (w_qa * softmax_scale).T, (_rot_cols(w_qa, N_HEADS_A, HEAD_DIM) * softmax_scale).T,
        w_qi.T, _rot_cols(w_qi, N_IDX_HEADS, IDX_DIM).T,
        w_qc_pad, w_vc.T,
        jnp.pad((w_wi * idx_scale).T, ((0, T_WI_ROWS - N_IDX_HEADS), (0, 0)))], axis=0).astype(BF16)
    wuk = jnp.zeros((KV_LATENT, LANES), F32).at[:, ROPE_DIM:HEAD_DIM].set(w_uk).astype(BF16)
    wuvt = w_uv.T.astype(BF16)
    wpool = jnp.zeros((POOL_WIDTH, POOL_WIDTH), F32)
    for g in range(N_POOL_GROUPS):
        sl = slice(g * POOL_GROUP_DIM, (g + 1) * POOL_GROUP_DIM)
        wpool = wpool.at[sl, sl].set(w_pool[g])
    return wrow, wt, wuk, wuvt, wpool.astype(BF16)


def _rope_tables(positions):
    inv = ROPE_THETA ** (-jnp.arange(0, ROPE_DIM, 2, dtype=F32) / ROPE_DIM)
    ang = positions.astype(F32)[..., None] * inv
    cos, sin = jnp.cos(ang), jnp.sin(ang)
    b, l = positions.shape
    ones = jnp.ones((b, l, HEAD_DIM - ROPE_DIM), F32)
    cos_h = jnp.concatenate([cos, cos, ones], axis=-1)
    sin_h = jnp.concatenate([sin, sin, jnp.zeros_like(ones)], axis=-1)
    cos_r = jnp.concatenate([cos_h, cos_h], axis=-1)
    sin_r = jnp.concatenate([sin_h, sin_h], axis=-1)
    return cos_r, sin_r, cos_h.transpose(0, 2, 1), sin_h.transpose(0, 2, 1)


def kernel(x, c, positions, w_ada, b_ada, norm_gains, w_in, g_kv_latent, w_uk, w_uv, w_pool, pool_scale,
           w_br_a, w_br_b, w_br_c, w_out, w_gate_dense, w_up_dense, w_down_dense,
           w_router, w_gate_moe, w_up_moe, w_down_moe):
    b, l, d = x.shape
    depth = w_in.shape[0]
    assert d == D_MODEL and l % max(KC, TM_FFN, TM_MOE) == 0
    cos_r, sin_r, cos_t, sin_t = _rope_tables(positions)
    c_pad = jnp.pad(c, ((0, 8 - b), (0, 0)))
    mod = _mod_call(c_pad, w_ada, b_ada)[:, :b]
    idx = lax.broadcasted_iota(I32, (QB, QB), 0)
    jdx = lax.broadcasted_iota(I32, (QB, QB), 1)
    tri = (jdx <= idx).astype(BF16)
    later = (jdx > idx).astype(BF16)
    later2 = jnp.concatenate([later, later], axis=1)
    tm_moe = min(TM_MOE, l)
    before = (lax.broadcasted_iota(I32, (tm_moe, tm_moe), 1)
              < lax.broadcasted_iota(I32, (tm_moe, tm_moe), 0)).astype(BF16)
    for layer in range(depth):
        sh1, sc1, gt1, sh2, sc2, gt2 = [m.reshape(b, 1, d) for m in jnp.split(mod[layer], 6, axis=-1)]
        gains = norm_gains[layer].reshape(4, 1, d)
        wrow, wt, wuk, wuvt, wpool = _layer_weights(w_in[layer], w_uk[layer], w_uv[layer], w_pool[layer])
        (qat, qit, wit, ka, vat, ki, qct, kc, vct, ga, gc, mb) = _inproj_call(
            x, sh1, sc1, gains[0], wrow, wt, cos_r, sin_r, cos_t, sin_t,
            g_kv_latent[layer].reshape(1, KV_LATENT), wuk, wuvt, wpool,
            pool_scale[layer].reshape(1, POOL_WIDTH), w_br_b[layer].astype(BF16))
        ya = _dsa_call(qit, wit, qat, ki, ka, vat, tri)
        yc = _stick_call(qct, kc, vct, later2)
        i = layer // 2
        merge_args = (x, ya, yc, mb, ga, gc, gt1, sh2, sc2, gains[1], gains[2],
                      w_br_a[layer].astype(BF16), w_br_c[layer].astype(BF16), w_out[layer].astype(BF16))
        if layer % 2 == 0:
            x1, h2 = _merge_call(*merge_args)
            x = _ffn_call(x1, h2, gt2, gains[3], w_gate_dense[i].astype(BF16), w_up_dense[i].astype(BF16),
                          w_down_dense[i].astype(BF16))
        else:
            x1, h2, route, counts = _merge_call(*merge_args, wr=_pad_cols(w_router[i], LANES))
            x = _moe_call(x1, h2, route, counts, before, gt2, gains[3], w_gate_moe[i].astype(BF16),
                          w_up_moe[i].astype(BF16), w_down_moe[i].astype(BF16))
    return x
```

```python
import functools

import jax
import jax.numpy as jnp
from jax import lax
from jax.experimental import pallas as pl
from jax.experimental.pallas import tpu as pltpu

F32 = jnp.float32
BF16 = jnp.bfloat16
I32 = jnp.int32
I16 = jnp.int16

D_MODEL = 1024
HEAD_DIM = 64
ROPE_DIM = HEAD_DIM // 4
NOPE_DIM = HEAD_DIM - ROPE_DIM
ROPE_THETA = 500000.0
N_HEADS_A = (3 * D_MODEL // 8) // HEAD_DIM
A_WIDTH = N_HEADS_A * HEAD_DIM
KV_LATENT = D_MODEL // 8
N_IDX_HEADS = 4
IDX_DIM = 64
TOPK_MAX = 256
N_POOL_GROUPS = 4
POOL_WINDOWS = (2, 4, 8, 16)
POOL_WIDTH = D_MODEL // 4
POOL_GROUP_DIM = POOL_WIDTH // N_POOL_GROUPS
N_HEADS_C = (D_MODEL // 4) // HEAD_DIM
C_WIDTH = N_HEADS_C * HEAD_DIM
N_BRANCHES = 3
IN_SIZES = (A_WIDTH, KV_LATENT, ROPE_DIM, N_IDX_HEADS * IDX_DIM, IDX_DIM, N_IDX_HEADS,
            POOL_WIDTH, C_WIDTH, C_WIDTH, C_WIDTH, N_BRANCHES * D_MODEL)
D_FF = 2816
N_EXPERTS = 8
TOP_K = 2
D_FF_EXPERT = D_FF // TOP_K
RMS_EPS = 1e-6

LANES = 128
QB = 128
KC = 512
QS_STICK = 512
QS_DSA = 256
KB_DSA = 512
VB_DSA = 256
V_ROWS = 80
STICK_DEAD = -106.0
TM_IN = 256
TM_MERGE = 256
TM_FFN = 512
TM_MOE = 1024
ROW_CHUNK = 128
POOL_HALO = 16
VMEM_LIMIT = 56 * 1024 * 1024
INT_MIN = -2147483648
LOG2_E = 1.4426950408889634
NEG_BIG = -1e30
NEG_MASK = -2e30

R_LAT, R_KR, R_KRR, R_KI, R_KIR, R_UP, R_KC, R_GATE = 0, 128, 256, 384, 512, 640, 896, 1152
R_WIDTH = R_GATE + N_BRANCHES * D_MODEL
T_QA, T_QAR, T_QI, T_QIR, T_QC, T_VC, T_WI = 0, 384, 768, 1024, 1280, 1792, 2048
T_WI_ROWS = 16
T_HEIGHT = T_WI + T_WI_ROWS


def _params(sem):
    return pltpu.CompilerParams(dimension_semantics=sem, vmem_limit_bytes=VMEM_LIMIT)


def _sigmoid(v):
    return 1.0 / (1.0 + jnp.exp(-v))


def _rms(v, gain):
    return v * lax.rsqrt(jnp.mean(v * v, axis=-1, keepdims=True) + RMS_EPS) * gain


def _mod_kernel(c_ref, w_ref, b_ref, o_ref):
    c = c_ref[...]
    cond = c * _sigmoid(c)
    o_ref[0] = jnp.dot(cond, w_ref[0], preferred_element_type=F32) + b_ref[0]


def _mod_call(c_pad, w_ada, b_ada):
    depth, d, n = w_ada.shape
    tn = 1024
    return pl.pallas_call(
        _mod_kernel,
        grid=(depth, n // tn),
        in_specs=[
            pl.BlockSpec((8, d), lambda l, j: (0, 0)),
            pl.BlockSpec((1, d, tn), lambda l, j: (l, 0, j)),
            pl.BlockSpec((1, 1, tn), lambda l, j: (l, 0, j)),
        ],
        out_specs=pl.BlockSpec((1, 8, tn), lambda l, j: (l, 0, j)),
        out_shape=jax.ShapeDtypeStruct((depth, 8, n), F32),
        compiler_params=_params(("arbitrary", "arbitrary")),
        name="adaln_mod",
    )(c_pad, w_ada, b_ada.reshape(depth, 1, n))


def _inproj_kernel(x_ref, sh_ref, sc_ref, g_ref, wrow_ref, wt_ref, c_ref, s_ref, ct_ref, st_ref,
                   glat_ref, wuk_ref, wuvt_ref, wpool_ref, pscale_ref, wbrb_ref,
                   qat_ref, qit_ref, wit_ref, ka_ref, vat_ref, ki_ref, qct_ref, kc_ref, vct_ref,
                   ga_ref, gc_ref, mb_ref,
                   h_scr, ht_scr, ext_scr, prev_scr):
    i = pl.program_id(1)
    tm = x_ref.shape[1]
    x = x_ref[0]
    h = _rms(x, g_ref[...]) * (1.0 + sc_ref[0]) + sh_ref[0]
    h_scr[...] = h.astype(BF16)
    ht_scr[...] = h.T.astype(BF16)

    def rowdot(a, width):
        return jnp.dot(h_scr[...], wrow_ref[:, a:a + width], preferred_element_type=F32)

    def tdot(a, height):
        return jnp.dot(wt_ref[a:a + height, :], ht_scr[...], preferred_element_type=F32)

    cos_r, sin_r = c_ref[0], s_ref[0]
    cos_t, sin_t = ct_ref[0], st_ref[0]

    latn = _rms(rowdot(R_LAT, KV_LATENT), glat_ref[...])
    ka = (rowdot(R_KR, LANES) * cos_r + rowdot(R_KRR, LANES) * sin_r
          + jnp.dot(latn.astype(BF16), wuk_ref[...], preferred_element_type=F32))
    ka_ref[0] = ka[:, :HEAD_DIM].astype(BF16)
    vat = jnp.dot(wuvt_ref[...], latn.T.astype(BF16), preferred_element_type=F32)
    pad_row = lax.broadcasted_iota(I32, (V_ROWS - HEAD_DIM, tm), 0)
    vat = jnp.concatenate([vat, jnp.where(pad_row == 0, 1.0, 0.0)], axis=0)
    for j in range(tm // VB_DSA):
        vat_ref[0, j] = vat[:, j * VB_DSA:(j + 1) * VB_DSA].astype(BF16)

    ki =rowdot(R_KI, LANES) * cos_r + rowdot(R_KIR, LANES) * sin_r
    ki_ref[0] = ki[:, :IDX_DIM].astype(BF16)

    qa, qar = tdot(T_QA, A_WIDTH), tdot(T_QAR, A_WIDTH)
    for hh in range(N_HEADS_A):
        r = slice(hh * HEAD_DIM, (hh + 1) * HEAD_DIM)
        qat_ref[0, r, :] = (qa[r] * cos_t + qar[r] * sin_t).astype(BF16)
    qi, qir = tdot(T_QI, N_IDX_HEADS * IDX_DIM), tdot(T_QIR, N_IDX_HEADS * IDX_DIM)
    for hh in range(N_IDX_HEADS):
        r = slice(hh * IDX_DIM, (hh + 1) * IDX_DIM)
        qit_ref[0, r, :] = (qi[r] * cos_t + qir[r] * sin_t).astype(BF16)
    wit_ref[0] = tdot(T_WI, T_WI_ROWS)

    qct_ref[0] = tdot(T_QC, 2 * C_WIDTH).astype(BF16)
    kc_ref[0] = rowdot(R_KC, C_WIDTH).astype(BF16)
    vct = tdot(T_VC, C_WIDTH)
    for j in range(tm // QB):
        vct_ref[0, j] = vct[:, j * QB:(j + 1) * QB].astype(BF16)

    up = rowdot(R_UP, POOL_WIDTH)

    @pl.when(i == 0)
    def _():
        prev_scr[...] = jnp.zeros_like(prev_scr)

    ext_scr[0:POOL_HALO, :] = prev_scr[...]
    ext_scr[POOL_HALO:POOL_HALO + tm, :] = up
    prev_scr[...] = up[tm - POOL_HALO:, :]
    lag = [ext_scr[POOL_HALO - j:POOL_HALO - j + tm, :] for j in range(POOL_HALO)]
    sums = {}
    run = lag[0]
    for j in range(1, POOL_HALO):
        run = run + lag[j]
        if j + 1 in POOL_WINDOWS:
            sums[j + 1] = run
    lane = lax.broadcasted_iota(I32, (tm, POOL_WIDTH), 1)
    pos = i * tm + lax.broadcasted_iota(I32, (tm, POOL_WIDTH), 0)
    pooled_sum = sums[POOL_WINDOWS[-1]]
    win = jnp.full((tm, POOL_WIDTH), POOL_WINDOWS[-1], I32)
    for g in range(N_POOL_GROUPS - 2, -1, -1):
        in_group = lane < (g + 1) * POOL_GROUP_DIM
        pooled_sum = jnp.where(in_group, sums[POOL_WINDOWS[g]], pooled_sum)
        win = jnp.where(in_group, POOL_WINDOWS[g], win)
    cnt = jnp.minimum(pos + 1, win).astype(F32)
    pooled = pooled_sum / cnt - up
    yb = jnp.dot(pooled.astype(BF16), wpool_ref[...], preferred_element_type=F32) * pscale_ref[...]

    ga_ref[0] = _sigmoid(rowdot(R_GATE, D_MODEL)).astype(BF16)
    gb = _sigmoid(rowdot(R_GATE + D_MODEL, D_MODEL))
    mb_ref[0] = (gb * jnp.dot(yb.astype(BF16), wbrb_ref[...], preferred_element_type=F32)).astype(BF16)
    gc_ref[0] = _sigmoid(rowdot(R_GATE + 2 * D_MODEL, D_MODEL)).astype(BF16)


def _inproj_call(x, sh, sc, gain, wrow, wt, cos_r, sin_r, cos_t, sin_t, glat, wuk, wuvt, wpool, pscale, wbrb):
    b, l, d = x.shape
    tm = min(TM_IN, l)
    nq = l // QB
    tok = lambda w: pl.BlockSpec((1, tm, w), lambda bi, i: (bi, i, 0))
    feat = lambda hgt: pl.BlockSpec((1, hgt, tm), lambda bi, i: (bi, 0, i))
    blk = lambda hgt, w: pl.BlockSpec((1, tm // w, hgt, w), lambda bi, i: (bi, i, 0, 0))
    full = lambda a: pl.BlockSpec(a.shape, lambda bi, i: (0,) * a.ndim)
    vec = pl.BlockSpec((1, 1, d), lambda bi, i: (bi, 0, 0))
    out_shape = (
        jax.ShapeDtypeStruct((b, A_WIDTH, l), BF16),
        jax.ShapeDtypeStruct((b, N_IDX_HEADS * IDX_DIM, l), BF16),
        jax.ShapeDtypeStruct((b, T_WI_ROWS, l), F32),
        jax.ShapeDtypeStruct((b, l, HEAD_DIM), BF16),
        jax.ShapeDtypeStruct((b, l // VB_DSA, V_ROWS, VB_DSA), BF16),
        jax.ShapeDtypeStruct((b, l, IDX_DIM), BF16),
        jax.ShapeDtypeStruct((b, 2 * C_WIDTH, l), BF16),
        jax.ShapeDtypeStruct((b, l, C_WIDTH), BF16),
        jax.ShapeDtypeStruct((b, nq, C_WIDTH, QB), BF16),
        jax.ShapeDtypeStruct((b, l, d), BF16),
        jax.ShapeDtypeStruct((b, l, d), BF16),
        jax.ShapeDtypeStruct((b, l, d), BF16),
    )
    out_specs = (feat(A_WIDTH), feat(N_IDX_HEADS * IDX_DIM), feat(T_WI_ROWS), tok(HEAD_DIM), blk(V_ROWS, VB_DSA),
                 tok(IDX_DIM), feat(2 * C_WIDTH), tok(C_WIDTH), blk(C_WIDTH, QB), tok(d), tok(d), tok(d))
    in_specs = [tok(d), vec, vec, full(gain), full(wrow), full(wt), tok(LANES), tok(LANES),
                feat(HEAD_DIM), feat(HEAD_DIM), full(glat), full(wuk), full(wuvt), full(wpool),
                full(pscale), full(wbrb)]
    return pl.pallas_call(
        _inproj_kernel,
        grid=(b, l // tm),
        in_specs=in_specs,
        out_specs=out_specs,
        out_shape=out_shape,
        scratch_shapes=[pltpu.VMEM((tm, d), BF16), pltpu.VMEM((d, tm), BF16),
                        pltpu.VMEM((tm + POOL_HALO, POOL_WIDTH), F32), pltpu.VMEM((POOL_HALO, POOL_WIDTH), F32)],
        compiler_params=_params(("arbitrary", "arbitrary")),
        name="inproj",
    )(x, sh, sc, gain, wrow, wt, cos_r, sin_r, cos_t, sin_t, glat, wuk, wuvt, wpool, pscale, wbrb)


def _dsa_kernel(qit_ref, wit_ref, qat_ref, ki_ref, ka_ref, vat_ref, tri_ref, ya_ref,
                keys_scr, top_scr, acc_scr, *, k_sel):
    i = pl.program_id(1)
    qs = qat_ref.shape[2]
    n_blocks = (i + 1) * (qs // QB)
    n_chunks = (n_blocks * QB + KC - 1) // KC
    qpos = i * qs + lax.broadcasted_iota(I32, (1, qs), 1)
    w_idx = wit_ref[0]

    def score_chunk(c, carry, masked):
        r0 = pl.multiple_of(c * KC, KC)
        kblk = ki_ref[0, pl.ds(r0, KC), :]
        parts = [jnp.dot(kblk, qit_ref[0, hh * IDX_DIM:(hh + 1) * IDX_DIM, :], preferred_element_type=F32)
                 for hh in range(N_IDX_HEADS)]
        score = jnp.maximum(parts[0], 0.0) * w_idx[0:1, :]
        for hh in range(1, N_IDX_HEADS):
            score = score + jnp.maximum(parts[hh], 0.0) * w_idx[hh:hh + 1, :]
        bits = lax.bitcast_convert_type(score, I32)
        key = jnp.where(bits < 0, INT_MIN - bits, bits)
        if masked:
            kpos = r0 + lax.broadcasted_iota(I32, (KC, qs), 0)
            key = jnp.where(kpos <= qpos, key, INT_MIN)
        keys_scr[pl.ds(r0, KC), :] = key
        top = lax.shift_right_arithmetic(key, 16).astype(I16)
        top_scr[pl.ds(r0, KC), :] = top
        groups = top.reshape(KC // 16, 16, qs)
        for j in range(KC // 16):
            carry = jnp.where(groups[j] > carry, groups[j], carry)
        return carry

    n_open = (i * qs) // KC
    top_max = lax.fori_loop(0, n_open, functools.partial(score_chunk, masked=False),
                            jnp.full((16, qs), -2 ** 15, I16))
    top_max = lax.fori_loop(n_open, n_chunks, functools.partial(score_chunk, masked=True), top_max)
    top_max = jnp.max(top_max.astype(I32), axis=0, keepdims=True)

    def count_rows(src, rows_per_vreg, trial):
        n_acc = 4
        groups = KC // rows_per_vreg

        def body(c, accs):
            r0 = pl.multiple_of(c * KC, KC)
            rows = src[pl.ds(r0, KC), :].reshape(groups, rows_per_vreg, qs)
            accs = list(accs)
            for j in range(groups):
                a = accs[j % n_acc]
                accs[j % n_acc] = jnp.where(rows[j] >= trial, a + 1, a)
            return tuple(accs)

        zero = jnp.zeros((rows_per_vreg, qs), src.dtype)
        accs = lax.fori_loop(0, n_chunks, body, tuple(zero for _ in range(n_acc)))
        total = (accs[0] + accs[1]) + (accs[2] + accs[3])
        return jnp.sum(total.astype(I32), axis=0, keepdims=True)

    def count_ge(trial):
        return count_rows(keys_scr, 8, trial)

    def count_ge_top(trial):
        return count_rows(top_scr, 16, lax.shift_right_arithmetic(trial, 16).astype(I16))

    c_zero = count_ge_top(jnp.zeros((1, qs), I32))
    c_pos = count_ge(jnp.ones((1, qs), I32))
    tie_at_zero = jnp.logical_and(c_zero >= k_sel, c_pos < k_sel)

    def unsettled(c_ans):
        settled = jnp.logical_or(tie_at_zero, c_ans == k_sel)
        return jnp.max(jnp.where(settled, 0, 1))

    def search(counter, lowest_bit, group, state):
        def refine(carry):
            bit, ans, c_ans, _ = carry
            for g in range(group):
                trial = ans + lax.shift_left(jnp.int32(1), bit - g)
                c = counter(trial)
                ok = c >= k_sel
                c_ans = jnp.where(ok, c, c_ans)
                ans = jnp.where(ok, trial, ans)
            return bit - group, ans, c_ans, unsettled(c_ans)

        return lax.while_loop(lambda cr: jnp.logical_and(cr[0] >= lowest_bit, cr[3] > 0), refine, state)

    def count_within(gap):
        return count_rows(top_scr, 16, jnp.maximum(top_max - gap, -2 ** 15).astype(I16))

    near_bits = 9
    c_near = count_within(2 ** near_bits - 1)
    short = qpos + 1 < k_sel
    known = jnp.logical_or(tie_at_zero, short)
    all_near = jnp.min(jnp.where(jnp.logical_or(c_near >= k_sel, known), 1, 0)) > 0
    first_bit = jnp.where(all_near, near_bits - 1, 15)

    def widen(group, carry):
        bit, gap, c_gap = carry
        for g in range(group):
            trial = gap + lax.shift_left(jnp.int32(1), bit - g) - 1
            c = count_within(trial)
            ok = c >= k_sel
            gap = jnp.where(ok, gap, trial + 1)
            c_gap = jnp.where(ok, c, c_gap)
        return bit - group, gap, c_gap

    c_far = jnp.where(all_near, c_near, jnp.int32(2 ** 30))
    state = (first_bit, jnp.zeros((1, qs), I32), c_far)
    state = lax.while_loop(lambda cr: cr[0] >= near_bits, functools.partial(widen, 16 - near_bits), state)
    _, gap, c_top = lax.while_loop(lambda cr: cr[0] >= 0, functools.partial(widen, 3), state)
    bucket = jnp.where(short, -2 ** 15, jnp.where(tie_at_zero, 0, jnp.maximum(top_max - gap, -2 ** 15)))
    c_top = jnp.where(tie_at_zero, c_zero, c_top)
    top = lax.shift_left(bucket, 16)
    alive = unsettled(c_top)
    last_bucket = bucket >= 2 ** 15 - 1
    c_over = jnp.where(last_bucket, 0,
                       count_rows(top_scr, 16, jnp.where(last_bucket, bucket, bucket + 1).astype(I16)))

    def pack_low(c, carry):
        r0 = pl.multiple_of(c * KC, KC)
        low = (keys_scr[pl.ds(r0, KC), :] ^ 0x8000).astype(I16)
        top_scr[pl.ds(r0, KC), :] = jnp.where(top_scr[pl.ds(r0, KC), :] == bucket.astype(I16), low, -2 ** 15)
        return carry

    lax.fori_loop(0, n_chunks, pack_low, 0)

    def count_ge_low(low_trial):
        return c_over + count_rows(top_scr, 16, (low_trial - 2 ** 15).astype(I16))

    zero_row = jnp.zeros((1, qs), I32)
    _, low, _, _ = search(count_ge_low, 0, 4, (jnp.int32(15), zero_row, c_top, alive))
    thr = top + low
    full_low = low >= 2 ** 16 - 1
    n_above = jnp.where(full_low, c_over, count_ge_low(jnp.where(full_low, low, low + 1)))
    n_ties = jnp.where(thr == INT_MIN, 0, k_sel - n_above).astype(F32)

    acc_scr[...] = jnp.zeros_like(acc_scr)
    heads = range(N_HEADS_A)

    kb_rows = KB_DSA
    vb_rows = vat_ref.shape[3]
    v_per_k = kb_rows // vb_rows

    n_steps = ((i + 1) * qs + kb_rows - 1) // kb_rows

    v_rows = vat_ref.shape[2]

    def attend(kb, carry):
        seen, ms = carry
        r0 = pl.multiple_of(kb * kb_rows, kb_rows)
        keyb = keys_scr[pl.ds(r0, kb_rows), :]
        tied = keyb == thr
        tied_b = jnp.where(tied, 1.0, 0.0).astype(BF16)
        ranks = []
        for j in range(kb_rows // QB):
            within = jnp.dot(tri_ref[...], tied_b[j * QB:(j + 1) * QB, :], preferred_element_type=F32)
            ranks.append(seen + within)
            seen = seen + within[QB - 1:QB, :]
        rank = jnp.concatenate(ranks, axis=0)
        keep = jnp.logical_or(keyb > thr, jnp.logical_and(tied, rank <= n_ties))
        bias = jnp.where(keep, 0.0, NEG_MASK).astype(BF16)
        kblk = ka_ref[0, pl.ds(r0, kb_rows), :]
        logits = [jnp.dot(kblk, qat_ref[0, hh * HEAD_DIM:(hh + 1) * HEAD_DIM, :],
                          preferred_element_type=F32).astype(BF16) + bias for hh in heads]
        new_ms = [jnp.maximum(ms[hh], jnp.max(logits[hh], axis=0, keepdims=True).astype(F32)) for hh in heads]
        probs = [jnp.exp2(logits[hh] - new_ms[hh].astype(BF16)) for hh in heads]
        alphas = [jnp.exp2(ms[hh] - new_ms[hh]) for hh in heads]
        outs = []
        for hh in heads:
            out = jnp.dot(vat_ref[0, kb * v_per_k], probs[hh][:vb_rows, :], preferred_element_type=F32)
            for j in range(1, v_per_k):
                out = out + jnp.dot(vat_ref[0, kb * v_per_k + j], probs[hh][j * vb_rows:(j + 1) * vb_rows, :],
                                    preferred_element_type=F32)
            outs.append(out)
        for hh in heads:
            r = slice(hh * v_rows, (hh + 1) * v_rows)
            acc_scr[r, :] = acc_scr[r, :] * alphas[hh] + outs[hh]
        return seen, tuple(new_ms)

    init = (jnp.zeros((1, qs), F32), tuple(jnp.full((1, qs), NEG_BIG, F32) for _ in heads))
    lax.fori_loop(0, n_steps, attend, init)
    normed = [acc_scr[hh * v_rows:hh * v_rows + HEAD_DIM, :] / acc_scr[hh * v_rows + HEAD_DIM:hh * v_rows + HEAD_DIM + 1, :]
              for hh in heads]
    ya_ref[0] = jnp.concatenate(normed, axis=0).T.astype(BF16)


def _dsa_call(qit, wit, qat, ki, ka, vat, tri):
    b, l, _ = ki.shape
    nq = l // QB
    qs = min(QS_DSA, l)
    k_sel = min(TOPK_MAX, l // 4)
    return pl.pallas_call(
        functools.partial(_dsa_kernel, k_sel=k_sel),
        grid=(b, l // qs),
        in_specs=[
            pl.BlockSpec((1, N_IDX_HEADS * IDX_DIM, qs), lambda bi, i: (bi, 0, i)),
            pl.BlockSpec((1, T_WI_ROWS, qs), lambda bi, i: (bi, 0, i)),
            pl.BlockSpec((1, A_WIDTH, qs), lambda bi, i: (bi, 0, i)),
            pl.BlockSpec((1, l, IDX_DIM), lambda bi, i: (bi, 0, 0), pipeline_mode=pl.Buffered(1)),
            pl.BlockSpec((1, l, HEAD_DIM), lambda bi, i: (bi, 0, 0), pipeline_mode=pl.Buffered(1)),
            pl.BlockSpec((1, l // VB_DSA, V_ROWS, VB_DSA), lambda bi, i: (bi, 0, 0, 0),
                         pipeline_mode=pl.Buffered(1)),
            pl.BlockSpec((QB, QB), lambda bi, i: (0, 0)),
        ],
        out_specs=pl.BlockSpec((1, qs, A_WIDTH), lambda bi, i: (bi, i, 0)),
        out_shape=jax.ShapeDtypeStruct((b, l, A_WIDTH), BF16),
        scratch_shapes=[pltpu.VMEM((l, qs), I32), pltpu.VMEM((l, qs), I16),
                        pltpu.VMEM((N_HEADS_A * V_ROWS, qs), F32)],
        compiler_params=_params(("arbitrary", "arbitrary")),
        name="dsa_attention",
    )(qit, wit, qat, ki, ka, vat, tri)


def _stick_kernel(qct_ref, kc_ref, vct_ref, later_ref, yc_ref, acc_scr):
    i = pl.program_id(1)
    qs = qct_ref.shape[2]
    diag_blocks = qs // QB
    qpos = i * qs + lax.broadcasted_iota(I32, (1, qs), 1)
    acc_scr[...] = jnp.zeros_like(acc_scr)

    def block(kb, tail, masked, lane0=0):
        r0 = pl.multiple_of(kb * QB, QB)
        kfull = kc_ref[0, pl.ds(r0, QB), :]
        vt = vct_ref[0, kb]
        heads = range(N_HEADS_C)
        old = [tail[hh][:, lane0:] for hh in heads]
        if masked:
            mask = (r0 + lax.broadcasted_iota(I32, (QB, qs - lane0), 0)) < qpos[:, lane0:]
        zs = [jnp.dot(kfull[:, (hh // 2) * LANES:(hh // 2 + 1) * LANES],
                      qct_ref[0, hh * LANES:(hh + 1) * LANES, lane0:], preferred_element_type=F32) for hh in heads]
        log_betas, splits, new_tail = [], [], []
        for hh in heads:
            z = zs[hh]
            log_beta = jnp.minimum(z, 0.0) - jnp.log(1.0 + jnp.exp(-jnp.abs(z)))
            log_keep = log_beta - z
            if masked:
                log_keep = jnp.where(mask, log_keep, 0.0)
            hi = log_keep.astype(BF16)
            lo = (log_keep - hi.astype(F32)).astype(BF16)
            log_betas.append(log_beta)
            splits.append(jnp.concatenate([hi, lo], axis=0))
            new = old[hh] + jnp.sum(log_keep, axis=0, keepdims=True)
            new_tail.append(new if lane0 == 0 else jnp.concatenate([tail[hh][:, :lane0], new], axis=1))
        withins = [jnp.dot(later_ref[...], splits[hh], preferred_element_type=F32) for hh in heads]
        weights = []
        for hh in heads:
            a = jnp.exp(log_betas[hh] + withins[hh] + old[hh])
            if masked:
                a = jnp.where(mask, a, 0.0)
            weights.append(a.astype(BF16))
        for hh in heads:
            r = slice(hh * HEAD_DIM, (hh + 1) * HEAD_DIM)
            acc_scr[r, lane0:] = acc_scr[r, lane0:] + jnp.dot(vt[r, :], weights[hh], preferred_element_type=F32)
        return tuple(new_tail)

    def largest(tail):
        worst = tail[0]
        for hh in range(1, N_HEADS_C):
            worst = jnp.maximum(worst, tail[hh])
        return jnp.max(worst)

    n_rest = i * diag_blocks
    tail = tuple(jnp.zeros((1, qs), F32) for _ in range(N_HEADS_C))
    for d in reversed(range(diag_blocks)):
        tail = block(n_rest + d, tail, True, lane0=d * QB)

    def live(cr):
        return jnp.logical_and(cr[0] < n_rest, cr[2] > STICK_DEAD)

    def step(cr):
        t, tl, _ = cr
        tl = block(n_rest - 1 - t, tl, False)
        return t + 1, tl, largest(tl)

    lax.while_loop(live, step, (jnp.int32(0), tail, largest(tail)))
    yc_ref[0] = acc_scr[...].T.astype(BF16)


def _stick_call(qct, kc, vct, later2):
    b, l, _ = kc.shape
    nq = l // QB
    qs = min(QS_STICK, l)
    return pl.pallas_call(
        _stick_kernel,
        grid=(b, l // qs),
        in_specs=[
            pl.BlockSpec((1, 2 * C_WIDTH, qs), lambda bi, i: (bi, 0, i)),
            pl.BlockSpec((1, l, C_WIDTH), lambda bi, i: (bi, 0, 0)),
            pl.BlockSpec((1, nq, C_WIDTH, QB), lambda bi, i: (bi, 0, 0, 0)),
            pl.BlockSpec((QB, 2 * QB), lambda bi, i: (0, 0)),
        ],
        out_specs=pl.BlockSpec((1, qs, C_WIDTH), lambda bi, i: (bi, i, 0)),
        out_shape=jax.ShapeDtypeStruct((b, l, C_WIDTH), BF16),
        scratch_shapes=[pltpu.VMEM((C_WIDTH, qs), F32)],
        compiler_params=_params(("arbitrary", "arbitrary")),
        name="stick_attention",
    )(qct, kc, vct, later2)


def _merge_kernel(x_ref, ya_ref, yc_ref, mb_ref, ga_ref, gc_ref, gt_ref, sh_ref, sc_ref, g1_ref, g2_ref,
                  wbra_ref, wbrc_ref, wout_ref, *rest, with_router):
    if with_router:
        wr_ref, x1_ref, h2_ref, route_ref, cnt_ref = rest
    else:
        x1_ref, h2_ref = rest
    merged = (ga_ref[0].astype(F32) * jnp.dot(ya_ref[0], wbra_ref[...], preferred_element_type=F32)
              + mb_ref[0].astype(F32)
              + gc_ref[0].astype(F32) * jnp.dot(yc_ref[0], wbrc_ref[...], preferred_element_type=F32))
    y = jnp.dot(merged.astype(BF16), wout_ref[...], preferred_element_type=F32)
    x1 = x_ref[0] + gt_ref[0] * _rms(y, g1_ref[...])
    x1_ref[0] = x1
    h2 = _rms(x1, g2_ref[...]) * (1.0 + sc_ref[0]) + sh_ref[0]
    h2_ref[0] = h2.astype(BF16)
    if with_router:
        logits = jnp.dot(h2, wr_ref[...], preferred_element_type=F32)
        lane = lax.broadcasted_iota(I32, logits.shape, 1)
        valid = lane < N_EXPERTS
        l1 = jnp.where(valid, logits, -jnp.inf)
        v1 = jnp.max(l1, axis=-1, keepdims=True)
        i1 = jnp.min(jnp.where(l1 == v1, lane, LANES), axis=-1, keepdims=True)
        l2 = jnp.where(lane == i1, -jnp.inf, l1)
        v2 = jnp.max(l2, axis=-1, keepdims=True)
        i2 = jnp.min(jnp.where(l2 == v2, lane, LANES), axis=-1, keepdims=True)
        e2 = jnp.exp(v2 - v1)
        p1 = 1.0 / (1.0 + e2)
        p2 = e2 / (1.0 + e2)
        route = jnp.where(lane == 0, i1.astype(F32), jnp.where(lane == 1, i2.astype(F32),
                          jnp.where(lane == 2, p1, jnp.where(lane == 3, p2, 0.0))))
        route_ref[0] = route
        chosen = jnp.where(jnp.logical_or(lane == i1, lane == i2), 1.0, 0.0)
        cnt_ref[0, 0] = jnp.broadcast_to(jnp.sum(chosen, axis=0, keepdims=True), (8, LANES))


def _merge_call(x, ya, yc, mb, ga, gc, gt, sh, sc, g1, g2, wbra, wbrc, wout, wr=None):
    b, l, d = x.shape
    tm = min(TM_MERGE if wr is None else TM_FFN, l)
    tok = lambda w: pl.BlockSpec((1, tm, w), lambda bi, i: (bi, i, 0))
    full = lambda a: pl.BlockSpec(a.shape, lambda bi, i: (0,) * a.ndim)
    vec = pl.BlockSpec((1, 1, d), lambda bi, i: (bi, 0, 0))
    in_specs = [tok(d), tok(A_WIDTH), tok(C_WIDTH), tok(d), tok(d), tok(d), vec, vec, vec,
                full(g1), full(g2), full(wbra), full(wbrc), full(wout)]
    args = [x, ya, yc, mb, ga, gc, gt, sh, sc, g1, g2, wbra, wbrc, wout]
    out_shape = [jax.ShapeDtypeStruct((b, l, d), F32), jax.ShapeDtypeStruct((b, l, d), BF16)]
    out_specs = [tok(d), tok(d)]
    if wr is not None:
        in_specs.append(full(wr))
        args.append(wr)
        out_shape += [jax.ShapeDtypeStruct((b, l, LANES), F32), jax.ShapeDtypeStruct((b, l // tm, 8, LANES), F32)]
        out_specs += [tok(LANES), pl.BlockSpec((1, 1, 8, LANES), lambda bi, i: (bi, i, 0, 0))]
    return pl.pallas_call(
        functools.partial(_merge_kernel, with_router=wr is not None),
        grid=(b, l // tm),
        in_specs=in_specs,
        out_specs=out_specs,
        out_shape=out_shape,
        compiler_params=_params(("arbitrary", "arbitrary")),
        name="merge_router" if wr is not None else "merge",
    )(*args)


def _ffn_kernel(x1_ref, h2_ref, gt_ref, g3_ref, wg_ref, wu_ref, wd_ref, o_ref):
    h2 = h2_ref[0]
    gate = jnp.dot(h2, wg_ref[...], preferred_element_type=F32)
    up = jnp.dot(h2, wu_ref[...], preferred_element_type=F32)
    act = (gate * _sigmoid(gate) * up).astype(BF16)
    y = jnp.dot(act, wd_ref[...], preferred_element_type=F32)
    o_ref[0] = x1_ref[0] + gt_ref[0] * _rms(y, g3_ref[...])


def _ffn_call(x1, h2, gt, g3, wg, wu, wd):
    b, l, d = x1.shape
    tm = min(TM_FFN, l)
    tok = pl.BlockSpec((1, tm, d), lambda bi, i: (bi, i, 0))
    full = lambda a: pl.BlockSpec(a.shape, lambda bi, i: (0,) * a.ndim)
    vec = pl.BlockSpec((1, 1, d), lambda bi, i: (bi, 0, 0))
    return pl.pallas_call(
        _ffn_kernel,
        grid=(b, l // tm),
        in_specs=[tok, tok, vec, full(g3), full(wg), full(wu), full(wd)],
        out_specs=tok,
        out_shape=jax.ShapeDtypeStruct((b, l, d), F32),
        compiler_params=_params(("arbitrary", "arbitrary")),
        name="ffn_dense",
    )(x1, h2, gt, g3, wg, wu, wd)


def _moe_kernel(nchunk_ref, first_ref, total_ref,
                x1_ref, h2_ref, route_ref, before_ref, gt_ref, g3_ref, wg_ref, wu_ref, wd_ref, o_ref,
                xs_scr, wrow_scr, dest_scr, acc_scr):
    tile = pl.program_id(0) * pl.num_programs(1) + pl.program_id(1)
    e = pl.program_id(2)
    n_e = pl.num_programs(2)
    tm = h2_ref.shape[1]
    total = total_ref[tile]

    @pl.when(e == 0)
    def _():
        route = route_ref[0]
        lane = lax.broadcasted_iota(I32, route.shape, 1).astype(F32)
        hot = [jnp.where(lane == route[:, s:s + 1], 1.0, 0.0) for s in range(TOP_K)]
        earlier = jnp.dot(before_ref[...], (hot[0] + hot[1]).astype(BF16), preferred_element_type=F32)
        start = jnp.zeros((1, LANES), F32)
        for ee in range(N_EXPERTS):
            start = jnp.where(lane[:1, :] == ee, (first_ref[tile * N_EXPERTS + ee] * ROW_CHUNK).astype(F32), start)
        where_to = earlier + start
        dest = [hot[s] * where_to for s in range(TOP_K)]
        prob = [hot[s] * route[:, TOP_K + s:TOP_K + s + 1] for s in range(TOP_K)]
        for s in range(TOP_K):
            dest_scr[s] = jnp.broadcast_to(jnp.sum(dest[s], axis=1, keepdims=True), (tm, LANES))
        dest_row = [jnp.sum(dest[s].T, axis=0, keepdims=True) for s in range(TOP_K)]
        prob_row = [jnp.sum(prob[s].T, axis=0, keepdims=True) for s in range(TOP_K)]
        h2 = h2_ref[0]

        def place(c, carry):
            r0 = pl.multiple_of(c * ROW_CHUNK, ROW_CHUNK)
            rows = (r0 + lax.broadcasted_iota(I32, (ROW_CHUNK, tm), 0)).astype(F32)
            here = [rows == dest_row[s] for s in range(TOP_K)]
            pick = jnp.where(jnp.logical_or(here[0], here[1]), 1.0, 0.0).astype(BF16)
            xs_scr[pl.ds(r0, ROW_CHUNK), :] = jnp.dot(pick, h2, preferred_element_type=F32).astype(BF16)
            weight = jnp.where(here[0], prob_row[0], 0.0) + jnp.where(here[1], prob_row[1], 0.0)
            wrow_scr[pl.ds(r0, ROW_CHUNK), :] = jnp.broadcast_to(jnp.sum(weight, axis=1, keepdims=True),
                                                                  (ROW_CHUNK, LANES))
            return carry

        lax.fori_loop(0, total + total % 2, place, 0)

    def expert_chunk(c, carry):
        r0 = pl.multiple_of((first_ref[tile * n_e + e] + c) * ROW_CHUNK, ROW_CHUNK)
        xs = xs_scr[pl.ds(r0, ROW_CHUNK), :]
        gate = jnp.dot(xs, wg_ref[0], preferred_element_type=F32)
        up = jnp.dot(xs, wu_ref[0], preferred_element_type=F32)
        act = (gate * _sigmoid(gate) * up).astype(BF16)
        y = jnp.dot(act, wd_ref[0], preferred_element_type=F32)
        xs_scr[pl.ds(r0, ROW_CHUNK), :] = (y * wrow_scr[pl.ds(r0, ROW_CHUNK), 0:1]).astype(BF16)
        return carry

    lax.fori_loop(0, nchunk_ref[tile * n_e + e], expert_chunk, 0)

    @pl.when(e == n_e - 1)
    def _():
        acc_scr[...] = jnp.zeros_like(acc_scr)

        def collect(c, carry):
            r0 = pl.multiple_of(c * 2 * ROW_CHUNK, 2 * ROW_CHUNK)
            halves = []
            for j in range(2):
                cols = (r0 + j * ROW_CHUNK + lax.broadcasted_iota(I32, (tm, ROW_CHUNK), 1)).astype(F32)
                mine = jnp.logical_or(cols == dest_scr[0], cols == dest_scr[1])
                halves.append(jnp.where(mine, 1.0, 0.0).astype(BF16))
            acc_scr[...] = acc_scr[...] + jnp.dot(jnp.concatenate(halves, axis=1),
                                                  xs_scr[pl.ds(r0, 2 * ROW_CHUNK), :], preferred_element_type=F32)
            return carry

        lax.fori_loop(0, (total + 1) // 2, collect, 0)
        o_ref[0] = x1_ref[0] + gt_ref[0] * _rms(acc_scr[...], g3_ref[...])


def _moe_call(x1, h2, route, counts, before, gt, g3, wg, wu, wd):
    b, l, d = x1.shape
    tm = before.shape[0]
    n_e, _, ff = wg.shape
    cnt = counts[:, :, 0, :n_e].astype(I32).reshape(b * (l // tm), -1, n_e).sum(axis=1)
    nchunk = (cnt + ROW_CHUNK - 1) // ROW_CHUNK
    first = jnp.cumsum(nchunk, axis=1) - nchunk
    total = jnp.sum(nchunk, axis=1)
    rows = TOP_K * tm + n_e * ROW_CHUNK
    tok = lambda w: pl.BlockSpec((1, tm, w), lambda bi, i, e, *_: (bi, i, 0))
    tok_once = lambda w: pl.BlockSpec((1, tm, w), lambda bi, i, e, *_: (bi, i, 0), pipeline_mode=pl.Buffered(1))
    vec = pl.BlockSpec((1, 1, d), lambda bi, i, e, *_: (bi, 0, 0))
    grid_spec = pltpu.PrefetchScalarGridSpec(
        num_scalar_prefetch=3,
        grid=(b, l // tm, n_e),
        in_specs=[tok_once(d), tok_once(d), tok(LANES),
                  pl.BlockSpec(before.shape, lambda bi, i, e, *_: (0, 0), pipeline_mode=pl.Buffered(1)), vec,
                  pl.BlockSpec(g3.shape, lambda bi, i, e, *_: (0, 0)),
                  pl.BlockSpec((1, d, ff), lambda bi, i, e, *_: (e, 0, 0)),
                  pl.BlockSpec((1, d, ff), lambda bi, i, e, *_: (e, 0, 0)),
                  pl.BlockSpec((1, ff, d), lambda bi, i, e, *_: (e, 0, 0))],
        out_specs=tok(d),
        scratch_shapes=[pltpu.VMEM((rows, d), BF16), pltpu.VMEM((rows, LANES), F32),
                        pltpu.VMEM((TOP_K, tm, LANES), F32), pltpu.VMEM((tm, d), F32)],
    )
    return pl.pallas_call(
        _moe_kernel,
        grid_spec=grid_spec,
        out_shape=jax.ShapeDtypeStruct((b, l, d), F32),
        compiler_params=_params(("arbitrary", "arbitrary", "arbitrary")),
        name="ffn_moe",
    )(nchunk.reshape(-1), first.reshape(-1), total, x1, h2, route, before, gt, g3, wg, wu, wd)


def _rot_cols(w, n_heads, head_dim):
    d = w.shape[0]
    w = w.reshape(d, n_heads, head_dim)
    half = ROPE_DIM // 2
    rot = jnp.concatenate([-w[..., half:ROPE_DIM], w[..., :half],
                           jnp.zeros((d, n_heads, head_dim - ROPE_DIM), w.dtype)], axis=-1)
    return rot.reshape(d, n_heads * head_dim)


def _pad_cols(w, width):
    return jnp.pad(w, ((0, 0), (0, width - w.shape[1])))


def _layer_weights(w_in, w_uk, w_uv, w_pool):
    d = w_in.shape[0]
    offs, o = [], 0
    for s in IN_SIZES:
        offs.append(o)
        o += s
    piece = lambda k: w_in[:, offs[k]:offs[k] + IN_SIZES[k]]
    w_qa, w_lat, w_kr, w_qi, w_ki, w_wi, w_up, w_qc, w_kc, w_vc, w_gate = [piece(k) for k in range(len(IN_SIZES))]
    attn_scale = HEAD_DIM ** -0.5
    idx_scale = IDX_DIM ** -0.5 * N_IDX_HEADS ** -0.5
    wrow = jnp.concatenate([
        w_lat,
        _pad_cols(w_kr, LANES), _pad_cols(_rot_cols(w_kr, 1, ROPE_DIM), LANES),
        _pad_cols(w_ki, LANES), _pad_cols(_rot_cols(w_ki, 1, IDX_DIM), LANES),
        w_up, w_kc, w_gate], axis=1).astype(BF16)
    w_qc_t = (w_qc * attn_scale).T.reshape(N_HEADS_C, HEAD_DIM, d)
    zeros = jnp.zeros_like(w_qc_t)
    even = (jnp.arange(N_HEADS_C) % 2 == 0)[:, None, None]
    w_qc_pad = jnp.concatenate([jnp.where(even, w_qc_t, zeros), jnp.where(even, zeros, w_qc_t)], axis=1)
    w_qc_pad = w_qc_pad.reshape(2 * C_WIDTH, d)
    softmax_scale = attn_scale * LOG2_E
    wt = jnp.concatenate([
        (w_qa * softmax_scale).T, (_rot_cols(w_qa, N_HEADS_A, HEAD_DIM) * softmax_scale).T,
        w_qi.T, _rot_cols(w_qi, N_IDX_HEADS, IDX_DIM).T,
        w_qc_pad, w_vc.T,
        jnp.pad((w_wi * idx_scale).T, ((0, T_WI_ROWS - N_IDX_HEADS), (0, 0)))], axis=0).astype(BF16)
    wuk = jnp.zeros((KV_LATENT, LANES), F32).at[:, ROPE_DIM:HEAD_DIM].set(w_uk).astype(BF16)
    wuvt = w_uv.T.astype(BF16)
    wpool = jnp.zeros((POOL_WIDTH, POOL_WIDTH), F32)
    for g in range(N_POOL_GROUPS):
        sl = slice(g * POOL_GROUP_DIM, (g + 1) * POOL_GROUP_DIM)
        wpool = wpool.at[sl, sl].set(w_pool[g])
    return wrow, wt, wuk, wuvt, wpool.astype(BF16)


def _rope_tables(positions):
    inv = ROPE_THETA ** (-jnp.arange(0, ROPE_DIM, 2, dtype=F32) / ROPE_DIM)
    ang = positions.astype(F32)[..., None] * inv
    cos, sin = jnp.cos(ang), jnp.sin(ang)
    b, l = positions.shape
    ones = jnp.ones((b, l, HEAD_DIM - ROPE_DIM), F32)
    cos_h = jnp.concatenate([cos, cos, ones], axis=-1)
    sin_h = jnp.concatenate([sin, sin, jnp.zeros_like(ones)], axis=-1)
    cos_r = jnp.concatenate([cos_h, cos_h], axis=-1)
    sin_r = jnp.concatenate([sin_h, sin_h], axis=-1)
    return cos_r, sin_r, cos_h.transpose(0, 2, 1), sin_h.transpose(0, 2, 1)


def kernel(x, c, positions, w_ada, b_ada, norm_gains, w_in, g_kv_latent, w_uk, w_uv, w_pool, pool_scale,
           w_br_a, w_br_b, w_br_c, w_out, w_gate_dense, w_up_dense, w_down_dense,
           w_router, w_gate_moe, w_up_moe, w_down_moe):
    b, l, d = x.shape
    depth = w_in.shape[0]
    assert d == D_MODEL and l % max(KC, TM_FFN, TM_MOE) == 0
    cos_r, sin_r, cos_t, sin_t = _rope_tables(positions)
    c_pad = jnp.pad(c, ((0, 8 - b), (0, 0)))
    mod = _mod_call(c_pad, w_ada, b_ada)[:, :b]
    idx = lax.broadcasted_iota(I32, (QB, QB), 0)
    jdx = lax.broadcasted_iota(I32, (QB, QB), 1)
    tri = (jdx <= idx).astype(BF16)
    later = (jdx > idx).astype(BF16)
    later2 = jnp.concatenate([later, later], axis=1)
    tm_moe = min(TM_MOE, l)
    before = (lax.broadcasted_iota(I32, (tm_moe, tm_moe), 1)
              < lax.broadcasted_iota(I32, (tm_moe, tm_moe), 0)).astype(BF16)
    for layer in range(depth):
        sh1, sc1, gt1, sh2, sc2, gt2 = [m.reshape(b, 1, d) for m in jnp.split(mod[layer], 6, axis=-1)]
        gains = norm_gains[layer].reshape(4, 1, d)
        wrow, wt, wuk, wuvt, wpool = _layer_weights(w_in[layer], w_uk[layer], w_uv[layer], w_pool[layer])
        (qat, qit, wit, ka, vat, ki, qct, kc, vct, ga, gc, mb) = _inproj_call(
            x, sh1, sc1, gains[0], wrow, wt, cos_r, sin_r, cos_t, sin_t,
            g_kv_latent[layer].reshape(1, KV_LATENT), wuk, wuvt, wpool,
            pool_scale[layer].reshape(1, POOL_WIDTH), w_br_b[layer].astype(BF16))
        ya = _dsa_call(qit, wit, qat, ki, ka, vat, tri)
        yc = _stick_call(qct, kc, vct, later2)
        i = layer // 2
        merge_args = (x, ya, yc, mb, ga, gc, gt1, sh2, sc2, gains[1], gains[2],
                      w_br_a[layer].astype(BF16), w_br_c[layer].astype(BF16), w_out[layer].astype(BF16))
        if layer % 2 == 0:
            x1, h2 = _merge_call(*merge_args)
            x = _ffn_call(x1, h2, gt2, gains[3], w_gate_dense[i].astype(BF16), w_up_dense[i].astype(BF16),
                          w_down_dense[i].astype(BF16))
        else:
            x1, h2, route, counts = _merge_call(*merge_args, wr=_pad_cols(w_router[i], LANES))
            x = _moe_call(x1, h2, route, counts, before, gt2, gains[3], w_gate_moe[i].astype(BF16),
                          w_up_moe[i].astype(BF16), w_down_moe[i].astype(BF16))
    return x
```

```python
import functools

import jax
import jax.numpy as jnp
from jax import lax
from jax.experimental import pallas as pl
from jax.experimental.pallas import tpu as pltpu

F32 = jnp.float32
BF16 = jnp.bfloat16
I32 = jnp.int32
I16 = jnp.int16

D_MODEL = 1024
HEAD_DIM = 64
ROPE_DIM = HEAD_DIM // 4
ROPE_THETA = 500000.0
N_HEADS_A = (3 * D_MODEL // 8) // HEAD_DIM
A_WIDTH = N_HEADS_A * HEAD_DIM
KV_LATENT = D_MODEL // 8
N_IDX_HEADS = 4
IDX_DIM = 64
TOPK_MAX = 256
N_POOL_GROUPS = 4
POOL_WINDOWS = (2, 4, 8, 16)
POOL_WIDTH = D_MODEL // 4
POOL_GROUP_DIM = POOL_WIDTH // N_POOL_GROUPS
N_HEADS_C = (D_MODEL // 4) // HEAD_DIM
C_WIDTH = N_HEADS_C * HEAD_DIM
N_BRANCHES = 3
IN_SIZES = (A_WIDTH, KV_LATENT, ROPE_DIM, N_IDX_HEADS * IDX_DIM, IDX_DIM, N_IDX_HEADS,
            POOL_WIDTH, C_WIDTH, C_WIDTH, C_WIDTH, N_BRANCHES * D_MODEL)
D_FF = 2816
N_EXPERTS = 8
TOP_K = 2
D_FF_EXPERT = D_FF // TOP_K
RMS_EPS = 1e-6

LANES = 128
QB = 128
KC = 512
QS_STICK = 512
QS_DSA = 256
KB_DSA = 512
VB_DSA = 256
V_ROWS = 80
STICK_DEAD = -106.0
TM_IN = 512
TM_MERGE = 256
TM_FFN = 512
TM_MOE = 1024
ROW_CHUNK = 128
POOL_HALO = 16
VMEM_LIMIT = 56 * 1024 * 1024
INT_MIN = -2147483648
LOG2_E = 1.4426950408889634
NEG_BIG = -1e30
NEG_MASK = -2e30

R_LAT, R_KR, R_KRR, R_KI, R_KIR, R_UP, R_KC, R_GATE = 0, 128, 256, 384, 512, 640, 896, 1152
T_QA, T_QAR, T_QI, T_QIR, T_QC, T_VC, T_WI = 0, 384, 768, 1024, 1280, 1792, 2048
T_WI_ROWS = 16


def _params(sem):
    return pltpu.CompilerParams(dimension_semantics=sem, vmem_limit_bytes=VMEM_LIMIT)


def _sigmoid(v):
    return 1.0 / (1.0 + jnp.exp(-v))


def _rms(v, gain):
    return v * lax.rsqrt(jnp.mean(v * v, axis=-1, keepdims=True) + RMS_EPS) * gain


def _mod_kernel(c_ref, w_ref, b_ref, o_ref):
    c = c_ref[...]
    cond = c * _sigmoid(c)
    o_ref[0] = jnp.dot(cond, w_ref[0], preferred_element_type=F32) + b_ref[0]


def _mod_call(c_pad, w_ada, b_ada):
    depth, d, n = w_ada.shape
    tn = 1024
    return pl.pallas_call(
        _mod_kernel,
        grid=(depth, n // tn),
        in_specs=[
            pl.BlockSpec((8, d), lambda l, j: (0, 0)),
            pl.BlockSpec((1, d, tn), lambda l, j: (l, 0, j)),
            pl.BlockSpec((1, 1, tn), lambda l, j: (l, 0, j)),
        ],
        out_specs=pl.BlockSpec((1, 8, tn), lambda l, j: (l, 0, j)),
        out_shape=jax.ShapeDtypeStruct((depth, 8, n), F32),
        compiler_params=_params(("arbitrary", "arbitrary")),
        name="adaln_mod",
    )(c_pad, w_ada, b_ada.reshape(depth, 1, n))


def _inproj_kernel(x_ref, sh_ref, sc_ref, g_ref, wrow_ref, wt_ref, c_ref, s_ref, ct_ref, st_ref,
                   glat_ref, wuk_ref, wuvt_ref, wpool_ref, pscale_ref, wbrb_ref,
                   qat_ref, qit_ref, wit_ref, ka_ref, vat_ref, ki_ref, qct_ref, kc_ref, vct_ref,
                   ga_ref, gc_ref, mb_ref,
                   h_scr, ht_scr, ext_scr, prev_scr):
    i = pl.program_id(1)
    tm = x_ref.shape[1]
    x = x_ref[0]
    h = _rms(x, g_ref[...]) * (1.0 + sc_ref[0]) + sh_ref[0]
    h_scr[...] = h.astype(BF16)
    ht_scr[...] = h.T.astype(BF16)

    def rowdot(a, width):
        return jnp.dot(h_scr[...], wrow_ref[:, a:a + width], preferred_element_type=F32)

    def tdot(a, height):
        return jnp.dot(wt_ref[a:a + height, :], ht_scr[...], preferred_element_type=F32)

    cos_r, sin_r = c_ref[0], s_ref[0]
    cos_t, sin_t = ct_ref[0], st_ref[0]

    latn = _rms(rowdot(R_LAT, KV_LATENT), glat_ref[...])
    ka = (rowdot(R_KR, LANES) * cos_r + rowdot(R_KRR, LANES) * sin_r
          + jnp.dot(latn.astype(BF16), wuk_ref[...], preferred_element_type=F32))
    ka_ref[0] = ka[:, :HEAD_DIM].astype(BF16)
    vat = jnp.dot(wuvt_ref[...], latn.T.astype(BF16), preferred_element_type=F32)
    pad_row = lax.broadcasted_iota(I32, (V_ROWS - HEAD_DIM, tm), 0)
    vat = jnp.concatenate([vat, jnp.where(pad_row == 0, 1.0, 0.0)], axis=0)
    for j in range(tm // VB_DSA):
        vat_ref[0, j] = vat[:, j * VB_DSA:(j + 1) * VB_DSA].astype(BF16)

    ki =rowdot(R_KI, LANES) * cos_r + rowdot(R_KIR, LANES) * sin_r
    ki_ref[0] = ki[:, :IDX_DIM].astype(BF16)

    qa, qar = tdot(T_QA, A_WIDTH), tdot(T_QAR, A_WIDTH)
    for hh in range(N_HEADS_A):
        r = slice(hh * HEAD_DIM, (hh + 1) * HEAD_DIM)
        qat_ref[0, r, :] = (qa[r] * cos_t + qar[r] * sin_t).astype(BF16)
    qi, qir = tdot(T_QI, N_IDX_HEADS * IDX_DIM), tdot(T_QIR, N_IDX_HEADS * IDX_DIM)
    for hh in range(N_IDX_HEADS):
        r = slice(hh * IDX_DIM, (hh + 1) * IDX_DIM)
        qit_ref[0, r, :] = (qi[r] * cos_t + qir[r] * sin_t).astype(BF16)
    wit_ref[0] = tdot(T_WI, T_WI_ROWS)

    qct_ref[0] = tdot(T_QC, 2 * C_WIDTH).astype(BF16)
    kc_ref[0] = rowdot(R_KC, C_WIDTH).astype(BF16)
    vct = tdot(T_VC, C_WIDTH)
    for j in range(tm // QB):
        vct_ref[0, j] = vct[:, j * QB:(j + 1) * QB].astype(BF16)

    up = rowdot(R_UP, POOL_WIDTH)

    @pl.when(i == 0)
    def _():
        prev_scr[...] = jnp.zeros_like(prev_scr)

    ext_scr[0:POOL_HALO, :] = prev_scr[...]
    ext_scr[POOL_HALO:POOL_HALO + tm, :] = up
    prev_scr[...] = up[tm - POOL_HALO:, :]
    lag = [ext_scr[POOL_HALO - j:POOL_HALO - j + tm, :] for j in range(POOL_HALO)]
    sums = {}
    run = lag[0]
    for j in range(1, POOL_HALO):
        run = run + lag[j]
        if j + 1 in POOL_WINDOWS:
            sums[j + 1] = run
    lane = lax.broadcasted_iota(I32, (tm, POOL_WIDTH), 1)
    pos = i * tm + lax.broadcasted_iota(I32, (tm, POOL_WIDTH), 0)
    pooled_sum = sums[POOL_WINDOWS[-1]]
    win = jnp.full((tm, POOL_WIDTH), POOL_WINDOWS[-1], I32)
    for g in range(N_POOL_GROUPS - 2, -1, -1):
        in_group = lane < (g + 1) * POOL_GROUP_DIM
        pooled_sum = jnp.where(in_group, sums[POOL_WINDOWS[g]], pooled_sum)
        win = jnp.where(in_group, POOL_WINDOWS[g], win)
    cnt = jnp.minimum(pos + 1, win).astype(F32)
    pooled = pooled_sum / cnt - up
    yb = jnp.dot(pooled.astype(BF16), wpool_ref[...], preferred_element_type=F32) * pscale_ref[...]

    ga_ref[0] = _sigmoid(rowdot(R_GATE, D_MODEL)).astype(BF16)
    gb = _sigmoid(rowdot(R_GATE + D_MODEL, D_MODEL))
    mb_ref[0] = (gb * jnp.dot(yb.astype(BF16), wbrb_ref[...], preferred_element_type=F32)).astype(BF16)
    gc_ref[0] = _sigmoid(rowdot(R_GATE + 2 * D_MODEL, D_MODEL)).astype(BF16)


def _inproj_call(x, sh, sc, gain, wrow, wt, cos_r, sin_r, cos_t, sin_t, glat, wuk, wuvt, wpool, pscale, wbrb):
    b, l, d = x.shape
    tm = min(TM_IN, l)
    nq = l // QB
    tok = lambda w: pl.BlockSpec((1, tm, w), lambda bi, i: (bi, i, 0))
    feat = lambda hgt: pl.BlockSpec((1, hgt, tm), lambda bi, i: (bi, 0, i))
    blk = lambda hgt, w: pl.BlockSpec((1, tm // w, hgt, w), lambda bi, i: (bi, i, 0, 0))
    full = lambda a: pl.BlockSpec(a.shape, lambda bi, i: (0,) * a.ndim)
    vec = pl.BlockSpec((1, 1, d), lambda bi, i: (bi, 0, 0))
    out_shape = (
        jax.ShapeDtypeStruct((b, A_WIDTH, l), BF16),
        jax.ShapeDtypeStruct((b, N_IDX_HEADS * IDX_DIM, l), BF16),
        jax.ShapeDtypeStruct((b, T_WI_ROWS, l), F32),
        jax.ShapeDtypeStruct((b, l, HEAD_DIM), BF16),
        jax.ShapeDtypeStruct((b, l // VB_DSA, V_ROWS, VB_DSA), BF16),
        jax.ShapeDtypeStruct((b, l, IDX_DIM), BF16),
        jax.ShapeDtypeStruct((b, 2 * C_WIDTH, l), BF16),
        jax.ShapeDtypeStruct((b, l, C_WIDTH), BF16),
        jax.ShapeDtypeStruct((b, nq, C_WIDTH, QB), BF16),
        jax.ShapeDtypeStruct((b, l, d), BF16),
        jax.ShapeDtypeStruct((b, l, d), BF16),
        jax.ShapeDtypeStruct((b, l, d), BF16),
    )
    out_specs = (feat(A_WIDTH), feat(N_IDX_HEADS * IDX_DIM), feat(T_WI_ROWS), tok(HEAD_DIM), blk(V_ROWS, VB_DSA),
                 tok(IDX_DIM), feat(2 * C_WIDTH), tok(C_WIDTH), blk(C_WIDTH, QB), tok(d), tok(d), tok(d))
    resident = lambda a: pl.BlockSpec(a.shape, lambda bi, i: (0,) * a.ndim, pipeline_mode=pl.Buffered(1))
    in_specs = [tok(d), vec, vec, full(gain), resident(wrow), resident(wt), tok(LANES), tok(LANES),
                feat(HEAD_DIM), feat(HEAD_DIM), full(glat), full(wuk), full(wuvt), full(wpool),
                full(pscale), full(wbrb)]
    return pl.pallas_call(
        _inproj_kernel,
        grid=(b, l // tm),
        in_specs=in_specs,
        out_specs=out_specs,
        out_shape=out_shape,
        scratch_shapes=[pltpu.VMEM((tm, d), BF16), pltpu.VMEM((d, tm), BF16),
                        pltpu.VMEM((tm + POOL_HALO, POOL_WIDTH), F32), pltpu.VMEM((POOL_HALO, POOL_WIDTH), F32)],
        compiler_params=_params(("arbitrary", "arbitrary")),
        name="inproj",
    )(x, sh, sc, gain, wrow, wt, cos_r, sin_r, cos_t, sin_t, glat, wuk, wuvt, wpool, pscale, wbrb)


def _dsa_kernel(qit_ref, wit_ref, qat_ref, ki_ref, ka_ref, vat_ref, tri_ref, ya_ref,
                keys_scr, top_scr, acc_scr, *, k_sel):
    i = pl.program_id(1)
    qs = qat_ref.shape[2]
    n_chunks = ((i + 1) * qs + KC - 1) // KC
    qpos = i * qs + lax.broadcasted_iota(I32, (1, qs), 1)
    w_idx = wit_ref[0]

    def score_chunk(c, carry, masked):
        r0 = pl.multiple_of(c * KC, KC)
        kblk = ki_ref[0, pl.ds(r0, KC), :]
        parts = [jnp.dot(kblk, qit_ref[0, hh * IDX_DIM:(hh + 1) * IDX_DIM, :], preferred_element_type=F32)
                 for hh in range(N_IDX_HEADS)]
        score = jnp.maximum(parts[0], 0.0) * w_idx[0:1, :]
        for hh in range(1, N_IDX_HEADS):
            score = score + jnp.maximum(parts[hh], 0.0) * w_idx[hh:hh + 1, :]
        bits = lax.bitcast_convert_type(score, I32)
        key = jnp.where(bits < 0, INT_MIN - bits, bits)
        if masked:
            kpos = r0 + lax.broadcasted_iota(I32, (KC, qs), 0)
            key = jnp.where(kpos <= qpos, key, INT_MIN)
        keys_scr[pl.ds(r0, KC), :] = key
        top = lax.shift_right_arithmetic(key, 16).astype(I16)
        top_scr[pl.ds(r0, KC), :] = top
        groups = top.reshape(KC // 16, 16, qs)
        for j in range(KC // 16):
            carry = jnp.where(groups[j] > carry, groups[j], carry)
        return carry

    n_open = (i * qs) // KC
    top_max = lax.fori_loop(0, n_open, functools.partial(score_chunk, masked=False),
                            jnp.full((16, qs), -2 ** 15, I16))
    top_max = lax.fori_loop(n_open, n_chunks, functools.partial(score_chunk, masked=True), top_max)
    top_max = jnp.max(top_max.astype(I32), axis=0, keepdims=True)

    def count_rows(src, rows_per_vreg, trial):
        n_acc = 4
        groups = KC // rows_per_vreg

        def body(c, accs):
            r0 = pl.multiple_of(c * KC, KC)
            rows = src[pl.ds(r0, KC), :].reshape(groups, rows_per_vreg, qs)
            accs = list(accs)
            for j in range(groups):
                a = accs[j % n_acc]
                accs[j % n_acc] = jnp.where(rows[j] >= trial, a + 1, a)
            return tuple(accs)

        zero = jnp.zeros((rows_per_vreg, qs), src.dtype)
        accs = lax.fori_loop(0, n_chunks, body, tuple(zero for _ in range(n_acc)))
        total = (accs[0] + accs[1]) + (accs[2] + accs[3])
        return jnp.sum(total.astype(I32), axis=0, keepdims=True)

    def count_ge(trial):
        return count_rows(keys_scr, 8, trial)

    def count_ge_top(trial):
        return count_rows(top_scr, 16, lax.shift_right_arithmetic(trial, 16).astype(I16))

    c_zero = count_ge_top(jnp.zeros((1, qs), I32))
    c_pos = count_ge(jnp.ones((1, qs), I32))
    tie_at_zero = jnp.logical_and(c_zero >= k_sel, c_pos < k_sel)

    def unsettled(c_ans):
        settled = jnp.logical_or(tie_at_zero, c_ans == k_sel)
        return jnp.max(jnp.where(settled, 0, 1))

    def search(counter, lowest_bit, group, state):
        def refine(carry):
            bit, ans, c_ans, _ = carry
            for g in range(group):
                trial = ans + lax.shift_left(jnp.int32(1), bit - g)
                c = counter(trial)
                ok = c >= k_sel
                c_ans = jnp.where(ok, c, c_ans)
                ans = jnp.where(ok, trial, ans)
            return bit - group, ans, c_ans, unsettled(c_ans)

        return lax.while_loop(lambda cr: jnp.logical_and(cr[0] >= lowest_bit, cr[3] > 0), refine, state)

    def count_within(gap):
        return count_rows(top_scr, 16, jnp.maximum(top_max - gap, -2 ** 15).astype(I16))

    near_bits = 9
    c_near = count_within(2 ** near_bits - 1)
    short = qpos + 1 < k_sel
    known = jnp.logical_or(tie_at_zero, short)
    all_near = jnp.min(jnp.where(jnp.logical_or(c_near >= k_sel, known), 1, 0)) > 0
    first_bit = jnp.where(all_near, near_bits - 1, 15)

    def widen(group, carry):
        bit, gap, c_gap = carry
        for g in range(group):
            trial = gap + lax.shift_left(jnp.int32(1), bit - g) - 1
            c = count_within(trial)
            ok = c >= k_sel
            gap = jnp.where(ok, gap, trial + 1)
            c_gap = jnp.where(ok, c, c_gap)
        return bit - group, gap, c_gap

    c_far = jnp.where(all_near, c_near, jnp.int32(2 ** 30))
    state = (first_bit, jnp.zeros((1, qs), I32), c_far)
    state = lax.while_loop(lambda cr: cr[0] >= near_bits, functools.partial(widen, 16 - near_bits), state)
    _, gap, c_top = lax.while_loop(lambda cr: cr[0] >= 0, functools.partial(widen, 3), state)
    bucket = jnp.where(short, -2 ** 15, jnp.where(tie_at_zero, 0, jnp.maximum(top_max - gap, -2 ** 15)))
    c_top = jnp.where(tie_at_zero, c_zero, c_top)
    top = lax.shift_left(bucket, 16)
    alive = unsettled(c_top)
    last_bucket = bucket >= 2 ** 15 - 1
    c_over = jnp.where(last_bucket, 0,
                       count_rows(top_scr, 16, jnp.where(last_bucket, bucket, bucket + 1).astype(I16)))

    def pack_low(c, carry):
        r0 = pl.multiple_of(c * KC, KC)
        low = (keys_scr[pl.ds(r0, KC), :] ^ 0x8000).astype(I16)
        top_scr[pl.ds(r0, KC), :] = jnp.where(top_scr[pl.ds(r0, KC), :] == bucket.astype(I16), low, -2 ** 15)
        return carry

    lax.fori_loop(0, n_chunks, pack_low, 0)

    def count_ge_low(low_trial):
        return c_over + count_rows(top_scr, 16, (low_trial - 2 ** 15).astype(I16))

    zero_row = jnp.zeros((1, qs), I32)
    _, low, _, _ = search(count_ge_low, 0, 4, (jnp.int32(15), zero_row, c_top, alive))
    thr = top + low
    full_low = low >= 2 ** 16 - 1
    n_above = jnp.where(full_low, c_over, count_ge_low(jnp.where(full_low, low, low + 1)))
    n_ties = jnp.where(thr == INT_MIN, 0, k_sel - n_above).astype(F32)

    acc_scr[...] = jnp.zeros_like(acc_scr)
    heads = range(N_HEADS_A)

    kb_rows = KB_DSA
    vb_rows = vat_ref.shape[3]
    v_per_k = kb_rows // vb_rows

    n_steps = ((i + 1) * qs + kb_rows - 1) // kb_rows

    v_rows = vat_ref.shape[2]

    def attend(kb, carry):
        seen, ms = carry
        r0 = pl.multiple_of(kb * kb_rows, kb_rows)
        keyb = keys_scr[pl.ds(r0, kb_rows), :]
        tied = keyb == thr
        tied_b = jnp.where(tied, 1.0, 0.0).astype(BF16)
        ranks = []
        for j in range(kb_rows // QB):
            within = jnp.dot(tri_ref[...], tied_b[j * QB:(j + 1) * QB, :], preferred_element_type=F32)
            ranks.append(seen + within)
            seen = seen + within[QB - 1:QB, :]
        rank = jnp.concatenate(ranks, axis=0)
        keep = jnp.logical_or(keyb > thr, jnp.logical_and(tied, rank <= n_ties))
        bias = jnp.where(keep, 0.0, NEG_MASK).astype(BF16)
        kblk = ka_ref[0, pl.ds(r0, kb_rows), :]
        logits = [jnp.dot(kblk, qat_ref[0, hh * HEAD_DIM:(hh + 1) * HEAD_DIM, :],
                          preferred_element_type=F32).astype(BF16) + bias for hh in heads]
        new_ms = [jnp.maximum(ms[hh], jnp.max(logits[hh], axis=0, keepdims=True).astype(F32)) for hh in heads]
        probs = [jnp.exp2(logits[hh] - new_ms[hh].astype(BF16)) for hh in heads]
        alphas = [jnp.exp2(ms[hh] - new_ms[hh]) for hh in heads]
        outs = []
        for hh in heads:
            out = jnp.dot(vat_ref[0, kb * v_per_k], probs[hh][:vb_rows, :], preferred_element_type=F32)
            for j in range(1, v_per_k):
                out = out + jnp.dot(vat_ref[0, kb * v_per_k + j], probs[hh][j * vb_rows:(j + 1) * vb_rows, :],
                                    preferred_element_type=F32)
            outs.append(out)
        for hh in heads:
            r = slice(hh * v_rows, (hh + 1) * v_rows)
            acc_scr[r, :] = acc_scr[r, :] * alphas[hh] + outs[hh]
        return seen, tuple(new_ms)

    init = (jnp.zeros((1, qs), F32), tuple(jnp.full((1, qs), NEG_BIG, F32) for _ in heads))
    lax.fori_loop(0, n_steps, attend, init)
    normed = [acc_scr[hh * v_rows:hh * v_rows + HEAD_DIM, :] / acc_scr[hh * v_rows + HEAD_DIM:hh * v_rows + HEAD_DIM + 1, :]
              for hh in heads]
    ya_ref[0] = jnp.concatenate(normed, axis=0).T.astype(BF16)


def _dsa_call(qit, wit, qat, ki, ka, vat, tri):
    b, l, _ = ki.shape
    qs = min(QS_DSA, l)
    k_sel = min(TOPK_MAX, l // 4)
    return pl.pallas_call(
        functools.partial(_dsa_kernel, k_sel=k_sel),
        grid=(b, l // qs),
        in_specs=[
            pl.BlockSpec((1, N_IDX_HEADS * IDX_DIM, qs), lambda bi, i: (bi, 0, i)),
            pl.BlockSpec((1, T_WI_ROWS, qs), lambda bi, i: (bi, 0, i)),
            pl.BlockSpec((1, A_WIDTH, qs), lambda bi, i: (bi, 0, i)),
            pl.BlockSpec((1, l, IDX_DIM), lambda bi, i: (bi, 0, 0), pipeline_mode=pl.Buffered(1)),
            pl.BlockSpec((1, l, HEAD_DIM), lambda bi, i: (bi, 0, 0), pipeline_mode=pl.Buffered(1)),
            pl.BlockSpec((1, l // VB_DSA, V_ROWS, VB_DSA), lambda bi, i: (bi, 0, 0, 0),
                         pipeline_mode=pl.Buffered(1)),
            pl.BlockSpec((QB, QB), lambda bi, i: (0, 0)),
        ],
        out_specs=pl.BlockSpec((1, qs, A_WIDTH), lambda bi, i: (bi, i, 0)),
        out_shape=jax.ShapeDtypeStruct((b, l, A_WIDTH), BF16),
        scratch_shapes=[pltpu.VMEM((l, qs), I32), pltpu.VMEM((l, qs), I16),
                        pltpu.VMEM((N_HEADS_A * V_ROWS, qs), F32)],
        compiler_params=_params(("arbitrary", "arbitrary")),
        name="dsa_attention",
    )(qit, wit, qat, ki, ka, vat, tri)


def _stick_kernel(qct_ref, kc_ref, vct_ref, later_ref, yc_ref, acc_scr):
    i = pl.program_id(1)
    qs = qct_ref.shape[2]
    diag_blocks = qs // QB
    qpos = i * qs + lax.broadcasted_iota(I32, (1, qs), 1)
    acc_scr[...] = jnp.zeros_like(acc_scr)

    def block(kb, tail, masked, lane0=0):
        r0 = pl.multiple_of(kb * QB, QB)
        kfull = kc_ref[0, pl.ds(r0, QB), :]
        vt = vct_ref[0, kb]
        heads = range(N_HEADS_C)
        old = [tail[hh][:, lane0:] for hh in heads]
        if masked:
            mask = (r0 + lax.broadcasted_iota(I32, (QB, qs - lane0), 0)) < qpos[:, lane0:]
        zs = [jnp.dot(kfull[:, (hh // 2) * LANES:(hh // 2 + 1) * LANES],
                      qct_ref[0, hh * LANES:(hh + 1) * LANES, lane0:], preferred_element_type=F32) for hh in heads]
        log_betas, splits, new_tail = [], [], []
        for hh in heads:
            z = zs[hh]
            log_beta = jnp.minimum(z, 0.0) - jnp.log(1.0 + jnp.exp(-jnp.abs(z)))
            log_keep = log_beta - z
            if masked:
                log_keep = jnp.where(mask, log_keep, 0.0)
            hi = log_keep.astype(BF16)
            lo = (log_keep - hi.astype(F32)).astype(BF16)
            log_betas.append(log_beta)
            splits.append(jnp.concatenate([hi, lo], axis=0))
            new = old[hh] + jnp.sum(log_keep, axis=0, keepdims=True)
            new_tail.append(new if lane0 == 0 else jnp.concatenate([tail[hh][:, :lane0], new], axis=1))
        withins = [jnp.dot(later_ref[...], splits[hh], preferred_element_type=F32) for hh in heads]
        weights = []
        for hh in heads:
            a = jnp.exp(log_betas[hh] + withins[hh] + old[hh])
            if masked:
                a = jnp.where(mask, a, 0.0)
            weights.append(a.astype(BF16))
        for hh in heads:
            r = slice(hh * HEAD_DIM, (hh + 1) * HEAD_DIM)
            acc_scr[r, lane0:] = acc_scr[r, lane0:] + jnp.dot(vt[r, :], weights[hh], preferred_element_type=F32)
        return tuple(new_tail)

    def largest(tail):
        worst = tail[0]
        for hh in range(1, N_HEADS_C):
            worst = jnp.maximum(worst, tail[hh])
        return jnp.max(worst)

    n_rest = i * diag_blocks
    tail = tuple(jnp.zeros((1, qs), F32) for _ in range(N_HEADS_C))
    for d in reversed(range(diag_blocks)):
        tail = block(n_rest + d, tail, True, lane0=d * QB)

    def live(cr):
        return jnp.logical_and(cr[0] < n_rest, cr[2] > STICK_DEAD)

    def step(cr):
        t, tl, _ = cr
        tl = block(n_rest - 1 - t, tl, False)
        return t + 1, tl, largest(tl)

    lax.while_loop(live, step, (jnp.int32(0), tail, largest(tail)))
    yc_ref[0] = acc_scr[...].T.astype(BF16)


def _stick_call(qct, kc, vct, later2):
    b, l, _ = kc.shape
    nq = l // QB
    qs = min(QS_STICK, l)
    return pl.pallas_call(
        _stick_kernel,
        grid=(b, l // qs),
        in_specs=[
            pl.BlockSpec((1, 2 * C_WIDTH, qs), lambda bi, i: (bi, 0, i)),
            pl.BlockSpec((1, l, C_WIDTH), lambda bi, i: (bi, 0, 0)),
            pl.BlockSpec((1, nq, C_WIDTH, QB), lambda bi, i: (bi, 0, 0, 0)),
            pl.BlockSpec((QB, 2 * QB), lambda bi, i: (0, 0)),
        ],
        out_specs=pl.BlockSpec((1, qs, C_WIDTH), lambda bi, i: (bi, i, 0)),
        out_shape=jax.ShapeDtypeStruct((b, l, C_WIDTH), BF16),
        scratch_shapes=[pltpu.VMEM((C_WIDTH, qs), F32)],
        compiler_params=_params(("arbitrary", "arbitrary")),
        name="stick_attention",
    )(qct, kc, vct, later2)


def _merge_kernel(x_ref, ya_ref, yc_ref, mb_ref, ga_ref, gc_ref, gt_ref, sh_ref, sc_ref, g1_ref, g2_ref,
                  wbra_ref, wbrc_ref, wout_ref, *rest, with_router):
    if with_router:
        wr_ref, x1_ref, h2_ref, route_ref, cnt_ref = rest
    else:
        x1_ref, h2_ref = rest
    merged = (ga_ref[0].astype(F32) * jnp.dot(ya_ref[0], wbra_ref[...], preferred_element_type=F32)
              + mb_ref[0].astype(F32)
              + gc_ref[0].astype(F32) * jnp.dot(yc_ref[0], wbrc_ref[...], preferred_element_type=F32))
    y = jnp.dot(merged.astype(BF16), wout_ref[...], preferred_element_type=F32)
    x1 = x_ref[0] + gt_ref[0] * _rms(y, g1_ref[...])
    x1_ref[0] = x1
    h2 = _rms(x1, g2_ref[...]) * (1.0 + sc_ref[0]) + sh_ref[0]
    h2_ref[0] = h2.astype(BF16)
    if with_router:
        logits = jnp.dot(h2, wr_ref[...], preferred_element_type=F32)
        lane = lax.broadcasted_iota(I32, logits.shape, 1)
        valid = lane < N_EXPERTS
        l1 = jnp.where(valid, logits, -jnp.inf)
        v1 = jnp.max(l1, axis=-1, keepdims=True)
        i1 = jnp.min(jnp.where(l1 == v1, lane, LANES), axis=-1, keepdims=True)
        l2 = jnp.where(lane == i1, -jnp.inf, l1)
        v2 = jnp.max(l2, axis=-1, keepdims=True)
        i2 = jnp.min(jnp.where(l2 == v2, lane, LANES), axis=-1, keepdims=True)
        e2 = jnp.exp(v2 - v1)
        p1 = 1.0 / (1.0 + e2)
        p2 = e2 / (1.0 + e2)
        route = jnp.where(lane == 0, i1.astype(F32), jnp.where(lane == 1, i2.astype(F32),
                          jnp.where(lane == 2, p1, jnp.where(lane == 3, p2, 0.0))))
        route_ref[0] = route
        chosen = jnp.where(jnp.logical_or(lane == i1, lane == i2), 1.0, 0.0)
        cnt_ref[0, 0] = jnp.broadcast_to(jnp.sum(chosen, axis=0, keepdims=True), (8, LANES))


def _merge_call(x, ya, yc, mb, ga, gc, gt, sh, sc, g1, g2, wbra, wbrc, wout, wr=None):
    b, l, d = x.shape
    tm = min(TM_MERGE if wr is None else TM_FFN, l)
    tok = lambda w: pl.BlockSpec((1, tm, w), lambda bi, i: (bi, i, 0))
    full = lambda a: pl.BlockSpec(a.shape, lambda bi, i: (0,) * a.ndim)
    vec = pl.BlockSpec((1, 1, d), lambda bi, i: (bi, 0, 0))
    in_specs = [tok(d), tok(A_WIDTH), tok(C_WIDTH), tok(d), tok(d), tok(d), vec, vec, vec,
                full(g1), full(g2), full(wbra), full(wbrc), full(wout)]
    args = [x, ya, yc, mb, ga, gc, gt, sh, sc, g1, g2, wbra, wbrc, wout]
    out_shape = [jax.ShapeDtypeStruct((b, l, d), F32), jax.ShapeDtypeStruct((b, l, d), BF16)]
    out_specs = [tok(d), tok(d)]
    if wr is not None:
        in_specs.append(full(wr))
        args.append(wr)
        out_shape += [jax.ShapeDtypeStruct((b, l, LANES), F32), jax.ShapeDtypeStruct((b, l // tm, 8, LANES), F32)]
        out_specs += [tok(LANES), pl.BlockSpec((1, 1, 8, LANES), lambda bi, i: (bi, i, 0, 0))]
    return pl.pallas_call(
        functools.partial(_merge_kernel, with_router=wr is not None),
        grid=(b, l // tm),
        in_specs=in_specs,
        out_specs=out_specs,
        out_shape=out_shape,
        compiler_params=_params(("arbitrary", "arbitrary")),
        name="merge_router" if wr is not None else "merge",
    )(*args)


def _ffn_kernel(x1_ref, h2_ref, gt_ref, g3_ref, wg_ref, wu_ref, wd_ref, o_ref):
    h2 = h2_ref[0]
    gate = jnp.dot(h2, wg_ref[...], preferred_element_type=F32)
    up = jnp.dot(h2, wu_ref[...], preferred_element_type=F32)
    act = (gate * _sigmoid(gate) * up).astype(BF16)
    y = jnp.dot(act, wd_ref[...], preferred_element_type=F32)
    o_ref[0] = x1_ref[0] + gt_ref[0] * _rms(y, g3_ref[...])


def _ffn_call(x1, h2, gt, g3, wg, wu, wd):
    b, l, d = x1.shape
    tm = min(TM_FFN, l)
    tok = pl.BlockSpec((1, tm, d), lambda bi, i: (bi, i, 0))
    full = lambda a: pl.BlockSpec(a.shape, lambda bi, i: (0,) * a.ndim)
    vec = pl.BlockSpec((1, 1, d), lambda bi, i: (bi, 0, 0))
    return pl.pallas_call(
        _ffn_kernel,
        grid=(b, l // tm),
        in_specs=[tok, tok, vec, full(g3), full(wg), full(wu), full(wd)],
        out_specs=tok,
        out_shape=jax.ShapeDtypeStruct((b, l, d), F32),
        compiler_params=_params(("arbitrary", "arbitrary")),
        name="ffn_dense",
    )(x1, h2, gt, g3, wg, wu, wd)


def _moe_kernel(nchunk_ref, first_ref, total_ref,
                x1_ref, h2_ref, route_ref, before_ref, gt_ref, g3_ref, wg_ref, wu_ref, wd_ref, o_ref,
                xs_scr, wrow_scr, dest_scr, acc_scr):
    tile = pl.program_id(0) * pl.num_programs(1) + pl.program_id(1)
    e = pl.program_id(2)
    n_e = pl.num_programs(2)
    tm = h2_ref.shape[1]
    total = total_ref[tile]

    @pl.when(e == 0)
    def _():
        route = route_ref[0]
        lane = lax.broadcasted_iota(I32, route.shape, 1).astype(F32)
        hot = [jnp.where(lane == route[:, s:s + 1], 1.0, 0.0) for s in range(TOP_K)]
        earlier = jnp.dot(before_ref[...], (hot[0] + hot[1]).astype(BF16), preferred_element_type=F32)
        start = jnp.zeros((1, LANES), F32)
        for ee in range(N_EXPERTS):
            start = jnp.where(lane[:1, :] == ee, (first_ref[tile * N_EXPERTS + ee] * ROW_CHUNK).astype(F32), start)
        where_to = earlier + start
        dest = [hot[s] * where_to for s in range(TOP_K)]
        prob = [hot[s] * route[:, TOP_K + s:TOP_K + s + 1] for s in range(TOP_K)]
        for s in range(TOP_K):
            dest_scr[s] = jnp.broadcast_to(jnp.sum(dest[s], axis=1, keepdims=True), (tm, LANES))
        dest_row = [jnp.sum(dest[s].T, axis=0, keepdims=True) for s in range(TOP_K)]
        prob_row = [jnp.sum(prob[s].T, axis=0, keepdims=True) for s in range(TOP_K)]
        h2 = h2_ref[0]

        def place(c, carry):
            r0 = pl.multiple_of(c * ROW_CHUNK, ROW_CHUNK)
            rows = (r0 + lax.broadcasted_iota(I32, (ROW_CHUNK, tm), 0)).astype(F32)
            here = [rows == dest_row[s] for s in range(TOP_K)]
            pick = jnp.where(jnp.logical_or(here[0], here[1]), 1.0, 0.0).astype(BF16)
            xs_scr[pl.ds(r0, ROW_CHUNK), :] = jnp.dot(pick, h2, preferred_element_type=F32).astype(BF16)
            weight = jnp.where(here[0], prob_row[0], 0.0) + jnp.where(here[1], prob_row[1], 0.0)
            wrow_scr[pl.ds(r0, ROW_CHUNK), :] = jnp.broadcast_to(jnp.sum(weight, axis=1, keepdims=True),
                                                                  (ROW_CHUNK, LANES))
            return carry

        lax.fori_loop(0, total + total % 2, place, 0)

    def expert_chunk(c, carry):
        r0 = pl.multiple_of((first_ref[tile * n_e + e] + c) * ROW_CHUNK, ROW_CHUNK)
        xs = xs_scr[pl.ds(r0, ROW_CHUNK), :]
        gate = jnp.dot(xs, wg_ref[0], preferred_element_type=F32)
        up = jnp.dot(xs, wu_ref[0], preferred_element_type=F32)
        act = (gate * _sigmoid(gate) * up).astype(BF16)
        y = jnp.dot(act, wd_ref[0], preferred_element_type=F32)
        xs_scr[pl.ds(r0, ROW_CHUNK), :] = (y * wrow_scr[pl.ds(r0, ROW_CHUNK), 0:1]).astype(BF16)
        return carry

    lax.fori_loop(0, nchunk_ref[tile * n_e + e], expert_chunk, 0)

    @pl.when(e == n_e - 1)
    def _():
        acc_scr[...] = jnp.zeros_like(acc_scr)

        def collect(c, carry):
            r0 = pl.multiple_of(c * 2 * ROW_CHUNK, 2 * ROW_CHUNK)
            halves = []
            for j in range(2):
                cols = (r0 + j * ROW_CHUNK + lax.broadcasted_iota(I32, (tm, ROW_CHUNK), 1)).astype(F32)
                mine = jnp.logical_or(cols == dest_scr[0], cols == dest_scr[1])
                halves.append(jnp.where(mine, 1.0, 0.0).astype(BF16))
            acc_scr[...] = acc_scr[...] + jnp.dot(jnp.concatenate(halves, axis=1),
                                                  xs_scr[pl.ds(r0, 2 * ROW_CHUNK), :], preferred_element_type=F32)
            return carry

        lax.fori_loop(0, (total + 1) // 2, collect, 0)
        o_ref[0] = x1_ref[0] + gt_ref[0] * _rms(acc_scr[...], g3_ref[...])


def _moe_call(x1, h2, route, counts, before, gt, g3, wg, wu, wd):
    b, l, d = x1.shape
    tm = before.shape[0]
    n_e, _, ff = wg.shape
    cnt = counts[:, :, 0, :n_e].astype(I32).reshape(b * (l // tm), -1, n_e).sum(axis=1)
    nchunk = (cnt + ROW_CHUNK - 1) // ROW_CHUNK
    first = jnp.cumsum(nchunk, axis=1) - nchunk
    total = jnp.sum(nchunk, axis=1)
    rows = TOP_K * tm + n_e * ROW_CHUNK
    tok = lambda w: pl.BlockSpec((1, tm, w), lambda bi, i, e, *_: (bi, i, 0))
    tok_once = lambda w: pl.BlockSpec((1, tm, w), lambda bi, i, e, *_: (bi, i, 0), pipeline_mode=pl.Buffered(1))
    vec = pl.BlockSpec((1, 1, d), lambda bi, i, e, *_: (bi, 0, 0))
    grid_spec = pltpu.PrefetchScalarGridSpec(
        num_scalar_prefetch=3,
        grid=(b, l // tm, n_e),
        in_specs=[tok_once(d), tok_once(d), tok(LANES),
                  pl.BlockSpec(before.shape, lambda bi, i, e, *_: (0, 0), pipeline_mode=pl.Buffered(1)), vec,
                  pl.BlockSpec(g3.shape, lambda bi, i, e, *_: (0, 0)),
                  pl.BlockSpec((1, d, ff), lambda bi, i, e, *_: (e, 0, 0)),
                  pl.BlockSpec((1, d, ff), lambda bi, i, e, *_: (e, 0, 0)),
                  pl.BlockSpec((1, ff, d), lambda bi, i, e, *_: (e, 0, 0))],
        out_specs=tok(d),
        scratch_shapes=[pltpu.VMEM((rows, d), BF16), pltpu.VMEM((rows, LANES), F32),
                        pltpu.VMEM((TOP_K, tm, LANES), F32), pltpu.VMEM((tm, d), F32)],
    )
    return pl.pallas_call(
        _moe_kernel,
        grid_spec=grid_spec,
        out_shape=jax.ShapeDtypeStruct((b, l, d), F32),
        compiler_params=_params(("arbitrary", "arbitrary", "arbitrary")),
        name="ffn_moe",
    )(nchunk.reshape(-1), first.reshape(-1), total, x1, h2, route, before, gt, g3, wg, wu, wd)


def _rot_cols(w, n_heads, head_dim):
    d = w.shape[0]
    w = w.reshape(d, n_heads, head_dim)
    half = ROPE_DIM // 2
    rot = jnp.concatenate([-w[..., half:ROPE_DIM], w[..., :half],
                           jnp.zeros((d, n_heads, head_dim - ROPE_DIM), w.dtype)], axis=-1)
    return rot.reshape(d, n_heads * head_dim)


def _pad_cols(w, width):
    return jnp.pad(w, ((0, 0), (0, width - w.shape[1])))


def _layer_weights(w_in, w_uk, w_uv, w_pool):
    d = w_in.shape[0]
    offs, o = [], 0
    for s in IN_SIZES:
        offs.append(o)
        o += s
    piece = lambda k: w_in[:, offs[k]:offs[k] + IN_SIZES[k]]
    w_qa, w_lat, w_kr, w_qi, w_ki, w_wi, w_up, w_qc, w_kc, w_vc, w_gate = [piece(k) for k in range(len(IN_SIZES))]
    attn_scale = HEAD_DIM ** -0.5
    idx_scale = IDX_DIM ** -0.5 * N_IDX_HEADS ** -0.5
    wrow = jnp.concatenate([
        w_lat,
        _pad_cols(w_kr, LANES), _pad_cols(_rot_cols(w_kr, 1, ROPE_DIM), LANES),
        _pad_cols(w_ki, LANES), _pad_cols(_rot_cols(w_ki, 1, IDX_DIM), LANES),
        w_up, w_kc, w_gate], axis=1).astype(BF16)
    w_qc_t = (w_qc * attn_scale).T.reshape(N_HEADS_C, HEAD_DIM, d)
    zeros = jnp.zeros_like(w_qc_t)
    even = (jnp.arange(N_HEADS_C) % 2 == 0)[:, None, None]
    w_qc_pad = jnp.concatenate([jnp.where(even, w_qc_t, zeros), jnp.where(even, zeros, w_qc_t)], axis=1)
    w_qc_pad = w_qc_pad.reshape(2 * C_WIDTH, d)
    softmax_scale = attn_scale * LOG2_E
    wt = jnp.concatenate([
        (w_qa * softmax_scale).T, (_rot_cols(w_qa, N_HEADS_A, HEAD_DIM) * softmax_scale).T,
        w_qi.T, _rot_cols(w_qi, N_IDX_HEADS, IDX_DIM).T,
        w_qc_pad, w_vc.T,
        jnp.pad((w_wi * idx_scale).T, ((0, T_WI_ROWS - N_IDX_HEADS), (0, 0)))], axis=0).astype(BF16)
    wuk = jnp.zeros((KV_LATENT, LANES), F32).at[:, ROPE_DIM:HEAD_DIM].set(w_uk).astype(BF16)
    wuvt = w_uv.T.astype(BF16)
    wpool = jnp.zeros((POOL_WIDTH, POOL_WIDTH), F32)
    for g in range(N_POOL_GROUPS):
        sl = slice(g * POOL_GROUP_DIM, (g + 1) * POOL_GROUP_DIM)
        wpool = wpool.at[sl, sl].set(w_pool[g])
    return wrow, wt, wuk, wuvt, wpool.astype(BF16)


def _rope_tables(positions):
    inv = ROPE_THETA ** (-jnp.arange(0, ROPE_DIM, 2, dtype=F32) / ROPE_DIM)
    ang = positions.astype(F32)[..., None] * inv
    cos, sin = jnp.cos(ang), jnp.sin(ang)
    b, l = positions.shape
    ones = jnp.ones((b, l, HEAD_DIM - ROPE_DIM), F32)
    cos_h = jnp.concatenate([cos, cos, ones], axis=-1)
    sin_h = jnp.concatenate([sin, sin, jnp.zeros_like(ones)], axis=-1)
    cos_r = jnp.concatenate([cos_h, cos_h], axis=-1)
    sin_r = jnp.concatenate([sin_h, sin_h], axis=-1)
    return cos_r, sin_r, cos_h.transpose(0, 2, 1), sin_h.transpose(0, 2, 1)


def kernel(x, c, positions, w_ada, b_ada, norm_gains, w_in, g_kv_latent, w_uk, w_uv, w_pool, pool_scale,
           w_br_a, w_br_b, w_br_c, w_out, w_gate_dense, w_up_dense, w_down_dense,
           w_router, w_gate_moe, w_up_moe, w_down_moe):
    b, l, d = x.shape
    depth = w_in.shape[0]
    assert d == D_MODEL and l % max(KC, TM_FFN, TM_MOE) == 0
    cos_r, sin_r, cos_t, sin_t = _rope_tables(positions)
    c_pad = jnp.pad(c, ((0, 8 - b), (0, 0)))
    mod = _mod_call(c_pad, w_ada, b_ada)[:, :b]
    idx = lax.broadcasted_iota(I32, (QB, QB), 0)
    jdx = lax.broadcasted_iota(I32, (QB, QB), 1)
    tri = (jdx <= idx).astype(BF16)
    later = (jdx > idx).astype(BF16)
    later2 = jnp.concatenate([later, later], axis=1)
    tm_moe = min(TM_MOE, l)
    before = (lax.broadcasted_iota(I32, (tm_moe, tm_moe), 1)
              < lax.broadcasted_iota(I32, (tm_moe, tm_moe), 0)).astype(BF16)
    for layer in range(depth):
        sh1, sc1, gt1, sh2, sc2, gt2 = [m.reshape(b, 1, d) for m in jnp.split(mod[layer], 6, axis=-1)]
        gains = norm_gains[layer].reshape(4, 1, d)
        wrow, wt, wuk, wuvt, wpool = _layer_weights(w_in[layer], w_uk[layer], w_uv[layer], w_pool[layer])
        (qat, qit, wit, ka, vat, ki, qct, kc, vct, ga, gc, mb) = _inproj_call(
            x, sh1, sc1, gains[0], wrow, wt, cos_r, sin_r, cos_t, sin_t,
            g_kv_latent[layer].reshape(1, KV_LATENT), wuk, wuvt, wpool,
            pool_scale[layer].reshape(1, POOL_WIDTH), w_br_b[layer].astype(BF16))
        ya = _dsa_call(qit, wit, qat, ki, ka, vat, tri)
        yc = _stick_call(qct, kc, vct, later2)
        i = layer // 2
        merge_args = (x, ya, yc, mb, ga, gc, gt1, sh2, sc2, gains[1], gains[2],
                      w_br_a[layer].astype(BF16), w_br_c[layer].astype(BF16), w_out[layer].astype(BF16))
        if layer % 2 == 0:
            x1, h2 = _merge_call(*merge_args)
            x = _ffn_call(x1, h2, gt2, gains[3], w_gate_dense[i].astype(BF16), w_up_dense[i].astype(BF16),
                          w_down_dense[i].astype(BF16))
        else:
            x1, h2, route, counts = _merge_call(*merge_args, wr=_pad_cols(w_router[i], LANES))
            x = _moe_call(x1, h2, route, counts, before, gt2, gains[3], w_gate_moe[i].astype(BF16),
                          w_up_moe[i].astype(BF16), w_down_moe[i].astype(BF16))
    return x
```

```python
import functools

import jax
import jax.numpy as jnp
from jax import lax
from jax.experimental import pallas as pl
from jax.experimental.pallas import tpu as pltpu

F32 = jnp.float32
BF16 = jnp.bfloat16
I32 = jnp.int32
I16 = jnp.int16

D_MODEL = 1024
HEAD_DIM = 64
ROPE_DIM = HEAD_DIM // 4
ROPE_THETA = 500000.0
N_HEADS_A = (3 * D_MODEL // 8) // HEAD_DIM
A_WIDTH = N_HEADS_A * HEAD_DIM
KV_LATENT = D_MODEL // 8
N_IDX_HEADS = 4
IDX_DIM = 64
TOPK_MAX = 256
N_POOL_GROUPS = 4
POOL_WINDOWS = (2, 4, 8, 16)
POOL_WIDTH = D_MODEL // 4
POOL_GROUP_DIM = POOL_WIDTH // N_POOL_GROUPS
N_HEADS_C = (D_MODEL // 4) // HEAD_DIM
C_WIDTH = N_HEADS_C * HEAD_DIM
N_BRANCHES = 3
IN_SIZES = (A_WIDTH, KV_LATENT, ROPE_DIM, N_IDX_HEADS * IDX_DIM, IDX_DIM, N_IDX_HEADS,
            POOL_WIDTH, C_WIDTH, C_WIDTH, C_WIDTH, N_BRANCHES * D_MODEL)
D_FF = 2816
N_EXPERTS = 8
TOP_K = 2
D_FF_EXPERT = D_FF // TOP_K
RMS_EPS = 1e-6

LANES = 128
QB = 128
KC = 512
QS_STICK = 512
QS_DSA = 256
KB_DSA = 512
VB_DSA = 256
V_ROWS = 80
STICK_DEAD = -106.0
TM_IN = 512
TM_MERGE = 512
TM_FFN = 512
TM_MOE = 1024
ROW_CHUNK = 128
POOL_HALO = 16
VMEM_LIMIT = 56 * 1024 * 1024
INT_MIN = -2147483648
LOG2_E = 1.4426950408889634
NEG_BIG = -1e30
NEG_MASK = -2e30

R_LAT, R_KR, R_KRR, R_KI, R_KIR, R_UP, R_KC, R_GATE = 0, 128, 256, 384, 512, 640, 896, 1152
T_QA, T_QAR, T_QI, T_QIR, T_QC, T_VC, T_WI = 0, 384, 768, 1024, 1280, 1792, 2048
T_WI_ROWS = 16


def _params(sem):
    return pltpu.CompilerParams(dimension_semantics=sem, vmem_limit_bytes=VMEM_LIMIT)


def _sigmoid(v):
    return 1.0 / (1.0 + jnp.exp(-v))


def _rms(v, gain):
    return v * lax.rsqrt(jnp.mean(v * v, axis=-1, keepdims=True) + RMS_EPS) * gain


def _mod_kernel(c_ref, w_ref, b_ref, o_ref):
    c = c_ref[...]
    cond = c * _sigmoid(c)
    o_ref[0] = jnp.dot(cond, w_ref[0], preferred_element_type=F32) + b_ref[0]


def _mod_call(c_pad, w_ada, b_ada):
    depth, d, n = w_ada.shape
    tn = 1024
    return pl.pallas_call(
        _mod_kernel,
        grid=(depth, n // tn),
        in_specs=[
            pl.BlockSpec((8, d), lambda l, j: (0, 0)),
            pl.BlockSpec((1, d, tn), lambda l, j: (l, 0, j)),
            pl.BlockSpec((1, 1, tn), lambda l, j: (l, 0, j)),
        ],
        out_specs=pl.BlockSpec((1, 8, tn), lambda l, j: (l, 0, j)),
        out_shape=jax.ShapeDtypeStruct((depth, 8, n), F32),
        compiler_params=_params(("arbitrary", "arbitrary")),
        name="adaln_mod",
    )(c_pad, w_ada, b_ada.reshape(depth, 1, n))


def _inproj_kernel(x_ref, sh_ref, sc_ref, g_ref, wrow_ref, wt_ref, c_ref, s_ref, ct_ref, st_ref,
                   glat_ref, wuk_ref, wuvt_ref, wpool_ref, pscale_ref, wbrb_ref,
                   qat_ref, qit_ref, wit_ref, ka_ref, vat_ref, ki_ref, qct_ref, kc_ref, vct_ref,
                   ga_ref, gc_ref, mb_ref,
                   h_scr, ht_scr, ext_scr, prev_scr):
    i = pl.program_id(1)
    tm = x_ref.shape[1]
    x = x_ref[0]
    h = _rms(x, g_ref[...]) * (1.0 + sc_ref[0]) + sh_ref[0]
    h_scr[...] = h.astype(BF16)
    ht_scr[...] = h.T.astype(BF16)

    def rowdot(a, width):
        return jnp.dot(h_scr[...], wrow_ref[:, a:a + width], preferred_element_type=F32)

    def tdot(a, height):
        return jnp.dot(wt_ref[a:a + height, :], ht_scr[...], preferred_element_type=F32)

    cos_r, sin_r = c_ref[0], s_ref[0]
    cos_t, sin_t = ct_ref[0], st_ref[0]

    latn = _rms(rowdot(R_LAT, KV_LATENT), glat_ref[...])
    ka = (rowdot(R_KR, LANES) * cos_r + rowdot(R_KRR, LANES) * sin_r
          + jnp.dot(latn.astype(BF16), wuk_ref[...], preferred_element_type=F32))
    ka_ref[0] = ka[:, :HEAD_DIM].astype(BF16)
    vat = jnp.dot(wuvt_ref[...], latn.T.astype(BF16), preferred_element_type=F32)
    pad_row = lax.broadcasted_iota(I32, (V_ROWS - HEAD_DIM, tm), 0)
    vat = jnp.concatenate([vat, jnp.where(pad_row == 0, 1.0, 0.0)], axis=0)
    for j in range(tm // VB_DSA):
        vat_ref[0, j] = vat[:, j * VB_DSA:(j + 1) * VB_DSA].astype(BF16)

    ki =rowdot(R_KI, LANES) * cos_r + rowdot(R_KIR, LANES) * sin_r
    ki_ref[0] = ki[:, :IDX_DIM].astype(BF16)

    qa, qar = tdot(T_QA, A_WIDTH), tdot(T_QAR, A_WIDTH)
    for hh in range(N_HEADS_A):
        r = slice(hh * HEAD_DIM, (hh + 1) * HEAD_DIM)
        qat_ref[0, r, :] = (qa[r] * cos_t + qar[r] * sin_t).astype(BF16)
    qi, qir = tdot(T_QI, N_IDX_HEADS * IDX_DIM), tdot(T_QIR, N_IDX_HEADS * IDX_DIM)
    for hh in range(N_IDX_HEADS):
        r = slice(hh * IDX_DIM, (hh + 1) * IDX_DIM)
        qit_ref[0, r, :] = (qi[r] * cos_t + qir[r] * sin_t).astype(BF16)
    wit_ref[0] = tdot(T_WI, T_WI_ROWS)

    qct_ref[0] = tdot(T_QC, 2 * C_WIDTH).astype(BF16)
    kc_ref[0] = rowdot(R_KC, C_WIDTH).astype(BF16)
    vct = tdot(T_VC, C_WIDTH)
    for j in range(tm // QB):
        vct_ref[0, j] = vct[:, j * QB:(j + 1) * QB].astype(BF16)

    up = rowdot(R_UP, POOL_WIDTH)

    @pl.when(i == 0)
    def _():
        prev_scr[...] = jnp.zeros_like(prev_scr)

    ext_scr[0:POOL_HALO, :] = prev_scr[...]
    ext_scr[POOL_HALO:POOL_HALO + tm, :] = up
    prev_scr[...] = up[tm - POOL_HALO:, :]
    lag = [ext_scr[POOL_HALO - j:POOL_HALO - j + tm, :] for j in range(POOL_HALO)]
    sums = {}
    run = lag[0]
    for j in range(1, POOL_HALO):
        run = run + lag[j]
        if j + 1 in POOL_WINDOWS:
            sums[j + 1] = run
    lane = lax.broadcasted_iota(I32, (tm, POOL_WIDTH), 1)
    pos = i * tm + lax.broadcasted_iota(I32, (tm, POOL_WIDTH), 0)
    pooled_sum = sums[POOL_WINDOWS[-1]]
    win = jnp.full((tm, POOL_WIDTH), POOL_WINDOWS[-1], I32)
    for g in range(N_POOL_GROUPS - 2, -1, -1):
        in_group = lane < (g + 1) * POOL_GROUP_DIM
        pooled_sum = jnp.where(in_group, sums[POOL_WINDOWS[g]], pooled_sum)
        win = jnp.where(in_group, POOL_WINDOWS[g], win)
    cnt = jnp.minimum(pos + 1, win).astype(F32)
    pooled = pooled_sum / cnt - up
    yb = jnp.dot(pooled.astype(BF16), wpool_ref[...], preferred_element_type=F32) * pscale_ref[...]

    ga_ref[0] = _sigmoid(rowdot(R_GATE, D_MODEL)).astype(BF16)
    gb = _sigmoid(rowdot(R_GATE + D_MODEL, D_MODEL))
    mb_ref[0] = (gb * jnp.dot(yb.astype(BF16), wbrb_ref[...], preferred_element_type=F32)).astype(BF16)
    gc_ref[0] = _sigmoid(rowdot(R_GATE + 2 * D_MODEL, D_MODEL)).astype(BF16)


def _inproj_call(x, sh, sc, gain, wrow, wt, cos_r, sin_r, cos_t, sin_t, glat, wuk, wuvt, wpool, pscale, wbrb):
    b, l, d = x.shape
    tm = min(TM_IN, l)
    nq = l // QB
    tok = lambda w: pl.BlockSpec((1, tm, w), lambda bi, i: (bi, i, 0))
    feat = lambda hgt: pl.BlockSpec((1, hgt, tm), lambda bi, i: (bi, 0, i))
    blk = lambda hgt, w: pl.BlockSpec((1, tm // w, hgt, w), lambda bi, i: (bi, i, 0, 0))
    full = lambda a: pl.BlockSpec(a.shape, lambda bi, i: (0,) * a.ndim)
    vec = pl.BlockSpec((1, 1, d), lambda bi, i: (bi, 0, 0))
    out_shape = (
        jax.ShapeDtypeStruct((b, A_WIDTH, l), BF16),
        jax.ShapeDtypeStruct((b, N_IDX_HEADS * IDX_DIM, l), BF16),
        jax.ShapeDtypeStruct((b, T_WI_ROWS, l), F32),
        jax.ShapeDtypeStruct((b, l, HEAD_DIM), BF16),
        jax.ShapeDtypeStruct((b, l // VB_DSA, V_ROWS, VB_DSA), BF16),
        jax.ShapeDtypeStruct((b, l, IDX_DIM), BF16),
        jax.ShapeDtypeStruct((b, 2 * C_WIDTH, l), BF16),
        jax.ShapeDtypeStruct((b, l, C_WIDTH), BF16),
        jax.ShapeDtypeStruct((b, nq, C_WIDTH, QB), BF16),
        jax.ShapeDtypeStruct((b, l, d), BF16),
        jax.ShapeDtypeStruct((b, l, d), BF16),
        jax.ShapeDtypeStruct((b, l, d), BF16),
    )
    out_specs = (feat(A_WIDTH), feat(N_IDX_HEADS * IDX_DIM), feat(T_WI_ROWS), tok(HEAD_DIM), blk(V_ROWS, VB_DSA),
                 tok(IDX_DIM), feat(2 * C_WIDTH), tok(C_WIDTH), blk(C_WIDTH, QB), tok(d), tok(d), tok(d))
    resident = lambda a: pl.BlockSpec(a.shape, lambda bi, i: (0,) * a.ndim, pipeline_mode=pl.Buffered(1))
    in_specs = [tok(d), vec, vec, full(gain), resident(wrow), resident(wt), tok(LANES), tok(LANES),
                feat(HEAD_DIM), feat(HEAD_DIM), full(glat), full(wuk), full(wuvt), full(wpool),
                full(pscale), full(wbrb)]
    return pl.pallas_call(
        _inproj_kernel,
        grid=(b, l // tm),
        in_specs=in_specs,
        out_specs=out_specs,
        out_shape=out_shape,
        scratch_shapes=[pltpu.VMEM((tm, d), BF16), pltpu.VMEM((d, tm), BF16),
                        pltpu.VMEM((tm + POOL_HALO, POOL_WIDTH), F32), pltpu.VMEM((POOL_HALO, POOL_WIDTH), F32)],
        compiler_params=_params(("arbitrary", "arbitrary")),
        name="inproj",
    )(x, sh, sc, gain, wrow, wt, cos_r, sin_r, cos_t, sin_t, glat, wuk, wuvt, wpool, pscale, wbrb)


def _dsa_kernel(qit_ref, wit_ref, qat_ref, ki_ref, ka_ref, vat_ref, tri_ref, ya_ref,
                keys_scr, top_scr, acc_scr, *, k_sel):
    i = pl.program_id(1)
    qs = qat_ref.shape[2]
    n_chunks = ((i + 1) * qs + KC - 1) // KC
    qpos = i * qs + lax.broadcasted_iota(I32, (1, qs), 1)
    w_idx = wit_ref[0]

    def score_chunk(c, carry, masked):
        r0 = pl.multiple_of(c * KC, KC)
        kblk = ki_ref[0, pl.ds(r0, KC), :]
        parts = [jnp.dot(kblk, qit_ref[0, hh * IDX_DIM:(hh + 1) * IDX_DIM, :], preferred_element_type=F32)
                 for hh in range(N_IDX_HEADS)]
        score = jnp.maximum(parts[0], 0.0) * w_idx[0:1, :]
        for hh in range(1, N_IDX_HEADS):
            score = score + jnp.maximum(parts[hh], 0.0) * w_idx[hh:hh + 1, :]
        bits = lax.bitcast_convert_type(score, I32)
        key = jnp.where(bits < 0, INT_MIN - bits, bits)
        if masked:
            kpos = r0 + lax.broadcasted_iota(I32, (KC, qs), 0)
            key = jnp.where(kpos <= qpos, key, INT_MIN)
        keys_scr[pl.ds(r0, KC), :] = key
        top = lax.shift_right_arithmetic(key, 16).astype(I16)
        top_scr[pl.ds(r0, KC), :] = top
        groups = top.reshape(KC // 16, 16, qs)
        for j in range(KC // 16):
            carry = jnp.where(groups[j] > carry, groups[j], carry)
        return carry

    n_open = (i * qs) // KC
    top_max = lax.fori_loop(0, n_open, functools.partial(score_chunk, masked=False),
                            jnp.full((16, qs), -2 ** 15, I16))
    top_max = lax.fori_loop(n_open, n_chunks, functools.partial(score_chunk, masked=True), top_max)
    top_max = jnp.max(top_max.astype(I32), axis=0, keepdims=True)

    def count_rows(src, rows_per_vreg, trial):
        n_acc = 4
        groups = KC // rows_per_vreg

        def body(c, accs):
            r0 = pl.multiple_of(c * KC, KC)
            rows = src[pl.ds(r0, KC), :].reshape(groups, rows_per_vreg, qs)
            accs = list(accs)
            for j in range(groups):
                a = accs[j % n_acc]
                accs[j % n_acc] = jnp.where(rows[j] >= trial, a + 1, a)
            return tuple(accs)

        zero = jnp.zeros((rows_per_vreg, qs), src.dtype)
        accs = lax.fori_loop(0, n_chunks, body, tuple(zero for _ in range(n_acc)))
        total = (accs[0] + accs[1]) + (accs[2] + accs[3])
        return jnp.sum(total.astype(I32), axis=0, keepdims=True)

    def count_ge(trial):
        return count_rows(keys_scr, 8, trial)

    def count_ge_top(trial):
        return count_rows(top_scr, 16, lax.shift_right_arithmetic(trial, 16).astype(I16))

    c_zero = count_ge_top(jnp.zeros((1, qs), I32))
    c_pos = count_ge(jnp.ones((1, qs), I32))
    tie_at_zero = jnp.logical_and(c_zero >= k_sel, c_pos < k_sel)

    def unsettled(c_ans):
        settled = jnp.logical_or(tie_at_zero, c_ans == k_sel)
        return jnp.max(jnp.where(settled, 0, 1))

    def search(counter, lowest_bit, group, state):
        def refine(carry):
            bit, ans, c_ans, _ = carry
            for g in range(group):
                trial = ans + lax.shift_left(jnp.int32(1), bit - g)
                c = counter(trial)
                ok = c >= k_sel
                c_ans = jnp.where(ok, c, c_ans)
                ans = jnp.where(ok, trial, ans)
            return bit - group, ans, c_ans, unsettled(c_ans)

        return lax.while_loop(lambda cr: jnp.logical_and(cr[0] >= lowest_bit, cr[3] > 0), refine, state)

    def count_within(gap):
        return count_rows(top_scr, 16, jnp.maximum(top_max - gap, -2 ** 15).astype(I16))

    near_bits = 9
    c_near = count_within(2 ** near_bits - 1)
    short = qpos + 1 < k_sel
    known = jnp.logical_or(tie_at_zero, short)
    all_near = jnp.min(jnp.where(jnp.logical_or(c_near >= k_sel, known), 1, 0)) > 0
    first_bit = jnp.where(all_near, near_bits - 1, 15)

    def widen(group, carry):
        bit, gap, c_gap = carry
        for g in range(group):
            trial = gap + lax.shift_left(jnp.int32(1), bit - g) - 1
            c = count_within(trial)
            ok = c >= k_sel
            gap = jnp.where(ok, gap, trial + 1)
            c_gap = jnp.where(ok, c, c_gap)
        return bit - group, gap, c_gap

    c_far = jnp.where(all_near, c_near, jnp.int32(2 ** 30))
    state = (first_bit, jnp.zeros((1, qs), I32), c_far)
    state = lax.while_loop(lambda cr: cr[0] >= near_bits, functools.partial(widen, 16 - near_bits), state)
    _, gap, c_top = lax.while_loop(lambda cr: cr[0] >= 0, functools.partial(widen, 3), state)
    bucket = jnp.where(short, -2 ** 15, jnp.where(tie_at_zero, 0, jnp.maximum(top_max - gap, -2 ** 15)))
    c_top = jnp.where(tie_at_zero, c_zero, c_top)
    top = lax.shift_left(bucket, 16)
    alive = unsettled(c_top)
    last_bucket = bucket >= 2 ** 15 - 1
    c_over = jnp.where(last_bucket, 0,
                       count_rows(top_scr, 16, jnp.where(last_bucket, bucket, bucket + 1).astype(I16)))

    def pack_low(c, carry):
        r0 = pl.multiple_of(c * KC, KC)
        low = (keys_scr[pl.ds(r0, KC), :] ^ 0x8000).astype(I16)
        top_scr[pl.ds(r0, KC), :] = jnp.where(top_scr[pl.ds(r0, KC), :] == bucket.astype(I16), low, -2 ** 15)
        return carry

    lax.fori_loop(0, n_chunks, pack_low, 0)

    def count_ge_low(low_trial):
        return c_over + count_rows(top_scr, 16, (low_trial - 2 ** 15).astype(I16))

    zero_row = jnp.zeros((1, qs), I32)
    _, low, _, _ = search(count_ge_low, 0, 4, (jnp.int32(15), zero_row, c_top, alive))
    thr = top + low
    full_low = low >= 2 ** 16 - 1
    n_above = jnp.where(full_low, c_over, count_ge_low(jnp.where(full_low, low, low + 1)))
    n_ties = jnp.where(thr == INT_MIN, 0, k_sel - n_above).astype(F32)

    acc_scr[...] = jnp.zeros_like(acc_scr)
    heads = range(N_HEADS_A)

    kb_rows = KB_DSA
    vb_rows = vat_ref.shape[3]
    v_per_k = kb_rows // vb_rows

    n_steps = ((i + 1) * qs + kb_rows - 1) // kb_rows

    v_rows = vat_ref.shape[2]

    def attend(kb, carry):
        seen, ms = carry
        r0 = pl.multiple_of(kb * kb_rows, kb_rows)
        keyb = keys_scr[pl.ds(r0, kb_rows), :]
        tied = keyb == thr
        tied_b = jnp.where(tied, 1.0, 0.0).astype(BF16)
        ranks = []
        for j in range(kb_rows // QB):
            within = jnp.dot(tri_ref[...], tied_b[j * QB:(j + 1) * QB, :], preferred_element_type=F32)
            ranks.append(seen + within)
            seen = seen + within[QB - 1:QB, :]
        rank = jnp.concatenate(ranks, axis=0)
        keep = jnp.logical_or(keyb > thr, jnp.logical_and(tied, rank <= n_ties))
        bias = jnp.where(keep, 0.0, NEG_MASK).astype(BF16)
        kblk = ka_ref[0, pl.ds(r0, kb_rows), :]
        logits = [jnp.dot(kblk, qat_ref[0, hh * HEAD_DIM:(hh + 1) * HEAD_DIM, :],
                          preferred_element_type=F32).astype(BF16) + bias for hh in heads]
        new_ms = [jnp.maximum(ms[hh], jnp.max(logits[hh], axis=0, keepdims=True).astype(F32)) for hh in heads]
        probs = [jnp.exp2(logits[hh] - new_ms[hh].astype(BF16)) for hh in heads]
        alphas = [jnp.exp2(ms[hh] - new_ms[hh]) for hh in heads]
        outs = []
        for hh in heads:
            out = jnp.dot(vat_ref[0, kb * v_per_k], probs[hh][:vb_rows, :], preferred_element_type=F32)
            for j in range(1, v_per_k):
                out = out + jnp.dot(vat_ref[0, kb * v_per_k + j], probs[hh][j * vb_rows:(j + 1) * vb_rows, :],
                                    preferred_element_type=F32)
            outs.append(out)
        for hh in heads:
            r = slice(hh * v_rows, (hh + 1) * v_rows)
            acc_scr[r, :] = acc_scr[r, :] * alphas[hh] + outs[hh]
        return seen, tuple(new_ms)

    init = (jnp.zeros((1, qs), F32), tuple(jnp.full((1, qs), NEG_BIG, F32) for _ in heads))
    lax.fori_loop(0, n_steps, attend, init)
    normed = [acc_scr[hh * v_rows:hh * v_rows + HEAD_DIM, :] / acc_scr[hh * v_rows + HEAD_DIM:hh * v_rows + HEAD_DIM + 1, :]
              for hh in heads]
    ya_ref[0] = jnp.concatenate(normed, axis=0).T.astype(BF16)


def _dsa_call(qit, wit, qat, ki, ka, vat, tri):
    b, l, _ = ki.shape
    qs = min(QS_DSA, l)
    k_sel = min(TOPK_MAX, l // 4)
    return pl.pallas_call(
        functools.partial(_dsa_kernel, k_sel=k_sel),
        grid=(b, l // qs),
        in_specs=[
            pl.BlockSpec((1, N_IDX_HEADS * IDX_DIM, qs), lambda bi, i: (bi, 0, i)),
            pl.BlockSpec((1, T_WI_ROWS, qs), lambda bi, i: (bi, 0, i)),
            pl.BlockSpec((1, A_WIDTH, qs), lambda bi, i: (bi, 0, i)),
            pl.BlockSpec((1, l, IDX_DIM), lambda bi, i: (bi, 0, 0), pipeline_mode=pl.Buffered(1)),
            pl.BlockSpec((1, l, HEAD_DIM), lambda bi, i: (bi, 0, 0), pipeline_mode=pl.Buffered(1)),
            pl.BlockSpec((1, l // VB_DSA, V_ROWS, VB_DSA), lambda bi, i: (bi, 0, 0, 0),
                         pipeline_mode=pl.Buffered(1)),
            pl.BlockSpec((QB, QB), lambda bi, i: (0, 0)),
        ],
        out_specs=pl.BlockSpec((1, qs, A_WIDTH), lambda bi, i: (bi, i, 0)),
        out_shape=jax.ShapeDtypeStruct((b, l, A_WIDTH), BF16),
        scratch_shapes=[pltpu.VMEM((l, qs), I32), pltpu.VMEM((l, qs), I16),
                        pltpu.VMEM((N_HEADS_A * V_ROWS, qs), F32)],
        compiler_params=_params(("arbitrary", "arbitrary")),
        name="dsa_attention",
    )(qit, wit, qat, ki, ka, vat, tri)


def _stick_kernel(qct_ref, kc_ref, vct_ref, later_ref, yc_ref, acc_scr):
    i = pl.program_id(1)
    qs = qct_ref.shape[2]
    diag_blocks = qs // QB
    qpos = i * qs + lax.broadcasted_iota(I32, (1, qs), 1)
    acc_scr[...] = jnp.zeros_like(acc_scr)

    def block(kb, tail, masked, lane0=0):
        r0 = pl.multiple_of(kb * QB, QB)
        kfull = kc_ref[0, pl.ds(r0, QB), :]
        vt = vct_ref[0, kb]
        heads = range(N_HEADS_C)
        old = [tail[hh][:, lane0:] for hh in heads]
        if masked:
            mask = (r0 + lax.broadcasted_iota(I32, (QB, qs - lane0), 0)) < qpos[:, lane0:]
        zs = [jnp.dot(kfull[:, (hh // 2) * LANES:(hh // 2 + 1) * LANES],
                      qct_ref[0, hh * LANES:(hh + 1) * LANES, lane0:], preferred_element_type=F32) for hh in heads]
        log_betas, splits, new_tail = [], [], []
        for hh in heads:
            z = zs[hh]
            log_beta = jnp.minimum(z, 0.0) - jnp.log(1.0 + jnp.exp(-jnp.abs(z)))
            log_keep = log_beta - z
            if masked:
                log_keep = jnp.where(mask, log_keep, 0.0)
            hi = log_keep.astype(BF16)
            lo = (log_keep - hi.astype(F32)).astype(BF16)
            log_betas.append(log_beta)
            splits.append(jnp.concatenate([hi, lo], axis=0))
            new = old[hh] + jnp.sum(log_keep, axis=0, keepdims=True)
            new_tail.append(new if lane0 == 0 else jnp.concatenate([tail[hh][:, :lane0], new], axis=1))
        withins = [jnp.dot(later_ref[...], splits[hh], preferred_element_type=F32) for hh in heads]
        weights = []
        for hh in heads:
            a = jnp.exp(log_betas[hh] + withins[hh] + old[hh])
            if masked:
                a = jnp.where(mask, a, 0.0)
            weights.append(a.astype(BF16))
        for hh in heads:
            r = slice(hh * HEAD_DIM, (hh + 1) * HEAD_DIM)
            acc_scr[r, lane0:] = acc_scr[r, lane0:] + jnp.dot(vt[r, :], weights[hh], preferred_element_type=F32)
        return tuple(new_tail)

    def largest(tail):
        worst = tail[0]
        for hh in range(1, N_HEADS_C):
            worst = jnp.maximum(worst, tail[hh])
        return jnp.max(worst)

    n_rest = i * diag_blocks
    tail = tuple(jnp.zeros((1, qs), F32) for _ in range(N_HEADS_C))
    for d in reversed(range(diag_blocks)):
        tail = block(n_rest + d, tail, True, lane0=d * QB)

    def live(cr):
        return jnp.logical_and(cr[0] < n_rest, cr[2] > STICK_DEAD)

    def step(cr):
        t, tl, _ = cr
        tl = block(n_rest - 1 - t, tl, False)
        return t + 1, tl, largest(tl)

    lax.while_loop(live, step, (jnp.int32(0), tail, largest(tail)))
    yc_ref[0] = acc_scr[...].T.astype(BF16)


def _stick_call(qct, kc, vct, later2):
    b, l, _ = kc.shape
    nq = l // QB
    qs = min(QS_STICK, l)
    return pl.pallas_call(
        _stick_kernel,
        grid=(b, l // qs),
        in_specs=[
            pl.BlockSpec((1, 2 * C_WIDTH, qs), lambda bi, i: (bi, 0, i)),
            pl.BlockSpec((1, l, C_WIDTH), lambda bi, i: (bi, 0, 0)),
            pl.BlockSpec((1, nq, C_WIDTH, QB), lambda bi, i: (bi, 0, 0, 0)),
            pl.BlockSpec((QB, 2 * QB), lambda bi, i: (0, 0)),
        ],
        out_specs=pl.BlockSpec((1, qs, C_WIDTH), lambda bi, i: (bi, i, 0)),
        out_shape=jax.ShapeDtypeStruct((b, l, C_WIDTH), BF16),
        scratch_shapes=[pltpu.VMEM((C_WIDTH, qs), F32)],
        compiler_params=_params(("arbitrary", "arbitrary")),
        name="stick_attention",
    )(qct, kc, vct, later2)


def _merge_kernel(x_ref, ya_ref, yc_ref, mb_ref, ga_ref, gc_ref, gt_ref, sh_ref, sc_ref, g1_ref, g2_ref,
                  wbra_ref, wbrc_ref, wout_ref, *rest, with_router):
    if with_router:
        wr_ref, x1_ref, h2_ref, route_ref, cnt_ref = rest
    else:
        x1_ref, h2_ref = rest
    merged = (ga_ref[0].astype(F32) * jnp.dot(ya_ref[0], wbra_ref[...], preferred_element_type=F32)
              + mb_ref[0].astype(F32)
              + gc_ref[0].astype(F32) * jnp.dot(yc_ref[0], wbrc_ref[...], preferred_element_type=F32))
    y = jnp.dot(merged.astype(BF16), wout_ref[...], preferred_element_type=F32)
    x1 = x_ref[0] + gt_ref[0] * _rms(y, g1_ref[...])
    x1_ref[0] = x1
    h2 = _rms(x1, g2_ref[...]) * (1.0 + sc_ref[0]) + sh_ref[0]
    h2_ref[0] = h2.astype(BF16)
    if with_router:
        logits = jnp.dot(h2, wr_ref[...], preferred_element_type=F32)
        lane = lax.broadcasted_iota(I32, logits.shape, 1)
        valid = lane < N_EXPERTS
        l1 = jnp.where(valid, logits, -jnp.inf)
        v1 = jnp.max(l1, axis=-1, keepdims=True)
        i1 = jnp.min(jnp.where(l1 == v1, lane, LANES), axis=-1, keepdims=True)
        l2 = jnp.where(lane == i1, -jnp.inf, l1)
        v2 = jnp.max(l2, axis=-1, keepdims=True)
        i2 = jnp.min(jnp.where(l2 == v2, lane, LANES), axis=-1, keepdims=True)
        e2 = jnp.exp(v2 - v1)
        p1 = 1.0 / (1.0 + e2)
        p2 = e2 / (1.0 + e2)
        route = jnp.where(lane == 0, i1.astype(F32), jnp.where(lane == 1, i2.astype(F32),
                          jnp.where(lane == 2, p1, jnp.where(lane == 3, p2, 0.0))))
        route_ref[0] = route
        chosen = jnp.where(jnp.logical_or(lane == i1, lane == i2), 1.0, 0.0)
        cnt_ref[0, 0] = jnp.broadcast_to(jnp.sum(chosen, axis=0, keepdims=True), (8, LANES))


def _merge_call(x, ya, yc, mb, ga, gc, gt, sh, sc, g1, g2, wbra, wbrc, wout, wr=None):
    b, l, d = x.shape
    tm = min(TM_MERGE if wr is None else TM_FFN, l)
    tok = lambda w: pl.BlockSpec((1, tm, w), lambda bi, i: (bi, i, 0))
    full = lambda a: pl.BlockSpec(a.shape, lambda bi, i: (0,) * a.ndim)
    vec = pl.BlockSpec((1, 1, d), lambda bi, i: (bi, 0, 0))
    in_specs = [tok(d), tok(A_WIDTH), tok(C_WIDTH), tok(d), tok(d), tok(d), vec, vec, vec,
                full(g1), full(g2), full(wbra), full(wbrc), full(wout)]
    args = [x, ya, yc, mb, ga, gc, gt, sh, sc, g1, g2, wbra, wbrc, wout]
    out_shape = [jax.ShapeDtypeStruct((b, l, d), F32), jax.ShapeDtypeStruct((b, l, d), BF16)]
    out_specs = [tok(d), tok(d)]
    if wr is not None:
        in_specs.append(full(wr))
        args.append(wr)
        out_shape += [jax.ShapeDtypeStruct((b, l, LANES), F32), jax.ShapeDtypeStruct((b, l // tm, 8, LANES), F32)]
        out_specs += [tok(LANES), pl.BlockSpec((1, 1, 8, LANES), lambda bi, i: (bi, i, 0, 0))]
    return pl.pallas_call(
        functools.partial(_merge_kernel, with_router=wr is not None),
        grid=(b, l // tm),
        in_specs=in_specs,
        out_specs=out_specs,
        out_shape=out_shape,
        compiler_params=_params(("arbitrary", "arbitrary")),
        name="merge_router" if wr is not None else "merge",
    )(*args)


def _ffn_kernel(x1_ref, h2_ref, gt_ref, g3_ref, wg_ref, wu_ref, wd_ref, o_ref):
    h2 = h2_ref[0]
    gate = jnp.dot(h2, wg_ref[...], preferred_element_type=F32)
    up = jnp.dot(h2, wu_ref[...], preferred_element_type=F32)
    act = (gate * _sigmoid(gate) * up).astype(BF16)
    y = jnp.dot(act, wd_ref[...], preferred_element_type=F32)
    o_ref[0] = x1_ref[0] + gt_ref[0] * _rms(y, g3_ref[...])


def _ffn_call(x1, h2, gt, g3, wg, wu, wd):
    b, l, d = x1.shape
    tm = min(TM_FFN, l)
    tok = pl.BlockSpec((1, tm, d), lambda bi, i: (bi, i, 0))
    full = lambda a: pl.BlockSpec(a.shape, lambda bi, i: (0,) * a.ndim)
    vec = pl.BlockSpec((1, 1, d), lambda bi, i: (bi, 0, 0))
    return pl.pallas_call(
        _ffn_kernel,
        grid=(b, l // tm),
        in_specs=[tok, tok, vec, full(g3), full(wg), full(wu), full(wd)],
        out_specs=tok,
        out_shape=jax.ShapeDtypeStruct((b, l, d), F32),
        compiler_params=_params(("arbitrary", "arbitrary")),
        name="ffn_dense",
    )(x1, h2, gt, g3, wg, wu, wd)


def _moe_kernel(nchunk_ref, first_ref, total_ref,
                x1_ref, h2_ref, route_ref, before_ref, gt_ref, g3_ref, wg_ref, wu_ref, wd_ref, o_ref,
                xs_scr, wrow_scr, dest_scr, acc_scr):
    tile = pl.program_id(0) * pl.num_programs(1) + pl.program_id(1)
    e = pl.program_id(2)
    n_e = pl.num_programs(2)
    tm = h2_ref.shape[1]
    total = total_ref[tile]

    @pl.when(e == 0)
    def _():
        route = route_ref[0]
        lane = lax.broadcasted_iota(I32, route.shape, 1).astype(F32)
        hot = [jnp.where(lane == route[:, s:s + 1], 1.0, 0.0) for s in range(TOP_K)]
        earlier = jnp.dot(before_ref[...], (hot[0] + hot[1]).astype(BF16), preferred_element_type=F32)
        start = jnp.zeros((1, LANES), F32)
        for ee in range(N_EXPERTS):
            start = jnp.where(lane[:1, :] == ee, (first_ref[tile * N_EXPERTS + ee] * ROW_CHUNK).astype(F32), start)
        where_to = earlier + start
        dest = [hot[s] * where_to for s in range(TOP_K)]
        prob = [hot[s] * route[:, TOP_K + s:TOP_K + s + 1] for s in range(TOP_K)]
        for s in range(TOP_K):
            dest_scr[s] = jnp.broadcast_to(jnp.sum(dest[s], axis=1, keepdims=True), (tm, LANES))
        dest_row = [jnp.sum(dest[s].T, axis=0, keepdims=True) for s in range(TOP_K)]
        prob_row = [jnp.sum(prob[s].T, axis=0, keepdims=True) for s in range(TOP_K)]
        h2 = h2_ref[0]

        def place(c, carry):
            r0 = pl.multiple_of(c * ROW_CHUNK, ROW_CHUNK)
            rows = (r0 + lax.broadcasted_iota(I32, (ROW_CHUNK, tm), 0)).astype(F32)
            here = [rows == dest_row[s] for s in range(TOP_K)]
            pick = jnp.where(jnp.logical_or(here[0], here[1]), 1.0, 0.0).astype(BF16)
            xs_scr[pl.ds(r0, ROW_CHUNK), :] = jnp.dot(pick, h2, preferred_element_type=F32).astype(BF16)
            weight = jnp.where(here[0], prob_row[0], 0.0) + jnp.where(here[1], prob_row[1], 0.0)
            wrow_scr[pl.ds(r0, ROW_CHUNK), :] = jnp.broadcast_to(jnp.sum(weight, axis=1, keepdims=True),
                                                                  (ROW_CHUNK, LANES))
            return carry

        lax.fori_loop(0, total + total % 2, place, 0)

    def expert_chunk(c, carry):
        r0 = pl.multiple_of((first_ref[tile * n_e + e] + c) * ROW_CHUNK, ROW_CHUNK)
        xs = xs_scr[pl.ds(r0, ROW_CHUNK), :]
        gate = jnp.dot(xs, wg_ref[0], preferred_element_type=F32)
        up = jnp.dot(xs, wu_ref[0], preferred_element_type=F32)
        act = (gate * _sigmoid(gate) * up).astype(BF16)
        y = jnp.dot(act, wd_ref[0], preferred_element_type=F32)
        xs_scr[pl.ds(r0, ROW_CHUNK), :] = (y * wrow_scr[pl.ds(r0, ROW_CHUNK), 0:1]).astype(BF16)
        return carry

    lax.fori_loop(0, nchunk_ref[tile * n_e + e], expert_chunk, 0)

    @pl.when(e == n_e - 1)
    def _():
        acc_scr[...] = jnp.zeros_like(acc_scr)

        def collect(c, carry):
            r0 = pl.multiple_of(c * 2 * ROW_CHUNK, 2 * ROW_CHUNK)
            halves = []
            for j in range(2):
                cols = (r0 + j * ROW_CHUNK + lax.broadcasted_iota(I32, (tm, ROW_CHUNK), 1)).astype(F32)
                mine = jnp.logical_or(cols == dest_scr[0], cols == dest_scr[1])
                halves.append(jnp.where(mine, 1.0, 0.0).astype(BF16))
            acc_scr[...] = acc_scr[...] + jnp.dot(jnp.concatenate(halves, axis=1),
                                                  xs_scr[pl.ds(r0, 2 * ROW_CHUNK), :], preferred_element_type=F32)
            return carry

        lax.fori_loop(0, (total + 1) // 2, collect, 0)
        o_ref[0] = x1_ref[0] + gt_ref[0] * _rms(acc_scr[...], g3_ref[...])


def _moe_call(x1, h2, route, counts, before, gt, g3, wg, wu, wd):
    b, l, d = x1.shape
    tm = before.shape[0]
    n_e, _, ff = wg.shape
    cnt = counts[:, :, 0, :n_e].astype(I32).reshape(b * (l // tm), -1, n_e).sum(axis=1)
    nchunk = (cnt + ROW_CHUNK - 1) // ROW_CHUNK
    first = jnp.cumsum(nchunk, axis=1) - nchunk
    total = jnp.sum(nchunk, axis=1)
    rows = TOP_K * tm + n_e * ROW_CHUNK
    tok = lambda w: pl.BlockSpec((1, tm, w), lambda bi, i, e, *_: (bi, i, 0))
    tok_once = lambda w: pl.BlockSpec((1, tm, w), lambda bi, i, e, *_: (bi, i, 0), pipeline_mode=pl.Buffered(1))
    vec = pl.BlockSpec((1, 1, d), lambda bi, i, e, *_: (bi, 0, 0))
    grid_spec = pltpu.PrefetchScalarGridSpec(
        num_scalar_prefetch=3,
        grid=(b, l // tm, n_e),
        in_specs=[tok_once(d), tok_once(d), tok(LANES),
                  pl.BlockSpec(before.shape, lambda bi, i, e, *_: (0, 0), pipeline_mode=pl.Buffered(1)), vec,
                  pl.BlockSpec(g3.shape, lambda bi, i, e, *_: (0, 0)),
                  pl.BlockSpec((1, d, ff), lambda bi, i, e, *_: (e, 0, 0)),
                  pl.BlockSpec((1, d, ff), lambda bi, i, e, *_: (e, 0, 0)),
                  pl.BlockSpec((1, ff, d), lambda bi, i, e, *_: (e, 0, 0))],
        out_specs=tok(d),
        scratch_shapes=[pltpu.VMEM((rows, d), BF16), pltpu.VMEM((rows, LANES), F32),
                        pltpu.VMEM((TOP_K, tm, LANES), F32), pltpu.VMEM((tm, d), F32)],
    )
    return pl.pallas_call(
        _moe_kernel,
        grid_spec=grid_spec,
        out_shape=jax.ShapeDtypeStruct((b, l, d), F32),
        compiler_params=_params(("arbitrary", "arbitrary", "arbitrary")),
        name="ffn_moe",
    )(nchunk.reshape(-1), first.reshape(-1), total, x1, h2, route, before, gt, g3, wg, wu, wd)


def _rot_cols(w, n_heads, head_dim):
    d = w.shape[0]
    w = w.reshape(d, n_heads, head_dim)
    half = ROPE_DIM // 2
    rot = jnp.concatenate([-w[..., half:ROPE_DIM], w[..., :half],
                           jnp.zeros((d, n_heads, head_dim - ROPE_DIM), w.dtype)], axis=-1)
    return rot.reshape(d, n_heads * head_dim)


def _pad_cols(w, width):
    return jnp.pad(w, ((0, 0), (0, width - w.shape[1])))


def _layer_weights(w_in, w_uk, w_uv, w_pool):
    d = w_in.shape[0]
    offs, o = [], 0
    for s in IN_SIZES:
        offs.append(o)
        o += s
    piece = lambda k: w_in[:, offs[k]:offs[k] + IN_SIZES[k]]
    w_qa, w_lat, w_kr, w_qi, w_ki, w_wi, w_up, w_qc, w_kc, w_vc, w_gate = [piece(k) for k in range(len(IN_SIZES))]
    attn_scale = HEAD_DIM ** -0.5
    idx_scale = IDX_DIM ** -0.5 * N_IDX_HEADS ** -0.5
    wrow = jnp.concatenate([
        w_lat,
        _pad_cols(w_kr, LANES), _pad_cols(_rot_cols(w_kr, 1, ROPE_DIM), LANES),
        _pad_cols(w_ki, LANES), _pad_cols(_rot_cols(w_ki, 1, IDX_DIM), LANES),
        w_up, w_kc, w_gate], axis=1).astype(BF16)
    w_qc_t = (w_qc * attn_scale).T.reshape(N_HEADS_C, HEAD_DIM, d)
    zeros = jnp.zeros_like(w_qc_t)
    even = (jnp.arange(N_HEADS_C) % 2 == 0)[:, None, None]
    w_qc_pad = jnp.concatenate([jnp.where(even, w_qc_t, zeros), jnp.where(even, zeros, w_qc_t)], axis=1)
    w_qc_pad = w_qc_pad.reshape(2 * C_WIDTH, d)
    softmax_scale = attn_scale * LOG2_E
    wt = jnp.concatenate([
        (w_qa * softmax_scale).T, (_rot_cols(w_qa, N_HEADS_A, HEAD_DIM) * softmax_scale).T,
        w_qi.T, _rot_cols(w_qi, N_IDX_HEADS, IDX_DIM).T,
        w_qc_pad, w_vc.T,
        jnp.pad((w_wi * idx_scale).T, ((0, T_WI_ROWS - N_IDX_HEADS), (0, 0)))], axis=0).astype(BF16)
    wuk = jnp.zeros((KV_LATENT, LANES), F32).at[:, ROPE_DIM:HEAD_DIM].set(w_uk).astype(BF16)
    wuvt = w_uv.T.astype(BF16)
    wpool = jnp.zeros((POOL_WIDTH, POOL_WIDTH), F32)
    for g in range(N_POOL_GROUPS):
        sl = slice(g * POOL_GROUP_DIM, (g + 1) * POOL_GROUP_DIM)
        wpool = wpool.at[sl, sl].set(w_pool[g])
    return wrow, wt, wuk, wuvt, wpool.astype(BF16)


def _rope_tables(positions):
    inv = ROPE_THETA ** (-jnp.arange(0, ROPE_DIM, 2, dtype=F32) / ROPE_DIM)
    ang = positions.astype(F32)[..., None] * inv
    cos, sin = jnp.cos(ang), jnp.sin(ang)
    b, l = positions.shape
    ones = jnp.ones((b, l, HEAD_DIM - ROPE_DIM), F32)
    cos_h = jnp.concatenate([cos, cos, ones], axis=-1)
    sin_h = jnp.concatenate([sin, sin, jnp.zeros_like(ones)], axis=-1)
    cos_r = jnp.concatenate([cos_h, cos_h], axis=-1)
    sin_r = jnp.concatenate([sin_h, sin_h], axis=-1)
    return cos_r, sin_r, cos_h.transpose(0, 2, 1), sin_h.transpose(0, 2, 1)


def kernel(x, c, positions, w_ada, b_ada, norm_gains, w_in, g_kv_latent, w_uk, w_uv, w_pool, pool_scale,
           w_br_a, w_br_b, w_br_c, w_out, w_gate_dense, w_up_dense, w_down_dense,
           w_router, w_gate_moe, w_up_moe, w_down_moe):
    b, l, d = x.shape
    depth = w_in.shape[0]
    assert d == D_MODEL and l % max(KC, TM_FFN, TM_MOE) == 0
    cos_r, sin_r, cos_t, sin_t = _rope_tables(positions)
    c_pad = jnp.pad(c, ((0, 8 - b), (0, 0)))
    mod = _mod_call(c_pad, w_ada, b_ada)[:, :b]
    idx = lax.broadcasted_iota(I32, (QB, QB), 0)
    jdx = lax.broadcasted_iota(I32, (QB, QB), 1)
    tri = (jdx <= idx).astype(BF16)
    later = (jdx > idx).astype(BF16)
    later2 = jnp.concatenate([later, later], axis=1)
    tm_moe = min(TM_MOE, l)
    before = (lax.broadcasted_iota(I32, (tm_moe, tm_moe), 1)
              < lax.broadcasted_iota(I32, (tm_moe, tm_moe), 0)).astype(BF16)
    for layer in range(depth):
        sh1, sc1, gt1, sh2, sc2, gt2 = [m.reshape(b, 1, d) for m in jnp.split(mod[layer], 6, axis=-1)]
        gains = norm_gains[layer].reshape(4, 1, d)
        wrow, wt, wuk, wuvt, wpool = _layer_weights(w_in[layer], w_uk[layer], w_uv[layer], w_pool[layer])
        (qat, qit, wit, ka, vat, ki, qct, kc, vct, ga, gc, mb) = _inproj_call(
            x, sh1, sc1, gains[0], wrow, wt, cos_r, sin_r, cos_t, sin_t,
            g_kv_latent[layer].reshape(1, KV_LATENT), wuk, wuvt, wpool,
            pool_scale[layer].reshape(1, POOL_WIDTH), w_br_b[layer].astype(BF16))
        ya = _dsa_call(qit, wit, qat, ki, ka, vat, tri)
        yc = _stick_call(qct, kc, vct, later2)
        i = layer // 2
        merge_args = (x, ya, yc, mb, ga, gc, gt1, sh2, sc2, gains[1], gains[2],
                      w_br_a[layer].astype(BF16), w_br_c[layer].astype(BF16), w_out[layer].astype(BF16))
        if layer % 2 == 0:
            x1, h2 = _merge_call(*merge_args)
            x = _ffn_call(x1, h2, gt2, gains[3], w_gate_dense[i].astype(BF16), w_up_dense[i].astype(BF16),
                          w_down_dense[i].astype(BF16))
        else:
            x1, h2, route, counts = _merge_call(*merge_args, wr=_pad_cols(w_router[i], LANES))
            x = _moe_call(x1, h2, route, counts, before, gt2, gains[3], w_gate_moe[i].astype(BF16),
                          w_up_moe[i].astype(BF16), w_down_moe[i].astype(BF16))
    return x
```

```python
import functools

import jax
import jax.numpy as jnp
from jax import lax
from jax.experimental import pallas as pl
from jax.experimental.pallas import tpu as pltpu

F32 = jnp.float32
BF16 = jnp.bfloat16
I32 = jnp.int32
I16 = jnp.int16

D_MODEL = 1024
HEAD_DIM = 64
ROPE_DIM = HEAD_DIM // 4
ROPE_THETA = 500000.0
N_HEADS_A = (3 * D_MODEL // 8) // HEAD_DIM
A_WIDTH = N_HEADS_A * HEAD_DIM
KV_LATENT = D_MODEL // 8
N_IDX_HEADS = 4
IDX_DIM = 64
TOPK_MAX = 256
N_POOL_GROUPS = 4
POOL_WINDOWS = (2, 4, 8, 16)
POOL_WIDTH = D_MODEL // 4
POOL_GROUP_DIM = POOL_WIDTH // N_POOL_GROUPS
N_HEADS_C = (D_MODEL // 4) // HEAD_DIM
C_WIDTH = N_HEADS_C * HEAD_DIM
N_BRANCHES = 3
IN_SIZES = (A_WIDTH, KV_LATENT, ROPE_DIM, N_IDX_HEADS * IDX_DIM, IDX_DIM, N_IDX_HEADS,
            POOL_WIDTH, C_WIDTH, C_WIDTH, C_WIDTH, N_BRANCHES * D_MODEL)
D_FF = 2816
N_EXPERTS = 8
TOP_K = 2
D_FF_EXPERT = D_FF // TOP_K
RMS_EPS = 1e-6

LANES = 128
QB = 128
KC = 512
QS_STICK = 512
QS_DSA = 256
KB_DSA = 512
VB_DSA = 256
V_ROWS = 80
STICK_DEAD = -106.0
TM_IN = 512
TM_MERGE = 512
TM_FFN = 512
TM_MOE = 1024
ROW_CHUNK = 128
POOL_HALO = 16
VMEM_LIMIT = 56 * 1024 * 1024
INT_MIN = -2147483648
LOG2_E = 1.4426950408889634
NEG_BIG = -1e30
NEG_MASK = -2e30

R_LAT, R_KR, R_KRR, R_KI, R_KIR, R_UP, R_KC, R_GATE = 0, 128, 256, 384, 512, 640, 896, 1152
T_QA, T_QI, T_QC, T_VC, T_WI = 0, 384, 640, 1152, 1408
T_WI_ROWS = 16


def _params(sem):
    return pltpu.CompilerParams(dimension_semantics=sem, vmem_limit_bytes=VMEM_LIMIT)


def _sigmoid(v):
    return 1.0 / (1.0 + jnp.exp(-v))


def _rms(v, gain):
    return v * lax.rsqrt(jnp.mean(v * v, axis=-1, keepdims=True) + RMS_EPS) * gain


def _mod_kernel(c_ref, w_ref, b_ref, o_ref):
    c = c_ref[...]
    cond = c * _sigmoid(c)
    o_ref[0] = jnp.dot(cond, w_ref[0], preferred_element_type=F32) + b_ref[0]


def _mod_call(c_pad, w_ada, b_ada):
    depth, d, n = w_ada.shape
    tn = 1024
    return pl.pallas_call(
        _mod_kernel,
        grid=(depth, n // tn),
        in_specs=[
            pl.BlockSpec((8, d), lambda l, j: (0, 0)),
            pl.BlockSpec((1, d, tn), lambda l, j: (l, 0, j)),
            pl.BlockSpec((1, 1, tn), lambda l, j: (l, 0, j)),
        ],
        out_specs=pl.BlockSpec((1, 8, tn), lambda l, j: (l, 0, j)),
        out_shape=jax.ShapeDtypeStruct((depth, 8, n), F32),
        compiler_params=_params(("arbitrary", "arbitrary")),
        name="adaln_mod",
    )(c_pad, w_ada, b_ada.reshape(depth, 1, n))


def _inproj_kernel(x_ref, sh_ref, sc_ref, g_ref, wrow_ref, wt_ref, c_ref, s_ref, ct_ref, st_ref,
                   glat_ref, wuk_ref, wuvt_ref, wpool_ref, pscale_ref, wbrb_ref,
                   qat_ref, qit_ref, wit_ref, ka_ref, vat_ref, ki_ref, qct_ref, kc_ref, vct_ref,
                   ga_ref, gc_ref, mb_ref,
                   h_scr, ht_scr, ext_scr, prev_scr):
    i = pl.program_id(1)
    tm = x_ref.shape[1]
    x = x_ref[0]
    h = _rms(x, g_ref[...]) * (1.0 + sc_ref[0]) + sh_ref[0]
    h_scr[...] = h.astype(BF16)
    ht_scr[...] = h.T.astype(BF16)

    def rowdot(a, width):
        return jnp.dot(h_scr[...], wrow_ref[:, a:a + width], preferred_element_type=F32)

    def tdot(a, height):
        return jnp.dot(wt_ref[a:a + height, :], ht_scr[...], preferred_element_type=F32)

    cos_r, sin_r = c_ref[0], s_ref[0]
    cos_t, sin_t = ct_ref[0], st_ref[0]

    latn = _rms(rowdot(R_LAT, KV_LATENT), glat_ref[...])
    ka = (rowdot(R_KR, LANES) * cos_r + rowdot(R_KRR, LANES) * sin_r
          + jnp.dot(latn.astype(BF16), wuk_ref[...], preferred_element_type=F32))
    ka_ref[0] = ka[:, :HEAD_DIM].astype(BF16)
    vat = jnp.dot(wuvt_ref[...], latn.T.astype(BF16), preferred_element_type=F32)
    pad_row = lax.broadcasted_iota(I32, (V_ROWS - HEAD_DIM, tm), 0)
    vat = jnp.concatenate([vat, jnp.where(pad_row == 0, 1.0, 0.0)], axis=0)
    for j in range(tm // VB_DSA):
        vat_ref[0, j] = vat[:, j * VB_DSA:(j + 1) * VB_DSA].astype(BF16)

    ki =rowdot(R_KI, LANES) * cos_r + rowdot(R_KIR, LANES) * sin_r
    ki_ref[0] = ki[:, :IDX_DIM].astype(BF16)

    half = ROPE_DIM // 2
    cos_f, sin_f = cos_t[:half], sin_t[:half]

    def rope_t(q):
        x1, x2 = q[:half], q[half:ROPE_DIM]
        return jnp.concatenate([x1 * cos_f - x2 * sin_f, x2 * cos_f + x1 * sin_f, q[ROPE_DIM:]], axis=0)

    qa = tdot(T_QA, A_WIDTH)
    for hh in range(N_HEADS_A):
        r = slice(hh * HEAD_DIM, (hh + 1) * HEAD_DIM)
        qat_ref[0, r, :] = rope_t(qa[r]).astype(BF16)
    qi = tdot(T_QI, N_IDX_HEADS * IDX_DIM)
    for hh in range(N_IDX_HEADS):
        r = slice(hh * IDX_DIM, (hh + 1) * IDX_DIM)
        qit_ref[0, r, :] = rope_t(qi[r]).astype(BF16)
    wit_ref[0] = tdot(T_WI, T_WI_ROWS)

    qct_ref[0] = tdot(T_QC, 2 * C_WIDTH).astype(BF16)
    kc_ref[0] = rowdot(R_KC, C_WIDTH).astype(BF16)
    vct = tdot(T_VC, C_WIDTH)
    for j in range(tm // QB):
        vct_ref[0, j] = vct[:, j * QB:(j + 1) * QB].astype(BF16)

    up = rowdot(R_UP, POOL_WIDTH)

    @pl.when(i == 0)
    def _():
        prev_scr[...] = jnp.zeros_like(prev_scr)

    ext_scr[0:POOL_HALO, :] = prev_scr[...]
    ext_scr[POOL_HALO:POOL_HALO + tm, :] = up
    prev_scr[...] = up[tm - POOL_HALO:, :]
    lag = [ext_scr[POOL_HALO - j:POOL_HALO - j + tm, :] for j in range(POOL_HALO)]
    sums = {}
    run = lag[0]
    for j in range(1, POOL_HALO):
        run = run + lag[j]
        if j + 1 in POOL_WINDOWS:
            sums[j + 1] = run
    lane = lax.broadcasted_iota(I32, (tm, POOL_WIDTH), 1)
    pos = i * tm + lax.broadcasted_iota(I32, (tm, POOL_WIDTH), 0)
    pooled_sum = sums[POOL_WINDOWS[-1]]
    win = jnp.full((tm, POOL_WIDTH), POOL_WINDOWS[-1], I32)
    for g in range(N_POOL_GROUPS - 2, -1, -1):
        in_group = lane < (g + 1) * POOL_GROUP_DIM
        pooled_sum = jnp.where(in_group, sums[POOL_WINDOWS[g]], pooled_sum)
        win = jnp.where(in_group, POOL_WINDOWS[g], win)
    cnt = jnp.minimum(pos + 1, win).astype(F32)
    pooled = pooled_sum / cnt - up
    yb = jnp.dot(pooled.astype(BF16), wpool_ref[...], preferred_element_type=F32) * pscale_ref[...]

    ga_ref[0] = _sigmoid(rowdot(R_GATE, D_MODEL)).astype(BF16)
    gb = _sigmoid(rowdot(R_GATE + D_MODEL, D_MODEL))
    mb_ref[0] = (gb * jnp.dot(yb.astype(BF16), wbrb_ref[...], preferred_element_type=F32)).astype(BF16)
    gc_ref[0] = _sigmoid(rowdot(R_GATE + 2 * D_MODEL, D_MODEL)).astype(BF16)


def _inproj_call(x, sh, sc, gain, wrow, wt, cos_r, sin_r, cos_t, sin_t, glat, wuk, wuvt, wpool, pscale, wbrb):
    b, l, d = x.shape
    tm = min(TM_IN, l)
    nq = l // QB
    tok = lambda w: pl.BlockSpec((1, tm, w), lambda bi, i: (bi, i, 0))
    feat = lambda hgt: pl.BlockSpec((1, hgt, tm), lambda bi, i: (bi, 0, i))
    blk = lambda hgt, w: pl.BlockSpec((1, tm // w, hgt, w), lambda bi, i: (bi, i, 0, 0))
    full = lambda a: pl.BlockSpec(a.shape, lambda bi, i: (0,) * a.ndim)
    vec = pl.BlockSpec((1, 1, d), lambda bi, i: (bi, 0, 0))
    out_shape = (
        jax.ShapeDtypeStruct((b, A_WIDTH, l), BF16),
        jax.ShapeDtypeStruct((b, N_IDX_HEADS * IDX_DIM, l), BF16),
        jax.ShapeDtypeStruct((b, T_WI_ROWS, l), F32),
        jax.ShapeDtypeStruct((b, l, HEAD_DIM), BF16),
        jax.ShapeDtypeStruct((b, l // VB_DSA, V_ROWS, VB_DSA), BF16),
        jax.ShapeDtypeStruct((b, l, IDX_DIM), BF16),
        jax.ShapeDtypeStruct((b, 2 * C_WIDTH, l), BF16),
        jax.ShapeDtypeStruct((b, l, C_WIDTH), BF16),
        jax.ShapeDtypeStruct((b, nq, C_WIDTH, QB), BF16),
        jax.ShapeDtypeStruct((b, l, d), BF16),
        jax.ShapeDtypeStruct((b, l, d), BF16),
        jax.ShapeDtypeStruct((b, l, d), BF16),
    )
    out_specs = (feat(A_WIDTH), feat(N_IDX_HEADS * IDX_DIM), feat(T_WI_ROWS), tok(HEAD_DIM), blk(V_ROWS, VB_DSA),
                 tok(IDX_DIM), feat(2 * C_WIDTH), tok(C_WIDTH), blk(C_WIDTH, QB), tok(d), tok(d), tok(d))
    resident = lambda a: pl.BlockSpec(a.shape, lambda bi, i: (0,) * a.ndim, pipeline_mode=pl.Buffered(1))
    in_specs = [tok(d), vec, vec, full(gain), resident(wrow), resident(wt), tok(LANES), tok(LANES),
                feat(HEAD_DIM), feat(HEAD_DIM), full(glat), full(wuk), full(wuvt), full(wpool),
                full(pscale), full(wbrb)]
    return pl.pallas_call(
        _inproj_kernel,
        grid=(b, l // tm),
        in_specs=in_specs,
        out_specs=out_specs,
        out_shape=out_shape,
        scratch_shapes=[pltpu.VMEM((tm, d), BF16), pltpu.VMEM((d, tm), BF16),
                        pltpu.VMEM((tm + POOL_HALO, POOL_WIDTH), F32), pltpu.VMEM((POOL_HALO, POOL_WIDTH), F32)],
        compiler_params=_params(("arbitrary", "arbitrary")),
        name="inproj",
    )(x, sh, sc, gain, wrow, wt, cos_r, sin_r, cos_t, sin_t, glat, wuk, wuvt, wpool, pscale, wbrb)


def _dsa_kernel(qit_ref, wit_ref, qat_ref, ki_ref, ka_ref, vat_ref, tri_ref, ya_ref,
                keys_scr, top_scr, acc_scr, *, k_sel):
    i = pl.program_id(1)
    qs = qat_ref.shape[2]
    n_chunks = ((i + 1) * qs + KC - 1) // KC
    qpos = i * qs + lax.broadcasted_iota(I32, (1, qs), 1)
    w_idx = wit_ref[0]

    def score_chunk(c, carry, masked):
        r0 = pl.multiple_of(c * KC, KC)
        kblk = ki_ref[0, pl.ds(r0, KC), :]
        parts = [jnp.dot(kblk, qit_ref[0, hh * IDX_DIM:(hh + 1) * IDX_DIM, :], preferred_element_type=F32)
                 for hh in range(N_IDX_HEADS)]
        score = jnp.maximum(parts[0], 0.0) * w_idx[0:1, :]
        for hh in range(1, N_IDX_HEADS):
            score = score + jnp.maximum(parts[hh], 0.0) * w_idx[hh:hh + 1, :]
        bits = lax.bitcast_convert_type(score, I32)
        key = jnp.where(bits < 0, INT_MIN - bits, bits)
        if masked:
            kpos = r0 + lax.broadcasted_iota(I32, (KC, qs), 0)
            key = jnp.where(kpos <= qpos, key, INT_MIN)
        keys_scr[pl.ds(r0, KC), :] = key
        top = lax.shift_right_arithmetic(key, 16).astype(I16)
        top_scr[pl.ds(r0, KC), :] = top
        groups = top.reshape(KC // 16, 16, qs)
        for j in range(KC // 16):
            carry = jnp.where(groups[j] > carry, groups[j], carry)
        return carry

    n_open = (i * qs) // KC
    top_max = lax.fori_loop(0, n_open, functools.partial(score_chunk, masked=False),
                            jnp.full((16, qs), -2 ** 15, I16))
    top_max = lax.fori_loop(n_open, n_chunks, functools.partial(score_chunk, masked=True), top_max)
    top_max = jnp.max(top_max.astype(I32), axis=0, keepdims=True)

    def count_rows(src, rows_per_vreg, trial):
        n_acc = 4
        groups = KC // rows_per_vreg

        def body(c, accs):
            r0 = pl.multiple_of(c * KC, KC)
            rows = src[pl.ds(r0, KC), :].reshape(groups, rows_per_vreg, qs)
            accs = list(accs)
            for j in range(groups):
                a = accs[j % n_acc]
                accs[j % n_acc] = jnp.where(rows[j] >= trial, a + 1, a)
            return tuple(accs)

        zero = jnp.zeros((rows_per_vreg, qs), src.dtype)
        accs = lax.fori_loop(0, n_chunks, body, tuple(zero for _ in range(n_acc)))
        total = (accs[0] + accs[1]) + (accs[2] + accs[3])
        return jnp.sum(total.astype(I32), axis=0, keepdims=True)

    def count_ge(trial):
        return count_rows(keys_scr, 8, trial)

    def count_ge_top(trial):
        return count_rows(top_scr, 16, lax.shift_right_arithmetic(trial, 16).astype(I16))

    c_zero = count_ge_top(jnp.zeros((1, qs), I32))
    c_pos = count_ge(jnp.ones((1, qs), I32))
    tie_at_zero = jnp.logical_and(c_zero >= k_sel, c_pos < k_sel)

    def unsettled(c_ans):
        settled = jnp.logical_or(tie_at_zero, c_ans == k_sel)
        return jnp.max(jnp.where(settled, 0, 1))

    def search(counter, lowest_bit, group, state):
        def refine(carry):
            bit, ans, c_ans, _ = carry
            for g in range(group):
                trial = ans + lax.shift_left(jnp.int32(1), bit - g)
                c = counter(trial)
                ok = c >= k_sel
                c_ans = jnp.where(ok, c, c_ans)
                ans = jnp.where(ok, trial, ans)
            return bit - group, ans, c_ans, unsettled(c_ans)

        return lax.while_loop(lambda cr: jnp.logical_and(cr[0] >= lowest_bit, cr[3] > 0), refine, state)

    def count_within(gap):
        return count_rows(top_scr, 16, jnp.maximum(top_max - gap, -2 ** 15).astype(I16))

    near_bits = 9
    c_near = count_within(2 ** near_bits - 1)
    short = qpos + 1 < k_sel
    known = jnp.logical_or(tie_at_zero, short)
    all_near = jnp.min(jnp.where(jnp.logical_or(c_near >= k_sel, known), 1, 0)) > 0
    first_bit = jnp.where(all_near, near_bits - 1, 15)

    def widen(group, carry):
        bit, gap, c_gap = carry
        for g in range(group):
            trial = gap + lax.shift_left(jnp.int32(1), bit - g) - 1
            c = count_within(trial)
            ok = c >= k_sel
            gap = jnp.where(ok, gap, trial + 1)
            c_gap = jnp.where(ok, c, c_gap)
        return bit - group, gap, c_gap

    c_far = jnp.where(all_near, c_near, jnp.int32(2 ** 30))
    state = (first_bit, jnp.zeros((1, qs), I32), c_far)
    state = lax.while_loop(lambda cr: cr[0] >= near_bits, functools.partial(widen, 16 - near_bits), state)
    _, gap, c_top = lax.while_loop(lambda cr: cr[0] >= 0, functools.partial(widen, 3), state)
    bucket = jnp.where(short, -2 ** 15, jnp.where(tie_at_zero, 0, jnp.maximum(top_max - gap, -2 ** 15)))
    c_top = jnp.where(tie_at_zero, c_zero, c_top)
    top = lax.shift_left(bucket, 16)
    alive = unsettled(c_top)
    last_bucket = bucket >= 2 ** 15 - 1
    c_over = jnp.where(last_bucket, 0,
                       count_rows(top_scr, 16, jnp.where(last_bucket, bucket, bucket + 1).astype(I16)))

    def pack_low(c, carry):
        r0 = pl.multiple_of(c * KC, KC)
        low = (keys_scr[pl.ds(r0, KC), :] ^ 0x8000).astype(I16)
        top_scr[pl.ds(r0, KC), :] = jnp.where(top_scr[pl.ds(r0, KC), :] == bucket.astype(I16), low, -2 ** 15)
        return carry

    lax.fori_loop(0, n_chunks, pack_low, 0)

    def count_ge_low(low_trial):
        return c_over + count_rows(top_scr, 16, (low_trial - 2 ** 15).astype(I16))

    zero_row = jnp.zeros((1, qs), I32)
    _, low, _, _ = search(count_ge_low, 0, 4, (jnp.int32(15), zero_row, c_top, alive))
    thr = top + low
    full_low = low >= 2 ** 16 - 1
    n_above = jnp.where(full_low, c_over, count_ge_low(jnp.where(full_low, low, low + 1)))
    n_ties = jnp.where(thr == INT_MIN, 0, k_sel - n_above).astype(F32)

    acc_scr[...] = jnp.zeros_like(acc_scr)
    heads = range(N_HEADS_A)

    kb_rows = KB_DSA
    vb_rows = vat_ref.shape[3]
    v_per_k = kb_rows // vb_rows

    n_steps = ((i + 1) * qs + kb_rows - 1) // kb_rows

    v_rows = vat_ref.shape[2]

    def attend(kb, carry):
        seen, ms = carry
        r0 = pl.multiple_of(kb * kb_rows, kb_rows)
        keyb = keys_scr[pl.ds(r0, kb_rows), :]
        tied = keyb == thr
        tied_b = jnp.where(tied, 1.0, 0.0).astype(BF16)
        ranks = []
        for j in range(kb_rows // QB):
            within = jnp.dot(tri_ref[...], tied_b[j * QB:(j + 1) * QB, :], preferred_element_type=F32)
            ranks.append(seen + within)
            seen = seen + within[QB - 1:QB, :]
        rank = jnp.concatenate(ranks, axis=0)
        keep = jnp.logical_or(keyb > thr, jnp.logical_and(tied, rank <= n_ties))
        bias = jnp.where(keep, 0.0, NEG_MASK).astype(BF16)
        kblk = ka_ref[0, pl.ds(r0, kb_rows), :]
        logits = [jnp.dot(kblk, qat_ref[0, hh * HEAD_DIM:(hh + 1) * HEAD_DIM, :],
                          preferred_element_type=F32).astype(BF16) + bias for hh in heads]
        new_ms = [jnp.maximum(ms[hh], jnp.max(logits[hh], axis=0, keepdims=True).astype(F32)) for hh in heads]
        probs = [jnp.exp2(logits[hh] - new_ms[hh].astype(BF16)) for hh in heads]
        alphas = [jnp.exp2(ms[hh] - new_ms[hh]) for hh in heads]
        outs = []
        for hh in heads:
            out = jnp.dot(vat_ref[0, kb * v_per_k], probs[hh][:vb_rows, :], preferred_element_type=F32)
            for j in range(1, v_per_k):
                out = out + jnp.dot(vat_ref[0, kb * v_per_k + j], probs[hh][j * vb_rows:(j + 1) * vb_rows, :],
                                    preferred_element_type=F32)
            outs.append(out)
        for hh in heads:
            r = slice(hh * v_rows, (hh + 1) * v_rows)
            acc_scr[r, :] = acc_scr[r, :] * alphas[hh] + outs[hh]
        return seen, tuple(new_ms)

    init = (jnp.zeros((1, qs), F32), tuple(jnp.full((1, qs), NEG_BIG, F32) for _ in heads))
    lax.fori_loop(0, n_steps, attend, init)
    normed = [acc_scr[hh * v_rows:hh * v_rows + HEAD_DIM, :] / acc_scr[hh * v_rows + HEAD_DIM:hh * v_rows + HEAD_DIM + 1, :]
              for hh in heads]
    ya_ref[0] = jnp.concatenate(normed, axis=0).T.astype(BF16)


def _dsa_call(qit, wit, qat, ki, ka, vat, tri):
    b, l, _ = ki.shape
    qs = min(QS_DSA, l)
    k_sel = min(TOPK_MAX, l // 4)
    return pl.pallas_call(
        functools.partial(_dsa_kernel, k_sel=k_sel),
        grid=(b, l // qs),
        in_specs=[
            pl.BlockSpec((1, N_IDX_HEADS * IDX_DIM, qs), lambda bi, i: (bi, 0, i)),
            pl.BlockSpec((1, T_WI_ROWS, qs), lambda bi, i: (bi, 0, i)),
            pl.BlockSpec((1, A_WIDTH, qs), lambda bi, i: (bi, 0, i)),
            pl.BlockSpec((1, l, IDX_DIM), lambda bi, i: (bi, 0, 0), pipeline_mode=pl.Buffered(1)),
            pl.BlockSpec((1, l, HEAD_DIM), lambda bi, i: (bi, 0, 0), pipeline_mode=pl.Buffered(1)),
            pl.BlockSpec((1, l // VB_DSA, V_ROWS, VB_DSA), lambda bi, i: (bi, 0, 0, 0),
                         pipeline_mode=pl.Buffered(1)),
            pl.BlockSpec((QB, QB), lambda bi, i: (0, 0)),
        ],
        out_specs=pl.BlockSpec((1, qs, A_WIDTH), lambda bi, i: (bi, i, 0)),
        out_shape=jax.ShapeDtypeStruct((b, l, A_WIDTH), BF16),
        scratch_shapes=[pltpu.VMEM((l, qs), I32), pltpu.VMEM((l, qs), I16),
                        pltpu.VMEM((N_HEADS_A * V_ROWS, qs), F32)],
        compiler_params=_params(("arbitrary", "arbitrary")),
        name="dsa_attention",
    )(qit, wit, qat, ki, ka, vat, tri)


def _stick_kernel(qct_ref, kc_ref, vct_ref, later_ref, yc_ref, acc_scr):
    i = pl.program_id(1)
    qs = qct_ref.shape[2]
    diag_blocks = qs // QB
    qpos = i * qs + lax.broadcasted_iota(I32, (1, qs), 1)
    acc_scr[...] = jnp.zeros_like(acc_scr)

    def block(kb, tail, masked, lane0=0):
        r0 = pl.multiple_of(kb * QB, QB)
        kfull = kc_ref[0, pl.ds(r0, QB), :]
        vt = vct_ref[0, kb]
        heads = range(N_HEADS_C)
        old = [tail[hh][:, lane0:] for hh in heads]
        if masked:
            mask = (r0 + lax.broadcasted_iota(I32, (QB, qs - lane0), 0)) < qpos[:, lane0:]
        zs = [jnp.dot(kfull[:, (hh // 2) * LANES:(hh // 2 + 1) * LANES],
                      qct_ref[0, hh * LANES:(hh + 1) * LANES, lane0:], preferred_element_type=F32) for hh in heads]
        log_betas, splits, new_tail = [], [], []
        for hh in heads:
            z = zs[hh]
            log_beta = jnp.minimum(z, 0.0) - jnp.log(1.0 + jnp.exp(-jnp.abs(z)))
            log_keep = log_beta - z
            if masked:
                log_keep = jnp.where(mask, log_keep, 0.0)
            hi = log_keep.astype(BF16)
            lo = (log_keep - hi.astype(F32)).astype(BF16)
            log_betas.append(log_beta)
            splits.append(jnp.concatenate([hi, lo], axis=0))
            new = old[hh] + jnp.sum(log_keep, axis=0, keepdims=True)
            new_tail.append(new if lane0 == 0 else jnp.concatenate([tail[hh][:, :lane0], new], axis=1))
        withins = [jnp.dot(later_ref[...], splits[hh], preferred_element_type=F32) for hh in heads]
        weights = []
        for hh in heads:
            a = jnp.exp(log_betas[hh] + withins[hh] + old[hh])
            if masked:
                a = jnp.where(mask, a, 0.0)
            weights.append(a.astype(BF16))
        for hh in heads:
            r = slice(hh * HEAD_DIM, (hh + 1) * HEAD_DIM)
            acc_scr[r, lane0:] = acc_scr[r, lane0:] + jnp.dot(vt[r, :], weights[hh], preferred_element_type=F32)
        return tuple(new_tail)

    def largest(tail):
        worst = tail[0]
        for hh in range(1, N_HEADS_C):
            worst = jnp.maximum(worst, tail[hh])
        return jnp.max(worst)

    n_rest = i * diag_blocks
    tail = tuple(jnp.zeros((1, qs), F32) for _ in range(N_HEADS_C))
    for d in reversed(range(diag_blocks)):
        tail = block(n_rest + d, tail, True, lane0=d * QB)

    def live(cr):
        return jnp.logical_and(cr[0] < n_rest, cr[2] > STICK_DEAD)

    def step(cr):
        t, tl, _ = cr
        tl = block(n_rest - 1 - t, tl, False)
        return t + 1, tl, largest(tl)

    lax.while_loop(live, step, (jnp.int32(0), tail, largest(tail)))
    yc_ref[0] = acc_scr[...].T.astype(BF16)


def _stick_call(qct, kc, vct, later2):
    b, l, _ = kc.shape
    nq = l // QB
    qs = min(QS_STICK, l)
    return pl.pallas_call(
        _stick_kernel,
        grid=(b, l // qs),
        in_specs=[
            pl.BlockSpec((1, 2 * C_WIDTH, qs), lambda bi, i: (bi, 0, i)),
            pl.BlockSpec((1, l, C_WIDTH), lambda bi, i: (bi, 0, 0)),
            pl.BlockSpec((1, nq, C_WIDTH, QB), lambda bi, i: (bi, 0, 0, 0)),
            pl.BlockSpec((QB, 2 * QB), lambda bi, i: (0, 0)),
        ],
        out_specs=pl.BlockSpec((1, qs, C_WIDTH), lambda bi, i: (bi, i, 0)),
        out_shape=jax.ShapeDtypeStruct((b, l, C_WIDTH), BF16),
        scratch_shapes=[pltpu.VMEM((C_WIDTH, qs), F32)],
        compiler_params=_params(("arbitrary", "arbitrary")),
        name="stick_attention",
    )(qct, kc, vct, later2)


def _merge_kernel(x_ref, ya_ref, yc_ref, mb_ref, ga_ref, gc_ref, gt_ref, sh_ref, sc_ref, g1_ref, g2_ref,
                  wbra_ref, wbrc_ref, wout_ref, *rest, with_router):
    if with_router:
        wr_ref, x1_ref, h2_ref, route_ref, cnt_ref = rest
    else:
        x1_ref, h2_ref = rest
    merged = (ga_ref[0].astype(F32) * jnp.dot(ya_ref[0], wbra_ref[...], preferred_element_type=F32)
              + mb_ref[0].astype(F32)
              + gc_ref[0].astype(F32) * jnp.dot(yc_ref[0], wbrc_ref[...], preferred_element_type=F32))
    y = jnp.dot(merged.astype(BF16), wout_ref[...], preferred_element_type=F32)
    x1 = x_ref[0] + gt_ref[0] * _rms(y, g1_ref[...])
    x1_ref[0] = x1
    h2 = _rms(x1, g2_ref[...]) * (1.0 + sc_ref[0]) + sh_ref[0]
    h2_ref[0] = h2.astype(BF16)
    if with_router:
        logits = jnp.dot(h2, wr_ref[...], preferred_element_type=F32)
        lane = lax.broadcasted_iota(I32, logits.shape, 1)
        valid = lane < N_EXPERTS
        l1 = jnp.where(valid, logits, -jnp.inf)
        v1 = jnp.max(l1, axis=-1, keepdims=True)
        i1 = jnp.min(jnp.where(l1 == v1, lane, LANES), axis=-1, keepdims=True)
        l2 = jnp.where(lane == i1, -jnp.inf, l1)
        v2 = jnp.max(l2, axis=-1, keepdims=True)
        i2 = jnp.min(jnp.where(l2 == v2, lane, LANES), axis=-1, keepdims=True)
        e2 = jnp.exp(v2 - v1)
        p1 = 1.0 / (1.0 + e2)
        p2 = e2 / (1.0 + e2)
        route = jnp.where(lane == 0, i1.astype(F32), jnp.where(lane == 1, i2.astype(F32),
                          jnp.where(lane == 2, p1, jnp.where(lane == 3, p2, 0.0))))
        route_ref[0] = route
        chosen = jnp.where(jnp.logical_or(lane == i1, lane == i2), 1.0, 0.0)
        cnt_ref[0, 0] = jnp.broadcast_to(jnp.sum(chosen, axis=0, keepdims=True), (8, LANES))


def _merge_call(x, ya, yc, mb, ga, gc, gt, sh, sc, g1, g2, wbra, wbrc, wout, wr=None):
    b, l, d = x.shape
    tm = min(TM_MERGE if wr is None else TM_FFN, l)
    tok = lambda w: pl.BlockSpec((1, tm, w), lambda bi, i: (bi, i, 0))
    full = lambda a: pl.BlockSpec(a.shape, lambda bi, i: (0,) * a.ndim)
    vec = pl.BlockSpec((1, 1, d), lambda bi, i: (bi, 0, 0))
    in_specs = [tok(d), tok(A_WIDTH), tok(C_WIDTH), tok(d), tok(d), tok(d), vec, vec, vec,
                full(g1), full(g2), full(wbra), full(wbrc), full(wout)]
    args = [x, ya, yc, mb, ga, gc, gt, sh, sc, g1, g2, wbra, wbrc, wout]
    out_shape = [jax.ShapeDtypeStruct((b, l, d), F32), jax.ShapeDtypeStruct((b, l, d), BF16)]
    out_specs = [tok(d), tok(d)]
    if wr is not None:
        in_specs.append(full(wr))
        args.append(wr)
        out_shape += [jax.ShapeDtypeStruct((b, l, LANES), F32), jax.ShapeDtypeStruct((b, l // tm, 8, LANES), F32)]
        out_specs += [tok(LANES), pl.BlockSpec((1, 1, 8, LANES), lambda bi, i: (bi, i, 0, 0))]
    return pl.pallas_call(
        functools.partial(_merge_kernel, with_router=wr is not None),
        grid=(b, l // tm),
        in_specs=in_specs,
        out_specs=out_specs,
        out_shape=out_shape,
        compiler_params=_params(("arbitrary", "arbitrary")),
        name="merge_router" if wr is not None else "merge",
    )(*args)


def _ffn_kernel(x1_ref, h2_ref, gt_ref, g3_ref, wg_ref, wu_ref, wd_ref, o_ref):
    h2 = h2_ref[0]
    gate = jnp.dot(h2, wg_ref[...], preferred_element_type=F32)
    up = jnp.dot(h2, wu_ref[...], preferred_element_type=F32)
    act = (gate * _sigmoid(gate) * up).astype(BF16)
    y = jnp.dot(act, wd_ref[...], preferred_element_type=F32)
    o_ref[0] = x1_ref[0] + gt_ref[0] * _rms(y, g3_ref[...])


def _ffn_call(x1, h2, gt, g3, wg, wu, wd):
    b, l, d = x1.shape
    tm = min(TM_FFN, l)
    tok = pl.BlockSpec((1, tm, d), lambda bi, i: (bi, i, 0))
    full = lambda a: pl.BlockSpec(a.shape, lambda bi, i: (0,) * a.ndim)
    vec = pl.BlockSpec((1, 1, d), lambda bi, i: (bi, 0, 0))
    return pl.pallas_call(
        _ffn_kernel,
        grid=(b, l // tm),
        in_specs=[tok, tok, vec, full(g3), full(wg), full(wu), full(wd)],
        out_specs=tok,
        out_shape=jax.ShapeDtypeStruct((b, l, d), F32),
        compiler_params=_params(("arbitrary", "arbitrary")),
        name="ffn_dense",
    )(x1, h2, gt, g3, wg, wu, wd)


def _moe_kernel(nchunk_ref, first_ref, total_ref,
                x1_ref, h2_ref, route_ref, before_ref, gt_ref, g3_ref, wg_ref, wu_ref, wd_ref, o_ref,
                xs_scr, wrow_scr, dest_scr, acc_scr):
    tile = pl.program_id(0) * pl.num_programs(1) + pl.program_id(1)
    e = pl.program_id(2)
    n_e = pl.num_programs(2)
    tm = h2_ref.shape[1]
    total = total_ref[tile]

    @pl.when(e == 0)
    def _():
        route = route_ref[0]
        lane = lax.broadcasted_iota(I32, route.shape, 1).astype(F32)
        hot = [jnp.where(lane == route[:, s:s + 1], 1.0, 0.0) for s in range(TOP_K)]
        earlier = jnp.dot(before_ref[...], (hot[0] + hot[1]).astype(BF16), preferred_element_type=F32)
        start = jnp.zeros((1, LANES), F32)
        for ee in range(N_EXPERTS):
            start = jnp.where(lane[:1, :] == ee, (first_ref[tile * N_EXPERTS + ee] * ROW_CHUNK).astype(F32), start)
        where_to = earlier + start
        dest = [hot[s] * where_to for s in range(TOP_K)]
        prob = [hot[s] * route[:, TOP_K + s:TOP_K + s + 1] for s in range(TOP_K)]
        for s in range(TOP_K):
            dest_scr[s] = jnp.broadcast_to(jnp.sum(dest[s], axis=1, keepdims=True), (tm, LANES))
        dest_row = [jnp.sum(dest[s].T, axis=0, keepdims=True) for s in range(TOP_K)]
        prob_row = [jnp.sum(prob[s].T, axis=0, keepdims=True) for s in range(TOP_K)]
        h2 = h2_ref[0]

        def place(c, carry):
            r0 = pl.multiple_of(c * ROW_CHUNK, ROW_CHUNK)
            rows = (r0 + lax.broadcasted_iota(I32, (ROW_CHUNK, tm), 0)).astype(F32)
            here = [rows == dest_row[s] for s in range(TOP_K)]
            pick = jnp.where(jnp.logical_or(here[0], here[1]), 1.0, 0.0).astype(BF16)
            xs_scr[pl.ds(r0, ROW_CHUNK), :] = jnp.dot(pick, h2, preferred_element_type=F32).astype(BF16)
            weight = jnp.where(here[0], prob_row[0], 0.0) + jnp.where(here[1], prob_row[1], 0.0)
            wrow_scr[pl.ds(r0, ROW_CHUNK), :] = jnp.broadcast_to(jnp.sum(weight, axis=1, keepdims=True),
                                                                  (ROW_CHUNK, LANES))
            return carry

        lax.fori_loop(0, total + total % 2, place, 0)

    def expert_chunk(c, carry):
        r0 = pl.multiple_of((first_ref[tile * n_e + e] + c) * ROW_CHUNK, ROW_CHUNK)
        xs = xs_scr[pl.ds(r0, ROW_CHUNK), :]
        gate = jnp.dot(xs, wg_ref[0], preferred_element_type=F32)
        up = jnp.dot(xs, wu_ref[0], preferred_element_type=F32)
        act = (gate * _sigmoid(gate) * up).astype(BF16)
        y = jnp.dot(act, wd_ref[0], preferred_element_type=F32)
        xs_scr[pl.ds(r0, ROW_CHUNK), :] = (y * wrow_scr[pl.ds(r0, ROW_CHUNK), 0:1]).astype(BF16)
        return carry

    lax.fori_loop(0, nchunk_ref[tile * n_e + e], expert_chunk, 0)

    @pl.when(e == n_e - 1)
    def _():
        acc_scr[...] = jnp.zeros_like(acc_scr)

        def collect(c, carry):
            r0 = pl.multiple_of(c * 2 * ROW_CHUNK, 2 * ROW_CHUNK)
            halves = []
            for j in range(2):
                cols = (r0 + j * ROW_CHUNK + lax.broadcasted_iota(I32, (tm, ROW_CHUNK), 1)).astype(F32)
                mine = jnp.logical_or(cols == dest_scr[0], cols == dest_scr[1])
                halves.append(jnp.where(mine, 1.0, 0.0).astype(BF16))
            acc_scr[...] = acc_scr[...] + jnp.dot(jnp.concatenate(halves, axis=1),
                                                  xs_scr[pl.ds(r0, 2 * ROW_CHUNK), :], preferred_element_type=F32)
            return carry

        lax.fori_loop(0, (total + 1) // 2, collect, 0)
        o_ref[0] = x1_ref[0] + gt_ref[0] * _rms(acc_scr[...], g3_ref[...])


def _moe_call(x1, h2, route, counts, before, gt, g3, wg, wu, wd):
    b, l, d = x1.shape
    tm = before.shape[0]
    n_e, _, ff = wg.shape
    cnt = counts[:, :, 0, :n_e].astype(I32).reshape(b * (l // tm), -1, n_e).sum(axis=1)
    nchunk = (cnt + ROW_CHUNK - 1) // ROW_CHUNK
    first = jnp.cumsum(nchunk, axis=1) - nchunk
    total = jnp.sum(nchunk, axis=1)
    rows = TOP_K * tm + n_e * ROW_CHUNK
    tok = lambda w: pl.BlockSpec((1, tm, w), lambda bi, i, e, *_: (bi, i, 0))
    tok_once = lambda w: pl.BlockSpec((1, tm, w), lambda bi, i, e, *_: (bi, i, 0), pipeline_mode=pl.Buffered(1))
    vec = pl.BlockSpec((1, 1, d), lambda bi, i, e, *_: (bi, 0, 0))
    grid_spec = pltpu.PrefetchScalarGridSpec(
        num_scalar_prefetch=3,
        grid=(b, l // tm, n_e),
        in_specs=[tok_once(d), tok_once(d), tok(LANES),
                  pl.BlockSpec(before.shape, lambda bi, i, e, *_: (0, 0), pipeline_mode=pl.Buffered(1)), vec,
                  pl.BlockSpec(g3.shape, lambda bi, i, e, *_: (0, 0)),
                  pl.BlockSpec((1, d, ff), lambda bi, i, e, *_: (e, 0, 0)),
                  pl.BlockSpec((1, d, ff), lambda bi, i, e, *_: (e, 0, 0)),
                  pl.BlockSpec((1, ff, d), lambda bi, i, e, *_: (e, 0, 0))],
        out_specs=tok(d),
        scratch_shapes=[pltpu.VMEM((rows, d), BF16), pltpu.VMEM((rows, LANES), F32),
                        pltpu.VMEM((TOP_K, tm, LANES), F32), pltpu.VMEM((tm, d), F32)],
    )
    return pl.pallas_call(
        _moe_kernel,
        grid_spec=grid_spec,
        out_shape=jax.ShapeDtypeStruct((b, l, d), F32),
        compiler_params=_params(("arbitrary", "arbitrary", "arbitrary")),
        name="ffn_moe",
    )(nchunk.reshape(-1), first.reshape(-1), total, x1, h2, route, before, gt, g3, wg, wu, wd)


def _rot_cols(w, n_heads, head_dim):
    d = w.shape[0]
    w = w.reshape(d, n_heads, head_dim)
    half = ROPE_DIM // 2
    rot = jnp.concatenate([-w[..., half:ROPE_DIM], w[..., :half],
                           jnp.zeros((d, n_heads, head_dim - ROPE_DIM), w.dtype)], axis=-1)
    return rot.reshape(d, n_heads * head_dim)


def _pad_cols(w, width):
    return jnp.pad(w, ((0, 0), (0, width - w.shape[1])))


def _layer_weights(w_in, w_uk, w_uv, w_pool):
    d = w_in.shape[0]
    offs, o = [], 0
    for s in IN_SIZES:
        offs.append(o)
        o += s
    piece = lambda k: w_in[:, offs[k]:offs[k] + IN_SIZES[k]]
    w_qa, w_lat, w_kr, w_qi, w_ki, w_wi, w_up, w_qc, w_kc, w_vc, w_gate = [piece(k) for k in range(len(IN_SIZES))]
    attn_scale = HEAD_DIM ** -0.5
    idx_scale = IDX_DIM ** -0.5 * N_IDX_HEADS ** -0.5
    wrow = jnp.concatenate([
        w_lat,
        _pad_cols(w_kr, LANES), _pad_cols(_rot_cols(w_kr, 1, ROPE_DIM), LANES),
        _pad_cols(w_ki, LANES), _pad_cols(_rot_cols(w_ki, 1, IDX_DIM), LANES),
        w_up, w_kc, w_gate], axis=1).astype(BF16)
    w_qc_t = (w_qc * attn_scale).T.reshape(N_HEADS_C, HEAD_DIM, d)
    zeros = jnp.zeros_like(w_qc_t)
    even = (jnp.arange(N_HEADS_C) % 2 == 0)[:, None, None]
    w_qc_pad = jnp.concatenate([jnp.where(even, w_qc_t, zeros), jnp.where(even, zeros, w_qc_t)], axis=1)
    w_qc_pad = w_qc_pad.reshape(2 * C_WIDTH, d)
    softmax_scale = attn_scale * LOG2_E
    wt = jnp.concatenate([
        (w_qa * softmax_scale).T, w_qi.T, w_qc_pad, w_vc.T,
        jnp.pad((w_wi * idx_scale).T, ((0, T_WI_ROWS - N_IDX_HEADS), (0, 0)))], axis=0).astype(BF16)
    wuk = jnp.zeros((KV_LATENT, LANES), F32).at[:, ROPE_DIM:HEAD_DIM].set(w_uk).astype(BF16)
    wuvt = w_uv.T.astype(BF16)
    wpool = jnp.zeros((POOL_WIDTH, POOL_WIDTH), F32)
    for g in range(N_POOL_GROUPS):
        sl = slice(g * POOL_GROUP_DIM, (g + 1) * POOL_GROUP_DIM)
        wpool = wpool.at[sl, sl].set(w_pool[g])
    return wrow, wt, wuk, wuvt, wpool.astype(BF16)


def _rope_tables(positions):
    inv = ROPE_THETA ** (-jnp.arange(0, ROPE_DIM, 2, dtype=F32) / ROPE_DIM)
    ang = positions.astype(F32)[..., None] * inv
    cos, sin = jnp.cos(ang), jnp.sin(ang)
    b, l = positions.shape
    ones = jnp.ones((b, l, HEAD_DIM - ROPE_DIM), F32)
    cos_h = jnp.concatenate([cos, cos, ones], axis=-1)
    sin_h = jnp.concatenate([sin, sin, jnp.zeros_like(ones)], axis=-1)
    cos_r = jnp.concatenate([cos_h, cos_h], axis=-1)
    sin_r = jnp.concatenate([sin_h, sin_h], axis=-1)
    return cos_r, sin_r, cos_h.transpose(0, 2, 1), sin_h.transpose(0, 2, 1)


def kernel(x, c, positions, w_ada, b_ada, norm_gains, w_in, g_kv_latent, w_uk, w_uv, w_pool, pool_scale,
           w_br_a, w_br_b, w_br_c, w_out, w_gate_dense, w_up_dense, w_down_dense,
           w_router, w_gate_moe, w_up_moe, w_down_moe):
    b, l, d = x.shape
    depth = w_in.shape[0]
    assert d == D_MODEL and l % max(KC, TM_FFN, TM_MOE) == 0
    cos_r, sin_r, cos_t, sin_t = _rope_tables(positions)
    c_pad = jnp.pad(c, ((0, 8 - b), (0, 0)))
    mod = _mod_call(c_pad, w_ada, b_ada)[:, :b]
    idx = lax.broadcasted_iota(I32, (QB, QB), 0)
    jdx = lax.broadcasted_iota(I32, (QB, QB), 1)
    tri = (jdx <= idx).astype(BF16)
    later = (jdx > idx).astype(BF16)
    later2 = jnp.concatenate([later, later], axis=1)
    tm_moe = min(TM_MOE, l)
    before = (lax.broadcasted_iota(I32, (tm_moe, tm_moe), 1)
              < lax.broadcasted_iota(I32, (tm_moe, tm_moe), 0)).astype(BF16)
    for layer in range(depth):
        sh1, sc1, gt1, sh2, sc2, gt2 = [m.reshape(b, 1, d) for m in jnp.split(mod[layer], 6, axis=-1)]
        gains = norm_gains[layer].reshape(4, 1, d)
        wrow, wt, wuk, wuvt, wpool = _layer_weights(w_in[layer], w_uk[layer], w_uv[layer], w_pool[layer])
        (qat, qit, wit, ka, vat, ki, qct, kc, vct, ga, gc, mb) = _inproj_call(
            x, sh1, sc1, gains[0], wrow, wt, cos_r, sin_r, cos_t, sin_t,
            g_kv_latent[layer].reshape(1, KV_LATENT), wuk, wuvt, wpool,
            pool_scale[layer].reshape(1, POOL_WIDTH), w_br_b[layer].astype(BF16))
        ya = _dsa_call(qit, wit, qat, ki, ka, vat, tri)
        yc = _stick_call(qct, kc, vct, later2)
        i = layer // 2
        merge_args = (x, ya, yc, mb, ga, gc, gt1, sh2, sc2, gains[1], gains[2],
                      w_br_a[layer].astype(BF16), w_br_c[layer].astype(BF16), w_out[layer].astype(BF16))
        if layer % 2 == 0:
            x1, h2 = _merge_call(*merge_args)
            x = _ffn_call(x1, h2, gt2, gains[3], w_gate_dense[i].astype(BF16), w_up_dense[i].astype(BF16),
                          w_down_dense[i].astype(BF16))
        else:
            x1, h2, route, counts = _merge_call(*merge_args, wr=_pad_cols(w_router[i], LANES))
            x = _moe_call(x1, h2, route, counts, before, gt2, gains[3], w_gate_moe[i].astype(BF16),
                          w_up_moe[i].astype(BF16), w_down_moe[i].astype(BF16))
    return x
```

```python
import functools

import jax
import jax.numpy as jnp
from jax import lax
from jax.experimental import pallas as pl
from jax.experimental.pallas import tpu as pltpu

F32 = jnp.float32
BF16 = jnp.bfloat16
I32 = jnp.int32
I16 = jnp.int16

D_MODEL = 1024
HEAD_DIM = 64
ROPE_DIM = HEAD_DIM // 4
ROPE_THETA = 500000.0
N_HEADS_A = (3 * D_MODEL // 8) // HEAD_DIM
A_WIDTH = N_HEADS_A * HEAD_DIM
KV_LATENT = D_MODEL // 8
N_IDX_HEADS = 4
IDX_DIM = 64
TOPK_MAX = 256
N_POOL_GROUPS = 4
POOL_WINDOWS = (2, 4, 8, 16)
POOL_WIDTH = D_MODEL // 4
POOL_GROUP_DIM = POOL_WIDTH // N_POOL_GROUPS
N_HEADS_C = (D_MODEL // 4) // HEAD_DIM
C_WIDTH = N_HEADS_C * HEAD_DIM
N_BRANCHES = 3
IN_SIZES = (A_WIDTH, KV_LATENT, ROPE_DIM, N_IDX_HEADS * IDX_DIM, IDX_DIM, N_IDX_HEADS,
            POOL_WIDTH, C_WIDTH, C_WIDTH, C_WIDTH, N_BRANCHES * D_MODEL)
D_FF = 2816
N_EXPERTS = 8
TOP_K = 2
D_FF_EXPERT = D_FF // TOP_K
RMS_EPS = 1e-6

LANES = 128
QB = 128
KC = 512
QS_STICK = 512
QS_DSA = 256
KB_DSA = 512
VB_DSA = 256
V_ROWS = 80
STICK_DEAD = -106.0
TM_IN = 512
TM_MERGE = 512
TM_FFN = 512
TM_MOE = 1024
ROW_CHUNK = 128
POOL_HALO = 16
VMEM_LIMIT = 56 * 1024 * 1024
INT_MIN = -2147483648
LOG2_E = 1.4426950408889634
NEG_BIG = -1e30
NEG_MASK = -2e30

R_LAT, R_KR, R_KRR, R_KI, R_KIR, R_UP, R_KC, R_GATE = 0, 128, 256, 384, 512, 640, 896, 1152
T_QA, T_QI, T_QC, T_VC, T_WI = 0, 384, 640, 1152, 1408
T_WI_ROWS = 16


def _params(sem):
    return pltpu.CompilerParams(dimension_semantics=sem, vmem_limit_bytes=VMEM_LIMIT)


def _sigmoid(v):
    return 1.0 / (1.0 + jnp.exp(-v))


def _rms(v, gain):
    return v * lax.rsqrt(jnp.mean(v * v, axis=-1, keepdims=True) + RMS_EPS) * gain


def _mod_kernel(c_ref, w_ref, b_ref, o_ref):
    c = c_ref[...]
    cond = c * _sigmoid(c)
    o_ref[0] = jnp.dot(cond, w_ref[0], preferred_element_type=F32) + b_ref[0]


def _mod_call(c_pad, w_ada, b_ada):
    depth, d, n = w_ada.shape
    tn = 1024
    return pl.pallas_call(
        _mod_kernel,
        grid=(depth, n // tn),
        in_specs=[
            pl.BlockSpec((8, d), lambda l, j: (0, 0)),
            pl.BlockSpec((1, d, tn), lambda l, j: (l, 0, j)),
            pl.BlockSpec((1, 1, tn), lambda l, j: (l, 0, j)),
        ],
        out_specs=pl.BlockSpec((1, 8, tn), lambda l, j: (l, 0, j)),
        out_shape=jax.ShapeDtypeStruct((depth, 8, n), F32),
        compiler_params=_params(("arbitrary", "arbitrary")),
        name="adaln_mod",
    )(c_pad, w_ada, b_ada.reshape(depth, 1, n))


def _inproj_kernel(x_ref, sh_ref, sc_ref, g_ref, wrow_ref, wt_ref, c_ref, s_ref, ct_ref, st_ref,
                   glat_ref, wuk_ref, wuvt_ref, wpool_ref, pscale_ref, wbrb_ref,
                   qat_ref, qit_ref, wit_ref, ka_ref, vat_ref, ki_ref, qct_ref, kc_ref, vct_ref,
                   ga_ref, gc_ref, mb_ref,
                   h_scr, ht_scr, ext_scr, prev_scr):
    i = pl.program_id(1)
    tm = x_ref.shape[1]
    x = x_ref[0]
    h = _rms(x, g_ref[...]) * (1.0 + sc_ref[0]) + sh_ref[0]
    h_scr[...] = h.astype(BF16)
    ht_scr[...] = h.T.astype(BF16)

    def rowdot(a, width):
        return jnp.dot(h_scr[...], wrow_ref[:, a:a + width], preferred_element_type=F32)

    def tdot(a, height):
        return jnp.dot(wt_ref[a:a + height, :], ht_scr[...], preferred_element_type=F32)

    cos_r, sin_r = c_ref[0], s_ref[0]
    cos_t, sin_t = ct_ref[0], st_ref[0]

    latn = _rms(rowdot(R_LAT, KV_LATENT), glat_ref[...])
    ka = (rowdot(R_KR, LANES) * cos_r + rowdot(R_KRR, LANES) * sin_r
          + jnp.dot(latn.astype(BF16), wuk_ref[...], preferred_element_type=F32))
    ka_ref[0] = ka[:, :HEAD_DIM].astype(BF16)
    vat = jnp.dot(wuvt_ref[...], latn.T.astype(BF16), preferred_element_type=F32)
    pad_row = lax.broadcasted_iota(I32, (V_ROWS - HEAD_DIM, tm), 0)
    vat = jnp.concatenate([vat, jnp.where(pad_row == 0, 1.0, 0.0)], axis=0)
    for j in range(tm // VB_DSA):
        vat_ref[0, j] = vat[:, j * VB_DSA:(j + 1) * VB_DSA].astype(BF16)

    ki =rowdot(R_KI, LANES) * cos_r + rowdot(R_KIR, LANES) * sin_r
    ki_ref[0] = ki[:, :IDX_DIM].astype(BF16)

    half = ROPE_DIM // 2
    cos_f, sin_f = cos_t[:half], sin_t[:half]

    def rope_t(q):
        x1, x2 = q[:half], q[half:ROPE_DIM]
        return jnp.concatenate([x1 * cos_f - x2 * sin_f, x2 * cos_f + x1 * sin_f, q[ROPE_DIM:]], axis=0)

    qa = tdot(T_QA, A_WIDTH)
    for hh in range(N_HEADS_A):
        r = slice(hh * HEAD_DIM, (hh + 1) * HEAD_DIM)
        qat_ref[0, r, :] = rope_t(qa[r]).astype(BF16)
    qi = tdot(T_QI, N_IDX_HEADS * IDX_DIM)
    for hh in range(N_IDX_HEADS):
        r = slice(hh * IDX_DIM, (hh + 1) * IDX_DIM)
        qit_ref[0, r, :] = rope_t(qi[r]).astype(BF16)
    wit_ref[0] = tdot(T_WI, T_WI_ROWS)

    qct_ref[0] = tdot(T_QC, 2 * C_WIDTH).astype(BF16)
    kc_ref[0] = rowdot(R_KC, C_WIDTH).astype(BF16)
    vct = tdot(T_VC, C_WIDTH)
    for j in range(tm // QB):
        vct_ref[0, j] = vct[:, j * QB:(j + 1) * QB].astype(BF16)

    up = rowdot(R_UP, POOL_WIDTH)

    @pl.when(i == 0)
    def _():
        prev_scr[...] = jnp.zeros_like(prev_scr)

    ext_scr[0:POOL_HALO, :] = prev_scr[...]
    ext_scr[POOL_HALO:POOL_HALO + tm, :] = up
    prev_scr[...] = up[tm - POOL_HALO:, :]
    lag = [ext_scr[POOL_HALO - j:POOL_HALO - j + tm, :] for j in range(POOL_HALO)]
    sums = {}
    run = lag[0]
    for j in range(1, POOL_HALO):
        run = run + lag[j]
        if j + 1 in POOL_WINDOWS:
            sums[j + 1] = run
    lane = lax.broadcasted_iota(I32, (tm, POOL_WIDTH), 1)
    pos = i * tm + lax.broadcasted_iota(I32, (tm, POOL_WIDTH), 0)
    pooled_sum = sums[POOL_WINDOWS[-1]]
    win = jnp.full((tm, POOL_WIDTH), POOL_WINDOWS[-1], I32)
    for g in range(N_POOL_GROUPS - 2, -1, -1):
        in_group = lane < (g + 1) * POOL_GROUP_DIM
        pooled_sum = jnp.where(in_group, sums[POOL_WINDOWS[g]], pooled_sum)
        win = jnp.where(in_group, POOL_WINDOWS[g], win)
    cnt = jnp.minimum(pos + 1, win).astype(F32)
    pooled = pooled_sum / cnt - up
    yb = jnp.dot(pooled.astype(BF16), wpool_ref[...], preferred_element_type=F32) * pscale_ref[...]

    ga_ref[0] = _sigmoid(rowdot(R_GATE, D_MODEL)).astype(BF16)
    gb = _sigmoid(rowdot(R_GATE + D_MODEL, D_MODEL))
    mb_ref[0] = (gb * jnp.dot(yb.astype(BF16), wbrb_ref[...], preferred_element_type=F32)).astype(BF16)
    gc_ref[0] = _sigmoid(rowdot(R_GATE + 2 * D_MODEL, D_MODEL)).astype(BF16)


def _inproj_call(x, sh, sc, gain, wrow, wt, cos_r, sin_r, cos_t, sin_t, glat, wuk, wuvt, wpool, pscale, wbrb):
    b, l, d = x.shape
    tm = min(TM_IN, l)
    nq = l // QB
    tok = lambda w: pl.BlockSpec((1, tm, w), lambda bi, i: (bi, i, 0))
    feat = lambda hgt: pl.BlockSpec((1, hgt, tm), lambda bi, i: (bi, 0, i))
    blk = lambda hgt, w: pl.BlockSpec((1, tm // w, hgt, w), lambda bi, i: (bi, i, 0, 0))
    full = lambda a: pl.BlockSpec(a.shape, lambda bi, i: (0,) * a.ndim)
    vec = pl.BlockSpec((1, 1, d), lambda bi, i: (bi, 0, 0))
    out_shape = (
        jax.ShapeDtypeStruct((b, A_WIDTH, l), BF16),
        jax.ShapeDtypeStruct((b, N_IDX_HEADS * IDX_DIM, l), BF16),
        jax.ShapeDtypeStruct((b, T_WI_ROWS, l), F32),
        jax.ShapeDtypeStruct((b, l, HEAD_DIM), BF16),
        jax.ShapeDtypeStruct((b, l // VB_DSA, V_ROWS, VB_DSA), BF16),
        jax.ShapeDtypeStruct((b, l, IDX_DIM), BF16),
        jax.ShapeDtypeStruct((b, 2 * C_WIDTH, l), BF16),
        jax.ShapeDtypeStruct((b, l, C_WIDTH), BF16),
        jax.ShapeDtypeStruct((b, nq, C_WIDTH, QB), BF16),
        jax.ShapeDtypeStruct((b, l, d), BF16),
        jax.ShapeDtypeStruct((b, l, d), BF16),
        jax.ShapeDtypeStruct((b, l, d), BF16),
    )
    out_specs = (feat(A_WIDTH), feat(N_IDX_HEADS * IDX_DIM), feat(T_WI_ROWS), tok(HEAD_DIM), blk(V_ROWS, VB_DSA),
                 tok(IDX_DIM), feat(2 * C_WIDTH), tok(C_WIDTH), blk(C_WIDTH, QB), tok(d), tok(d), tok(d))
    resident = lambda a: pl.BlockSpec(a.shape, lambda bi, i: (0,) * a.ndim, pipeline_mode=pl.Buffered(1))
    in_specs = [tok(d), vec, vec, full(gain), resident(wrow), resident(wt), tok(LANES), tok(LANES),
                feat(HEAD_DIM), feat(HEAD_DIM), full(glat), full(wuk), full(wuvt), full(wpool),
                full(pscale), full(wbrb)]
    return pl.pallas_call(
        _inproj_kernel,
        grid=(b, l // tm),
        in_specs=in_specs,
        out_specs=out_specs,
        out_shape=out_shape,
        scratch_shapes=[pltpu.VMEM((tm, d), BF16), pltpu.VMEM((d, tm), BF16),
                        pltpu.VMEM((tm + POOL_HALO, POOL_WIDTH), F32), pltpu.VMEM((POOL_HALO, POOL_WIDTH), F32)],
        compiler_params=_params(("arbitrary", "arbitrary")),
        name="inproj",
    )(x, sh, sc, gain, wrow, wt, cos_r, sin_r, cos_t, sin_t, glat, wuk, wuvt, wpool, pscale, wbrb)


def _dsa_kernel(qit_ref, wit_ref, qat_ref, ki_ref, ka_ref, vat_ref, tri_ref, ya_ref,
                keys_scr, top_scr, acc_scr, *, k_sel):
    i = pl.program_id(1)
    qs = qat_ref.shape[2]
    n_chunks = ((i + 1) * qs + KC - 1) // KC
    qpos = i * qs + lax.broadcasted_iota(I32, (1, qs), 1)
    w_idx = wit_ref[0]

    def score_chunk(c, carry, masked):
        r0 = pl.multiple_of(c * KC, KC)
        kblk = ki_ref[0, pl.ds(r0, KC), :]
        parts = [jnp.dot(kblk, qit_ref[0, hh * IDX_DIM:(hh + 1) * IDX_DIM, :], preferred_element_type=F32)
                 for hh in range(N_IDX_HEADS)]
        score = jnp.maximum(parts[0], 0.0) * w_idx[0:1, :]
        for hh in range(1, N_IDX_HEADS):
            score = score + jnp.maximum(parts[hh], 0.0) * w_idx[hh:hh + 1, :]
        bits = lax.bitcast_convert_type(score, I32)
        key = jnp.where(bits < 0, INT_MIN - bits, bits)
        if masked:
            kpos = r0 + lax.broadcasted_iota(I32, (KC, qs), 0)
            key = jnp.where(kpos <= qpos, key, INT_MIN)
        keys_scr[pl.ds(r0, KC), :] = key
        top = lax.shift_right_arithmetic(key, 16).astype(I16)
        top_scr[pl.ds(r0, KC), :] = top
        groups = top.reshape(KC // 16, 16, qs)
        for j in range(KC // 16):
            carry = jnp.where(groups[j] > carry, groups[j], carry)
        return carry

    n_open = (i * qs) // KC
    top_max = lax.fori_loop(0, n_open, functools.partial(score_chunk, masked=False),
                            jnp.full((16, qs), -2 ** 15, I16))
    top_max = lax.fori_loop(n_open, n_chunks, functools.partial(score_chunk, masked=True), top_max)
    top_max = jnp.max(top_max.astype(I32), axis=0, keepdims=True)

    def count_rows(src, rows_per_vreg, trial):
        n_acc = 4
        groups = KC // rows_per_vreg

        def body(c, accs):
            r0 = pl.multiple_of(c * KC, KC)
            rows = src[pl.ds(r0, KC), :].reshape(groups, rows_per_vreg, qs)
            accs = list(accs)
            for j in range(groups):
                a = accs[j % n_acc]
                accs[j % n_acc] = jnp.where(rows[j] >= trial, a + 1, a)
            return tuple(accs)

        zero = jnp.zeros((rows_per_vreg, qs), src.dtype)
        accs = lax.fori_loop(0, n_chunks, body, tuple(zero for _ in range(n_acc)))
        total = (accs[0] + accs[1]) + (accs[2] + accs[3])
        return jnp.sum(total.astype(I32), axis=0, keepdims=True)

    def count_ge(trial):
        return count_rows(keys_scr, 8, trial)

    def count_ge_top(trial):
        return count_rows(top_scr, 16, lax.shift_right_arithmetic(trial, 16).astype(I16))

    c_zero = count_ge_top(jnp.zeros((1, qs), I32))
    c_pos = count_ge(jnp.ones((1, qs), I32))
    tie_at_zero = jnp.logical_and(c_zero >= k_sel, c_pos < k_sel)

    def unsettled(c_ans):
        settled = jnp.logical_or(tie_at_zero, c_ans == k_sel)
        return jnp.max(jnp.where(settled, 0, 1))

    def search(counter, lowest_bit, group, state):
        def refine(carry):
            bit, ans, c_ans, _ = carry
            for g in range(group):
                trial = ans + lax.shift_left(jnp.int32(1), bit - g)
                c = counter(trial)
                ok = c >= k_sel
                c_ans = jnp.where(ok, c, c_ans)
                ans = jnp.where(ok, trial, ans)
            return bit - group, ans, c_ans, unsettled(c_ans)

        return lax.while_loop(lambda cr: jnp.logical_and(cr[0] >= lowest_bit, cr[3] > 0), refine, state)

    def count_within(gap):
        return count_rows(top_scr, 16, jnp.maximum(top_max - gap, -2 ** 15).astype(I16))

    near_bits = 9
    c_near = count_within(2 ** near_bits - 1)
    short = qpos + 1 < k_sel
    known = jnp.logical_or(tie_at_zero, short)
    all_near = jnp.min(jnp.where(jnp.logical_or(c_near >= k_sel, known), 1, 0)) > 0
    first_bit = jnp.where(all_near, near_bits - 1, 15)

    def widen(group, carry):
        bit, gap, c_gap = carry
        for g in range(group):
            trial = gap + lax.shift_left(jnp.int32(1), bit - g) - 1
            c = count_within(trial)
            ok = c >= k_sel
            gap = jnp.where(ok, gap, trial + 1)
            c_gap = jnp.where(ok, c, c_gap)
        return bit - group, gap, c_gap

    c_far = jnp.where(all_near, c_near, jnp.int32(2 ** 30))
    state = (first_bit, jnp.zeros((1, qs), I32), c_far)
    state = lax.while_loop(lambda cr: cr[0] >= near_bits, functools.partial(widen, 16 - near_bits), state)
    _, gap, c_top = lax.while_loop(lambda cr: cr[0] >= 0, functools.partial(widen, 3), state)
    bucket = jnp.where(short, -2 ** 15, jnp.where(tie_at_zero, 0, jnp.maximum(top_max - gap, -2 ** 15)))
    c_top = jnp.where(tie_at_zero, c_zero, c_top)
    top = lax.shift_left(bucket, 16)
    alive = unsettled(c_top)
    last_bucket = bucket >= 2 ** 15 - 1
    c_over = jnp.where(last_bucket, 0,
                       count_rows(top_scr, 16, jnp.where(last_bucket, bucket, bucket + 1).astype(I16)))

    def pack_low(c, carry):
        r0 = pl.multiple_of(c * KC, KC)
        low = (keys_scr[pl.ds(r0, KC), :] ^ 0x8000).astype(I16)
        top_scr[pl.ds(r0, KC), :] = jnp.where(top_scr[pl.ds(r0, KC), :] == bucket.astype(I16), low, -2 ** 15)
        return carry

    lax.fori_loop(0, n_chunks, pack_low, 0)

    def count_ge_low(low_trial):
        return c_over + count_rows(top_scr, 16, (low_trial - 2 ** 15).astype(I16))

    zero_row = jnp.zeros((1, qs), I32)
    _, low, _, _ = search(count_ge_low, 0, 4, (jnp.int32(15), zero_row, c_top, alive))
    thr = top + low
    full_low = low >= 2 ** 16 - 1
    n_above = jnp.where(full_low, c_over, count_ge_low(jnp.where(full_low, low, low + 1)))
    n_ties = jnp.where(thr == INT_MIN, 0, k_sel - n_above).astype(F32)

    acc_scr[...] = jnp.zeros_like(acc_scr)
    heads = range(N_HEADS_A)

    kb_rows = KB_DSA
    vb_rows = vat_ref.shape[3]
    v_per_k = kb_rows // vb_rows

    n_steps = ((i + 1) * qs + kb_rows - 1) // kb_rows

    v_rows = vat_ref.shape[2]

    def attend(kb, carry):
        seen, ms = carry
        r0 = pl.multiple_of(kb * kb_rows, kb_rows)
        keyb = keys_scr[pl.ds(r0, kb_rows), :]
        tied = keyb == thr
        tied_b = jnp.where(tied, 1.0, 0.0).astype(BF16)
        ranks = []
        for j in range(kb_rows // QB):
            within = jnp.dot(tri_ref[...], tied_b[j * QB:(j + 1) * QB, :], preferred_element_type=F32)
            ranks.append(seen + within)
            seen = seen + within[QB - 1:QB, :]
        rank = jnp.concatenate(ranks, axis=0)
        keep = jnp.logical_or(keyb > thr, jnp.logical_and(tied, rank <= n_ties))
        bias = jnp.where(keep, 0.0, NEG_MASK).astype(BF16)
        kblk = ka_ref[0, pl.ds(r0, kb_rows), :]
        logits = [jnp.dot(kblk, qat_ref[0, hh * HEAD_DIM:(hh + 1) * HEAD_DIM, :],
                          preferred_element_type=F32).astype(BF16) + bias for hh in heads]
        new_ms = [jnp.maximum(ms[hh], jnp.max(logits[hh], axis=0, keepdims=True).astype(F32)) for hh in heads]
        probs = [jnp.exp2(logits[hh] - new_ms[hh].astype(BF16)) for hh in heads]
        alphas = [jnp.exp2(ms[hh] - new_ms[hh]) for hh in heads]
        outs = []
        for hh in heads:
            out = jnp.dot(vat_ref[0, kb * v_per_k], probs[hh][:vb_rows, :], preferred_element_type=F32)
            for j in range(1, v_per_k):
                out = out + jnp.dot(vat_ref[0, kb * v_per_k + j], probs[hh][j * vb_rows:(j + 1) * vb_rows, :],
                                    preferred_element_type=F32)
            outs.append(out)
        for hh in heads:
            r = slice(hh * v_rows, (hh + 1) * v_rows)
            acc_scr[r, :] = acc_scr[r, :] * alphas[hh] + outs[hh]
        return seen, tuple(new_ms)

    init = (jnp.zeros((1, qs), F32), tuple(jnp.full((1, qs), NEG_BIG, F32) for _ in heads))
    lax.fori_loop(0, n_steps, attend, init)
    normed = [acc_scr[hh * v_rows:hh * v_rows + HEAD_DIM, :] / acc_scr[hh * v_rows + HEAD_DIM:hh * v_rows + HEAD_DIM + 1, :]
              for hh in heads]
    ya_ref[0] = jnp.concatenate(normed, axis=0).T.astype(BF16)


def _dsa_call(qit, wit, qat, ki, ka, vat, tri):
    b, l, _ = ki.shape
    qs = min(QS_DSA, l)
    k_sel = min(TOPK_MAX, l // 4)
    return pl.pallas_call(
        functools.partial(_dsa_kernel, k_sel=k_sel),
        grid=(b, l // qs),
        in_specs=[
            pl.BlockSpec((1, N_IDX_HEADS * IDX_DIM, qs), lambda bi, i: (bi, 0, i)),
            pl.BlockSpec((1, T_WI_ROWS, qs), lambda bi, i: (bi, 0, i)),
            pl.BlockSpec((1, A_WIDTH, qs), lambda bi, i: (bi, 0, i)),
            pl.BlockSpec((1, l, IDX_DIM), lambda bi, i: (bi, 0, 0), pipeline_mode=pl.Buffered(1)),
            pl.BlockSpec((1, l, HEAD_DIM), lambda bi, i: (bi, 0, 0), pipeline_mode=pl.Buffered(1)),
            pl.BlockSpec((1, l // VB_DSA, V_ROWS, VB_DSA), lambda bi, i: (bi, 0, 0, 0),
                         pipeline_mode=pl.Buffered(1)),
            pl.BlockSpec((QB, QB), lambda bi, i: (0, 0)),
        ],
        out_specs=pl.BlockSpec((1, qs, A_WIDTH), lambda bi, i: (bi, i, 0)),
        out_shape=jax.ShapeDtypeStruct((b, l, A_WIDTH), BF16),
        scratch_shapes=[pltpu.VMEM((l, qs), I32), pltpu.VMEM((l, qs), I16),
                        pltpu.VMEM((N_HEADS_A * V_ROWS, qs), F32)],
        compiler_params=_params(("arbitrary", "arbitrary")),
        name="dsa_attention",
    )(qit, wit, qat, ki, ka, vat, tri)


def _stick_kernel(qct_ref, kc_ref, vct_ref, later_ref, yc_ref, acc_scr):
    i = pl.program_id(1)
    qs = qct_ref.shape[2]
    diag_blocks = qs // QB
    qpos = i * qs + lax.broadcasted_iota(I32, (1, qs), 1)
    acc_scr[...] = jnp.zeros_like(acc_scr)

    def block(kb, tail, masked, lane0=0):
        r0 = pl.multiple_of(kb * QB, QB)
        kfull = kc_ref[0, pl.ds(r0, QB), :]
        vt = vct_ref[0, kb]
        heads = range(N_HEADS_C)
        old = [tail[hh][:, lane0:] for hh in heads]
        if masked:
            mask = (r0 + lax.broadcasted_iota(I32, (QB, qs - lane0), 0)) < qpos[:, lane0:]
        zs = [jnp.dot(kfull[:, (hh // 2) * LANES:(hh // 2 + 1) * LANES],
                      qct_ref[0, hh * LANES:(hh + 1) * LANES, lane0:], preferred_element_type=F32) for hh in heads]
        log_betas, splits, new_tail = [], [], []
        for hh in heads:
            z = zs[hh]
            log_beta = jnp.minimum(z, 0.0) - jnp.log(1.0 + jnp.exp(-jnp.abs(z)))
            log_keep = log_beta - z
            if masked:
                log_keep = jnp.where(mask, log_keep, 0.0)
            hi = log_keep.astype(BF16)
            lo = (log_keep - hi.astype(F32)).astype(BF16)
            log_betas.append(log_beta)
            splits.append(jnp.concatenate([hi, lo], axis=0))
            new = old[hh] + jnp.sum(log_keep, axis=0, keepdims=True)
            new_tail.append(new if lane0 == 0 else jnp.concatenate([tail[hh][:, :lane0], new], axis=1))
        withins = [jnp.dot(later_ref[...], splits[hh], preferred_element_type=F32) for hh in heads]
        weights = []
        for hh in heads:
            a = jnp.exp(log_betas[hh] + withins[hh] + old[hh])
            if masked:
                a = jnp.where(mask, a, 0.0)
            weights.append(a.astype(BF16))
        for hh in heads:
            r = slice(hh * HEAD_DIM, (hh + 1) * HEAD_DIM)
            acc_scr[r, lane0:] = acc_scr[r, lane0:] + jnp.dot(vt[r, :], weights[hh], preferred_element_type=F32)
        return tuple(new_tail)

    def largest(tail):
        worst = tail[0]
        for hh in range(1, N_HEADS_C):
            worst = jnp.maximum(worst, tail[hh])
        return jnp.max(worst)

    n_rest = i * diag_blocks
    tail = tuple(jnp.zeros((1, qs), F32) for _ in range(N_HEADS_C))
    for d in reversed(range(diag_blocks)):
        tail = block(n_rest + d, tail, True, lane0=d * QB)

    def live(cr):
        return jnp.logical_and(cr[0] < n_rest, cr[2] > STICK_DEAD)

    def step(cr):
        t, tl, _ = cr
        tl = block(n_rest - 1 - t, tl, False)
        return t + 1, tl, largest(tl)

    lax.while_loop(live, step, (jnp.int32(0), tail, largest(tail)))
    yc_ref[0] = acc_scr[...].T.astype(BF16)


def _stick_call(qct, kc, vct, later2):
    b, l, _ = kc.shape
    nq = l // QB
    qs = min(QS_STICK, l)
    return pl.pallas_call(
        _stick_kernel,
        grid=(b, l // qs),
        in_specs=[
            pl.BlockSpec((1, 2 * C_WIDTH, qs), lambda bi, i: (bi, 0, i)),
            pl.BlockSpec((1, l, C_WIDTH), lambda bi, i: (bi, 0, 0)),
            pl.BlockSpec((1, nq, C_WIDTH, QB), lambda bi, i: (bi, 0, 0, 0)),
            pl.BlockSpec((QB, 2 * QB), lambda bi, i: (0, 0)),
        ],
        out_specs=pl.BlockSpec((1, qs, C_WIDTH), lambda bi, i: (bi, i, 0)),
        out_shape=jax.ShapeDtypeStruct((b, l, C_WIDTH), BF16),
        scratch_shapes=[pltpu.VMEM((C_WIDTH, qs), F32)],
        compiler_params=_params(("arbitrary", "arbitrary")),
        name="stick_attention",
    )(qct, kc, vct, later2)


def _merge_kernel(x_ref, ya_ref, yc_ref, mb_ref, ga_ref, gc_ref, gt_ref, sh_ref, sc_ref, g1_ref, g2_ref,
                  wbra_ref, wbrc_ref, wout_ref, *rest, with_router):
    if with_router:
        wr_ref, x1_ref, h2_ref, route_ref, cnt_ref = rest
    else:
        x1_ref, h2_ref = rest
    merged = (ga_ref[0].astype(F32) * jnp.dot(ya_ref[0], wbra_ref[...], preferred_element_type=F32)
              + mb_ref[0].astype(F32)
              + gc_ref[0].astype(F32) * jnp.dot(yc_ref[0], wbrc_ref[...], preferred_element_type=F32))
    y = jnp.dot(merged.astype(BF16), wout_ref[...], preferred_element_type=F32)
    x1 = x_ref[0] + gt_ref[0] * _rms(y, g1_ref[...])
    x1_ref[0] = x1
    h2 = _rms(x1, g2_ref[...]) * (1.0 + sc_ref[0]) + sh_ref[0]
    h2_ref[0] = h2.astype(BF16)
    if with_router:
        logits = jnp.dot(h2, wr_ref[...], preferred_element_type=F32)
        lane = lax.broadcasted_iota(I32, logits.shape, 1)
        valid = lane < N_EXPERTS
        l1 = jnp.where(valid, logits, -jnp.inf)
        v1 = jnp.max(l1, axis=-1, keepdims=True)
        i1 = jnp.min(jnp.where(l1 == v1, lane, LANES), axis=-1, keepdims=True)
        l2 = jnp.where(lane == i1, -jnp.inf, l1)
        v2 = jnp.max(l2, axis=-1, keepdims=True)
        i2 = jnp.min(jnp.where(l2 == v2, lane, LANES), axis=-1, keepdims=True)
        e2 = jnp.exp(v2 - v1)
        p1 = 1.0 / (1.0 + e2)
        p2 = e2 / (1.0 + e2)
        route = jnp.where(lane == 0, i1.astype(F32), jnp.where(lane == 1, i2.astype(F32),
                          jnp.where(lane == 2, p1, jnp.where(lane == 3, p2, 0.0))))
        route_ref[0] = route
        chosen = jnp.where(jnp.logical_or(lane == i1, lane == i2), 1.0, 0.0)
        cnt_ref[0, 0] = jnp.broadcast_to(jnp.sum(chosen, axis=0, keepdims=True), (8, LANES))


def _merge_call(x, ya, yc, mb, ga, gc, gt, sh, sc, g1, g2, wbra, wbrc, wout, wr=None):
    b, l, d = x.shape
    tm = min(TM_MERGE if wr is None else TM_FFN, l)
    tok = lambda w: pl.BlockSpec((1, tm, w), lambda bi, i: (bi, i, 0))
    full = lambda a: pl.BlockSpec(a.shape, lambda bi, i: (0,) * a.ndim)
    vec = pl.BlockSpec((1, 1, d), lambda bi, i: (bi, 0, 0))
    in_specs = [tok(d), tok(A_WIDTH), tok(C_WIDTH), tok(d), tok(d), tok(d), vec, vec, vec,
                full(g1), full(g2), full(wbra), full(wbrc), full(wout)]
    args = [x, ya, yc, mb, ga, gc, gt, sh, sc, g1, g2, wbra, wbrc, wout]
    out_shape = [jax.ShapeDtypeStruct((b, l, d), F32), jax.ShapeDtypeStruct((b, l, d), BF16)]
    out_specs = [tok(d), tok(d)]
    if wr is not None:
        in_specs.append(full(wr))
        args.append(wr)
        out_shape += [jax.ShapeDtypeStruct((b, l, LANES), F32), jax.ShapeDtypeStruct((b, l // tm, 8, LANES), F32)]
        out_specs += [tok(LANES), pl.BlockSpec((1, 1, 8, LANES), lambda bi, i: (bi, i, 0, 0))]
    return pl.pallas_call(
        functools.partial(_merge_kernel, with_router=wr is not None),
        grid=(b, l // tm),
        in_specs=in_specs,
        out_specs=out_specs,
        out_shape=out_shape,
        compiler_params=_params(("arbitrary", "arbitrary")),
        name="merge_router" if wr is not None else "merge",
    )(*args)


def _ffn_kernel(x1_ref, h2_ref, gt_ref, g3_ref, wg_ref, wu_ref, wd_ref, o_ref):
    h2 = h2_ref[0]
    gate = jnp.dot(h2, wg_ref[...], preferred_element_type=F32)
    up = jnp.dot(h2, wu_ref[...], preferred_element_type=F32)
    act = (gate * _sigmoid(gate) * up).astype(BF16)
    y = jnp.dot(act, wd_ref[...], preferred_element_type=F32)
    o_ref[0] = x1_ref[0] + gt_ref[0] * _rms(y, g3_ref[...])


def _ffn_call(x1, h2, gt, g3, wg, wu, wd):
    b, l, d = x1.shape
    tm = min(TM_FFN, l)
    tok = pl.BlockSpec((1, tm, d), lambda bi, i: (bi, i, 0))
    full = lambda a: pl.BlockSpec(a.shape, lambda bi, i: (0,) * a.ndim)
    vec = pl.BlockSpec((1, 1, d), lambda bi, i: (bi, 0, 0))
    return pl.pallas_call(
        _ffn_kernel,
        grid=(b, l // tm),
        in_specs=[tok, tok, vec, full(g3), full(wg), full(wu), full(wd)],
        out_specs=tok,
        out_shape=jax.ShapeDtypeStruct((b, l, d), F32),
        compiler_params=_params(("arbitrary", "arbitrary")),
        name="ffn_dense",
    )(x1, h2, gt, g3, wg, wu, wd)


def _moe_kernel(nchunk_ref, first_ref, total_ref,
                x1_ref, h2_ref, route_ref, before_ref, gt_ref, g3_ref, wg_ref, wu_ref, wd_ref, o_ref,
                xs_scr, wrow_scr, dest_scr, acc_scr):
    tile = pl.program_id(0) * pl.num_programs(1) + pl.program_id(1)
    e = pl.program_id(2)
    n_e = pl.num_programs(2)
    tm = h2_ref.shape[1]
    total = total_ref[tile]

    @pl.when(e == 0)
    def _():
        route = route_ref[0]
        lane = lax.broadcasted_iota(I32, route.shape, 1).astype(F32)
        hot = [jnp.where(lane == route[:, s:s + 1], 1.0, 0.0) for s in range(TOP_K)]
        earlier = jnp.dot(before_ref[...], (hot[0] + hot[1]).astype(BF16), preferred_element_type=F32)
        start = jnp.zeros((1, LANES), F32)
        for ee in range(N_EXPERTS):
            start = jnp.where(lane[:1, :] == ee, (first_ref[tile * N_EXPERTS + ee] * ROW_CHUNK).astype(F32), start)
        where_to = earlier + start
        dest = [hot[s] * where_to for s in range(TOP_K)]
        prob = [hot[s] * route[:, TOP_K + s:TOP_K + s + 1] for s in range(TOP_K)]
        for s in range(TOP_K):
            dest_scr[s] = jnp.broadcast_to(jnp.sum(dest[s], axis=1, keepdims=True), (tm, LANES))
        dest_row = [jnp.sum(dest[s].T, axis=0, keepdims=True) for s in range(TOP_K)]
        prob_row = [jnp.sum(prob[s].T, axis=0, keepdims=True) for s in range(TOP_K)]
        h2 = h2_ref[0]

        def place(c, carry):
            step = 2 * ROW_CHUNK
            r0 = pl.multiple_of(c * step, step)
            rows = (r0 + lax.broadcasted_iota(I32, (step, tm), 0)).astype(F32)
            here = [rows == dest_row[s] for s in range(TOP_K)]
            pick = jnp.where(jnp.logical_or(here[0], here[1]), 1.0, 0.0).astype(BF16)
            xs_scr[pl.ds(r0, step), :] = jnp.dot(pick, h2, preferred_element_type=F32).astype(BF16)
            weight = jnp.where(here[0], prob_row[0], 0.0) + jnp.where(here[1], prob_row[1], 0.0)
            wrow_scr[pl.ds(r0, step), :] = jnp.broadcast_to(jnp.sum(weight, axis=1, keepdims=True), (step, LANES))
            return carry

        lax.fori_loop(0, (total + 1) // 2, place, 0)

    def expert_chunk(c, carry):
        r0 = pl.multiple_of((first_ref[tile * n_e + e] + c) * ROW_CHUNK, ROW_CHUNK)
        xs = xs_scr[pl.ds(r0, ROW_CHUNK), :]
        gate = jnp.dot(xs, wg_ref[0], preferred_element_type=F32)
        up = jnp.dot(xs, wu_ref[0], preferred_element_type=F32)
        act = (gate * _sigmoid(gate) * up).astype(BF16)
        y = jnp.dot(act, wd_ref[0], preferred_element_type=F32)
        xs_scr[pl.ds(r0, ROW_CHUNK), :] = (y * wrow_scr[pl.ds(r0, ROW_CHUNK), 0:1]).astype(BF16)
        return carry

    lax.fori_loop(0, nchunk_ref[tile * n_e + e], expert_chunk, 0)

    @pl.when(e == n_e - 1)
    def _():
        acc_scr[...] = jnp.zeros_like(acc_scr)

        def collect(c, carry):
            r0 = pl.multiple_of(c * 2 * ROW_CHUNK, 2 * ROW_CHUNK)
            halves = []
            for j in range(2):
                cols = (r0 + j * ROW_CHUNK + lax.broadcasted_iota(I32, (tm, ROW_CHUNK), 1)).astype(F32)
                mine = jnp.logical_or(cols == dest_scr[0], cols == dest_scr[1])
                halves.append(jnp.where(mine, 1.0, 0.0).astype(BF16))
            acc_scr[...] = acc_scr[...] + jnp.dot(jnp.concatenate(halves, axis=1),
                                                  xs_scr[pl.ds(r0, 2 * ROW_CHUNK), :], preferred_element_type=F32)
            return carry

        lax.fori_loop(0, (total + 1) // 2, collect, 0)
        o_ref[0] = x1_ref[0] + gt_ref[0] * _rms(acc_scr[...], g3_ref[...])


def _moe_call(x1, h2, route, counts, before, gt, g3, wg, wu, wd):
    b, l, d = x1.shape
    tm = before.shape[0]
    n_e, _, ff = wg.shape
    cnt = counts[:, :, 0, :n_e].astype(I32).reshape(b * (l // tm), -1, n_e).sum(axis=1)
    nchunk = (cnt + ROW_CHUNK - 1) // ROW_CHUNK
    first = jnp.cumsum(nchunk, axis=1) - nchunk
    total = jnp.sum(nchunk, axis=1)
    rows = TOP_K * tm + n_e * ROW_CHUNK
    tok = lambda w: pl.BlockSpec((1, tm, w), lambda bi, i, e, *_: (bi, i, 0))
    tok_once = lambda w: pl.BlockSpec((1, tm, w), lambda bi, i, e, *_: (bi, i, 0), pipeline_mode=pl.Buffered(1))
    vec = pl.BlockSpec((1, 1, d), lambda bi, i, e, *_: (bi, 0, 0))
    grid_spec = pltpu.PrefetchScalarGridSpec(
        num_scalar_prefetch=3,
        grid=(b, l // tm, n_e),
        in_specs=[tok_once(d), tok_once(d), tok(LANES),
                  pl.BlockSpec(before.shape, lambda bi, i, e, *_: (0, 0), pipeline_mode=pl.Buffered(1)), vec,
                  pl.BlockSpec(g3.shape, lambda bi, i, e, *_: (0, 0)),
                  pl.BlockSpec((1, d, ff), lambda bi, i, e, *_: (e, 0, 0)),
                  pl.BlockSpec((1, d, ff), lambda bi, i, e, *_: (e, 0, 0)),
                  pl.BlockSpec((1, ff, d), lambda bi, i, e, *_: (e, 0, 0))],
        out_specs=tok(d),
        scratch_shapes=[pltpu.VMEM((rows, d), BF16), pltpu.VMEM((rows, LANES), F32),
                        pltpu.VMEM((TOP_K, tm, LANES), F32), pltpu.VMEM((tm, d), F32)],
    )
    return pl.pallas_call(
        _moe_kernel,
        grid_spec=grid_spec,
        out_shape=jax.ShapeDtypeStruct((b, l, d), F32),
        compiler_params=_params(("arbitrary", "arbitrary", "arbitrary")),
        name="ffn_moe",
    )(nchunk.reshape(-1), first.reshape(-1), total, x1, h2, route, before, gt, g3, wg, wu, wd)


def _rot_cols(w, n_heads, head_dim):
    d = w.shape[0]
    w = w.reshape(d, n_heads, head_dim)
    half = ROPE_DIM // 2
    rot = jnp.concatenate([-w[..., half:ROPE_DIM], w[..., :half],
                           jnp.zeros((d, n_heads, head_dim - ROPE_DIM), w.dtype)], axis=-1)
    return rot.reshape(d, n_heads * head_dim)


def _pad_cols(w, width):
    return jnp.pad(w, ((0, 0), (0, width - w.shape[1])))


def _layer_weights(w_in, w_uk, w_uv, w_pool):
    d = w_in.shape[0]
    offs, o = [], 0
    for s in IN_SIZES:
        offs.append(o)
        o += s
    piece = lambda k: w_in[:, offs[k]:offs[k] + IN_SIZES[k]]
    w_qa, w_lat, w_kr, w_qi, w_ki, w_wi, w_up, w_qc, w_kc, w_vc, w_gate = [piece(k) for k in range(len(IN_SIZES))]
    attn_scale = HEAD_DIM ** -0.5
    idx_scale = IDX_DIM ** -0.5 * N_IDX_HEADS ** -0.5
    wrow = jnp.concatenate([
        w_lat,
        _pad_cols(w_kr, LANES), _pad_cols(_rot_cols(w_kr, 1, ROPE_DIM), LANES),
        _pad_cols(w_ki, LANES), _pad_cols(_rot_cols(w_ki, 1, IDX_DIM), LANES),
        w_up, w_kc, w_gate], axis=1).astype(BF16)
    w_qc_t = (w_qc * attn_scale).T.reshape(N_HEADS_C, HEAD_DIM, d)
    zeros = jnp.zeros_like(w_qc_t)
    even = (jnp.arange(N_HEADS_C) % 2 == 0)[:, None, None]
    w_qc_pad = jnp.concatenate([jnp.where(even, w_qc_t, zeros), jnp.where(even, zeros, w_qc_t)], axis=1)
    w_qc_pad = w_qc_pad.reshape(2 * C_WIDTH, d)
    softmax_scale = attn_scale * LOG2_E
    wt = jnp.concatenate([
        (w_qa * softmax_scale).T, w_qi.T, w_qc_pad, w_vc.T,
        jnp.pad((w_wi * idx_scale).T, ((0, T_WI_ROWS - N_IDX_HEADS), (0, 0)))], axis=0).astype(BF16)
    wuk = jnp.zeros((KV_LATENT, LANES), F32).at[:, ROPE_DIM:HEAD_DIM].set(w_uk).astype(BF16)
    wuvt = w_uv.T.astype(BF16)
    wpool = jnp.zeros((POOL_WIDTH, POOL_WIDTH), F32)
    for g in range(N_POOL_GROUPS):
        sl = slice(g * POOL_GROUP_DIM, (g + 1) * POOL_GROUP_DIM)
        wpool = wpool.at[sl, sl].set(w_pool[g])
    return wrow, wt, wuk, wuvt, wpool.astype(BF16)


def _rope_tables(positions):
    inv = ROPE_THETA ** (-jnp.arange(0, ROPE_DIM, 2, dtype=F32) / ROPE_DIM)
    ang = positions.astype(F32)[..., None] * inv
    cos, sin = jnp.cos(ang), jnp.sin(ang)
    b, l = positions.shape
    ones = jnp.ones((b, l, HEAD_DIM - ROPE_DIM), F32)
    cos_h = jnp.concatenate([cos, cos, ones], axis=-1)
    sin_h = jnp.concatenate([sin, sin, jnp.zeros_like(ones)], axis=-1)
    cos_r = jnp.concatenate([cos_h, cos_h], axis=-1)
    sin_r = jnp.concatenate([sin_h, sin_h], axis=-1)
    return cos_r, sin_r, cos_h.transpose(0, 2, 1), sin_h.transpose(0, 2, 1)


def kernel(x, c, positions, w_ada, b_ada, norm_gains, w_in, g_kv_latent, w_uk, w_uv, w_pool, pool_scale,
           w_br_a, w_br_b, w_br_c, w_out, w_gate_dense, w_up_dense, w_down_dense,
           w_router, w_gate_moe, w_up_moe, w_down_moe):
    b, l, d = x.shape
    depth = w_in.shape[0]
    assert d == D_MODEL and l % max(KC, TM_FFN, TM_MOE) == 0
    cos_r, sin_r, cos_t, sin_t = _rope_tables(positions)
    c_pad = jnp.pad(c, ((0, 8 - b), (0, 0)))
    mod = _mod_call(c_pad, w_ada, b_ada)[:, :b]
    idx = lax.broadcasted_iota(I32, (QB, QB), 0)
    jdx = lax.broadcasted_iota(I32, (QB, QB), 1)
    tri = (jdx <= idx).astype(BF16)
    later = (jdx > idx).astype(BF16)
    later2 = jnp.concatenate([later, later], axis=1)
    tm_moe = min(TM_MOE, l)
    before = (lax.broadcasted_iota(I32, (tm_moe, tm_moe), 1)
              < lax.broadcasted_iota(I32, (tm_moe, tm_moe), 0)).astype(BF16)
    for layer in range(depth):
        sh1, sc1, gt1, sh2, sc2, gt2 = [m.reshape(b, 1, d) for m in jnp.split(mod[layer], 6, axis=-1)]
        gains = norm_gains[layer].reshape(4, 1, d)
        wrow, wt, wuk, wuvt, wpool = _layer_weights(w_in[layer], w_uk[layer], w_uv[layer], w_pool[layer])
        (qat, qit, wit, ka, vat, ki, qct, kc, vct, ga, gc, mb) = _inproj_call(
            x, sh1, sc1, gains[0], wrow, wt, cos_r, sin_r, cos_t, sin_t,
            g_kv_latent[layer].reshape(1, KV_LATENT), wuk, wuvt, wpool,
            pool_scale[layer].reshape(1, POOL_WIDTH), w_br_b[layer].astype(BF16))
        ya = _dsa_call(qit, wit, qat, ki, ka, vat, tri)
        yc = _stick_call(qct, kc, vct, later2)
        i = layer // 2
        merge_args = (x, ya, yc, mb, ga, gc, gt1, sh2, sc2, gains[1], gains[2],
                      w_br_a[layer].astype(BF16), w_br_c[layer].astype(BF16), w_out[layer].astype(BF16))
        if layer % 2 == 0:
            x1, h2 = _merge_call(*merge_args)
            x = _ffn_call(x1, h2, gt2, gains[3], w_gate_dense[i].astype(BF16), w_up_dense[i].astype(BF16),
                          w_down_dense[i].astype(BF16))
        else:
            x1, h2, route, counts = _merge_call(*merge_args, wr=_pad_cols(w_router[i], LANES))
            x = _moe_call(x1, h2, route, counts, before, gt2, gains[3], w_gate_moe[i].astype(BF16),
                          w_up_moe[i].astype(BF16), w_down_moe[i].astype(BF16))
    return x
```

```python
import functools

import jax
import jax.numpy as jnp
from jax import lax
from jax.experimental import pallas as pl
from jax.experimental.pallas import tpu as pltpu

F32 = jnp.float32
BF16 = jnp.bfloat16
I32 = jnp.int32
I16 = jnp.int16

D_MODEL = 1024
HEAD_DIM = 64
ROPE_DIM = HEAD_DIM // 4
ROPE_THETA = 500000.0
N_HEADS_A = (3 * D_MODEL // 8) // HEAD_DIM
A_WIDTH = N_HEADS_A * HEAD_DIM
KV_LATENT = D_MODEL // 8
N_IDX_HEADS = 4
IDX_DIM = 64
TOPK_MAX = 256
N_POOL_GROUPS = 4
POOL_WINDOWS = (2, 4, 8, 16)
POOL_WIDTH = D_MODEL // 4
POOL_GROUP_DIM = POOL_WIDTH // N_POOL_GROUPS
N_HEADS_C = (D_MODEL // 4) // HEAD_DIM
C_WIDTH = N_HEADS_C * HEAD_DIM
N_BRANCHES = 3
IN_SIZES = (A_WIDTH, KV_LATENT, ROPE_DIM, N_IDX_HEADS * IDX_DIM, IDX_DIM, N_IDX_HEADS,
            POOL_WIDTH, C_WIDTH, C_WIDTH, C_WIDTH, N_BRANCHES * D_MODEL)
D_FF = 2816
N_EXPERTS = 8
TOP_K = 2
D_FF_EXPERT = D_FF // TOP_K
RMS_EPS = 1e-6

LANES = 128
QB = 128
KC = 512
QS_STICK = 512
QS_DSA = 256
KB_DSA = 512
VB_DSA = 256
V_ROWS = 80
STICK_DEAD = -106.0
TM_IN = 512
TM_MERGE = 512
TM_FFN = 512
TM_MOE = 1024
ROW_CHUNK = 128
POOL_HALO = 16
VMEM_LIMIT = 56 * 1024 * 1024
INT_MIN = -2147483648
LOG2_E = 1.4426950408889634
NEG_BIG = -1e30
NEG_MASK = -2e30

R_LAT, R_KR, R_KRR, R_KI, R_KIR, R_UP, R_KC, R_GATE = 0, 128, 256, 384, 512, 640, 896, 1152
T_QA, T_QI, T_QC, T_VC, T_WI = 0, 384, 640, 1152, 1408
T_WI_ROWS = 16


def _params(sem):
    return pltpu.CompilerParams(dimension_semantics=sem, vmem_limit_bytes=VMEM_LIMIT)


def _sigmoid(v):
    return 1.0 / (1.0 + jnp.exp(-v))


def _rms(v, gain):
    return v * lax.rsqrt(jnp.mean(v * v, axis=-1, keepdims=True) + RMS_EPS) * gain


def _mod_kernel(c_ref, w_ref, b_ref, o_ref):
    c = c_ref[...]
    cond = c * _sigmoid(c)
    o_ref[0] = jnp.dot(cond, w_ref[0], preferred_element_type=F32) + b_ref[0]


def _mod_call(c_pad, w_ada, b_ada):
    depth, d, n = w_ada.shape
    tn = 1024
    return pl.pallas_call(
        _mod_kernel,
        grid=(depth, n // tn),
        in_specs=[
            pl.BlockSpec((8, d), lambda l, j: (0, 0)),
            pl.BlockSpec((1, d, tn), lambda l, j: (l, 0, j)),
            pl.BlockSpec((1, 1, tn), lambda l, j: (l, 0, j)),
        ],
        out_specs=pl.BlockSpec((1, 8, tn), lambda l, j: (l, 0, j)),
        out_shape=jax.ShapeDtypeStruct((depth, 8, n), F32),
        compiler_params=_params(("arbitrary", "arbitrary")),
        name="adaln_mod",
    )(c_pad, w_ada, b_ada.reshape(depth, 1, n))


def _inproj_kernel(x_ref, sh_ref, sc_ref, g_ref, wrow_ref, wt_ref, c_ref, s_ref, ct_ref, st_ref,
                   glat_ref, wuk_ref, wuvt_ref, wpool_ref, pscale_ref, wbrb_ref,
                   qat_ref, qit_ref, wit_ref, ka_ref, vat_ref, ki_ref, qct_ref, kc_ref, vct_ref,
                   ga_ref, gc_ref, mb_ref,
                   h_scr, ht_scr, ext_scr, prev_scr):
    i = pl.program_id(1)
    tm = x_ref.shape[1]
    x = x_ref[0]
    h = _rms(x, g_ref[...]) * (1.0 + sc_ref[0]) + sh_ref[0]
    h_scr[...] = h.astype(BF16)
    ht_scr[...] = h.T.astype(BF16)

    def rowdot(a, width):
        return jnp.dot(h_scr[...], wrow_ref[:, a:a + width], preferred_element_type=F32)

    def tdot(a, height):
        return jnp.dot(wt_ref[a:a + height, :], ht_scr[...], preferred_element_type=F32)

    cos_r, sin_r = c_ref[0], s_ref[0]
    cos_t, sin_t = ct_ref[0], st_ref[0]

    latn = _rms(rowdot(R_LAT, KV_LATENT), glat_ref[...])
    ka = (rowdot(R_KR, LANES) * cos_r + rowdot(R_KRR, LANES) * sin_r
          + jnp.dot(latn.astype(BF16), wuk_ref[...], preferred_element_type=F32))
    ka_ref[0] = ka[:, :HEAD_DIM].astype(BF16)
    vat = jnp.dot(wuvt_ref[...], latn.T.astype(BF16), preferred_element_type=F32)
    pad_row = lax.broadcasted_iota(I32, (V_ROWS - HEAD_DIM, tm), 0)
    vat = jnp.concatenate([vat, jnp.where(pad_row == 0, 1.0, 0.0)], axis=0)
    for j in range(tm // VB_DSA):
        vat_ref[0, j] = vat[:, j * VB_DSA:(j + 1) * VB_DSA].astype(BF16)

    ki =rowdot(R_KI, LANES) * cos_r + rowdot(R_KIR, LANES) * sin_r
    ki_ref[0] = ki[:, :IDX_DIM].astype(BF16)

    half = ROPE_DIM // 2
    cos_f, sin_f = cos_t[:half], sin_t[:half]

    def rope_t(q):
        x1, x2 = q[:half], q[half:ROPE_DIM]
        return jnp.concatenate([x1 * cos_f - x2 * sin_f, x2 * cos_f + x1 * sin_f, q[ROPE_DIM:]], axis=0)

    qa = tdot(T_QA, A_WIDTH)
    for hh in range(N_HEADS_A):
        r = slice(hh * HEAD_DIM, (hh + 1) * HEAD_DIM)
        qat_ref[0, r, :] = rope_t(qa[r]).astype(BF16)
    qi = tdot(T_QI, N_IDX_HEADS * IDX_DIM)
    for hh in range(N_IDX_HEADS):
        r = slice(hh * IDX_DIM, (hh + 1) * IDX_DIM)
        qit_ref[0, r, :] = rope_t(qi[r]).astype(BF16)
    wit_ref[0] = tdot(T_WI, T_WI_ROWS)

    qct_ref[0] = tdot(T_QC, 2 * C_WIDTH).astype(BF16)
    kc_ref[0] = rowdot(R_KC, C_WIDTH).astype(BF16)
    vct = tdot(T_VC, C_WIDTH)
    for j in range(tm // QB):
        vct_ref[0, j] = vct[:, j * QB:(j + 1) * QB].astype(BF16)

    up = rowdot(R_UP, POOL_WIDTH)

    @pl.when(i == 0)
    def _():
        prev_scr[...] = jnp.zeros_like(prev_scr)

    ext_scr[0:POOL_HALO, :] = prev_scr[...]
    ext_scr[POOL_HALO:POOL_HALO + tm, :] = up
    prev_scr[...] = up[tm - POOL_HALO:, :]
    lag = [ext_scr[POOL_HALO - j:POOL_HALO - j + tm, :] for j in range(POOL_HALO)]
    sums = {}
    run = lag[0]
    for j in range(1, POOL_HALO):
        run = run + lag[j]
        if j + 1 in POOL_WINDOWS:
            sums[j + 1] = run
    lane = lax.broadcasted_iota(I32, (tm, POOL_WIDTH), 1)
    pos = i * tm + lax.broadcasted_iota(I32, (tm, POOL_WIDTH), 0)
    pooled_sum = sums[POOL_WINDOWS[-1]]
    win = jnp.full((tm, POOL_WIDTH), POOL_WINDOWS[-1], I32)
    for g in range(N_POOL_GROUPS - 2, -1, -1):
        in_group = lane < (g + 1) * POOL_GROUP_DIM
        pooled_sum = jnp.where(in_group, sums[POOL_WINDOWS[g]], pooled_sum)
        win = jnp.where(in_group, POOL_WINDOWS[g], win)
    cnt = jnp.minimum(pos + 1, win).astype(F32)
    pooled = pooled_sum / cnt - up
    yb = jnp.dot(pooled.astype(BF16), wpool_ref[...], preferred_element_type=F32) * pscale_ref[...]

    ga_ref[0] = _sigmoid(rowdot(R_GATE, D_MODEL)).astype(BF16)
    gb = _sigmoid(rowdot(R_GATE + D_MODEL, D_MODEL))
    mb_ref[0] = (gb * jnp.dot(yb.astype(BF16), wbrb_ref[...], preferred_element_type=F32)).astype(BF16)
    gc_ref[0] = _sigmoid(rowdot(R_GATE + 2 * D_MODEL, D_MODEL)).astype(BF16)


def _inproj_call(x, sh, sc, gain, wrow, wt, cos_r, sin_r, cos_t, sin_t, glat, wuk, wuvt, wpool, pscale, wbrb):
    b, l, d = x.shape
    tm = min(TM_IN, l)
    nq = l // QB
    tok = lambda w: pl.BlockSpec((1, tm, w), lambda bi, i: (bi, i, 0))
    feat = lambda hgt: pl.BlockSpec((1, hgt, tm), lambda bi, i: (bi, 0, i))
    blk = lambda hgt, w: pl.BlockSpec((1, tm // w, hgt, w), lambda bi, i: (bi, i, 0, 0))
    full = lambda a: pl.BlockSpec(a.shape, lambda bi, i: (0,) * a.ndim)
    vec = pl.BlockSpec((1, 1, d), lambda bi, i: (bi, 0, 0))
    out_shape = (
        jax.ShapeDtypeStruct((b, A_WIDTH, l), BF16),
        jax.ShapeDtypeStruct((b, N_IDX_HEADS * IDX_DIM, l), BF16),
        jax.ShapeDtypeStruct((b, T_WI_ROWS, l), F32),
        jax.ShapeDtypeStruct((b, l, HEAD_DIM), BF16),
        jax.ShapeDtypeStruct((b, l // VB_DSA, V_ROWS, VB_DSA), BF16),
        jax.ShapeDtypeStruct((b, l, IDX_DIM), BF16),
        jax.ShapeDtypeStruct((b, 2 * C_WIDTH, l), BF16),
        jax.ShapeDtypeStruct((b, l, C_WIDTH), BF16),
        jax.ShapeDtypeStruct((b, nq, C_WIDTH, QB), BF16),
        jax.ShapeDtypeStruct((b, l, d), BF16),
        jax.ShapeDtypeStruct((b, l, d), BF16),
        jax.ShapeDtypeStruct((b, l, d), BF16),
    )
    out_specs = (feat(A_WIDTH), feat(N_IDX_HEADS * IDX_DIM), feat(T_WI_ROWS), tok(HEAD_DIM), blk(V_ROWS, VB_DSA),
                 tok(IDX_DIM), feat(2 * C_WIDTH), tok(C_WIDTH), blk(C_WIDTH, QB), tok(d), tok(d), tok(d))
    resident = lambda a: pl.BlockSpec(a.shape, lambda bi, i: (0,) * a.ndim, pipeline_mode=pl.Buffered(1))
    in_specs = [tok(d), vec, vec, full(gain), resident(wrow), resident(wt), tok(LANES), tok(LANES),
                feat(HEAD_DIM), feat(HEAD_DIM), full(glat), full(wuk), full(wuvt), full(wpool),
                full(pscale), full(wbrb)]
    return pl.pallas_call(
        _inproj_kernel,
        grid=(b, l // tm),
        in_specs=in_specs,
        out_specs=out_specs,
        out_shape=out_shape,
        scratch_shapes=[pltpu.VMEM((tm, d), BF16), pltpu.VMEM((d, tm), BF16),
                        pltpu.VMEM((tm + POOL_HALO, POOL_WIDTH), F32), pltpu.VMEM((POOL_HALO, POOL_WIDTH), F32)],
        compiler_params=_params(("arbitrary", "arbitrary")),
        name="inproj",
    )(x, sh, sc, gain, wrow, wt, cos_r, sin_r, cos_t, sin_t, glat, wuk, wuvt, wpool, pscale, wbrb)


def _dsa_kernel(qit_ref, wit_ref, qat_ref, ki_ref, ka_ref, vat_ref, tri_ref, ya_ref,
                keys_scr, top_scr, acc_scr, *, k_sel):
    i = pl.program_id(1)
    qs = qat_ref.shape[2]
    n_chunks = ((i + 1) * qs + KC - 1) // KC
    qpos = i * qs + lax.broadcasted_iota(I32, (1, qs), 1)
    w_idx = wit_ref[0]

    def score_chunk(c, carry, masked):
        r0 = pl.multiple_of(c * KC, KC)
        kblk = ki_ref[0, pl.ds(r0, KC), :]
        parts = [jnp.dot(kblk, qit_ref[0, hh * IDX_DIM:(hh + 1) * IDX_DIM, :], preferred_element_type=F32)
                 for hh in range(N_IDX_HEADS)]
        score = jnp.maximum(parts[0], 0.0) * w_idx[0:1, :]
        for hh in range(1, N_IDX_HEADS):
            score = score + jnp.maximum(parts[hh], 0.0) * w_idx[hh:hh + 1, :]
        bits = lax.bitcast_convert_type(score, I32)
        key = jnp.where(bits < 0, INT_MIN - bits, bits)
        if masked:
            kpos = r0 + lax.broadcasted_iota(I32, (KC, qs), 0)
            key = jnp.where(kpos <= qpos, key, INT_MIN)
        keys_scr[pl.ds(r0, KC), :] = key
        top = lax.shift_right_arithmetic(key, 16).astype(I16)
        top_scr[pl.ds(r0, KC), :] = top
        groups = top.reshape(KC // 16, 16, qs)
        for j in range(KC // 16):
            carry = jnp.where(groups[j] > carry, groups[j], carry)
        return carry

    n_open = (i * qs) // KC
    top_max = lax.fori_loop(0, n_open, functools.partial(score_chunk, masked=False),
                            jnp.full((16, qs), -2 ** 15, I16))
    top_max = lax.fori_loop(n_open, n_chunks, functools.partial(score_chunk, masked=True), top_max)
    top_max = jnp.max(top_max.astype(I32), axis=0, keepdims=True)

    def count_rows(src, rows_per_vreg, trial):
        n_acc = 4
        groups = KC // rows_per_vreg

        def body(c, accs):
            r0 = pl.multiple_of(c * KC, KC)
            rows = src[pl.ds(r0, KC), :].reshape(groups, rows_per_vreg, qs)
            accs = list(accs)
            for j in range(groups):
                a = accs[j % n_acc]
                accs[j % n_acc] = jnp.where(rows[j] >= trial, a + 1, a)
            return tuple(accs)

        zero = jnp.zeros((rows_per_vreg, qs), src.dtype)
        accs = lax.fori_loop(0, n_chunks, body, tuple(zero for _ in range(n_acc)))
        total = (accs[0] + accs[1]) + (accs[2] + accs[3])
        return jnp.sum(total.astype(I32), axis=0, keepdims=True)

    def count_ge(trial):
        return count_rows(keys_scr, 8, trial)

    def count_ge_top(trial):
        return count_rows(top_scr, 16, lax.shift_right_arithmetic(trial, 16).astype(I16))

    c_zero = count_ge_top(jnp.zeros((1, qs), I32))
    c_pos = count_ge(jnp.ones((1, qs), I32))
    tie_at_zero = jnp.logical_and(c_zero >= k_sel, c_pos < k_sel)

    def unsettled(c_ans):
        settled = jnp.logical_or(tie_at_zero, c_ans == k_sel)
        return jnp.max(jnp.where(settled, 0, 1))

    def search(counter, lowest_bit, group, state):
        def refine(carry):
            bit, ans, c_ans, _ = carry
            for g in range(group):
                trial = ans + lax.shift_left(jnp.int32(1), bit - g)
                c = counter(trial)
                ok = c >= k_sel
                c_ans = jnp.where(ok, c, c_ans)
                ans = jnp.where(ok, trial, ans)
            return bit - group, ans, c_ans, unsettled(c_ans)

        return lax.while_loop(lambda cr: jnp.logical_and(cr[0] >= lowest_bit, cr[3] > 0), refine, state)

    def count_within(gap):
        return count_rows(top_scr, 16, jnp.maximum(top_max - gap, -2 ** 15).astype(I16))

    near_bits = 9
    c_near = count_within(2 ** near_bits - 1)
    short = qpos + 1 < k_sel
    known = jnp.logical_or(tie_at_zero, short)
    all_near = jnp.min(jnp.where(jnp.logical_or(c_near >= k_sel, known), 1, 0)) > 0
    first_bit = jnp.where(all_near, near_bits - 1, 15)

    def widen(group, carry):
        bit, gap, c_gap = carry
        for g in range(group):
            trial = gap + lax.shift_left(jnp.int32(1), bit - g) - 1
            c = count_within(trial)
            ok = c >= k_sel
            gap = jnp.where(ok, gap, trial + 1)
            c_gap = jnp.where(ok, c, c_gap)
        return bit - group, gap, c_gap

    c_far = jnp.where(all_near, c_near, jnp.int32(2 ** 30))
    state = (first_bit, jnp.zeros((1, qs), I32), c_far)
    state = lax.while_loop(lambda cr: cr[0] >= near_bits, functools.partial(widen, 16 - near_bits), state)
    _, gap, c_top = lax.while_loop(lambda cr: cr[0] >= 0, functools.partial(widen, 3), state)
    bucket = jnp.where(short, -2 ** 15, jnp.where(tie_at_zero, 0, jnp.maximum(top_max - gap, -2 ** 15)))
    c_top = jnp.where(tie_at_zero, c_zero, c_top)
    top = lax.shift_left(bucket, 16)
    alive = unsettled(c_top)
    last_bucket = bucket >= 2 ** 15 - 1
    c_over = jnp.where(last_bucket, 0,
                       count_rows(top_scr, 16, jnp.where(last_bucket, bucket, bucket + 1).astype(I16)))

    def pack_low(c, carry):
        r0 = pl.multiple_of(c * KC, KC)
        low = (keys_scr[pl.ds(r0, KC), :] ^ 0x8000).astype(I16)
        top_scr[pl.ds(r0, KC), :] = jnp.where(top_scr[pl.ds(r0, KC), :] == bucket.astype(I16), low, -2 ** 15)
        return carry

    lax.fori_loop(0, n_chunks, pack_low, 0)

    def count_ge_low(low_trial):
        return c_over + count_rows(top_scr, 16, (low_trial - 2 ** 15).astype(I16))

    zero_row = jnp.zeros((1, qs), I32)
    _, low, _, _ = search(count_ge_low, 0, 8, (jnp.int32(15), zero_row, c_top, alive))
    thr = top + low
    full_low = low >= 2 ** 16 - 1
    n_above = jnp.where(full_low, c_over, count_ge_low(jnp.where(full_low, low, low + 1)))
    n_ties = jnp.where(thr == INT_MIN, 0, k_sel - n_above).astype(F32)

    acc_scr[...] = jnp.zeros_like(acc_scr)
    heads = range(N_HEADS_A)

    kb_rows = KB_DSA
    vb_rows = vat_ref.shape[3]
    v_per_k = kb_rows // vb_rows

    n_steps = ((i + 1) * qs + kb_rows - 1) // kb_rows

    v_rows = vat_ref.shape[2]

    def attend(kb, carry):
        seen, ms = carry
        r0 = pl.multiple_of(kb * kb_rows, kb_rows)
        keyb = keys_scr[pl.ds(r0, kb_rows), :]
        tied = keyb == thr
        tied_b = jnp.where(tied, 1.0, 0.0).astype(BF16)
        ranks = []
        for j in range(kb_rows // QB):
            within = jnp.dot(tri_ref[...], tied_b[j * QB:(j + 1) * QB, :], preferred_element_type=F32)
            ranks.append(seen + within)
            seen = seen + within[QB - 1:QB, :]
        rank = jnp.concatenate(ranks, axis=0)
        keep = jnp.logical_or(keyb > thr, jnp.logical_and(tied, rank <= n_ties))
        bias = jnp.where(keep, 0.0, NEG_MASK).astype(BF16)
        kblk = ka_ref[0, pl.ds(r0, kb_rows), :]
        logits = [jnp.dot(kblk, qat_ref[0, hh * HEAD_DIM:(hh + 1) * HEAD_DIM, :],
                          preferred_element_type=F32).astype(BF16) + bias for hh in heads]
        new_ms = [jnp.maximum(ms[hh], jnp.max(logits[hh], axis=0, keepdims=True).astype(F32)) for hh in heads]
        probs = [jnp.exp2(logits[hh] - new_ms[hh].astype(BF16)) for hh in heads]
        alphas = [jnp.exp2(ms[hh] - new_ms[hh]) for hh in heads]
        outs = []
        for hh in heads:
            out = jnp.dot(vat_ref[0, kb * v_per_k], probs[hh][:vb_rows, :], preferred_element_type=F32)
            for j in range(1, v_per_k):
                out = out + jnp.dot(vat_ref[0, kb * v_per_k + j], probs[hh][j * vb_rows:(j + 1) * vb_rows, :],
                                    preferred_element_type=F32)
            outs.append(out)
        for hh in heads:
            r = slice(hh * v_rows, (hh + 1) * v_rows)
            acc_scr[r, :] = acc_scr[r, :] * alphas[hh] + outs[hh]
        return seen, tuple(new_ms)

    init = (jnp.zeros((1, qs), F32), tuple(jnp.full((1, qs), NEG_BIG, F32) for _ in heads))
    lax.fori_loop(0, n_steps, attend, init)
    normed = [acc_scr[hh * v_rows:hh * v_rows + HEAD_DIM, :] / acc_scr[hh * v_rows + HEAD_DIM:hh * v_rows + HEAD_DIM + 1, :]
              for hh in heads]
    ya_ref[0] = jnp.concatenate(normed, axis=0).T.astype(BF16)


def _dsa_call(qit, wit, qat, ki, ka, vat, tri):
    b, l, _ = ki.shape
    qs = min(QS_DSA, l)
    k_sel = min(TOPK_MAX, l // 4)
    return pl.pallas_call(
        functools.partial(_dsa_kernel, k_sel=k_sel),
        grid=(b, l // qs),
        in_specs=[
            pl.BlockSpec((1, N_IDX_HEADS * IDX_DIM, qs), lambda bi, i: (bi, 0, i)),
            pl.BlockSpec((1, T_WI_ROWS, qs), lambda bi, i: (bi, 0, i)),
            pl.BlockSpec((1, A_WIDTH, qs), lambda bi, i: (bi, 0, i)),
            pl.BlockSpec((1, l, IDX_DIM), lambda bi, i: (bi, 0, 0), pipeline_mode=pl.Buffered(1)),
            pl.BlockSpec((1, l, HEAD_DIM), lambda bi, i: (bi, 0, 0), pipeline_mode=pl.Buffered(1)),
            pl.BlockSpec((1, l // VB_DSA, V_ROWS, VB_DSA), lambda bi, i: (bi, 0, 0, 0),
                         pipeline_mode=pl.Buffered(1)),
            pl.BlockSpec((QB, QB), lambda bi, i: (0, 0)),
        ],
        out_specs=pl.BlockSpec((1, qs, A_WIDTH), lambda bi, i: (bi, i, 0)),
        out_shape=jax.ShapeDtypeStruct((b, l, A_WIDTH), BF16),
        scratch_shapes=[pltpu.VMEM((l, qs), I32), pltpu.VMEM((l, qs), I16),
                        pltpu.VMEM((N_HEADS_A * V_ROWS, qs), F32)],
        compiler_params=_params(("arbitrary", "arbitrary")),
        name="dsa_attention",
    )(qit, wit, qat, ki, ka, vat, tri)


def _stick_kernel(qct_ref, kc_ref, vct_ref, later_ref, yc_ref, acc_scr):
    i = pl.program_id(1)
    qs = qct_ref.shape[2]
    diag_blocks = qs // QB
    qpos = i * qs + lax.broadcasted_iota(I32, (1, qs), 1)
    acc_scr[...] = jnp.zeros_like(acc_scr)

    def block(kb, tail, masked, lane0=0):
        r0 = pl.multiple_of(kb * QB, QB)
        kfull = kc_ref[0, pl.ds(r0, QB), :]
        vt = vct_ref[0, kb]
        heads = range(N_HEADS_C)
        old = [tail[hh][:, lane0:] for hh in heads]
        if masked:
            mask = (r0 + lax.broadcasted_iota(I32, (QB, qs - lane0), 0)) < qpos[:, lane0:]
        zs = [jnp.dot(kfull[:, (hh // 2) * LANES:(hh // 2 + 1) * LANES],
                      qct_ref[0, hh * LANES:(hh + 1) * LANES, lane0:], preferred_element_type=F32) for hh in heads]
        log_betas, splits, new_tail = [], [], []
        for hh in heads:
            z = zs[hh]
            log_beta = jnp.minimum(z, 0.0) - jnp.log(1.0 + jnp.exp(-jnp.abs(z)))
            log_keep = log_beta - z
            if masked:
                log_keep = jnp.where(mask, log_keep, 0.0)
            hi = log_keep.astype(BF16)
            lo = (log_keep - hi.astype(F32)).astype(BF16)
            log_betas.append(log_beta)
            splits.append(jnp.concatenate([hi, lo], axis=0))
            new = old[hh] + jnp.sum(log_keep, axis=0, keepdims=True)
            new_tail.append(new if lane0 == 0 else jnp.concatenate([tail[hh][:, :lane0], new], axis=1))
        withins = [jnp.dot(later_ref[...], splits[hh], preferred_element_type=F32) for hh in heads]
        weights = []
        for hh in heads:
            a = jnp.exp(log_betas[hh] + withins[hh] + old[hh])
            if masked:
                a = jnp.where(mask, a, 0.0)
            weights.append(a.astype(BF16))
        for hh in heads:
            r = slice(hh * HEAD_DIM, (hh + 1) * HEAD_DIM)
            acc_scr[r, lane0:] = acc_scr[r, lane0:] + jnp.dot(vt[r, :], weights[hh], preferred_element_type=F32)
        return tuple(new_tail)

    def largest(tail):
        worst = tail[0]
        for hh in range(1, N_HEADS_C):
            worst = jnp.maximum(worst, tail[hh])
        return jnp.max(worst)

    n_rest = i * diag_blocks
    tail = tuple(jnp.zeros((1, qs), F32) for _ in range(N_HEADS_C))
    for d in reversed(range(diag_blocks)):
        tail = block(n_rest + d, tail, True, lane0=d * QB)

    def live(cr):
        return jnp.logical_and(cr[0] < n_rest, cr[2] > STICK_DEAD)

    def step(cr):
        t, tl, _ = cr
        tl = block(n_rest - 1 - t, tl, False)
        return t + 1, tl, largest(tl)

    lax.while_loop(live, step, (jnp.int32(0), tail, largest(tail)))
    yc_ref[0] = acc_scr[...].T.astype(BF16)


def _stick_call(qct, kc, vct, later2):
    b, l, _ = kc.shape
    nq = l // QB
    qs = min(QS_STICK, l)
    return pl.pallas_call(
        _stick_kernel,
        grid=(b, l // qs),
        in_specs=[
            pl.BlockSpec((1, 2 * C_WIDTH, qs), lambda bi, i: (bi, 0, i)),
            pl.BlockSpec((1, l, C_WIDTH), lambda bi, i: (bi, 0, 0)),
            pl.BlockSpec((1, nq, C_WIDTH, QB), lambda bi, i: (bi, 0, 0, 0)),
            pl.BlockSpec((QB, 2 * QB), lambda bi, i: (0, 0)),
        ],
        out_specs=pl.BlockSpec((1, qs, C_WIDTH), lambda bi, i: (bi, i, 0)),
        out_shape=jax.ShapeDtypeStruct((b, l, C_WIDTH), BF16),
        scratch_shapes=[pltpu.VMEM((C_WIDTH, qs), F32)],
        compiler_params=_params(("arbitrary", "arbitrary")),
        name="stick_attention",
    )(qct, kc, vct, later2)


def _merge_kernel(x_ref, ya_ref, yc_ref, mb_ref, ga_ref, gc_ref, gt_ref, sh_ref, sc_ref, g1_ref, g2_ref,
                  wbra_ref, wbrc_ref, wout_ref, *rest, with_router):
    if with_router:
        wr_ref, x1_ref, h2_ref, route_ref, cnt_ref = rest
    else:
        x1_ref, h2_ref = rest
    merged = (ga_ref[0].astype(F32) * jnp.dot(ya_ref[0], wbra_ref[...], preferred_element_type=F32)
              + mb_ref[0].astype(F32)
              + gc_ref[0].astype(F32) * jnp.dot(yc_ref[0], wbrc_ref[...], preferred_element_type=F32))
    y = jnp.dot(merged.astype(BF16), wout_ref[...], preferred_element_type=F32)
    x1 = x_ref[0] + gt_ref[0] * _rms(y, g1_ref[...])
    x1_ref[0] = x1
    h2 = _rms(x1, g2_ref[...]) * (1.0 + sc_ref[0]) + sh_ref[0]
    h2_ref[0] = h2.astype(BF16)
    if with_router:
        logits = jnp.dot(h2, wr_ref[...], preferred_element_type=F32)
        lane = lax.broadcasted_iota(I32, logits.shape, 1)
        valid = lane < N_EXPERTS
        l1 = jnp.where(valid, logits, -jnp.inf)
        v1 = jnp.max(l1, axis=-1, keepdims=True)
        i1 = jnp.min(jnp.where(l1 == v1, lane, LANES), axis=-1, keepdims=True)
        l2 = jnp.where(lane == i1, -jnp.inf, l1)
        v2 = jnp.max(l2, axis=-1, keepdims=True)
        i2 = jnp.min(jnp.where(l2 == v2, lane, LANES), axis=-1, keepdims=True)
        e2 = jnp.exp(v2 - v1)
        p1 = 1.0 / (1.0 + e2)
        p2 = e2 / (1.0 + e2)
        route = jnp.where(lane == 0, i1.astype(F32), jnp.where(lane == 1, i2.astype(F32),
                          jnp.where(lane == 2, p1, jnp.where(lane == 3, p2, 0.0))))
        route_ref[0] = route
        chosen = jnp.where(jnp.logical_or(lane == i1, lane == i2), 1.0, 0.0)
        cnt_ref[0, 0] = jnp.broadcast_to(jnp.sum(chosen, axis=0, keepdims=True), (8, LANES))


def _merge_call(x, ya, yc, mb, ga, gc, gt, sh, sc, g1, g2, wbra, wbrc, wout, wr=None):
    b, l, d = x.shape
    tm = min(TM_MERGE if wr is None else TM_FFN, l)
    tok = lambda w: pl.BlockSpec((1, tm, w), lambda bi, i: (bi, i, 0))
    full = lambda a: pl.BlockSpec(a.shape, lambda bi, i: (0,) * a.ndim)
    vec = pl.BlockSpec((1, 1, d), lambda bi, i: (bi, 0, 0))
    in_specs = [tok(d), tok(A_WIDTH), tok(C_WIDTH), tok(d), tok(d), tok(d), vec, vec, vec,
                full(g1), full(g2), full(wbra), full(wbrc), full(wout)]
    args = [x, ya, yc, mb, ga, gc, gt, sh, sc, g1, g2, wbra, wbrc, wout]
    out_shape = [jax.ShapeDtypeStruct((b, l, d), F32), jax.ShapeDtypeStruct((b, l, d), BF16)]
    out_specs = [tok(d), tok(d)]
    if wr is not None:
        in_specs.append(full(wr))
        args.append(wr)
        out_shape += [jax.ShapeDtypeStruct((b, l, LANES), F32), jax.ShapeDtypeStruct((b, l // tm, 8, LANES), F32)]
        out_specs += [tok(LANES), pl.BlockSpec((1, 1, 8, LANES), lambda bi, i: (bi, i, 0, 0))]
    return pl.pallas_call(
        functools.partial(_merge_kernel, with_router=wr is not None),
        grid=(b, l // tm),
        in_specs=in_specs,
        out_specs=out_specs,
        out_shape=out_shape,
        compiler_params=_params(("arbitrary", "arbitrary")),
        name="merge_router" if wr is not None else "merge",
    )(*args)


def _ffn_kernel(x1_ref, h2_ref, gt_ref, g3_ref, wg_ref, wu_ref, wd_ref, o_ref):
    h2 = h2_ref[0]
    gate = jnp.dot(h2, wg_ref[...], preferred_element_type=F32)
    up = jnp.dot(h2, wu_ref[...], preferred_element_type=F32)
    act = (gate * _sigmoid(gate) * up).astype(BF16)
    y = jnp.dot(act, wd_ref[...], preferred_element_type=F32)
    o_ref[0] = x1_ref[0] + gt_ref[0] * _rms(y, g3_ref[...])


def _ffn_call(x1, h2, gt, g3, wg, wu, wd):
    b, l, d = x1.shape
    tm = min(TM_FFN, l)
    tok = pl.BlockSpec((1, tm, d), lambda bi, i: (bi, i, 0))
    full = lambda a: pl.BlockSpec(a.shape, lambda bi, i: (0,) * a.ndim)
    vec = pl.BlockSpec((1, 1, d), lambda bi, i: (bi, 0, 0))
    return pl.pallas_call(
        _ffn_kernel,
        grid=(b, l // tm),
        in_specs=[tok, tok, vec, full(g3), full(wg), full(wu), full(wd)],
        out_specs=tok,
        out_shape=jax.ShapeDtypeStruct((b, l, d), F32),
        compiler_params=_params(("arbitrary", "arbitrary")),
        name="ffn_dense",
    )(x1, h2, gt, g3, wg, wu, wd)


def _moe_kernel(nchunk_ref, first_ref, total_ref,
                x1_ref, h2_ref, route_ref, before_ref, gt_ref, g3_ref, wg_ref, wu_ref, wd_ref, o_ref,
                xs_scr, wrow_scr, dest_scr, acc_scr):
    tile = pl.program_id(0) * pl.num_programs(1) + pl.program_id(1)
    e = pl.program_id(2)
    n_e = pl.num_programs(2)
    tm = h2_ref.shape[1]
    total = total_ref[tile]

    @pl.when(e == 0)
    def _():
        route = route_ref[0]
        lane = lax.broadcasted_iota(I32, route.shape, 1).astype(F32)
        hot = [jnp.where(lane == route[:, s:s + 1], 1.0, 0.0) for s in range(TOP_K)]
        earlier = jnp.dot(before_ref[...], (hot[0] + hot[1]).astype(BF16), preferred_element_type=F32)
        start = jnp.zeros((1, LANES), F32)
        for ee in range(N_EXPERTS):
            start = jnp.where(lane[:1, :] == ee, (first_ref[tile * N_EXPERTS + ee] * ROW_CHUNK).astype(F32), start)
        where_to = earlier + start
        dest = [hot[s] * where_to for s in range(TOP_K)]
        prob = [hot[s] * route[:, TOP_K + s:TOP_K + s + 1] for s in range(TOP_K)]
        for s in range(TOP_K):
            dest_scr[s] = jnp.broadcast_to(jnp.sum(dest[s], axis=1, keepdims=True), (tm, LANES))
        dest_row = [jnp.sum(dest[s].T, axis=0, keepdims=True) for s in range(TOP_K)]
        prob_row = [jnp.sum(prob[s].T, axis=0, keepdims=True) for s in range(TOP_K)]
        h2 = h2_ref[0]

        def place(c, carry):
            step = 2 * ROW_CHUNK
            r0 = pl.multiple_of(c * step, step)
            rows = (r0 + lax.broadcasted_iota(I32, (step, tm), 0)).astype(F32)
            here = [rows == dest_row[s] for s in range(TOP_K)]
            pick = jnp.where(jnp.logical_or(here[0], here[1]), 1.0, 0.0).astype(BF16)
            xs_scr[pl.ds(r0, step), :] = jnp.dot(pick, h2, preferred_element_type=F32).astype(BF16)
            weight = jnp.where(here[0], prob_row[0], 0.0) + jnp.where(here[1], prob_row[1], 0.0)
            wrow_scr[pl.ds(r0, step), :] = jnp.broadcast_to(jnp.sum(weight, axis=1, keepdims=True), (step, LANES))
            return carry

        lax.fori_loop(0, (total + 1) // 2, place, 0)

    def expert_chunk(c, carry):
        r0 = pl.multiple_of((first_ref[tile * n_e + e] + c) * ROW_CHUNK, ROW_CHUNK)
        xs = xs_scr[pl.ds(r0, ROW_CHUNK), :]
        gate = jnp.dot(xs, wg_ref[0], preferred_element_type=F32)
        up = jnp.dot(xs, wu_ref[0], preferred_element_type=F32)
        act = (gate * _sigmoid(gate) * up).astype(BF16)
        y = jnp.dot(act, wd_ref[0], preferred_element_type=F32)
        xs_scr[pl.ds(r0, ROW_CHUNK), :] = (y * wrow_scr[pl.ds(r0, ROW_CHUNK), 0:1]).astype(BF16)
        return carry

    lax.fori_loop(0, nchunk_ref[tile * n_e + e], expert_chunk, 0)

    @pl.when(e == n_e - 1)
    def _():
        acc_scr[...] = jnp.zeros_like(acc_scr)

        def collect(c, carry):
            r0 = pl.multiple_of(c * 2 * ROW_CHUNK, 2 * ROW_CHUNK)
            halves = []
            for j in range(2):
                cols = (r0 + j * ROW_CHUNK + lax.broadcasted_iota(I32, (tm, ROW_CHUNK), 1)).astype(F32)
                mine = jnp.logical_or(cols == dest_scr[0], cols == dest_scr[1])
                halves.append(jnp.where(mine, 1.0, 0.0).astype(BF16))
            acc_scr[...] = acc_scr[...] + jnp.dot(jnp.concatenate(halves, axis=1),
                                                  xs_scr[pl.ds(r0, 2 * ROW_CHUNK), :], preferred_element_type=F32)
            return carry

        lax.fori_loop(0, (total + 1) // 2, collect, 0)
        o_ref[0] = x1_ref[0] + gt_ref[0] * _rms(acc_scr[...], g3_ref[...])


def _moe_call(x1, h2, route, counts, before, gt, g3, wg, wu, wd):
    b, l, d = x1.shape
    tm = before.shape[0]
    n_e, _, ff = wg.shape
    cnt = counts[:, :, 0, :n_e].astype(I32).reshape(b * (l // tm), -1, n_e).sum(axis=1)
    nchunk = (cnt + ROW_CHUNK - 1) // ROW_CHUNK
    first = jnp.cumsum(nchunk, axis=1) - nchunk
    total = jnp.sum(nchunk, axis=1)
    rows = TOP_K * tm + n_e * ROW_CHUNK
    tok = lambda w: pl.BlockSpec((1, tm, w), lambda bi, i, e, *_: (bi, i, 0))
    tok_once = lambda w: pl.BlockSpec((1, tm, w), lambda bi, i, e, *_: (bi, i, 0), pipeline_mode=pl.Buffered(1))
    vec = pl.BlockSpec((1, 1, d), lambda bi, i, e, *_: (bi, 0, 0))
    grid_spec = pltpu.PrefetchScalarGridSpec(
        num_scalar_prefetch=3,
        grid=(b, l // tm, n_e),
        in_specs=[tok_once(d), tok_once(d), tok(LANES),
                  pl.BlockSpec(before.shape, lambda bi, i, e, *_: (0, 0), pipeline_mode=pl.Buffered(1)), vec,
                  pl.BlockSpec(g3.shape, lambda bi, i, e, *_: (0, 0)),
                  pl.BlockSpec((1, d, ff), lambda bi, i, e, *_: (e, 0, 0)),
                  pl.BlockSpec((1, d, ff), lambda bi, i, e, *_: (e, 0, 0)),
                  pl.BlockSpec((1, ff, d), lambda bi, i, e, *_: (e, 0, 0))],
        out_specs=tok(d),
        scratch_shapes=[pltpu.VMEM((rows, d), BF16), pltpu.VMEM((rows, LANES), F32),
                        pltpu.VMEM((TOP_K, tm, LANES), F32), pltpu.VMEM((tm, d), F32)],
    )
    return pl.pallas_call(
        _moe_kernel,
        grid_spec=grid_spec,
        out_shape=jax.ShapeDtypeStruct((b, l, d), F32),
        compiler_params=_params(("arbitrary", "arbitrary", "arbitrary")),
        name="ffn_moe",
    )(nchunk.reshape(-1), first.reshape(-1), total, x1, h2, route, before, gt, g3, wg, wu, wd)


def _rot_cols(w, n_heads, head_dim):
    d = w.shape[0]
    w = w.reshape(d, n_heads, head_dim)
    half = ROPE_DIM // 2
    rot = jnp.concatenate([-w[..., half:ROPE_DIM], w[..., :half],
                           jnp.zeros((d, n_heads, head_dim - ROPE_DIM), w.dtype)], axis=-1)
    return rot.reshape(d, n_heads * head_dim)


def _pad_cols(w, width):
    return jnp.pad(w, ((0, 0), (0, width - w.shape[1])))


def _layer_weights(w_in, w_uk, w_uv, w_pool):
    d = w_in.shape[0]
    offs, o = [], 0
    for s in IN_SIZES:
        offs.append(o)
        o += s
    piece = lambda k: w_in[:, offs[k]:offs[k] + IN_SIZES[k]]
    w_qa, w_lat, w_kr, w_qi, w_ki, w_wi, w_up, w_qc, w_kc, w_vc, w_gate = [piece(k) for k in range(len(IN_SIZES))]
    attn_scale = HEAD_DIM ** -0.5
    idx_scale = IDX_DIM ** -0.5 * N_IDX_HEADS ** -0.5
    wrow = jnp.concatenate([
        w_lat,
        _pad_cols(w_kr, LANES), _pad_cols(_rot_cols(w_kr, 1, ROPE_DIM), LANES),
        _pad_cols(w_ki, LANES), _pad_cols(_rot_cols(w_ki, 1, IDX_DIM), LANES),
        w_up, w_kc, w_gate], axis=1).astype(BF16)
    w_qc_t = (w_qc * attn_scale).T.reshape(N_HEADS_C, HEAD_DIM, d)
    zeros = jnp.zeros_like(w_qc_t)
    even = (jnp.arange(N_HEADS_C) % 2 == 0)[:, None, None]
    w_qc_pad = jnp.concatenate([jnp.where(even, w_qc_t, zeros), jnp.where(even, zeros, w_qc_t)], axis=1)
    w_qc_pad = w_qc_pad.reshape(2 * C_WIDTH, d)
    softmax_scale = attn_scale * LOG2_E
    wt = jnp.concatenate([
        (w_qa * softmax_scale).T, w_qi.T, w_qc_pad, w_vc.T,
        jnp.pad((w_wi * idx_scale).T, ((0, T_WI_ROWS - N_IDX_HEADS), (0, 0)))], axis=0).astype(BF16)
    wuk = jnp.zeros((KV_LATENT, LANES), F32).at[:, ROPE_DIM:HEAD_DIM].set(w_uk).astype(BF16)
    wuvt = w_uv.T.astype(BF16)
    wpool = jnp.zeros((POOL_WIDTH, POOL_WIDTH), F32)
    for g in range(N_POOL_GROUPS):
        sl = slice(g * POOL_GROUP_DIM, (g + 1) * POOL_GROUP_DIM)
        wpool = wpool.at[sl, sl].set(w_pool[g])
    return wrow, wt, wuk, wuvt, wpool.astype(BF16)


def _rope_tables(positions):
    inv = ROPE_THETA ** (-jnp.arange(0, ROPE_DIM, 2, dtype=F32) / ROPE_DIM)
    ang = positions.astype(F32)[..., None] * inv
    cos, sin = jnp.cos(ang), jnp.sin(ang)
    b, l = positions.shape
    ones = jnp.ones((b, l, HEAD_DIM - ROPE_DIM), F32)
    cos_h = jnp.concatenate([cos, cos, ones], axis=-1)
    sin_h = jnp.concatenate([sin, sin, jnp.zeros_like(ones)], axis=-1)
    cos_r = jnp.concatenate([cos_h, cos_h], axis=-1)
    sin_r = jnp.concatenate([sin_h, sin_h], axis=-1)
    return cos_r, sin_r, cos_h.transpose(0, 2, 1), sin_h.transpose(0, 2, 1)


def kernel(x, c, positions, w_ada, b_ada, norm_gains, w_in, g_kv_latent, w_uk, w_uv, w_pool, pool_scale,
           w_br_a, w_br_b, w_br_c, w_out, w_gate_dense, w_up_dense, w_down_dense,
           w_router, w_gate_moe, w_up_moe, w_down_moe):
    b, l, d = x.shape
    depth = w_in.shape[0]
    assert d == D_MODEL and l % max(KC, TM_FFN, TM_MOE) == 0
    cos_r, sin_r, cos_t, sin_t = _rope_tables(positions)
    c_pad = jnp.pad(c, ((0, 8 - b), (0, 0)))
    mod = _mod_call(c_pad, w_ada, b_ada)[:, :b]
    idx = lax.broadcasted_iota(I32, (QB, QB), 0)
    jdx = lax.broadcasted_iota(I32, (QB, QB), 1)
    tri = (jdx <= idx).astype(BF16)
    later = (jdx > idx).astype(BF16)
    later2 = jnp.concatenate([later, later], axis=1)
    tm_moe = min(TM_MOE, l)
    before = (lax.broadcasted_iota(I32, (tm_moe, tm_moe), 1)
              < lax.broadcasted_iota(I32, (tm_moe, tm_moe), 0)).astype(BF16)
    for layer in range(depth):
        sh1, sc1, gt1, sh2, sc2, gt2 = [m.reshape(b, 1, d) for m in jnp.split(mod[layer], 6, axis=-1)]
        gains = norm_gains[layer].reshape(4, 1, d)
        wrow, wt, wuk, wuvt, wpool = _layer_weights(w_in[layer], w_uk[layer], w_uv[layer], w_pool[layer])
        (qat, qit, wit, ka, vat, ki, qct, kc, vct, ga, gc, mb) = _inproj_call(
            x, sh1, sc1, gains[0], wrow, wt, cos_r, sin_r, cos_t, sin_t,
            g_kv_latent[layer].reshape(1, KV_LATENT), wuk, wuvt, wpool,
            pool_scale[layer].reshape(1, POOL_WIDTH), w_br_b[layer].astype(BF16))
        ya = _dsa_call(qit, wit, qat, ki, ka, vat, tri)
        yc = _stick_call(qct, kc, vct, later2)
        i = layer // 2
        merge_args = (x, ya, yc, mb, ga, gc, gt1, sh2, sc2, gains[1], gains[2],
                      w_br_a[layer].astype(BF16), w_br_c[layer].astype(BF16), w_out[layer].astype(BF16))
        if layer % 2 == 0:
            x1, h2 = _merge_call(*merge_args)
            x = _ffn_call(x1, h2, gt2, gains[3], w_gate_dense[i].astype(BF16), w_up_dense[i].astype(BF16),
                          w_down_dense[i].astype(BF16))
        else:
            x1, h2, route, counts = _merge_call(*merge_args, wr=_pad_cols(w_router[i], LANES))
            x = _moe_call(x1, h2, route, counts, before, gt2, gains[3], w_gate_moe[i].astype(BF16),
                          w_up_moe[i].astype(BF16), w_down_moe[i].astype(BF16))
    return x
```
